```python
import math
import jax, jax.numpy as jnp
from jax import lax
import numpy as np

D_MODEL = 1024
BATCH = 16
SEQ = 2048
DEPTH = 2

D_MIX = D_MODEL
D_ATTN = D_MIX // 2
D_POOL = D_MIX - D_ATTN
N_HEADS = 8
HEAD_DIM = D_ATTN // N_HEADS
ROT_DIM = HEAD_DIM // 4
ROPE_THETA = 500000.0
MOBA_BLOCK = 256
MOBA_TOPK = 3
Q_CHUNK = 32
POOL_WINDOWS = (2, 4, 8, 16)
N_POOL_GROUPS = len(POOL_WINDOWS)
POOL_GROUP = D_POOL // N_POOL_GROUPS
D_IN = 3 * D_ATTN + D_POOL
N_EXPERTS = 16
N_EXPERT_GROUPS = 4
EXPERTS_PER_GROUP = N_EXPERTS // N_EXPERT_GROUPS
TOP_K = 2
D_EXPERT = 512
DEEPNORM_ALPHA = (2 * DEPTH) ** 0.25
DEEPNORM_BETA = (8 * DEPTH) ** -0.25
N_MOD = 6
LN_EPS = 1e-5
NEG_INF = -1e30

kernel_name = "hymba_moba_pool_grouped_moe_deepnorm"


def layer_norm_plain(x):
    xf = x.astype(jnp.float32)
    mu = xf.mean(-1, keepdims=True)
    var = jnp.square(xf - mu).mean(-1, keepdims=True)
    return ((xf - mu) * lax.rsqrt(var + LN_EPS)).astype(x.dtype)


def layer_norm_affine(x, g, b):
    xf = x.astype(jnp.float32)
    mu = xf.mean(-1, keepdims=True)
    var = jnp.square(xf - mu).mean(-1, keepdims=True)
    y = (xf - mu) * lax.rsqrt(var + LN_EPS) * g.astype(jnp.float32) + b.astype(jnp.float32)
    return y.astype(x.dtype)


def split_heads(t):
    B, S, _ = t.shape
    return t.reshape(B, S, N_HEADS, HEAD_DIM).transpose(0, 2, 1, 3)


def partial_rope(x, pos):
    half = ROT_DIM // 2
    inv_freq = ROPE_THETA ** (-(jnp.arange(half, dtype=jnp.float32) * 2.0 / ROT_DIM))
    ang = pos.astype(jnp.float32)[:, None] * inv_freq[None, :]
    cos, sin = jnp.cos(ang), jnp.sin(ang)
    xr = x[..., :ROT_DIM].astype(jnp.float32)
    x1, x2 = xr[..., :half], xr[..., half:]
    rot = jnp.concatenate([x1 * cos - x2 * sin, x2 * cos + x1 * sin], axis=-1).astype(x.dtype)
    return jnp.concatenate([rot, x[..., ROT_DIM:]], axis=-1)


def moba_attention(q, k, v):
    B, H, S, Dh = q.shape
    n_blk = -(-S // MOBA_BLOCK)
    S_pad = n_blk * MOBA_BLOCK
    pad = ((0, 0), (0, 0), (0, S_pad - S), (0, 0))
    q, k, v = jnp.pad(q, pad), jnp.pad(k, pad), jnp.pad(v, pad)
    n_sel = min(MOBA_TOPK, n_blk - 1)
    scale = HEAD_DIM ** -0.5
    kb = k.reshape(B, H, n_blk, MOBA_BLOCK, Dh)
    vb = v.reshape(B, H, n_blk, MOBA_BLOCK, Dh)
    qblk = jnp.arange(S_pad) // MOBA_BLOCK
    if n_sel > 0:
        kbar = kb.astype(jnp.float32).mean(axis=3)
        gate = jnp.einsum('bhsd,bhnd->bhsn', q.astype(jnp.float32), kbar)
        past = jnp.arange(n_blk)[None, :] < qblk[:, None]
        gate = jnp.where(past, gate, NEG_INF)
        _, sel = lax.top_k(gate, n_sel)
        sel_ok = jnp.arange(n_sel)[None, :] < qblk[:, None]
    else:
        sel = jnp.zeros((B, H, S_pad, 0), jnp.int32)
        sel_ok = jnp.zeros((S_pad, 0), bool)
    n_chunk = S_pad // Q_CHUNK
    q_all = q.reshape(B, H, n_chunk, Q_CHUNK, Dh).transpose(2, 0, 1, 3, 4)
    sel_all = sel.reshape(B, H, n_chunk, Q_CHUNK, n_sel).transpose(2, 0, 1, 3, 4)
    ok_all = sel_ok.reshape(n_chunk, Q_CHUNK, n_sel)
    starts = jnp.arange(n_chunk, dtype=jnp.int32) * Q_CHUNK
    b_ix = jnp.arange(B)[:, None, None]
    h_ix = jnp.arange(H)[None, :, None]

    def chunk(args):
        qc, selc, okc, start = args
        blk_start = (start // MOBA_BLOCK) * MOBA_BLOCK
        k_own = lax.dynamic_slice_in_dim(k, blk_start, MOBA_BLOCK, axis=2)
        v_own = lax.dynamic_slice_in_dim(v, blk_start, MOBA_BLOCK, axis=2)
        qp = start + jnp.arange(Q_CHUNK)
        kp = blk_start + jnp.arange(MOBA_BLOCK)
        own = jnp.einsum('bhqd,bhkd->bhqk', qc, k_own, preferred_element_type=jnp.float32) * scale
        logits = [jnp.where(kp[None, :] <= qp[:, None], own, NEG_INF)]
        for r in range(n_sel):
            kg = kb[b_ix, h_ix, selc[..., r]]
            lr = jnp.einsum('bhqd,bhqkd->bhqk', qc, kg, preferred_element_type=jnp.float32) * scale
            logits.append(jnp.where(okc[:, r][:, None], lr, NEG_INF))
        p = jax.nn.softmax(jnp.concatenate(logits, axis=-1), axis=-1)
        ps = jnp.split(p, n_sel + 1, axis=-1)
        out = jnp.einsum('bhqk,bhkd->bhqd', ps[0].astype(v.dtype), v_own, preferred_element_type=jnp.float32)
        for r in range(n_sel):
            vg = vb[b_ix, h_ix, selc[..., r]]
            out = out + jnp.einsum('bhqk,bhqkd->bhqd', ps[r + 1].astype(v.dtype), vg, preferred_element_type=jnp.float32)
        return out.astype(v.dtype)

    out = lax.map(chunk, (q_all, sel_all, ok_all, starts))
    return out.transpose(1, 2, 0, 3, 4).reshape(B, H, S_pad, Dh)[:, :, :S]


def pool_mixer(p, w_pool, pool_scale):
    B, S, _ = p.shape
    pf = p.astype(jnp.float32).reshape(B, S, N_POOL_GROUPS, POOL_GROUP)
    cs = jnp.concatenate([jnp.zeros_like(pf[:, :1]), jnp.cumsum(pf, axis=1)], axis=1)
    t = jnp.arange(1, S + 1, dtype=jnp.float32)
    outs = []
    for g, w in enumerate(POOL_WINDOWS):
        hi = cs[:, 1:, g]
        lo = jnp.concatenate([jnp.zeros_like(cs[:, :w - 1, g]), cs[:, :S - w + 1, g]], axis=1)
        cnt = jnp.minimum(t, float(w))[None, :, None]
        outs.append((hi - lo) / cnt - pf[:, :, g])
    d = jnp.stack(outs, axis=2).astype(p.dtype)
    y = jnp.einsum('bsgc,gce->bsge', d, w_pool).reshape(B, S, D_POOL)
    return y * pool_scale


def grouped_moe(h, w_router, router_bias, w_gate, w_up, w_down):
    B, S, D = h.shape
    hf = h.reshape(B * S, D)
    scores = jax.nn.softmax((hf @ w_router).astype(jnp.float32), axis=-1)
    sel = scores + router_bias.astype(jnp.float32)
    grp = sel.reshape(-1, N_EXPERT_GROUPS, EXPERTS_PER_GROUP)
    grp_score = lax.top_k(grp, TOP_K)[0].sum(-1)
    best = jnp.argmax(grp_score, axis=-1)
    in_grp = (jnp.arange(N_EXPERTS) // EXPERTS_PER_GROUP)[None, :] == best[:, None]
    _, idx = lax.top_k(jnp.where(in_grp, sel, NEG_INF), TOP_K)
    wts = jnp.take_along_axis(scores, idx, axis=-1)
    wts = wts / wts.sum(-1, keepdims=True)
    gates = jnp.sum(jax.nn.one_hot(idx, N_EXPERTS, dtype=jnp.float32) * wts[..., None], axis=1)
    y = jnp.zeros((B * S, D), jnp.float32)
    for e in range(N_EXPERTS):
        a = jax.nn.silu(hf @ w_gate[e]) * (hf @ w_up[e])
        y = y + gates[:, e:e + 1] * (a @ w_down[e]).astype(jnp.float32)
    return y.astype(h.dtype).reshape(B, S, D)


def setup_inputs(seed: int = 0) -> dict:
    key = jax.random.key(seed)
    ks = jax.random.split(key, 20)
    f32 = jnp.float32
    nrm = lambda k, shape, s: jax.random.normal(k, shape, f32) * s
    x = jax.random.normal(ks[0], (BATCH, SEQ, D_MODEL), f32)
    c = jax.random.normal(ks[1], (BATCH, D_MODEL), f32)
    w_mod = nrm(ks[2], (DEPTH, D_MODEL, N_MOD * D_MODEL), 0.5 * D_MODEL ** -0.5)
    b_mod = nrm(ks[3], (DEPTH, N_MOD * D_MODEL), 0.02)
    w_in = nrm(ks[4], (DEPTH, D_MODEL, D_IN), D_MODEL ** -0.5)
    col_scale = jnp.concatenate([jnp.ones((2 * D_ATTN,), f32),
                                 jnp.full((D_ATTN,), DEEPNORM_BETA, f32),
                                 jnp.ones((D_POOL,), f32)])
    w_in = w_in * col_scale
    w_pool = nrm(ks[5], (DEPTH, N_POOL_GROUPS, POOL_GROUP, POOL_GROUP), POOL_GROUP ** -0.5 * DEEPNORM_BETA)
    pool_scale = 1.0 + nrm(ks[6], (DEPTH, D_POOL), 0.1)
    w_out = nrm(ks[7], (DEPTH, D_MIX, D_MODEL), D_MIX ** -0.5 * DEEPNORM_BETA)
    ln1_g = 1.0 + nrm(ks[8], (DEPTH, D_MODEL), 0.05)
    ln1_b = nrm(ks[9], (DEPTH, D_MODEL), 0.02)
    w_router = nrm(ks[10], (D_MODEL, N_EXPERTS), D_MODEL ** -0.5)
    router_bias = nrm(ks[11], (N_EXPERTS,), 0.01)
    w_gate = nrm(ks[12], (DEPTH, N_EXPERTS, D_MODEL, D_EXPERT), D_MODEL ** -0.5)
    w_up = nrm(ks[13], (DEPTH, N_EXPERTS, D_MODEL, D_EXPERT), D_MODEL ** -0.5)
    w_down = nrm(ks[14], (DEPTH, N_EXPERTS, D_EXPERT, D_MODEL), D_EXPERT ** -0.5 * DEEPNORM_BETA)
    ln2_g = 1.0 + nrm(ks[15], (DEPTH, D_MODEL), 0.05)
    ln2_b = nrm(ks[16], (DEPTH, D_MODEL), 0.02)
    return {"x": x, "c": c, "w_mod": w_mod, "b_mod": b_mod, "w_in": w_in,
            "w_pool": w_pool, "pool_scale": pool_scale, "w_out": w_out,
            "ln1_g": ln1_g, "ln1_b": ln1_b, "w_router": w_router,
            "router_bias": router_bias, "w_gate": w_gate, "w_up": w_up,
            "w_down": w_down, "ln2_g": ln2_g, "ln2_b": ln2_b}


def reference(x, c, w_mod, b_mod, w_in, w_pool, pool_scale, w_out, ln1_g, ln1_b,
              w_router, router_bias, w_gate, w_up, w_down, ln2_g, ln2_b):
    B, S, D = x.shape
    pos = jnp.arange(S)
    cond = jax.nn.silu(c)
    for l in range(DEPTH):
        mod = cond @ w_mod[l] + b_mod[l]
        sh1, sc1, g1, sh2, sc2, g2 = [m[:, None, :] for m in jnp.split(mod, N_MOD, axis=-1)]
        h = layer_norm_plain(x) * (1 + sc1) + sh1
        proj = h @ w_in[l]
        q, k, v, p = jnp.split(proj, [D_ATTN, 2 * D_ATTN, 3 * D_ATTN], axis=-1)
        q = partial_rope(split_heads(q), pos)
        k = partial_rope(split_heads(k), pos)
        a = moba_attention(q, k, split_heads(v))
        a = a.transpose(0, 2, 1, 3).reshape(B, S, D_ATTN)
        m = pool_mixer(p, w_pool[l], pool_scale[l])
        y = jnp.concatenate([a, m], axis=-1) @ w_out[l]
        x = layer_norm_affine(DEEPNORM_ALPHA * x + g1 * y, ln1_g[l], ln1_b[l])
        h = layer_norm_plain(x) * (1 + sc2) + sh2
        y = grouped_moe(h, w_router, router_bias, w_gate[l], w_up[l], w_down[l])
        x = layer_norm_affine(DEEPNORM_ALPHA * x + g2 * y, ln2_g[l], ln2_b[l])
    return x
```

```python
import functools
import math

import jax
import jax.numpy as jnp
from jax import lax
from jax.experimental import pallas as pl
from jax.experimental.pallas import tpu as pltpu

D_MODEL = 1024
DEPTH = 2
D_ATTN = 512
D_POOL = 512
N_HEADS = 8
HEAD_DIM = 64
ROT_DIM = 16
ROPE_THETA = 500000.0
MOBA_BLOCK = 256
MOBA_TOPK = 3
POOL_GROUP = 128
N_POOL_GROUPS = 4
D_IN = 3 * D_ATTN + D_POOL
N_EXPERTS = 16
EXPERTS_PER_GROUP = 4
D_EXPERT = 512
DEEPNORM_ALPHA = (2 * DEPTH) ** 0.25
N_MOD = 6
LN_EPS = 1e-5
NEG_INF = -1e30

LANES = 128
AUG_LANE0 = HEAD_DIM
VMEM_LIMIT = 56 * 1024 * 1024

_HI = lax.Precision.HIGHEST
_NT = (((1,), (1,)), ((), ()))


def _params(sem):
    return pltpu.CompilerParams(dimension_semantics=sem, vmem_limit_bytes=VMEM_LIMIT)


def _mod_kernel(c_ref, w_ref, b_ref, o_ref):
    c = c_ref[...]
    cond = c * (1.0 / (1.0 + jnp.exp(-c)))
    o_ref[0] = jnp.dot(cond, w_ref[0], precision=_HI,
                       preferred_element_type=jnp.float32) + b_ref[0]


def _mod_call(c, w_mod, b_mod):
    B = c.shape[0]
    nt = N_MOD
    return pl.pallas_call(
        _mod_kernel,
        grid=(DEPTH, nt),
        in_specs=[
            pl.BlockSpec((B, D_MODEL), lambda l, j: (0, 0)),
            pl.BlockSpec((1, D_MODEL, D_MODEL), lambda l, j: (l, 0, j)),
            pl.BlockSpec((1, 1, D_MODEL), lambda l, j: (l, 0, j)),
        ],
        out_specs=pl.BlockSpec((1, B, D_MODEL), lambda l, j: (l, 0, j)),
        out_shape=jax.ShapeDtypeStruct((DEPTH, B, N_MOD * D_MODEL), jnp.float32),
        compiler_params=_params(("arbitrary", "arbitrary")),
    )(c, w_mod, b_mod.reshape(DEPTH, 1, N_MOD * D_MODEL))


def _layer_norm(x):
    mu = jnp.mean(x, axis=-1, keepdims=True)
    xc = x - mu
    var = jnp.mean(xc * xc, axis=-1, keepdims=True)
    return xc * lax.rsqrt(var + LN_EPS)


def _proj_kernel(x_ref, mod_ref, w_ref, cos_ref, sina_ref, sinb_ref,
                 qa_ref, ka_ref, v_ref, p_ref, kbar_ref):
    i = pl.program_id(1)

    @pl.when(i == 0)
    def _():
        kbar_ref[...] = jnp.zeros_like(kbar_ref)

    x = x_ref[0]
    shift = mod_ref[0, 0:1, :]
    scale = mod_ref[0, 1:2, :]
    h = (_layer_norm(x) * (1.0 + scale) + shift).astype(jnp.bfloat16)

    cos = cos_ref[...]
    sina = sina_ref[...]
    sinb = sinb_ref[...]

    def rope(t):
        return (t * cos + pltpu.roll(t, 8, 1) * sina
                + pltpu.roll(t, LANES - 8, 1) * sinb)

    def proj(c0, width):
        return jnp.dot(h, w_ref[:, c0:c0 + width], preferred_element_type=jnp.float32)

    n_slab = D_ATTN // LANES
    q = proj(0, D_ATTN)
    q_slabs = [rope(q[:, LANES * s:LANES * (s + 1)]) * (HEAD_DIM ** -0.5)
               for s in range(n_slab)]
    k = proj(D_ATTN, D_ATTN)
    k_slabs = [rope(k[:, LANES * s:LANES * (s + 1)]) for s in range(n_slab)]
    v_ref[0] = proj(2 * D_ATTN, D_ATTN).astype(jnp.bfloat16)
    p_ref[0] = proj(3 * D_ATTN, D_POOL)

    kmean = jnp.concatenate(
        [jnp.mean(ks, axis=0, keepdims=True) for ks in k_slabs], axis=1)
    kbar_ref[0, pl.ds(i, 1), :] = kmean

    kb = kbar_ref[0]
    kb_rows = jnp.concatenate([kb] * (LANES // 8), axis=0)
    r_head = lax.broadcasted_iota(jnp.int32, (LANES, D_ATTN), 1) >> 6
    c_head = lax.broadcasted_iota(jnp.int32, (LANES, D_ATTN), 0) >> 3
    kbd_t = jnp.where(r_head == c_head, kb_rows, 0.0)
    q_full = jnp.concatenate(q_slabs, axis=1)
    gate = lax.dot_general(q_full, kbd_t, _NT, precision=_HI,
                           preferred_element_type=jnp.float32)

    lane = lax.broadcasted_iota(jnp.int32, (MOBA_BLOCK, LANES), 1)
    j_of = lane & 7
    past = j_of < i
    gm = jnp.where(past, gate, NEG_INF)
    rank = jnp.zeros((MOBA_BLOCK, LANES), jnp.int32)
    for r in range(1, 8):
        other = jnp.where(j_of >= r, pltpu.roll(gm, r, 1),
                          pltpu.roll(gm, LANES - 8 + r, 1))
        beats = (other > gm) | ((other == gm) & (j_of >= r))
        rank = rank + beats.astype(jnp.int32)
    allowed = (past & (rank < MOBA_TOPK)) | (j_of == i)
    bias = jnp.where(allowed, 0.0, NEG_INF)

    aug = (lane >= AUG_LANE0) & (lane < AUG_LANE0 + 8)
    own_onehot = jnp.where(lane == AUG_LANE0 + i, 1.0, 0.0)
    for hd in range(N_HEADS):
        qs = q_slabs[hd // 2]
        ks = k_slabs[hd // 2]
        if hd % 2 == 1:
            qs = pltpu.roll(qs, HEAD_DIM, 1)
            ks = pltpu.roll(ks, HEAD_DIM, 1)
        sh = (AUG_LANE0 - 8 * hd) % LANES
        b_h = pltpu.roll(bias, sh, 1) if sh else bias
        qa = jnp.where(lane < HEAD_DIM, qs, jnp.where(aug, b_h, 0.0))
        ka = jnp.where(lane < HEAD_DIM, ks, own_onehot)
        qa_ref[0, hd] = qa.astype(jnp.bfloat16)
        ka_ref[0, hd] = ka.astype(jnp.bfloat16)


def _proj_call(x, modl, w_in_bf, cos_t, sina_t, sinb_t):
    B, S, D = x.shape
    nb = S // MOBA_BLOCK
    tm = MOBA_BLOCK
    tab = pl.BlockSpec((tm, LANES), lambda b, i: (i, 0))
    head_spec = pl.BlockSpec((1, N_HEADS, tm, LANES), lambda b, i: (b, 0, i, 0))
    return pl.pallas_call(
        _proj_kernel,
        grid=(B, nb),
        in_specs=[
            pl.BlockSpec((1, tm, D), lambda b, i: (b, i, 0)),
            pl.BlockSpec((1, N_MOD, D), lambda b, i: (b, 0, 0)),
            pl.BlockSpec((D, D_IN), lambda b, i: (0, 0)),
            tab, tab, tab,
        ],
        out_specs=[
            head_spec, head_spec,
            pl.BlockSpec((1, tm, D_ATTN), lambda b, i: (b, i, 0)),
            pl.BlockSpec((1, tm, D_POOL), lambda b, i: (b, i, 0)),
            pl.BlockSpec((1, nb, D_ATTN), lambda b, i: (b, 0, 0)),
        ],
        out_shape=[
            jax.ShapeDtypeStruct((B, N_HEADS, S, LANES), jnp.bfloat16),
            jax.ShapeDtypeStruct((B, N_HEADS, S, LANES), jnp.bfloat16),
            jax.ShapeDtypeStruct((B, S, D_ATTN), jnp.bfloat16),
            jax.ShapeDtypeStruct((B, S, D_POOL), jnp.float32),
            jax.ShapeDtypeStruct((B, nb, D_ATTN), jnp.float32),
        ],
        compiler_params=_params(("arbitrary", "arbitrary")),
    )(x, modl, w_in_bf, cos_t, sina_t, sinb_t)


def _attn_kernel(qa_ref, ka_ref, v_ref, o_ref):
    nb = v_ref.shape[1] // MOBA_BLOCK
    row = lax.broadcasted_iota(jnp.int32, (MOBA_BLOCK, MOBA_BLOCK), 0)
    col = lax.broadcasted_iota(jnp.int32, (MOBA_BLOCK, MOBA_BLOCK), 1)
    causal = col <= row
    lane = lax.broadcasted_iota(jnp.int32, (MOBA_BLOCK, LANES), 1)

    def q_block(i, carry):
        r0 = pl.multiple_of(i * MOBA_BLOCK, MOBA_BLOCK)
        outs = []
        for hh in range(2):
            q = qa_ref[0, hh, pl.ds(r0, MOBA_BLOCK), :]
            s = lax.dot_general(q, ka_ref[0, hh, pl.ds(r0, MOBA_BLOCK), :], _NT,
                                preferred_element_type=jnp.float32)
            s = jnp.where(causal, s, NEG_INF)
            m = jnp.max(s, axis=1, keepdims=True)
            p = jnp.exp(s - m)
            l = jnp.sum(p, axis=1, keepdims=True)
            acc = jnp.dot(p.astype(jnp.bfloat16), v_ref[0, pl.ds(r0, MOBA_BLOCK), :],
                          preferred_element_type=jnp.float32)

            def kv_block(j, c, q=q, hh=hh):
                m, l, acc = c
                c0 = pl.multiple_of(j * MOBA_BLOCK, MOBA_BLOCK)
                s = lax.dot_general(q, ka_ref[0, hh, pl.ds(c0, MOBA_BLOCK), :], _NT,
                                    preferred_element_type=jnp.float32)
                m_new = jnp.maximum(m, jnp.max(s, axis=1, keepdims=True))
                alpha = jnp.exp(m - m_new)
                p = jnp.exp(s - m_new)
                l = alpha * l + jnp.sum(p, axis=1, keepdims=True)
                acc = alpha * acc + jnp.dot(
                    p.astype(jnp.bfloat16), v_ref[0, pl.ds(c0, MOBA_BLOCK), :],
                    preferred_element_type=jnp.float32)
                return m_new, l, acc

            m, l, acc = lax.fori_loop(0, i, kv_block, (m, l, acc))
            outs.append(acc / l)
        o = jnp.where(lane < HEAD_DIM, outs[0], outs[1])
        o_ref[0, pl.ds(r0, MOBA_BLOCK), :] = o.astype(o_ref.dtype)
        return carry

    lax.fori_loop(0, nb, q_block, 0)


def _attn_call(qa, ka, v):
    B, _, S, _ = qa.shape
    n_pair = N_HEADS // 2
    pair_spec = pl.BlockSpec((1, 2, S, LANES), lambda b, hp: (b, hp, 0, 0))
    slab_spec = pl.BlockSpec((1, S, LANES), lambda b, hp: (b, 0, hp))
    return pl.pallas_call(
        _attn_kernel,
        grid=(B, n_pair),
        in_specs=[pair_spec, pair_spec, slab_spec],
        out_specs=slab_spec,
        out_shape=jax.ShapeDtypeStruct((B, S, D_ATTN), jnp.bfloat16),
        compiler_params=_params(("arbitrary", "arbitrary")),
    )(qa, ka, v)


def _pool_kernel(p_ref, w_ref, sc_ref, o_ref):
    g = pl.program_id(1)
    S = p_ref.shape[1]
    p = p_ref[0]
    t = lax.broadcasted_iota(jnp.int32, (S, LANES), 0)

    def shifted(x, k):
        return jnp.where(t >= k, pltpu.roll(x, k, 0), 0.0)

    win = p
    sums = []
    for step in range(N_POOL_GROUPS):
        win = win + shifted(win, 1 << step)
        sums.append(win)
    wsum = jnp.where(g == 0, sums[0],
                     jnp.where(g == 1, sums[1], jnp.where(g == 2, sums[2], sums[3])))
    window = jnp.left_shift(2, g)
    cnt = jnp.minimum(t + 1, window).astype(jnp.float32)
    d = (wsum / cnt - p).astype(jnp.bfloat16)
    y = jnp.dot(d, w_ref[0].astype(jnp.bfloat16), preferred_element_type=jnp.float32)
    o_ref[0] = (y * sc_ref[...]).astype(o_ref.dtype)


def _pool_call(p, w_pool_l, pool_scale_l):
    B, S, _ = p.shape
    slab = pl.BlockSpec((1, S, LANES), lambda b, g: (b, 0, g))
    return pl.pallas_call(
        _pool_kernel,
        grid=(B, N_POOL_GROUPS),
        in_specs=[
            slab,
            pl.BlockSpec((1, POOL_GROUP, POOL_GROUP), lambda b, g: (g, 0, 0)),
            pl.BlockSpec((1, LANES), lambda b, g: (0, g)),
        ],
        out_specs=slab,
        out_shape=jax.ShapeDtypeStruct((B, S, D_POOL), jnp.bfloat16),
        compiler_params=_params(("arbitrary", "arbitrary")),
    )(p, w_pool_l, pool_scale_l.reshape(1, D_POOL))


def _mixout_kernel(a_ref, m_ref, x_ref, mod_ref, w_ref, g_ref, b_ref, o_ref):
    y = jnp.dot(a_ref[0], w_ref[0:D_ATTN, :], preferred_element_type=jnp.float32)
    y = y + jnp.dot(m_ref[0], w_ref[D_ATTN:, :], preferred_element_type=jnp.float32)
    gate = mod_ref[0, 2:3, :]
    z = DEEPNORM_ALPHA * x_ref[0] + gate * y
    o_ref[0] = _layer_norm(z) * g_ref[...] + b_ref[...]


def _mixout_call(a, m, x, modl, w_out_bf, ln_g, ln_b):
    B, S, D = x.shape
    tm = 512
    vec = pl.BlockSpec((1, D), lambda b, i: (0, 0))
    return pl.pallas_call(
        _mixout_kernel,
        grid=(B, S // tm),
        in_specs=[
            pl.BlockSpec((1, tm, D_ATTN), lambda b, i: (b, i, 0)),
            pl.BlockSpec((1, tm, D_POOL), lambda b, i: (b, i, 0)),
            pl.BlockSpec((1, tm, D), lambda b, i: (b, i, 0)),
            pl.BlockSpec((1, N_MOD, D), lambda b, i: (b, 0, 0)),
            pl.BlockSpec((D, D), lambda b, i: (0, 0)),
            vec, vec,
        ],
        out_specs=pl.BlockSpec((1, tm, D), lambda b, i: (b, i, 0)),
        out_shape=jax.ShapeDtypeStruct((B, S, D), jnp.float32),
        compiler_params=_params(("arbitrary", "arbitrary")),
    )(a, m, x, modl, w_out_bf, ln_g.reshape(1, D), ln_b.reshape(1, D))


def _router_gates(logits, rbias):
    T = logits.shape[0]
    lane = lax.broadcasted_iota(jnp.int32, (T, LANES), 1)
    valid = lane < N_EXPERTS
    neg_inf = -jnp.inf
    lg = jnp.where(valid, logits, neg_inf)
    mx = jnp.max(lg, axis=1, keepdims=True)
    ex = jnp.exp(lg - mx)
    scores = ex / jnp.sum(ex, axis=1, keepdims=True)
    sel = scores + rbias
    lane_f = lane.astype(jnp.float32)
    big = jnp.float32(1 << 20)

    def top2(vals):
        m1 = jnp.max(vals, axis=1, keepdims=True)
        i1 = jnp.min(jnp.where(vals == m1, lane_f, big), axis=1, keepdims=True)
        rest = jnp.where(lane_f == i1, neg_inf, vals)
        m2 = jnp.max(rest, axis=1, keepdims=True)
        i2 = jnp.min(jnp.where(rest == m2, lane_f, big), axis=1, keepdims=True)
        return m1, i1, m2, i2

    best_score = None
    best = None
    n_groups = N_EXPERTS // EXPERTS_PER_GROUP
    for g in range(n_groups):
        in_g = (lane >= g * EXPERTS_PER_GROUP) & (lane < (g + 1) * EXPERTS_PER_GROUP)
        m1, _, m2, _ = top2(jnp.where(in_g, sel, neg_inf))
        gs = m1 + m2
        if g == 0:
            best_score, best = gs, jnp.zeros_like(gs, dtype=jnp.int32)
        else:
            better = gs > best_score
            best_score = jnp.where(better, gs, best_score)
            best = jnp.where(better, g, best)
    in_best = valid & ((lane >> 2) == best)
    masked = jnp.where(valid, jnp.where(in_best, sel, NEG_INF), neg_inf)
    _, i1, _, i2 = top2(masked)
    w1 = jnp.sum(jnp.where(lane_f == i1, scores, 0.0), axis=1, keepdims=True)
    w2 = jnp.sum(jnp.where(lane_f == i2, scores, 0.0), axis=1, keepdims=True)
    tot = w1 + w2
    return jnp.where(lane_f == i1, w1 / tot, jnp.where(lane_f == i2, w2 / tot, 0.0))


def _moe_kernel(x_ref, mod_ref, wr_ref, rb_ref, wg_ref, wu_ref, wd_ref, g_ref, b_ref,
                o_ref, h_scr, gate_scr, acc_scr):
    e = pl.program_id(1)

    @pl.when(e == 0)
    def _():
        shift = mod_ref[0, 3:4, :]
        scale = mod_ref[0, 4:5, :]
        h = _layer_norm(x_ref[0]) * (1.0 + scale) + shift
        h_scr[...] = h.astype(jnp.bfloat16)
        logits = jnp.dot(h, wr_ref[...], precision=_HI, preferred_element_type=jnp.float32)
        gate_scr[...] = _router_gates(logits, rb_ref[...])
        acc_scr[...] = jnp.zeros_like(acc_scr)

    h = h_scr[...]
    gt = jnp.dot(h, wg_ref[0], preferred_element_type=jnp.float32)
    up = jnp.dot(h, wu_ref[0], preferred_element_type=jnp.float32)
    act = (gt * (1.0 / (1.0 + jnp.exp(-gt))) * up).astype(jnp.bfloat16)
    y = jnp.dot(act, wd_ref[0], preferred_element_type=jnp.float32)
    lane = lax.broadcasted_iota(jnp.int32, gate_scr.shape, 1)
    gcol = jnp.sum(jnp.where(lane == e, gate_scr[...], 0.0), axis=1, keepdims=True)
    acc_scr[...] += gcol * y

    @pl.when(e == N_EXPERTS - 1)
    def _():
        gate = mod_ref[0, 5:6, :]
        z = DEEPNORM_ALPHA * x_ref[0] + gate * acc_scr[...]
        o_ref[0] = _layer_norm(z) * g_ref[...] + b_ref[...]


def _moe_call(x, modl, w_router_pad, rbias_pad, wg_bf, wu_bf, wd_bf, ln_g, ln_b):
    B, S, D = x.shape
    tm = 1024
    nt = S // tm
    vec = pl.BlockSpec((1, D), lambda t, e: (0, 0))
    lanes_vec = pl.BlockSpec((1, LANES), lambda t, e: (0, 0))
    xspec = pl.BlockSpec((1, tm, D), lambda t, e: (t // nt, t % nt, 0))
    return pl.pallas_call(
        _moe_kernel,
        grid=(B * nt, N_EXPERTS),
        in_specs=[
            xspec,
            pl.BlockSpec((1, N_MOD, D), lambda t, e: (t // nt, 0, 0)),
            pl.BlockSpec((D, LANES), lambda t, e: (0, 0)),
            lanes_vec,
            pl.BlockSpec((1, D, D_EXPERT), lambda t, e: (e, 0, 0)),
            pl.BlockSpec((1, D, D_EXPERT), lambda t, e: (e, 0, 0)),
            pl.BlockSpec((1, D_EXPERT, D), lambda t, e: (e, 0, 0)),
            vec, vec,
        ],
        out_specs=xspec,
        out_shape=jax.ShapeDtypeStruct((B, S, D), jnp.float32),
        scratch_shapes=[
            pltpu.VMEM((tm, D), jnp.bfloat16),
            pltpu.VMEM((tm, LANES), jnp.float32),
            pltpu.VMEM((tm, D), jnp.float32),
        ],
        compiler_params=_params(("arbitrary", "arbitrary")),
    )(x, modl, w_router_pad, rbias_pad, wg_bf, wu_bf, wd_bf,
      ln_g.reshape(1, D), ln_b.reshape(1, D))


def _rope_tables(S):
    half = ROT_DIM // 2
    inv_freq = ROPE_THETA ** (-(jnp.arange(half, dtype=jnp.float32) * 2.0 / ROT_DIM))
    ang = jnp.arange(S, dtype=jnp.float32)[:, None] * inv_freq[None, :]
    cos, sin = jnp.cos(ang), jnp.sin(ang)
    d = jnp.arange(LANES) % HEAD_DIM
    f = d % half
    cos_t = jnp.where(d[None, :] < ROT_DIM, cos[:, f], 1.0)
    sina_t = jnp.where((d[None, :] >= half) & (d[None, :] < ROT_DIM), sin[:, f], 0.0)
    sinb_t = jnp.where(d[None, :] < half, -sin[:, f], 0.0)
    return cos_t, sina_t, sinb_t


def kernel(x, c, w_mod, b_mod, w_in, w_pool, pool_scale, w_out, ln1_g, ln1_b,
           w_router, router_bias, w_gate, w_up, w_down, ln2_g, ln2_b):
    B, S, D = x.shape
    bf = jnp.bfloat16
    mod = _mod_call(c, w_mod, b_mod).reshape(DEPTH, B, N_MOD, D)
    cos_t, sina_t, sinb_t = _rope_tables(S)
    w_router_pad = jnp.pad(w_router, ((0, 0), (0, LANES - N_EXPERTS)))
    rbias_pad = jnp.pad(router_bias, (0, LANES - N_EXPERTS)).reshape(1, LANES)
    for l in range(DEPTH):
        modl = mod[l]
        qa, ka, v, p, _ = _proj_call(x, modl, w_in[l].astype(bf), cos_t, sina_t, sinb_t)
        a = _attn_call(qa, ka, v)
        m = _pool_call(p, w_pool[l], pool_scale[l])
        x = _mixout_call(a, m, x, modl, w_out[l].astype(bf), ln1_g[l], ln1_b[l])
        x = _moe_call(x, modl, w_router_pad, rbias_pad, w_gate[l].astype(bf),
                      w_up[l].astype(bf), w_down[l].astype(bf), ln2_g[l], ln2_b[l])
    return x
```

```python
import jax
import jax.numpy as jnp
from jax import lax
from jax.experimental import pallas as pl
from jax.experimental.pallas import tpu as pltpu

D_MODEL = 1024
DEPTH = 2
D_ATTN = 512
D_POOL = 512
N_HEADS = 8
HEAD_DIM = 64
ROT_DIM = 16
ROPE_THETA = 500000.0
MOBA_BLOCK = 256
MOBA_TOPK = 3
POOL_GROUP = 128
N_POOL_GROUPS = 4
D_IN = 3 * D_ATTN + D_POOL
N_EXPERTS = 16
EXPERTS_PER_GROUP = 4
D_EXPERT = 512
DEEPNORM_ALPHA = (2 * DEPTH) ** 0.25
N_MOD = 6
LN_EPS = 1e-5
NEG_INF = -1e30

LANES = 128
VMEM_LIMIT = 56 * 1024 * 1024

_HI = lax.Precision.HIGHEST
_NT = (((1,), (1,)), ((), ()))


def _params(sem):
    return pltpu.CompilerParams(dimension_semantics=sem, vmem_limit_bytes=VMEM_LIMIT)


def _nt_dot(a, b):
    return lax.dot_general(a, b, _NT, preferred_element_type=jnp.float32)


def _mod_kernel(c_ref, w_ref, b_ref, o_ref):
    c = c_ref[...]
    cond = c * (1.0 / (1.0 + jnp.exp(-c)))
    o_ref[0] = jnp.dot(cond, w_ref[0], precision=_HI,
                       preferred_element_type=jnp.float32) + b_ref[0]


def _mod_call(c, w_mod, b_mod):
    B = c.shape[0]
    return pl.pallas_call(
        _mod_kernel,
        grid=(DEPTH, N_MOD),
        in_specs=[
            pl.BlockSpec((B, D_MODEL), lambda l, j: (0, 0)),
            pl.BlockSpec((1, D_MODEL, D_MODEL), lambda l, j: (l, 0, j)),
            pl.BlockSpec((1, 1, D_MODEL), lambda l, j: (l, 0, j)),
        ],
        out_specs=pl.BlockSpec((1, B, D_MODEL), lambda l, j: (l, 0, j)),
        out_shape=jax.ShapeDtypeStruct((DEPTH, B, N_MOD * D_MODEL), jnp.float32),
        compiler_params=_params(("arbitrary", "arbitrary")),
    )(c, w_mod, b_mod.reshape(DEPTH, 1, N_MOD * D_MODEL))


def _layer_norm(x):
    mu = jnp.mean(x, axis=-1, keepdims=True)
    xc = x - mu
    var = jnp.mean(xc * xc, axis=-1, keepdims=True)
    return xc * lax.rsqrt(var + LN_EPS)


def _split_bf16(t):
    hi = t.astype(jnp.bfloat16)
    lo = (t - hi.astype(jnp.float32)).astype(jnp.bfloat16)
    return hi, lo


def _proj_kernel(x_ref, mod_ref, w_ref, cos_ref, sin_ref, place_ref,
                 qa_ref, ka_ref, v_ref, p_ref, kbar_ref):
    i = pl.program_id(1)

    @pl.when(i == 0)
    def _():
        kbar_ref[...] = jnp.zeros_like(kbar_ref)

    x = x_ref[0]
    shift = mod_ref[0, 0:1, :]
    scale = mod_ref[0, 1:2, :]
    h = (_layer_norm(x) * (1.0 + scale) + shift).astype(jnp.bfloat16)

    cos = cos_ref[...]
    sin = sin_ref[...]

    def proj(c0, width):
        return jnp.dot(h, w_ref[:, c0:c0 + width], preferred_element_type=jnp.float32)

    def slab(t, s):
        return t[:, LANES * s:LANES * (s + 1)]

    n_slab = D_ATTN // LANES
    q = proj(0, D_ATTN)
    qp = proj(D_IN, D_ATTN)
    q_slabs = [(slab(q, s) * cos + slab(qp, s) * sin) * (HEAD_DIM ** -0.5)
               for s in range(n_slab)]
    k = proj(D_ATTN, D_ATTN)
    kp = proj(D_IN + D_ATTN, D_ATTN)
    k_slabs = [slab(k, s) * cos + slab(kp, s) * sin for s in range(n_slab)]
    v_ref[0] = proj(2 * D_ATTN, D_ATTN).astype(jnp.bfloat16)
    p_ref[0] = proj(3 * D_ATTN, D_POOL)

    kmean = jnp.concatenate(
        [jnp.mean(ks, axis=0, keepdims=True) for ks in k_slabs], axis=1)
    kbar_ref[0, pl.ds(i, 1), :] = kmean

    kb = kbar_ref[0]
    kb_rows = jnp.concatenate([kb] * (LANES // 8), axis=0)
    r_head = lax.broadcasted_iota(jnp.int32, (LANES, D_ATTN), 1) >> 6
    c_head = lax.broadcasted_iota(jnp.int32, (LANES, D_ATTN), 0) >> 4
    kbd_hi, kbd_lo = _split_bf16(jnp.where(r_head == c_head, kb_rows, 0.0))
    q_hi, q_lo = _split_bf16(jnp.concatenate(q_slabs, axis=1))
    gate = _nt_dot(q_hi, kbd_hi) + (_nt_dot(q_lo, kbd_hi) + _nt_dot(q_hi, kbd_lo))

    lane = lax.broadcasted_iota(jnp.int32, (MOBA_BLOCK, LANES), 1)
    j_of = lane & 7
    past = j_of < i
    gm = jnp.where(past, gate, NEG_INF)
    rank = jnp.zeros((MOBA_BLOCK, LANES), jnp.int32)
    for r in range(1, 8):
        other = pltpu.roll(gm, r, 1)
        beats = (other > gm) | ((other == gm) & (j_of >= r))
        rank = rank + beats.astype(jnp.int32)
    allowed = (past & (rank < MOBA_TOPK)) | (j_of == i)
    bias = jnp.where(allowed, 0.0, NEG_INF).astype(jnp.bfloat16)
    bias_cols = jnp.dot(bias, place_ref[...], preferred_element_type=jnp.float32)

    for hd in range(N_HEADS):
        own = (lane < HEAD_DIM) if hd % 2 == 0 else (lane >= HEAD_DIM)
        col0 = HEAD_DIM if hd % 2 == 0 else 0
        qa = jnp.where(own, q_slabs[hd // 2], slab(bias_cols, hd))
        ka = jnp.where(own, k_slabs[hd // 2], jnp.where(lane == col0 + i, 1.0, 0.0))
        qa_ref[0, hd] = qa.astype(jnp.bfloat16)
        ka_ref[0, hd] = ka.astype(jnp.bfloat16)


def _proj_call(x, modl, w_in_aug, cos_t, sin_t, place):
    B, S, D = x.shape
    nb = S // MOBA_BLOCK
    tm = MOBA_BLOCK
    tab = pl.BlockSpec((tm, LANES), lambda b, i: (i, 0))
    head_spec = pl.BlockSpec((1, N_HEADS, tm, LANES), lambda b, i: (b, 0, i, 0))
    return pl.pallas_call(
        _proj_kernel,
        grid=(B, nb),
        in_specs=[
            pl.BlockSpec((1, tm, D), lambda b, i: (b, i, 0)),
            pl.BlockSpec((1, N_MOD, D), lambda b, i: (b, 0, 0)),
            pl.BlockSpec((D, D_IN + 2 * D_ATTN), lambda b, i: (0, 0)),
            tab, tab,
            pl.BlockSpec((LANES, N_HEADS * LANES), lambda b, i: (0, 0)),
        ],
        out_specs=[
            head_spec, head_spec,
            pl.BlockSpec((1, tm, D_ATTN), lambda b, i: (b, i, 0)),
            pl.BlockSpec((1, tm, D_POOL), lambda b, i: (b, i, 0)),
            pl.BlockSpec((1, nb, D_ATTN), lambda b, i: (b, 0, 0)),
        ],
        out_shape=[
            jax.ShapeDtypeStruct((B, N_HEADS, S, LANES), jnp.bfloat16),
            jax.ShapeDtypeStruct((B, N_HEADS, S, LANES), jnp.bfloat16),
            jax.ShapeDtypeStruct((B, S, D_ATTN), jnp.bfloat16),
            jax.ShapeDtypeStruct((B, S, D_POOL), jnp.float32),
            jax.ShapeDtypeStruct((B, nb, D_ATTN), jnp.float32),
        ],
        compiler_params=_params(("arbitrary", "arbitrary")),
    )(x, modl, w_in_aug, cos_t, sin_t, place)


def _attn_kernel(qa_ref, ka_ref, v_ref, o_ref):
    nb = v_ref.shape[1] // MOBA_BLOCK
    row = lax.broadcasted_iota(jnp.int32, (MOBA_BLOCK, MOBA_BLOCK), 0)
    col = lax.broadcasted_iota(jnp.int32, (MOBA_BLOCK, MOBA_BLOCK), 1)
    causal = col <= row
    lane = lax.broadcasted_iota(jnp.int32, (MOBA_BLOCK, LANES), 1)

    for i in range(nb):
        r0 = i * MOBA_BLOCK
        outs = []
        for hh in range(2):
            q = qa_ref[0, hh, r0:r0 + MOBA_BLOCK, :]
            s_own = jnp.where(causal, _nt_dot(q, ka_ref[0, hh, r0:r0 + MOBA_BLOCK, :]), NEG_INF)
            m = jnp.max(s_own, axis=1, keepdims=True)
            if i > 0:
                s_past = _nt_dot(q, ka_ref[0, hh, 0:r0, :])
                m = jnp.maximum(m, jnp.max(s_past, axis=1, keepdims=True))
            p_own = jnp.exp(s_own - m)
            l = jnp.sum(p_own, axis=1, keepdims=True)
            acc = jnp.dot(p_own.astype(jnp.bfloat16), v_ref[0, r0:r0 + MOBA_BLOCK, :],
                          preferred_element_type=jnp.float32)
            if i > 0:
                p_past = jnp.exp(s_past - m)
                l = l + jnp.sum(p_past, axis=1, keepdims=True)
                acc = acc + jnp.dot(p_past.astype(jnp.bfloat16), v_ref[0, 0:r0, :],
                                    preferred_element_type=jnp.float32)
            outs.append(acc / l)
        o = jnp.where(lane < HEAD_DIM, outs[0], outs[1])
        o_ref[0, r0:r0 + MOBA_BLOCK, :] = o.astype(o_ref.dtype)


def _attn_call(qa, ka, v):
    B, _, S, _ = qa.shape
    n_pair = N_HEADS // 2
    pair_spec = pl.BlockSpec((1, 2, S, LANES), lambda b, hp: (b, hp, 0, 0))
    slab_spec = pl.BlockSpec((1, S, LANES), lambda b, hp: (b, 0, hp))
    return pl.pallas_call(
        _attn_kernel,
        grid=(B, n_pair),
        in_specs=[pair_spec, pair_spec, slab_spec],
        out_specs=slab_spec,
        out_shape=jax.ShapeDtypeStruct((B, S, D_ATTN), jnp.bfloat16),
        compiler_params=_params(("arbitrary", "arbitrary")),
    )(qa, ka, v)


def _pool_kernel(p_ref, w_ref, sc_ref, o_ref):
    g = pl.program_id(1)
    S = p_ref.shape[1]
    p = p_ref[0]
    t = lax.broadcasted_iota(jnp.int32, (S, LANES), 0)

    def shifted(x, k):
        return jnp.where(t >= k, pltpu.roll(x, k, 0), 0.0)

    win = p
    sums = []
    for step in range(N_POOL_GROUPS):
        win = win + shifted(win, 1 << step)
        sums.append(win)
    wsum = jnp.where(g == 0, sums[0],
                     jnp.where(g == 1, sums[1], jnp.where(g == 2, sums[2], sums[3])))
    window = jnp.left_shift(2, g)
    cnt = jnp.minimum(t + 1, window).astype(jnp.float32)
    d = (wsum / cnt - p).astype(jnp.bfloat16)
    y = jnp.dot(d, w_ref[0].astype(jnp.bfloat16), preferred_element_type=jnp.float32)
    o_ref[0] = (y * sc_ref[...]).astype(o_ref.dtype)


def _pool_call(p, w_pool_l, pool_scale_l):
    B, S, _ = p.shape
    slab = pl.BlockSpec((1, S, LANES), lambda b, g: (b, 0, g))
    return pl.pallas_call(
        _pool_kernel,
        grid=(B, N_POOL_GROUPS),
        in_specs=[
            slab,
            pl.BlockSpec((1, POOL_GROUP, POOL_GROUP), lambda b, g: (g, 0, 0)),
            pl.BlockSpec((1, LANES), lambda b, g: (0, g)),
        ],
        out_specs=slab,
        out_shape=jax.ShapeDtypeStruct((B, S, D_POOL), jnp.bfloat16),
        compiler_params=_params(("arbitrary", "arbitrary")),
    )(p, w_pool_l, pool_scale_l.reshape(1, D_POOL))


def _mixout_kernel(a_ref, m_ref, x_ref, mod_ref, w_ref, g_ref, b_ref, o_ref):
    y = jnp.dot(a_ref[0], w_ref[0:D_ATTN, :], preferred_element_type=jnp.float32)
    y = y + jnp.dot(m_ref[0], w_ref[D_ATTN:, :], preferred_element_type=jnp.float32)
    gate = mod_ref[0, 2:3, :]
    z = DEEPNORM_ALPHA * x_ref[0] + gate * y
    o_ref[0] = _layer_norm(z) * g_ref[...] + b_ref[...]


def _mixout_call(a, m, x, modl, w_out_bf, ln_g, ln_b):
    B, S, D = x.shape
    tm = 512
    vec = pl.BlockSpec((1, D), lambda b, i: (0, 0))
    return pl.pallas_call(
        _mixout_kernel,
        grid=(B, S // tm),
        in_specs=[
            pl.BlockSpec((1, tm, D_ATTN), lambda b, i: (b, i, 0)),
            pl.BlockSpec((1, tm, D_POOL), lambda b, i: (b, i, 0)),
            pl.BlockSpec((1, tm, D), lambda b, i: (b, i, 0)),
            pl.BlockSpec((1, N_MOD, D), lambda b, i: (b, 0, 0)),
            pl.BlockSpec((D, D), lambda b, i: (0, 0)),
            vec, vec,
        ],
        out_specs=pl.BlockSpec((1, tm, D), lambda b, i: (b, i, 0)),
        out_shape=jax.ShapeDtypeStruct((B, S, D), jnp.float32),
        compiler_params=_params(("arbitrary", "arbitrary")),
    )(a, m, x, modl, w_out_bf, ln_g.reshape(1, D), ln_b.reshape(1, D))


def _router_gates(logits, rbias):
    T = logits.shape[0]
    lane = lax.broadcasted_iota(jnp.int32, (T, LANES), 1)
    valid = lane < N_EXPERTS
    neg_inf = -jnp.inf
    lg = jnp.where(valid, logits, neg_inf)
    mx = jnp.max(lg, axis=1, keepdims=True)
    ex = jnp.exp(lg - mx)
    scores = ex / jnp.sum(ex, axis=1, keepdims=True)
    sel = scores + rbias
    lane_f = lane.astype(jnp.float32)
    big = jnp.float32(1 << 20)

    def top2(vals):
        m1 = jnp.max(vals, axis=1, keepdims=True)
        i1 = jnp.min(jnp.where(vals == m1, lane_f, big), axis=1, keepdims=True)
        rest = jnp.where(lane_f == i1, neg_inf, vals)
        m2 = jnp.max(rest, axis=1, keepdims=True)
        i2 = jnp.min(jnp.where(rest == m2, lane_f, big), axis=1, keepdims=True)
        return m1, i1, m2, i2

    best_score = None
    best = None
    n_groups = N_EXPERTS // EXPERTS_PER_GROUP
    for g in range(n_groups):
        in_g = (lane >= g * EXPERTS_PER_GROUP) & (lane < (g + 1) * EXPERTS_PER_GROUP)
        m1, _, m2, _ = top2(jnp.where(in_g, sel, neg_inf))
        gs = m1 + m2
        if g == 0:
            best_score, best = gs, jnp.zeros_like(gs, dtype=jnp.int32)
        else:
            better = gs > best_score
            best_score = jnp.where(better, gs, best_score)
            best = jnp.where(better, g, best)
    in_best = valid & ((lane >> 2) == best)
    masked = jnp.where(valid, jnp.where(in_best, sel, NEG_INF), neg_inf)
    _, i1, _, i2 = top2(masked)
    w1 = jnp.sum(jnp.where(lane_f == i1, scores, 0.0), axis=1, keepdims=True)
    w2 = jnp.sum(jnp.where(lane_f == i2, scores, 0.0), axis=1, keepdims=True)
    tot = w1 + w2
    return jnp.where(lane_f == i1, w1 / tot, jnp.where(lane_f == i2, w2 / tot, 0.0))


def _moe_kernel(x_ref, mod_ref, wr_ref, rb_ref, wg_ref, wu_ref, wd_ref, g_ref, b_ref,
                o_ref, h_scr, gate_scr, acc_scr):
    e = pl.program_id(1)

    @pl.when(e == 0)
    def _():
        shift = mod_ref[0, 3:4, :]
        scale = mod_ref[0, 4:5, :]
        h = _layer_norm(x_ref[0]) * (1.0 + scale) + shift
        h_scr[...] = h.astype(jnp.bfloat16)
        logits = jnp.dot(h, wr_ref[...], precision=_HI, preferred_element_type=jnp.float32)
        gate_scr[...] = _router_gates(logits, rb_ref[...])
        acc_scr[...] = jnp.zeros_like(acc_scr)

    h = h_scr[...]
    gt = jnp.dot(h, wg_ref[0], preferred_element_type=jnp.float32)
    up = jnp.dot(h, wu_ref[0], preferred_element_type=jnp.float32)
    act = (gt * (1.0 / (1.0 + jnp.exp(-gt))) * up).astype(jnp.bfloat16)
    y = jnp.dot(act, wd_ref[0], preferred_element_type=jnp.float32)
    lane = lax.broadcasted_iota(jnp.int32, gate_scr.shape, 1)
    gcol = jnp.sum(jnp.where(lane == e, gate_scr[...], 0.0), axis=1, keepdims=True)
    acc_scr[...] += gcol * y

    @pl.when(e == N_EXPERTS - 1)
    def _():
        gate = mod_ref[0, 5:6, :]
        z = DEEPNORM_ALPHA * x_ref[0] + gate * acc_scr[...]
        o_ref[0] = _layer_norm(z) * g_ref[...] + b_ref[...]


def _moe_call(x, modl, w_router_pad, rbias_pad, wg_bf, wu_bf, wd_bf, ln_g, ln_b):
    B, S, D = x.shape
    tm = 1024
    nt = S // tm
    vec = pl.BlockSpec((1, D), lambda t, e: (0, 0))
    lanes_vec = pl.BlockSpec((1, LANES), lambda t, e: (0, 0))
    xspec = pl.BlockSpec((1, tm, D), lambda t, e: (t // nt, t % nt, 0))
    return pl.pallas_call(
        _moe_kernel,
        grid=(B * nt, N_EXPERTS),
        in_specs=[
            xspec,
            pl.BlockSpec((1, N_MOD, D), lambda t, e: (t // nt, 0, 0)),
            pl.BlockSpec((D, LANES), lambda t, e: (0, 0)),
            lanes_vec,
            pl.BlockSpec((1, D, D_EXPERT), lambda t, e: (e, 0, 0)),
            pl.BlockSpec((1, D, D_EXPERT), lambda t, e: (e, 0, 0)),
            pl.BlockSpec((1, D_EXPERT, D), lambda t, e: (e, 0, 0)),
            vec, vec,
        ],
        out_specs=xspec,
        out_shape=jax.ShapeDtypeStruct((B, S, D), jnp.float32),
        scratch_shapes=[
            pltpu.VMEM((tm, D), jnp.bfloat16),
            pltpu.VMEM((tm, LANES), jnp.float32),
            pltpu.VMEM((tm, D), jnp.float32),
        ],
        compiler_params=_params(("arbitrary", "arbitrary")),
    )(x, modl, w_router_pad, rbias_pad, wg_bf, wu_bf, wd_bf,
      ln_g.reshape(1, D), ln_b.reshape(1, D))


def _rope_tables(S):
    half = ROT_DIM // 2
    inv_freq = ROPE_THETA ** (-(jnp.arange(half, dtype=jnp.float32) * 2.0 / ROT_DIM))
    ang = jnp.arange(S, dtype=jnp.float32)[:, None] * inv_freq[None, :]
    cos, sin = jnp.cos(ang), jnp.sin(ang)
    d = jnp.arange(LANES) % HEAD_DIM
    f = d % half
    rot = d[None, :] < ROT_DIM
    return jnp.where(rot, cos[:, f], 1.0), jnp.where(rot, sin[:, f], 0.0)


def _with_rotary_partner_columns(w_in_l):
    half = ROT_DIM // 2
    c = jnp.arange(2 * D_ATTN)
    d = c % HEAD_DIM
    src = jnp.where(d < half, c + half, c - half)
    sign = jnp.where(d < half, -1.0, jnp.where(d < ROT_DIM, 1.0, 0.0))
    partner = w_in_l[:, src] * sign[None, :]
    return jnp.concatenate([w_in_l, partner], axis=1)


def _bias_placement():
    src = jnp.arange(LANES)
    hd, u, j = src >> 4, (src >> 3) & 1, src & 7
    col0 = jnp.where(hd % 2 == 0, HEAD_DIM, 0)
    dst = LANES * hd + col0 + j
    onehot = (jnp.arange(N_HEADS * LANES)[None, :] == dst[:, None]) & (u[:, None] == 1)
    return onehot.astype(jnp.bfloat16)


def kernel(x, c, w_mod, b_mod, w_in, w_pool, pool_scale, w_out, ln1_g, ln1_b,
           w_router, router_bias, w_gate, w_up, w_down, ln2_g, ln2_b):
    B, S, D = x.shape
    bf = jnp.bfloat16
    mod = _mod_call(c, w_mod, b_mod).reshape(DEPTH, B, N_MOD, D)
    cos_t, sin_t = _rope_tables(S)
    place = _bias_placement()
    w_router_pad = jnp.pad(w_router, ((0, 0), (0, LANES - N_EXPERTS)))
    rbias_pad = jnp.pad(router_bias, (0, LANES - N_EXPERTS)).reshape(1, LANES)
    for l in range(DEPTH):
        modl = mod[l]
        w_in_aug = _with_rotary_partner_columns(w_in[l]).astype(bf)
        qa, ka, v, p, _ = _proj_call(x, modl, w_in_aug, cos_t, sin_t, place)
        a = _attn_call(qa, ka, v)
        m = _pool_call(p, w_pool[l], pool_scale[l])
        x = _mixout_call(a, m, x, modl, w_out[l].astype(bf), ln1_g[l], ln1_b[l])
        x = _moe_call(x, modl, w_router_pad, rbias_pad, w_gate[l].astype(bf),
                      w_up[l].astype(bf), w_down[l].astype(bf), ln2_g[l], ln2_b[l])
    return x
```

```python
import jax
import jax.numpy as jnp
from jax import lax
from jax.experimental import pallas as pl
from jax.experimental.pallas import tpu as pltpu

D_MODEL = 1024
DEPTH = 2
D_ATTN = 512
D_POOL = 512
N_HEADS = 8
HEAD_DIM = 64
ROT_DIM = 16
ROPE_THETA = 500000.0
MOBA_BLOCK = 256
MOBA_TOPK = 3
POOL_GROUP = 128
N_POOL_GROUPS = 4
D_IN = 3 * D_ATTN + D_POOL
N_EXPERTS = 16
EXPERTS_PER_GROUP = 4
D_EXPERT = 512
DEEPNORM_ALPHA = (2 * DEPTH) ** 0.25
N_MOD = 6
LN_EPS = 1e-5
NEG_INF = -1e30

LANES = 128
VMEM_LIMIT = 56 * 1024 * 1024

_HI = lax.Precision.HIGHEST
_NT = (((1,), (1,)), ((), ()))


def _params(sem):
    return pltpu.CompilerParams(dimension_semantics=sem, vmem_limit_bytes=VMEM_LIMIT)


def _nt_dot(a, b):
    return lax.dot_general(a, b, _NT, preferred_element_type=jnp.float32)


def _mod_kernel(c_ref, w_ref, b_ref, o_ref):
    c = c_ref[...]
    cond = c * (1.0 / (1.0 + jnp.exp(-c)))
    o_ref[0] = jnp.dot(cond, w_ref[0], precision=_HI,
                       preferred_element_type=jnp.float32) + b_ref[0]


def _mod_call(c, w_mod, b_mod):
    B = c.shape[0]
    return pl.pallas_call(
        _mod_kernel,
        grid=(DEPTH, N_MOD),
        in_specs=[
            pl.BlockSpec((B, D_MODEL), lambda l, j: (0, 0)),
            pl.BlockSpec((1, D_MODEL, D_MODEL), lambda l, j: (l, 0, j)),
            pl.BlockSpec((1, 1, D_MODEL), lambda l, j: (l, 0, j)),
        ],
        out_specs=pl.BlockSpec((1, B, D_MODEL), lambda l, j: (l, 0, j)),
        out_shape=jax.ShapeDtypeStruct((DEPTH, B, N_MOD * D_MODEL), jnp.float32),
        compiler_params=_params(("arbitrary", "arbitrary")),
    )(c, w_mod, b_mod.reshape(DEPTH, 1, N_MOD * D_MODEL))


def _layer_norm(x):
    mu = jnp.mean(x, axis=-1, keepdims=True)
    xc = x - mu
    var = jnp.mean(xc * xc, axis=-1, keepdims=True)
    return xc * lax.rsqrt(var + LN_EPS)


def _split_bf16(t):
    hi = t.astype(jnp.bfloat16)
    lo = (t - hi.astype(jnp.float32)).astype(jnp.bfloat16)
    return hi, lo


def _proj_kernel(x_ref, mod_ref, w_ref, cos_ref, sin_ref, place_ref,
                 qa_ref, ka_ref, v_ref, p_ref, kbar_ref):
    i = pl.program_id(1)

    @pl.when(i == 0)
    def _():
        kbar_ref[...] = jnp.zeros_like(kbar_ref)

    x = x_ref[0]
    shift = mod_ref[0, 0:1, :]
    scale = mod_ref[0, 1:2, :]
    h = (_layer_norm(x) * (1.0 + scale) + shift).astype(jnp.bfloat16)

    cos = cos_ref[...]
    sin = sin_ref[...]

    def proj(c0, width):
        return jnp.dot(h, w_ref[:, c0:c0 + width], preferred_element_type=jnp.float32)

    def slab(t, s):
        return t[:, LANES * s:LANES * (s + 1)]

    n_slab = D_ATTN // LANES
    q = proj(0, D_ATTN)
    qp = proj(D_IN, D_ATTN)
    q_slabs = [(slab(q, s) * cos + slab(qp, s) * sin) * (HEAD_DIM ** -0.5)
               for s in range(n_slab)]
    k = proj(D_ATTN, D_ATTN)
    kp = proj(D_IN + D_ATTN, D_ATTN)
    k_slabs = [slab(k, s) * cos + slab(kp, s) * sin for s in range(n_slab)]
    v_ref[0] = proj(2 * D_ATTN, D_ATTN).astype(jnp.bfloat16)
    p_ref[0] = proj(3 * D_ATTN, D_POOL)

    kmean = jnp.concatenate(
        [jnp.mean(ks, axis=0, keepdims=True) for ks in k_slabs], axis=1)
    kbar_ref[0, pl.ds(i, 1), :] = kmean

    kb = kbar_ref[0]
    kb_rows = jnp.concatenate([kb] * (LANES // 8), axis=0)
    r_head = lax.broadcasted_iota(jnp.int32, (LANES, D_ATTN), 1) >> 6
    c_head = lax.broadcasted_iota(jnp.int32, (LANES, D_ATTN), 0) >> 4
    kbd_hi, kbd_lo = _split_bf16(jnp.where(r_head == c_head, kb_rows, 0.0))
    q_hi, q_lo = _split_bf16(jnp.concatenate(q_slabs, axis=1))
    gate = _nt_dot(q_hi, kbd_hi) + (_nt_dot(q_lo, kbd_hi) + _nt_dot(q_hi, kbd_lo))

    lane = lax.broadcasted_iota(jnp.int32, (MOBA_BLOCK, LANES), 1)
    j_of = lane & 7
    past = j_of < i
    gm = jnp.where(past, gate, NEG_INF)
    rank = jnp.zeros((MOBA_BLOCK, LANES), jnp.int32)
    for r in range(1, 8):
        other = pltpu.roll(gm, r, 1)
        beats = (other > gm) | ((other == gm) & (j_of >= r))
        rank = rank + beats.astype(jnp.int32)
    allowed = (past & (rank < MOBA_TOPK)) | (j_of == i)
    bias = jnp.where(allowed, 0.0, NEG_INF).astype(jnp.bfloat16)
    bias_cols = jnp.dot(bias, place_ref[...], preferred_element_type=jnp.float32)

    for hd in range(N_HEADS):
        own = (lane < HEAD_DIM) if hd % 2 == 0 else (lane >= HEAD_DIM)
        col0 = HEAD_DIM if hd % 2 == 0 else 0
        qa = jnp.where(own, q_slabs[hd // 2], slab(bias_cols, hd))
        ka = jnp.where(own, k_slabs[hd // 2], jnp.where(lane == col0 + i, 1.0, 0.0))
        qa_ref[0, hd] = qa.astype(jnp.bfloat16)
        ka_ref[0, hd] = ka.astype(jnp.bfloat16)


def _proj_call(x, modl, w_in_aug, cos_t, sin_t, place):
    B, S, D = x.shape
    nb = S // MOBA_BLOCK
    tm = MOBA_BLOCK
    tab = pl.BlockSpec((tm, LANES), lambda b, i: (i, 0))
    head_spec = pl.BlockSpec((1, N_HEADS, tm, LANES), lambda b, i: (b, 0, i, 0))
    return pl.pallas_call(
        _proj_kernel,
        grid=(B, nb),
        in_specs=[
            pl.BlockSpec((1, tm, D), lambda b, i: (b, i, 0)),
            pl.BlockSpec((1, N_MOD, D), lambda b, i: (b, 0, 0)),
            pl.BlockSpec((D, D_IN + 2 * D_ATTN), lambda b, i: (0, 0)),
            tab, tab,
            pl.BlockSpec((LANES, N_HEADS * LANES), lambda b, i: (0, 0)),
        ],
        out_specs=[
            head_spec, head_spec,
            pl.BlockSpec((1, tm, D_ATTN), lambda b, i: (b, i, 0)),
            pl.BlockSpec((1, tm, D_POOL), lambda b, i: (b, i, 0)),
            pl.BlockSpec((1, nb, D_ATTN), lambda b, i: (b, 0, 0)),
        ],
        out_shape=[
            jax.ShapeDtypeStruct((B, N_HEADS, S, LANES), jnp.bfloat16),
            jax.ShapeDtypeStruct((B, N_HEADS, S, LANES), jnp.bfloat16),
            jax.ShapeDtypeStruct((B, S, D_ATTN), jnp.bfloat16),
            jax.ShapeDtypeStruct((B, S, D_POOL), jnp.float32),
            jax.ShapeDtypeStruct((B, nb, D_ATTN), jnp.float32),
        ],
        compiler_params=_params(("arbitrary", "arbitrary")),
    )(x, modl, w_in_aug, cos_t, sin_t, place)


def _attn_kernel(qa_ref, ka_ref, v_ref, o_ref):
    nb = v_ref.shape[1] // MOBA_BLOCK
    row = lax.broadcasted_iota(jnp.int32, (MOBA_BLOCK, MOBA_BLOCK), 0)
    col = lax.broadcasted_iota(jnp.int32, (MOBA_BLOCK, MOBA_BLOCK), 1)
    causal = col <= row
    lane = lax.broadcasted_iota(jnp.int32, (MOBA_BLOCK, LANES), 1)

    for i in range(nb):
        r0 = i * MOBA_BLOCK
        outs = []
        for hh in range(2):
            q = qa_ref[0, hh, r0:r0 + MOBA_BLOCK, :]
            s_own = jnp.where(causal, _nt_dot(q, ka_ref[0, hh, r0:r0 + MOBA_BLOCK, :]), NEG_INF)
            m = jnp.max(s_own, axis=1, keepdims=True)
            if i > 0:
                s_past = _nt_dot(q, ka_ref[0, hh, 0:r0, :])
                m = jnp.maximum(m, jnp.max(s_past, axis=1, keepdims=True))
            p_own = jnp.exp(s_own - m)
            l = jnp.sum(p_own, axis=1, keepdims=True)
            acc = jnp.dot(p_own.astype(jnp.bfloat16), v_ref[0, r0:r0 + MOBA_BLOCK, :],
                          preferred_element_type=jnp.float32)
            if i > 0:
                p_past = jnp.exp(s_past - m)
                l = l + jnp.sum(p_past, axis=1, keepdims=True)
                acc = acc + jnp.dot(p_past.astype(jnp.bfloat16), v_ref[0, 0:r0, :],
                                    preferred_element_type=jnp.float32)
            outs.append(acc / l)
        o = jnp.where(lane < HEAD_DIM, outs[0], outs[1])
        o_ref[0, r0:r0 + MOBA_BLOCK, :] = o.astype(o_ref.dtype)


def _attn_call(qa, ka, v):
    B, _, S, _ = qa.shape
    n_pair = N_HEADS // 2
    pair_spec = pl.BlockSpec((1, 2, S, LANES), lambda b, hp: (b, hp, 0, 0))
    slab_spec = pl.BlockSpec((1, S, LANES), lambda b, hp: (b, 0, hp))
    return pl.pallas_call(
        _attn_kernel,
        grid=(B, n_pair),
        in_specs=[pair_spec, pair_spec, slab_spec],
        out_specs=slab_spec,
        out_shape=jax.ShapeDtypeStruct((B, S, D_ATTN), jnp.bfloat16),
        compiler_params=_params(("arbitrary", "arbitrary")),
    )(qa, ka, v)


def _pool_kernel(p_ref, w_ref, sc_ref, o_ref):
    g = pl.program_id(1)
    S = p_ref.shape[1]
    p = p_ref[0]
    t = lax.broadcasted_iota(jnp.int32, (S, LANES), 0)

    def shifted(x, k):
        return jnp.where(t >= k, pltpu.roll(x, k, 0), 0.0)

    win = p
    sums = []
    for step in range(N_POOL_GROUPS):
        win = win + shifted(win, 1 << step)
        sums.append(win)
    wsum = jnp.where(g == 0, sums[0],
                     jnp.where(g == 1, sums[1], jnp.where(g == 2, sums[2], sums[3])))
    window = jnp.left_shift(2, g)
    cnt = jnp.minimum(t + 1, window).astype(jnp.float32)
    d = (wsum / cnt - p).astype(jnp.bfloat16)
    y = jnp.dot(d, w_ref[0].astype(jnp.bfloat16), preferred_element_type=jnp.float32)
    o_ref[0] = (y * sc_ref[...]).astype(o_ref.dtype)


def _pool_call(p, w_pool_l, pool_scale_l):
    B, S, _ = p.shape
    slab = pl.BlockSpec((1, S, LANES), lambda b, g: (b, 0, g))
    return pl.pallas_call(
        _pool_kernel,
        grid=(B, N_POOL_GROUPS),
        in_specs=[
            slab,
            pl.BlockSpec((1, POOL_GROUP, POOL_GROUP), lambda b, g: (g, 0, 0)),
            pl.BlockSpec((1, LANES), lambda b, g: (0, g)),
        ],
        out_specs=slab,
        out_shape=jax.ShapeDtypeStruct((B, S, D_POOL), jnp.bfloat16),
        compiler_params=_params(("arbitrary", "arbitrary")),
    )(p, w_pool_l, pool_scale_l.reshape(1, D_POOL))


def _mixout_kernel(a_ref, m_ref, x_ref, mod_ref, w_ref, g_ref, b_ref, o_ref):
    y = jnp.dot(a_ref[0], w_ref[0:D_ATTN, :], preferred_element_type=jnp.float32)
    y = y + jnp.dot(m_ref[0], w_ref[D_ATTN:, :], preferred_element_type=jnp.float32)
    gate = mod_ref[0, 2:3, :]
    z = DEEPNORM_ALPHA * x_ref[0] + gate * y
    o_ref[0] = _layer_norm(z) * g_ref[...] + b_ref[...]


def _mixout_call(a, m, x, modl, w_out_bf, ln_g, ln_b):
    B, S, D = x.shape
    tm = 512
    vec = pl.BlockSpec((1, D), lambda b, i: (0, 0))
    return pl.pallas_call(
        _mixout_kernel,
        grid=(B, S // tm),
        in_specs=[
            pl.BlockSpec((1, tm, D_ATTN), lambda b, i: (b, i, 0)),
            pl.BlockSpec((1, tm, D_POOL), lambda b, i: (b, i, 0)),
            pl.BlockSpec((1, tm, D), lambda b, i: (b, i, 0)),
            pl.BlockSpec((1, N_MOD, D), lambda b, i: (b, 0, 0)),
            pl.BlockSpec((D, D), lambda b, i: (0, 0)),
            vec, vec,
        ],
        out_specs=pl.BlockSpec((1, tm, D), lambda b, i: (b, i, 0)),
        out_shape=jax.ShapeDtypeStruct((B, S, D), jnp.float32),
        compiler_params=_params(("arbitrary", "arbitrary")),
    )(a, m, x, modl, w_out_bf, ln_g.reshape(1, D), ln_b.reshape(1, D))


MOE_TILE = 1024
MOE_CHUNK = 192
MOE_ROWS = 2304
MOE_ROWS_ALLOC = MOE_ROWS + MOE_CHUNK
MOE_G_COLS = 768


def _router_top2(logits, rbias):
    T = logits.shape[0]
    lane = lax.broadcasted_iota(jnp.int32, (T, LANES), 1)
    valid = lane < N_EXPERTS
    neg_inf = -jnp.inf
    lg = jnp.where(valid, logits, neg_inf)
    mx = jnp.max(lg, axis=1, keepdims=True)
    ex = jnp.exp(lg - mx)
    scores = ex / jnp.sum(ex, axis=1, keepdims=True)
    sel = scores + rbias
    lane_f = lane.astype(jnp.float32)
    big = jnp.float32(1 << 20)

    def top2(vals):
        m1 = jnp.max(vals, axis=1, keepdims=True)
        i1 = jnp.min(jnp.where(vals == m1, lane_f, big), axis=1, keepdims=True)
        rest = jnp.where(lane_f == i1, neg_inf, vals)
        m2 = jnp.max(rest, axis=1, keepdims=True)
        i2 = jnp.min(jnp.where(rest == m2, lane_f, big), axis=1, keepdims=True)
        return m1, i1, m2, i2

    best_score = None
    best = None
    n_groups = N_EXPERTS // EXPERTS_PER_GROUP
    for g in range(n_groups):
        in_g = (lane >= g * EXPERTS_PER_GROUP) & (lane < (g + 1) * EXPERTS_PER_GROUP)
        m1, _, m2, _ = top2(jnp.where(in_g, sel, neg_inf))
        gs = m1 + m2
        if g == 0:
            best_score, best = gs, jnp.zeros_like(gs, dtype=jnp.int32)
        else:
            better = gs > best_score
            best_score = jnp.where(better, gs, best_score)
            best = jnp.where(better, g, best)
    in_best = valid & ((lane >> 2) == best)
    masked = jnp.where(valid, jnp.where(in_best, sel, NEG_INF), neg_inf)
    _, i1, _, i2 = top2(masked)
    w1 = jnp.sum(jnp.where(lane_f == i1, scores, 0.0), axis=1, keepdims=True)
    w2 = jnp.sum(jnp.where(lane_f == i2, scores, 0.0), axis=1, keepdims=True)
    tot = w1 + w2
    return i1, i2, w1 / tot, w2 / tot


def _split3_bf16(w):
    hi = w.astype(jnp.bfloat16).astype(jnp.float32)
    r1 = w - hi
    mid = r1.astype(jnp.bfloat16).astype(jnp.float32)
    lo = (r1 - mid).astype(jnp.bfloat16).astype(jnp.float32)
    return hi, mid, lo


def _moe_route(x_ref, mod_ref, wr_ref, rb_ref, h_scr, col_scr, row_scr, ysh_scr, ysl_scr,
               start_smem, cnt_smem):
    T = MOE_TILE
    shift = mod_ref[0, 3:4, :]
    scale = mod_ref[0, 4:5, :]
    h = _layer_norm(x_ref[0]) * (1.0 + scale) + shift
    h_scr[...] = h.astype(jnp.bfloat16)
    logits = jnp.dot(h, wr_ref[...], precision=_HI, preferred_element_type=jnp.float32)
    i1, i2, w1, w2 = _router_top2(logits, rb_ref[...])

    lane = lax.broadcasted_iota(jnp.int32, (T, LANES), 1)
    lane_f = lane.astype(jnp.float32)
    sel1 = lane_f == i1
    sel2 = lane_f == i2
    onehot = jnp.where(sel1 | sel2, 1.0, 0.0)
    tr = lax.broadcasted_iota(jnp.int32, (T, T), 0)
    tc = lax.broadcasted_iota(jnp.int32, (T, T), 1)
    before = jnp.where(tc < tr, 1.0, 0.0).astype(jnp.bfloat16)
    excl = jnp.dot(before, onehot.astype(jnp.bfloat16),
                   preferred_element_type=jnp.float32)
    cnt = jnp.sum(onehot, axis=0, keepdims=True)
    units = jnp.floor((cnt + 15.0) * (1.0 / 16.0))
    er = lax.broadcasted_iota(jnp.int32, (LANES, LANES), 0)
    ec = lax.broadcasted_iota(jnp.int32, (LANES, LANES), 1)
    earlier = jnp.where(er < ec, 1.0, 0.0).astype(jnp.bfloat16)
    start = 16.0 * jnp.dot(jnp.broadcast_to(units, (8, LANES)).astype(jnp.bfloat16), earlier,
                           preferred_element_type=jnp.float32)[0:1]
    slot = start + excl
    pos1 = jnp.sum(jnp.where(sel1, slot, 0.0), axis=1, keepdims=True)
    pos2 = jnp.sum(jnp.where(sel2, slot, 0.0), axis=1, keepdims=True)

    for ex in range(N_EXPERTS):
        start_smem[ex] = start[0, ex].astype(jnp.int32)
        cnt_smem[ex] = (16.0 * units[0, ex]).astype(jnp.int32)

    col_scr[...] = jnp.where(lane == 0, pos1, jnp.where(lane == 1, pos2, 0.0))

    terms = []
    for p in (pos1, pos2):
        a = jnp.floor(p * (1.0 / 64.0))
        terms += [a, p - 64.0 * a]
    for w in (w1, w2):
        terms += list(_split3_bf16(w))
    packed = jnp.zeros((T, LANES), jnp.float32)
    for k, t in enumerate(terms):
        packed = jnp.where(lane == k, t, packed)
    sr = lax.broadcasted_iota(jnp.int32, (16, LANES), 0)
    sc = lax.broadcasted_iota(jnp.int32, (16, LANES), 1)
    pick = jnp.where(sr == sc, 1.0, 0.0).astype(jnp.bfloat16)
    rows = _nt_dot(pick, packed.astype(jnp.bfloat16))
    row_scr[0:1, :] = 64.0 * rows[0:1] + rows[1:2]
    row_scr[1:2, :] = 64.0 * rows[2:3] + rows[3:4]
    row_scr[2:3, :] = rows[4:5] + rows[5:6] + rows[6:7]
    row_scr[3:4, :] = rows[7:8] + rows[8:9] + rows[9:10]

    ysh_scr[...] = jnp.zeros_like(ysh_scr)
    ysl_scr[...] = jnp.zeros_like(ysl_scr)


def _moe_kernel(x_ref, mod_ref, wr_ref, rb_ref, wg_ref, wu_ref, wd_ref, g_ref, b_ref,
                o_ref, h_scr, col_scr, row_scr, ysh_scr, ysl_scr, start_smem, cnt_smem):
    e = pl.program_id(1)
    T = MOE_TILE

    @pl.when(e == 0)
    def _():
        _moe_route(x_ref, mod_ref, wr_ref, rb_ref, h_scr, col_scr, row_scr, ysh_scr, ysl_scr,
                   start_smem, cnt_smem)

    seg_start = start_smem[e]
    n_chunk = (cnt_smem[e] + (MOE_CHUNK - 1)) // MOE_CHUNK
    pos1_row = row_scr[0:1, :]
    pos2_row = row_scr[1:2, :]
    w1_row = row_scr[2:3, :]
    w2_row = row_scr[3:4, :]

    def chunk(j, carry):
        base = pl.multiple_of(seg_start + j * MOE_CHUNK, 16)
        r = (base + lax.broadcasted_iota(jnp.int32, (MOE_CHUNK, T), 0)).astype(jnp.float32)
        m1 = r == pos1_row
        m2 = r == pos2_row
        gather = jnp.where(m1 | m2, 1.0, 0.0).astype(jnp.bfloat16)
        w_row = jnp.sum(jnp.where(m1, w1_row, 0.0) + jnp.where(m2, w2_row, 0.0),
                        axis=1, keepdims=True)
        xs = jnp.dot(gather, h_scr[...], preferred_element_type=jnp.float32).astype(jnp.bfloat16)
        gt = jnp.dot(xs, wg_ref[0], preferred_element_type=jnp.float32)
        up = jnp.dot(xs, wu_ref[0], preferred_element_type=jnp.float32)
        act = (gt * (1.0 / (1.0 + jnp.exp(-gt))) * up).astype(jnp.bfloat16)
        y = w_row * jnp.dot(act, wd_ref[0], preferred_element_type=jnp.float32)
        y_hi = y.astype(jnp.bfloat16)
        ysh_scr[pl.ds(base, MOE_CHUNK), :] = y_hi
        ysl_scr[pl.ds(base, MOE_CHUNK), :] = (y - y_hi.astype(jnp.float32)).astype(jnp.bfloat16)
        return carry

    lax.fori_loop(0, n_chunk, chunk, 0)

    @pl.when(e == N_EXPERTS - 1)
    def _():
        pos1 = col_scr[:, 0:1]
        pos2 = col_scr[:, 1:2]
        y = jnp.zeros((T, D_MODEL), jnp.float32)
        for c0 in range(0, MOE_ROWS, MOE_G_COLS):
            r = (c0 + lax.broadcasted_iota(jnp.int32, (T, MOE_G_COLS), 1)).astype(jnp.float32)
            scatter = jnp.where((r == pos1) | (r == pos2), 1.0, 0.0).astype(jnp.bfloat16)
            y = y + jnp.dot(scatter, ysh_scr[c0:c0 + MOE_G_COLS, :],
                            preferred_element_type=jnp.float32)
            y = y + jnp.dot(scatter, ysl_scr[c0:c0 + MOE_G_COLS, :],
                            preferred_element_type=jnp.float32)
        gate = mod_ref[0, 5:6, :]
        z = DEEPNORM_ALPHA * x_ref[0] + gate * y
        o_ref[0] = _layer_norm(z) * g_ref[...] + b_ref[...]


def _moe_call(x, modl, w_router_pad, rbias_pad, wg_bf, wu_bf, wd_bf, ln_g, ln_b):
    B, S, D = x.shape
    tm = MOE_TILE
    nt = S // tm
    vec = pl.BlockSpec((1, D), lambda t, e: (0, 0))
    lanes_vec = pl.BlockSpec((1, LANES), lambda t, e: (0, 0))
    xspec = pl.BlockSpec((1, tm, D), lambda t, e: (t // nt, t % nt, 0))
    return pl.pallas_call(
        _moe_kernel,
        grid=(B * nt, N_EXPERTS),
        in_specs=[
            xspec,
            pl.BlockSpec((1, N_MOD, D), lambda t, e: (t // nt, 0, 0)),
            pl.BlockSpec((D, LANES), lambda t, e: (0, 0)),
            lanes_vec,
            pl.BlockSpec((1, D, D_EXPERT), lambda t, e: (e, 0, 0)),
            pl.BlockSpec((1, D, D_EXPERT), lambda t, e: (e, 0, 0)),
            pl.BlockSpec((1, D_EXPERT, D), lambda t, e: (e, 0, 0)),
            vec, vec,
        ],
        out_specs=xspec,
        out_shape=jax.ShapeDtypeStruct((B, S, D), jnp.float32),
        scratch_shapes=[
            pltpu.VMEM((tm, D), jnp.bfloat16),
            pltpu.VMEM((tm, LANES), jnp.float32),
            pltpu.VMEM((8, tm), jnp.float32),
            pltpu.VMEM((MOE_ROWS_ALLOC, D), jnp.bfloat16),
            pltpu.VMEM((MOE_ROWS_ALLOC, D), jnp.bfloat16),
            pltpu.SMEM((N_EXPERTS,), jnp.int32),
            pltpu.SMEM((N_EXPERTS,), jnp.int32),
        ],
        compiler_params=_params(("arbitrary", "arbitrary")),
    )(x, modl, w_router_pad, rbias_pad, wg_bf, wu_bf, wd_bf,
      ln_g.reshape(1, D), ln_b.reshape(1, D))


def _rope_tables(S):
    half = ROT_DIM // 2
    inv_freq = ROPE_THETA ** (-(jnp.arange(half, dtype=jnp.float32) * 2.0 / ROT_DIM))
    ang = jnp.arange(S, dtype=jnp.float32)[:, None] * inv_freq[None, :]
    cos, sin = jnp.cos(ang), jnp.sin(ang)
    d = jnp.arange(LANES) % HEAD_DIM
    f = d % half
    rot = d[None, :] < ROT_DIM
    return jnp.where(rot, cos[:, f], 1.0), jnp.where(rot, sin[:, f], 0.0)


def _with_rotary_partner_columns(w_in_l):
    half = ROT_DIM // 2
    c = jnp.arange(2 * D_ATTN)
    d = c % HEAD_DIM
    src = jnp.where(d < half, c + half, c - half)
    sign = jnp.where(d < half, -1.0, jnp.where(d < ROT_DIM, 1.0, 0.0))
    partner = w_in_l[:, src] * sign[None, :]
    return jnp.concatenate([w_in_l, partner], axis=1)


def _bias_placement():
    src = jnp.arange(LANES)
    hd, u, j = src >> 4, (src >> 3) & 1, src & 7
    col0 = jnp.where(hd % 2 == 0, HEAD_DIM, 0)
    dst = LANES * hd + col0 + j
    onehot = (jnp.arange(N_HEADS * LANES)[None, :] == dst[:, None]) & (u[:, None] == 1)
    return onehot.astype(jnp.bfloat16)


def kernel(x, c, w_mod, b_mod, w_in, w_pool, pool_scale, w_out, ln1_g, ln1_b,
           w_router, router_bias, w_gate, w_up, w_down, ln2_g, ln2_b):
    B, S, D = x.shape
    bf = jnp.bfloat16
    mod = _mod_call(c, w_mod, b_mod).reshape(DEPTH, B, N_MOD, D)
    cos_t, sin_t = _rope_tables(S)
    place = _bias_placement()
    w_router_pad = jnp.pad(w_router, ((0, 0), (0, LANES - N_EXPERTS)))
    rbias_pad = jnp.pad(router_bias, (0, LANES - N_EXPERTS)).reshape(1, LANES)
    for l in range(DEPTH):
        modl = mod[l]
        w_in_aug = _with_rotary_partner_columns(w_in[l]).astype(bf)
        qa, ka, v, p, _ = _proj_call(x, modl, w_in_aug, cos_t, sin_t, place)
        a = _attn_call(qa, ka, v)
        m = _pool_call(p, w_pool[l], pool_scale[l])
        x = _mixout_call(a, m, x, modl, w_out[l].astype(bf), ln1_g[l], ln1_b[l])
        x = _moe_call(x, modl, w_router_pad, rbias_pad, w_gate[l].astype(bf),
                      w_up[l].astype(bf), w_down[l].astype(bf), ln2_g[l], ln2_b[l])
    return x
```

```python
import jax
import jax.numpy as jnp
from jax import lax
from jax.experimental import pallas as pl
from jax.experimental.pallas import tpu as pltpu

D_MODEL = 1024
DEPTH = 2
D_ATTN = 512
D_POOL = 512
N_HEADS = 8
HEAD_DIM = 64
ROT_DIM = 16
ROPE_THETA = 500000.0
MOBA_BLOCK = 256
MOBA_TOPK = 3
POOL_GROUP = 128
N_POOL_GROUPS = 4
D_IN = 3 * D_ATTN + D_POOL
N_EXPERTS = 16
EXPERTS_PER_GROUP = 4
D_EXPERT = 512
DEEPNORM_ALPHA = (2 * DEPTH) ** 0.25
N_MOD = 6
LN_EPS = 1e-5
NEG_INF = -1e30

LANES = 128
VMEM_LIMIT = 56 * 1024 * 1024

_HI = lax.Precision.HIGHEST
_NT = (((1,), (1,)), ((), ()))


def _params(sem):
    return pltpu.CompilerParams(dimension_semantics=sem, vmem_limit_bytes=VMEM_LIMIT)


def _nt_dot(a, b):
    return lax.dot_general(a, b, _NT, preferred_element_type=jnp.float32)


def _mod_kernel(c_ref, w_ref, b_ref, o_ref):
    c = c_ref[...]
    cond = c * (1.0 / (1.0 + jnp.exp(-c)))
    o_ref[0] = jnp.dot(cond, w_ref[0], precision=_HI,
                       preferred_element_type=jnp.float32) + b_ref[0]


def _mod_call(c, w_mod, b_mod):
    B = c.shape[0]
    return pl.pallas_call(
        _mod_kernel,
        grid=(DEPTH, N_MOD),
        in_specs=[
            pl.BlockSpec((B, D_MODEL), lambda l, j: (0, 0)),
            pl.BlockSpec((1, D_MODEL, D_MODEL), lambda l, j: (l, 0, j)),
            pl.BlockSpec((1, 1, D_MODEL), lambda l, j: (l, 0, j)),
        ],
        out_specs=pl.BlockSpec((1, B, D_MODEL), lambda l, j: (l, 0, j)),
        out_shape=jax.ShapeDtypeStruct((DEPTH, B, N_MOD * D_MODEL), jnp.float32),
        compiler_params=_params(("arbitrary", "arbitrary")),
    )(c, w_mod, b_mod.reshape(DEPTH, 1, N_MOD * D_MODEL))


def _layer_norm(x):
    mu = jnp.mean(x, axis=-1, keepdims=True)
    xc = x - mu
    var = jnp.mean(xc * xc, axis=-1, keepdims=True)
    return xc * lax.rsqrt(var + LN_EPS)


def _split_bf16(t):
    hi = t.astype(jnp.bfloat16)
    lo = (t - hi.astype(jnp.float32)).astype(jnp.bfloat16)
    return hi, lo


def _proj_kernel(x_ref, mod_ref, w_ref, cos_ref, sin_ref, place_ref,
                 qa_ref, ka_ref, v_ref, p_ref, kbar_ref):
    i = pl.program_id(1)

    @pl.when(i == 0)
    def _():
        kbar_ref[...] = jnp.zeros_like(kbar_ref)

    x = x_ref[0]
    shift = mod_ref[0, 0:1, :]
    scale = mod_ref[0, 1:2, :]
    h = (_layer_norm(x) * (1.0 + scale) + shift).astype(jnp.bfloat16)

    cos = cos_ref[...]
    sin = sin_ref[...]

    def proj(c0, width):
        return jnp.dot(h, w_ref[:, c0:c0 + width], preferred_element_type=jnp.float32)

    def slab(t, s):
        return t[:, LANES * s:LANES * (s + 1)]

    n_slab = D_ATTN // LANES
    q = proj(0, D_ATTN)
    qp = proj(D_IN, D_ATTN)
    q_slabs = [(slab(q, s) * cos + slab(qp, s) * sin) * (HEAD_DIM ** -0.5)
               for s in range(n_slab)]
    k = proj(D_ATTN, D_ATTN)
    kp = proj(D_IN + D_ATTN, D_ATTN)
    k_slabs = [slab(k, s) * cos + slab(kp, s) * sin for s in range(n_slab)]
    v_ref[0] = proj(2 * D_ATTN, D_ATTN).astype(jnp.bfloat16)
    p_ref[0] = proj(3 * D_ATTN, D_POOL)

    kmean = jnp.concatenate(
        [jnp.mean(ks, axis=0, keepdims=True) for ks in k_slabs], axis=1)
    kbar_ref[0, pl.ds(i, 1), :] = kmean

    kb = kbar_ref[0]
    kb_rows = jnp.concatenate([kb] * (LANES // 8), axis=0)
    r_head = lax.broadcasted_iota(jnp.int32, (LANES, D_ATTN), 1) >> 6
    c_head = lax.broadcasted_iota(jnp.int32, (LANES, D_ATTN), 0) >> 4
    kbd_hi, kbd_lo = _split_bf16(jnp.where(r_head == c_head, kb_rows, 0.0))
    q_hi, q_lo = _split_bf16(jnp.concatenate(q_slabs, axis=1))
    gate = _nt_dot(q_hi, kbd_hi) + (_nt_dot(q_lo, kbd_hi) + _nt_dot(q_hi, kbd_lo))

    lane = lax.broadcasted_iota(jnp.int32, (MOBA_BLOCK, LANES), 1)
    j_of = lane & 7
    past = j_of < i
    gm = jnp.where(past, gate, NEG_INF)
    rank = jnp.zeros((MOBA_BLOCK, LANES), jnp.int32)
    for r in range(1, 8):
        other = pltpu.roll(gm, r, 1)
        beats = (other > gm) | ((other == gm) & (j_of >= r))
        rank = rank + beats.astype(jnp.int32)
    allowed = (past & (rank < MOBA_TOPK)) | (j_of == i)
    bias = jnp.where(allowed, 0.0, NEG_INF).astype(jnp.bfloat16)
    bias_cols = jnp.dot(bias, place_ref[...], preferred_element_type=jnp.float32)

    for hd in range(N_HEADS):
        own = (lane < HEAD_DIM) if hd % 2 == 0 else (lane >= HEAD_DIM)
        col0 = HEAD_DIM if hd % 2 == 0 else 0
        qa = jnp.where(own, q_slabs[hd // 2], slab(bias_cols, hd))
        ka = jnp.where(own, k_slabs[hd // 2], jnp.where(lane == col0 + i, 1.0, 0.0))
        qa_ref[0, hd] = qa.astype(jnp.bfloat16)
        ka_ref[0, hd] = ka.astype(jnp.bfloat16)


def _proj_call(x, modl, w_in_aug, cos_t, sin_t, place):
    B, S, D = x.shape
    nb = S // MOBA_BLOCK
    tm = MOBA_BLOCK
    tab = pl.BlockSpec((tm, LANES), lambda b, i: (i, 0))
    head_spec = pl.BlockSpec((1, N_HEADS, tm, LANES), lambda b, i: (b, 0, i, 0))
    return pl.pallas_call(
        _proj_kernel,
        grid=(B, nb),
        in_specs=[
            pl.BlockSpec((1, tm, D), lambda b, i: (b, i, 0)),
            pl.BlockSpec((1, N_MOD, D), lambda b, i: (b, 0, 0)),
            pl.BlockSpec((D, D_IN + 2 * D_ATTN), lambda b, i: (0, 0)),
            tab, tab,
            pl.BlockSpec((LANES, N_HEADS * LANES), lambda b, i: (0, 0)),
        ],
        out_specs=[
            head_spec, head_spec,
            pl.BlockSpec((1, tm, D_ATTN), lambda b, i: (b, i, 0)),
            pl.BlockSpec((1, tm, D_POOL), lambda b, i: (b, i, 0)),
            pl.BlockSpec((1, nb, D_ATTN), lambda b, i: (b, 0, 0)),
        ],
        out_shape=[
            jax.ShapeDtypeStruct((B, N_HEADS, S, LANES), jnp.bfloat16),
            jax.ShapeDtypeStruct((B, N_HEADS, S, LANES), jnp.bfloat16),
            jax.ShapeDtypeStruct((B, S, D_ATTN), jnp.bfloat16),
            jax.ShapeDtypeStruct((B, S, D_POOL), jnp.float32),
            jax.ShapeDtypeStruct((B, nb, D_ATTN), jnp.float32),
        ],
        compiler_params=_params(("arbitrary", "arbitrary")),
    )(x, modl, w_in_aug, cos_t, sin_t, place)


def _attn_kernel(qa_ref, ka_ref, v_ref, o_ref):
    nb = v_ref.shape[1] // MOBA_BLOCK
    row = lax.broadcasted_iota(jnp.int32, (MOBA_BLOCK, MOBA_BLOCK), 0)
    col = lax.broadcasted_iota(jnp.int32, (MOBA_BLOCK, MOBA_BLOCK), 1)
    causal = col <= row
    lane = lax.broadcasted_iota(jnp.int32, (MOBA_BLOCK, LANES), 1)

    for i in range(nb):
        r0 = i * MOBA_BLOCK
        outs = []
        for hh in range(2):
            q = qa_ref[0, hh, r0:r0 + MOBA_BLOCK, :]
            s_own = jnp.where(causal, _nt_dot(q, ka_ref[0, hh, r0:r0 + MOBA_BLOCK, :]), NEG_INF)
            m = jnp.max(s_own, axis=1, keepdims=True)
            if i > 0:
                s_past = _nt_dot(q, ka_ref[0, hh, 0:r0, :])
                m = jnp.maximum(m, jnp.max(s_past, axis=1, keepdims=True))
            p_own = jnp.exp(s_own - m)
            l = jnp.sum(p_own, axis=1, keepdims=True)
            acc = jnp.dot(p_own.astype(jnp.bfloat16), v_ref[0, r0:r0 + MOBA_BLOCK, :],
                          preferred_element_type=jnp.float32)
            if i > 0:
                p_past = jnp.exp(s_past - m)
                l = l + jnp.sum(p_past, axis=1, keepdims=True)
                acc = acc + jnp.dot(p_past.astype(jnp.bfloat16), v_ref[0, 0:r0, :],
                                    preferred_element_type=jnp.float32)
            outs.append(acc / l)
        o = jnp.where(lane < HEAD_DIM, outs[0], outs[1])
        o_ref[0, r0:r0 + MOBA_BLOCK, :] = o.astype(o_ref.dtype)


def _attn_call(qa, ka, v):
    B, _, S, _ = qa.shape
    n_pair = N_HEADS // 2
    pair_spec = pl.BlockSpec((1, 2, S, LANES), lambda b, hp: (b, hp, 0, 0))
    slab_spec = pl.BlockSpec((1, S, LANES), lambda b, hp: (b, 0, hp))
    return pl.pallas_call(
        _attn_kernel,
        grid=(B, n_pair),
        in_specs=[pair_spec, pair_spec, slab_spec],
        out_specs=slab_spec,
        out_shape=jax.ShapeDtypeStruct((B, S, D_ATTN), jnp.bfloat16),
        compiler_params=_params(("arbitrary", "arbitrary")),
    )(qa, ka, v)


def _pool_kernel(p_ref, w_ref, sc_ref, o_ref):
    g = pl.program_id(1)
    S = p_ref.shape[1]
    p = p_ref[0]
    t = lax.broadcasted_iota(jnp.int32, (S, LANES), 0)

    def shifted(x, k):
        return jnp.where(t >= k, pltpu.roll(x, k, 0), 0.0)

    win = p
    sums = []
    for step in range(N_POOL_GROUPS):
        win = win + shifted(win, 1 << step)
        sums.append(win)
    wsum = jnp.where(g == 0, sums[0],
                     jnp.where(g == 1, sums[1], jnp.where(g == 2, sums[2], sums[3])))
    window = jnp.left_shift(2, g)
    cnt = jnp.minimum(t + 1, window).astype(jnp.float32)
    d = (wsum / cnt - p).astype(jnp.bfloat16)
    y = jnp.dot(d, w_ref[0].astype(jnp.bfloat16), preferred_element_type=jnp.float32)
    o_ref[0] = (y * sc_ref[...]).astype(o_ref.dtype)


def _pool_call(p, w_pool_l, pool_scale_l):
    B, S, _ = p.shape
    slab = pl.BlockSpec((1, S, LANES), lambda b, g: (b, 0, g))
    return pl.pallas_call(
        _pool_kernel,
        grid=(B, N_POOL_GROUPS),
        in_specs=[
            slab,
            pl.BlockSpec((1, POOL_GROUP, POOL_GROUP), lambda b, g: (g, 0, 0)),
            pl.BlockSpec((1, LANES), lambda b, g: (0, g)),
        ],
        out_specs=slab,
        out_shape=jax.ShapeDtypeStruct((B, S, D_POOL), jnp.bfloat16),
        compiler_params=_params(("arbitrary", "arbitrary")),
    )(p, w_pool_l, pool_scale_l.reshape(1, D_POOL))


def _mixout_kernel(a_ref, m_ref, x_ref, mod_ref, w_ref, g_ref, b_ref, o_ref):
    y = jnp.dot(a_ref[0], w_ref[0:D_ATTN, :], preferred_element_type=jnp.float32)
    y = y + jnp.dot(m_ref[0], w_ref[D_ATTN:, :], preferred_element_type=jnp.float32)
    gate = mod_ref[0, 2:3, :]
    z = DEEPNORM_ALPHA * x_ref[0] + gate * y
    o_ref[0] = _layer_norm(z) * g_ref[...] + b_ref[...]


def _mixout_call(a, m, x, modl, w_out_bf, ln_g, ln_b):
    B, S, D = x.shape
    tm = 512
    vec = pl.BlockSpec((1, D), lambda b, i: (0, 0))
    return pl.pallas_call(
        _mixout_kernel,
        grid=(B, S // tm),
        in_specs=[
            pl.BlockSpec((1, tm, D_ATTN), lambda b, i: (b, i, 0)),
            pl.BlockSpec((1, tm, D_POOL), lambda b, i: (b, i, 0)),
            pl.BlockSpec((1, tm, D), lambda b, i: (b, i, 0)),
            pl.BlockSpec((1, N_MOD, D), lambda b, i: (b, 0, 0)),
            pl.BlockSpec((D, D), lambda b, i: (0, 0)),
            vec, vec,
        ],
        out_specs=pl.BlockSpec((1, tm, D), lambda b, i: (b, i, 0)),
        out_shape=jax.ShapeDtypeStruct((B, S, D), jnp.float32),
        compiler_params=_params(("arbitrary", "arbitrary")),
    )(a, m, x, modl, w_out_bf, ln_g.reshape(1, D), ln_b.reshape(1, D))


MOE_TILE = 1024
MOE_CHUNK = 160
MOE_ROWS = 2304
MOE_ROWS_ALLOC = MOE_ROWS + MOE_CHUNK
MOE_G_COLS = 768


def _top2_rows(vals):
    def first_max(rows):
        m = rows[0]
        for v in rows[1:]:
            m = jnp.maximum(m, v)
        idx = jnp.full_like(m, float(len(rows) - 1))
        for k in range(len(rows) - 2, -1, -1):
            idx = jnp.where(rows[k] == m, float(k), idx)
        return m, idx

    m1, i1 = first_max(vals)
    rest = [jnp.where(i1 == float(k), -jnp.inf, v) for k, v in enumerate(vals)]
    m2, i2 = first_max(rest)
    return m1, i1, m2, i2


def _router_rows(logits_t, rb_ref):
    lg = [logits_t[e:e + 1, :] for e in range(N_EXPERTS)]
    mx = lg[0]
    for v in lg[1:]:
        mx = jnp.maximum(mx, v)
    ex = [jnp.exp(v - mx) for v in lg]
    den = ex[0]
    for v in ex[1:]:
        den = den + v
    scores = [v / den for v in ex]
    sel = [scores[e] + rb_ref[e:e + 1, :] for e in range(N_EXPERTS)]
    best_score = None
    best = None
    for g in range(N_EXPERTS // EXPERTS_PER_GROUP):
        m1, _, m2, _ = _top2_rows(sel[g * EXPERTS_PER_GROUP:(g + 1) * EXPERTS_PER_GROUP])
        gs = m1 + m2
        if g == 0:
            best_score, best = gs, jnp.zeros_like(gs)
        else:
            better = gs > best_score
            best_score = jnp.where(better, gs, best_score)
            best = jnp.where(better, float(g), best)
    masked = [jnp.where(best == float(e // EXPERTS_PER_GROUP), sel[e], NEG_INF)
              for e in range(N_EXPERTS)]
    _, i1, _, i2 = _top2_rows(masked)
    w1 = jnp.zeros_like(i1)
    w2 = jnp.zeros_like(i2)
    for e in range(N_EXPERTS):
        w1 = jnp.where(i1 == float(e), scores[e], w1)
        w2 = jnp.where(i2 == float(e), scores[e], w2)
    tot = w1 + w2
    return i1, i2, w1 / tot, w2 / tot


def _moe_route(x_ref, mod_ref, wr_ref, rb_ref, h_scr, col_scr, row_scr, ysh_scr, ysl_scr,
               start_smem, cnt_smem):
    T = MOE_TILE
    shift = mod_ref[0, 3:4, :]
    scale = mod_ref[0, 4:5, :]
    h = _layer_norm(x_ref[0]) * (1.0 + scale) + shift
    h_hi = h.astype(jnp.bfloat16)
    h_scr[...] = h_hi
    h_lo = (h - h_hi.astype(jnp.float32)).astype(jnp.bfloat16)
    wr_hi, wr_lo = _split_bf16(wr_ref[...])
    logits_t = _nt_dot(wr_hi, h_hi) + (_nt_dot(wr_lo, h_hi) + _nt_dot(wr_hi, h_lo))
    i1, i2, w1, w2 = _router_rows(logits_t, rb_ref)

    e_iota = lax.broadcasted_iota(jnp.int32, (N_EXPERTS, T), 0).astype(jnp.float32)
    sel1 = e_iota == i1
    sel2 = e_iota == i2
    onehot = jnp.where(sel1 | sel2, 1.0, 0.0)
    tr = lax.broadcasted_iota(jnp.int32, (T, T), 0)
    tc = lax.broadcasted_iota(jnp.int32, (T, T), 1)
    earlier = jnp.where(tr < tc, 1.0, 0.0).astype(jnp.bfloat16)
    excl = jnp.dot(onehot.astype(jnp.bfloat16), earlier,
                   preferred_element_type=jnp.float32)
    cnt = jnp.sum(onehot, axis=1, keepdims=True)

    e_col = lax.broadcasted_iota(jnp.int32, (N_EXPERTS, 1), 0)
    start_v = jnp.zeros((N_EXPERTS, 1), jnp.float32)
    run = jnp.int32(0)
    for ex in range(N_EXPERTS):
        padded = lax.shift_left(lax.shift_right_logical(cnt[ex, 0].astype(jnp.int32) + 15, 4), 4)
        start_smem[ex] = run
        cnt_smem[ex] = padded
        start_v = jnp.where(e_col == ex, run.astype(jnp.float32), start_v)
        run = run + padded

    slot = excl + start_v
    pos1 = jnp.sum(jnp.where(sel1, slot, 0.0), axis=0, keepdims=True)
    pos2 = jnp.sum(jnp.where(sel2, slot, 0.0), axis=0, keepdims=True)
    row_scr[0:1, :] = pos1
    row_scr[1:2, :] = pos2
    row_scr[2:3, :] = w1
    row_scr[3:4, :] = w2

    r128 = lax.broadcasted_iota(jnp.int32, (LANES, T), 0)
    terms = jnp.zeros((LANES, T), jnp.float32)
    k = 0
    for p in (pos1, pos2):
        a = jnp.floor(p * (1.0 / 64.0))
        for t in (a, p - 64.0 * a):
            terms = jnp.where(r128 == k, t, terms)
            k += 1
    eye = jnp.where(tr == tc, 1.0, 0.0).astype(jnp.bfloat16)
    col_scr[...] = _nt_dot(eye, terms.astype(jnp.bfloat16))

    tail = MOE_ROWS_ALLOC - 2 * T
    ysh_scr[2 * T:, :] = jnp.zeros((tail, D_MODEL), jnp.bfloat16)
    ysl_scr[2 * T:, :] = jnp.zeros((tail, D_MODEL), jnp.bfloat16)


def _moe_kernel(x_ref, mod_ref, wr_ref, rb_ref, wg_ref, wu_ref, wd_ref, g_ref, b_ref,
                o_ref, h_scr, col_scr, row_scr, ysh_scr, ysl_scr, start_smem, cnt_smem):
    e = pl.program_id(1)
    T = MOE_TILE

    @pl.when(e == 0)
    def _():
        _moe_route(x_ref, mod_ref, wr_ref, rb_ref, h_scr, col_scr, row_scr, ysh_scr, ysl_scr,
                   start_smem, cnt_smem)

    seg_start = start_smem[e]
    n_chunk = (cnt_smem[e] + (MOE_CHUNK - 1)) // MOE_CHUNK
    pos1_row = row_scr[0:1, :]
    pos2_row = row_scr[1:2, :]
    w1_row = row_scr[2:3, :]
    w2_row = row_scr[3:4, :]

    def chunk(j, carry):
        base = pl.multiple_of(seg_start + j * MOE_CHUNK, 16)
        r = (base + lax.broadcasted_iota(jnp.int32, (MOE_CHUNK, T), 0)).astype(jnp.float32)
        m1 = r == pos1_row
        m2 = r == pos2_row
        gather = jnp.where(m1 | m2, 1.0, 0.0).astype(jnp.bfloat16)
        w_row = jnp.sum(jnp.where(m1, w1_row, 0.0) + jnp.where(m2, w2_row, 0.0),
                        axis=1, keepdims=True)
        xs = jnp.dot(gather, h_scr[...], preferred_element_type=jnp.float32).astype(jnp.bfloat16)
        gt = jnp.dot(xs, wg_ref[0], preferred_element_type=jnp.float32)
        up = jnp.dot(xs, wu_ref[0], preferred_element_type=jnp.float32)
        act = (gt * (1.0 / (1.0 + jnp.exp(-gt))) * up).astype(jnp.bfloat16)
        y = w_row * jnp.dot(act, wd_ref[0], preferred_element_type=jnp.float32)
        y_hi = y.astype(jnp.bfloat16)
        ysh_scr[pl.ds(base, MOE_CHUNK), :] = y_hi
        ysl_scr[pl.ds(base, MOE_CHUNK), :] = (y - y_hi.astype(jnp.float32)).astype(jnp.bfloat16)
        return carry

    lax.fori_loop(0, n_chunk, chunk, 0)

    @pl.when(e == N_EXPERTS - 1)
    def _():
        pos1 = 64.0 * col_scr[:, 0:1] + col_scr[:, 1:2]
        pos2 = 64.0 * col_scr[:, 2:3] + col_scr[:, 3:4]
        y = jnp.zeros((T, D_MODEL), jnp.float32)
        for c0 in range(0, MOE_ROWS, MOE_G_COLS):
            r = (c0 + lax.broadcasted_iota(jnp.int32, (T, MOE_G_COLS), 1)).astype(jnp.float32)
            scatter = jnp.where((r == pos1) | (r == pos2), 1.0, 0.0).astype(jnp.bfloat16)
            y = y + jnp.dot(scatter, ysh_scr[c0:c0 + MOE_G_COLS, :],
                            preferred_element_type=jnp.float32)
            y = y + jnp.dot(scatter, ysl_scr[c0:c0 + MOE_G_COLS, :],
                            preferred_element_type=jnp.float32)
        gate = mod_ref[0, 5:6, :]
        z = DEEPNORM_ALPHA * x_ref[0] + gate * y
        o_ref[0] = _layer_norm(z) * g_ref[...] + b_ref[...]


def _moe_call(x, modl, w_router_t, rbias_col, wg_bf, wu_bf, wd_bf, ln_g, ln_b):
    B, S, D = x.shape
    tm = MOE_TILE
    nt = S // tm
    vec = pl.BlockSpec((1, D), lambda t, e: (0, 0))
    xspec = pl.BlockSpec((1, tm, D), lambda t, e: (t // nt, t % nt, 0))
    return pl.pallas_call(
        _moe_kernel,
        grid=(B * nt, N_EXPERTS),
        in_specs=[
            xspec,
            pl.BlockSpec((1, N_MOD, D), lambda t, e: (t // nt, 0, 0)),
            pl.BlockSpec((N_EXPERTS, D), lambda t, e: (0, 0)),
            pl.BlockSpec((N_EXPERTS, 1), lambda t, e: (0, 0)),
            pl.BlockSpec((1, D, D_EXPERT), lambda t, e: (e, 0, 0)),
            pl.BlockSpec((1, D, D_EXPERT), lambda t, e: (e, 0, 0)),
            pl.BlockSpec((1, D_EXPERT, D), lambda t, e: (e, 0, 0)),
            vec, vec,
        ],
        out_specs=xspec,
        out_shape=jax.ShapeDtypeStruct((B, S, D), jnp.float32),
        scratch_shapes=[
            pltpu.VMEM((tm, D), jnp.bfloat16),
            pltpu.VMEM((tm, LANES), jnp.float32),
            pltpu.VMEM((8, tm), jnp.float32),
            pltpu.VMEM((MOE_ROWS_ALLOC, D), jnp.bfloat16),
            pltpu.VMEM((MOE_ROWS_ALLOC, D), jnp.bfloat16),
            pltpu.SMEM((N_EXPERTS,), jnp.int32),
            pltpu.SMEM((N_EXPERTS,), jnp.int32),
        ],
        compiler_params=_params(("arbitrary", "arbitrary")),
    )(x, modl, w_router_t, rbias_col, wg_bf, wu_bf, wd_bf,
      ln_g.reshape(1, D), ln_b.reshape(1, D))


def _rope_tables(S):
    half = ROT_DIM // 2
    inv_freq = ROPE_THETA ** (-(jnp.arange(half, dtype=jnp.float32) * 2.0 / ROT_DIM))
    ang = jnp.arange(S, dtype=jnp.float32)[:, None] * inv_freq[None, :]
    cos, sin = jnp.cos(ang), jnp.sin(ang)
    d = jnp.arange(LANES) % HEAD_DIM
    f = d % half
    rot = d[None, :] < ROT_DIM
    return jnp.where(rot, cos[:, f], 1.0), jnp.where(rot, sin[:, f], 0.0)


def _with_rotary_partner_columns(w_in_l):
    half = ROT_DIM // 2
    c = jnp.arange(2 * D_ATTN)
    d = c % HEAD_DIM
    src = jnp.where(d < half, c + half, c - half)
    sign = jnp.where(d < half, -1.0, jnp.where(d < ROT_DIM, 1.0, 0.0))
    partner = w_in_l[:, src] * sign[None, :]
    return jnp.concatenate([w_in_l, partner], axis=1)


def _bias_placement():
    src = jnp.arange(LANES)
    hd, u, j = src >> 4, (src >> 3) & 1, src & 7
    col0 = jnp.where(hd % 2 == 0, HEAD_DIM, 0)
    dst = LANES * hd + col0 + j
    onehot = (jnp.arange(N_HEADS * LANES)[None, :] == dst[:, None]) & (u[:, None] == 1)
    return onehot.astype(jnp.bfloat16)


def kernel(x, c, w_mod, b_mod, w_in, w_pool, pool_scale, w_out, ln1_g, ln1_b,
           w_router, router_bias, w_gate, w_up, w_down, ln2_g, ln2_b):
    B, S, D = x.shape
    bf = jnp.bfloat16
    mod = _mod_call(c, w_mod, b_mod).reshape(DEPTH, B, N_MOD, D)
    cos_t, sin_t = _rope_tables(S)
    place = _bias_placement()
    w_router_t = w_router.T
    rbias_col = router_bias.reshape(N_EXPERTS, 1)
    for l in range(DEPTH):
        modl = mod[l]
        w_in_aug = _with_rotary_partner_columns(w_in[l]).astype(bf)
        qa, ka, v, p, _ = _proj_call(x, modl, w_in_aug, cos_t, sin_t, place)
        a = _attn_call(qa, ka, v)
        m = _pool_call(p, w_pool[l], pool_scale[l])
        x = _mixout_call(a, m, x, modl, w_out[l].astype(bf), ln1_g[l], ln1_b[l])
        x = _moe_call(x, modl, w_router_t, rbias_col, w_gate[l].astype(bf),
                      w_up[l].astype(bf), w_down[l].astype(bf), ln2_g[l], ln2_b[l])
    return x
```

```python
import jax
import jax.numpy as jnp
from jax import lax
from jax.experimental import pallas as pl
from jax.experimental.pallas import tpu as pltpu

D_MODEL = 1024
DEPTH = 2
D_ATTN = 512
D_POOL = 512
N_HEADS = 8
HEAD_DIM = 64
ROT_DIM = 16
ROPE_THETA = 500000.0
MOBA_BLOCK = 256
MOBA_TOPK = 3
POOL_GROUP = 128
N_POOL_GROUPS = 4
D_IN = 3 * D_ATTN + D_POOL
N_EXPERTS = 16
EXPERTS_PER_GROUP = 4
D_EXPERT = 512
DEEPNORM_ALPHA = (2 * DEPTH) ** 0.25
N_MOD = 6
LN_EPS = 1e-5
NEG_INF = -1e30

LANES = 128
VMEM_LIMIT = 56 * 1024 * 1024

_HI = lax.Precision.HIGHEST
_NT = (((1,), (1,)), ((), ()))


def _params(sem):
    return pltpu.CompilerParams(dimension_semantics=sem, vmem_limit_bytes=VMEM_LIMIT)


def _nt_dot(a, b):
    return lax.dot_general(a, b, _NT, preferred_element_type=jnp.float32)


def _mod_kernel(c_ref, w_ref, b_ref, o_ref):
    c = c_ref[...]
    cond = c * (1.0 / (1.0 + jnp.exp(-c)))
    o_ref[0] = jnp.dot(cond, w_ref[0], precision=_HI,
                       preferred_element_type=jnp.float32) + b_ref[0]


def _mod_call(c, w_mod, b_mod):
    B = c.shape[0]
    return pl.pallas_call(
        _mod_kernel,
        grid=(DEPTH, N_MOD),
        in_specs=[
            pl.BlockSpec((B, D_MODEL), lambda l, j: (0, 0)),
            pl.BlockSpec((1, D_MODEL, D_MODEL), lambda l, j: (l, 0, j)),
            pl.BlockSpec((1, 1, D_MODEL), lambda l, j: (l, 0, j)),
        ],
        out_specs=pl.BlockSpec((1, B, D_MODEL), lambda l, j: (l, 0, j)),
        out_shape=jax.ShapeDtypeStruct((DEPTH, B, N_MOD * D_MODEL), jnp.float32),
        compiler_params=_params(("arbitrary", "arbitrary")),
    )(c, w_mod, b_mod.reshape(DEPTH, 1, N_MOD * D_MODEL))


def _layer_norm(x):
    mu = jnp.mean(x, axis=-1, keepdims=True)
    xc = x - mu
    var = jnp.mean(xc * xc, axis=-1, keepdims=True)
    return xc * lax.rsqrt(var + LN_EPS)


def _split_bf16(t):
    hi = t.astype(jnp.bfloat16)
    lo = (t - hi.astype(jnp.float32)).astype(jnp.bfloat16)
    return hi, lo


def _proj_kernel(x_ref, mod_ref, w_ref, cos_ref, sin_ref, place_ref,
                 qa_ref, ka_ref, v_ref, p_ref, kbar_ref):
    i = pl.program_id(1)

    @pl.when(i == 0)
    def _():
        kbar_ref[...] = jnp.zeros_like(kbar_ref)

    x = x_ref[0]
    shift = mod_ref[0, 0:1, :]
    scale = mod_ref[0, 1:2, :]
    h = (_layer_norm(x) * (1.0 + scale) + shift).astype(jnp.bfloat16)

    cos = cos_ref[...]
    sin = sin_ref[...]

    def proj(c0, width):
        return jnp.dot(h, w_ref[:, c0:c0 + width], preferred_element_type=jnp.float32)

    def slab(t, s):
        return t[:, LANES * s:LANES * (s + 1)]

    n_slab = D_ATTN // LANES
    q = proj(0, D_ATTN)
    qp = proj(D_IN, D_ATTN)
    q_slabs = [(slab(q, s) * cos + slab(qp, s) * sin) * (HEAD_DIM ** -0.5)
               for s in range(n_slab)]
    k = proj(D_ATTN, D_ATTN)
    kp = proj(D_IN + D_ATTN, D_ATTN)
    k_slabs = [slab(k, s) * cos + slab(kp, s) * sin for s in range(n_slab)]
    v_ref[0] = proj(2 * D_ATTN, D_ATTN).astype(jnp.bfloat16)
    p_ref[0] = proj(3 * D_ATTN, D_POOL)

    kmean = jnp.concatenate(
        [jnp.mean(ks, axis=0, keepdims=True) for ks in k_slabs], axis=1)
    kbar_ref[0, pl.ds(i, 1), :] = kmean

    kb = kbar_ref[0]
    kb_rows = jnp.concatenate([kb] * (LANES // 8), axis=0)
    r_head = lax.broadcasted_iota(jnp.int32, (LANES, D_ATTN), 1) >> 6
    c_head = lax.broadcasted_iota(jnp.int32, (LANES, D_ATTN), 0) >> 4
    kbd_hi, kbd_lo = _split_bf16(jnp.where(r_head == c_head, kb_rows, 0.0))
    q_hi, q_lo = _split_bf16(jnp.concatenate(q_slabs, axis=1))
    gate = _nt_dot(q_hi, kbd_hi) + (_nt_dot(q_lo, kbd_hi) + _nt_dot(q_hi, kbd_lo))

    lane = lax.broadcasted_iota(jnp.int32, (MOBA_BLOCK, LANES), 1)
    j_of = lane & 7
    past = j_of < i
    gm = jnp.where(past, gate, NEG_INF)
    rank = jnp.zeros((MOBA_BLOCK, LANES), jnp.int32)
    for r in range(1, 8):
        other = pltpu.roll(gm, r, 1)
        beats = (other > gm) | ((other == gm) & (j_of >= r))
        rank = rank + beats.astype(jnp.int32)
    allowed = (past & (rank < MOBA_TOPK)) | (j_of == i)
    bias = jnp.where(allowed, 0.0, NEG_INF).astype(jnp.bfloat16)
    bias_cols = jnp.dot(bias, place_ref[...], preferred_element_type=jnp.float32)

    for hd in range(N_HEADS):
        own = (lane < HEAD_DIM) if hd % 2 == 0 else (lane >= HEAD_DIM)
        col0 = HEAD_DIM if hd % 2 == 0 else 0
        qa = jnp.where(own, q_slabs[hd // 2], slab(bias_cols, hd))
        ka = jnp.where(own, k_slabs[hd // 2], jnp.where(lane == col0 + i, 1.0, 0.0))
        qa_ref[0, hd] = qa.astype(jnp.bfloat16)
        ka_ref[0, hd] = ka.astype(jnp.bfloat16)


def _proj_call(x, modl, w_in_aug, cos_t, sin_t, place):
    B, S, D = x.shape
    nb = S // MOBA_BLOCK
    tm = MOBA_BLOCK
    tab = pl.BlockSpec((tm, LANES), lambda b, i: (i, 0))
    head_spec = pl.BlockSpec((1, N_HEADS, tm, LANES), lambda b, i: (b, 0, i, 0))
    return pl.pallas_call(
        _proj_kernel,
        grid=(B, nb),
        in_specs=[
            pl.BlockSpec((1, tm, D), lambda b, i: (b, i, 0)),
            pl.BlockSpec((1, N_MOD, D), lambda b, i: (b, 0, 0)),
            pl.BlockSpec((D, D_IN + 2 * D_ATTN), lambda b, i: (0, 0)),
            tab, tab,
            pl.BlockSpec((LANES, N_HEADS * LANES), lambda b, i: (0, 0)),
        ],
        out_specs=[
            head_spec, head_spec,
            pl.BlockSpec((1, tm, D_ATTN), lambda b, i: (b, i, 0)),
            pl.BlockSpec((1, tm, D_POOL), lambda b, i: (b, i, 0)),
            pl.BlockSpec((1, nb, D_ATTN), lambda b, i: (b, 0, 0)),
        ],
        out_shape=[
            jax.ShapeDtypeStruct((B, N_HEADS, S, LANES), jnp.bfloat16),
            jax.ShapeDtypeStruct((B, N_HEADS, S, LANES), jnp.bfloat16),
            jax.ShapeDtypeStruct((B, S, D_ATTN), jnp.bfloat16),
            jax.ShapeDtypeStruct((B, S, D_POOL), jnp.float32),
            jax.ShapeDtypeStruct((B, nb, D_ATTN), jnp.float32),
        ],
        compiler_params=_params(("arbitrary", "arbitrary")),
    )(x, modl, w_in_aug, cos_t, sin_t, place)


def _attn_kernel(qa_ref, ka_ref, v_ref, o_ref):
    nb = v_ref.shape[1] // MOBA_BLOCK
    row = lax.broadcasted_iota(jnp.int32, (MOBA_BLOCK, MOBA_BLOCK), 0)
    col = lax.broadcasted_iota(jnp.int32, (MOBA_BLOCK, MOBA_BLOCK), 1)
    causal = col <= row
    lane = lax.broadcasted_iota(jnp.int32, (MOBA_BLOCK, LANES), 1)

    for i in reversed(range(nb)):
        r0 = i * MOBA_BLOCK
        outs = []
        for hh in range(2):
            q = qa_ref[0, hh, r0:r0 + MOBA_BLOCK, :]
            s_own = jnp.where(causal, _nt_dot(q, ka_ref[0, hh, r0:r0 + MOBA_BLOCK, :]), NEG_INF)
            m = jnp.max(s_own, axis=1, keepdims=True)
            if i > 0:
                s_past = _nt_dot(q, ka_ref[0, hh, 0:r0, :])
                m = jnp.maximum(m, jnp.max(s_past, axis=1, keepdims=True))
            p_own = jnp.exp(s_own - m)
            l = jnp.sum(p_own, axis=1, keepdims=True)
            acc = jnp.dot(p_own.astype(jnp.bfloat16), v_ref[0, r0:r0 + MOBA_BLOCK, :],
                          preferred_element_type=jnp.float32)
            if i > 0:
                p_past = jnp.exp(s_past - m)
                l = l + jnp.sum(p_past, axis=1, keepdims=True)
                acc = acc + jnp.dot(p_past.astype(jnp.bfloat16), v_ref[0, 0:r0, :],
                                    preferred_element_type=jnp.float32)
            outs.append(acc / l)
        o = jnp.where(lane < HEAD_DIM, outs[0], outs[1])
        o_ref[0, r0:r0 + MOBA_BLOCK, :] = o.astype(o_ref.dtype)


def _attn_call(qa, ka, v):
    B, _, S, _ = qa.shape
    n_pair = N_HEADS // 2
    pair_spec = pl.BlockSpec((1, 2, S, LANES), lambda b, hp: (b, hp, 0, 0))
    slab_spec = pl.BlockSpec((1, S, LANES), lambda b, hp: (b, 0, hp))
    return pl.pallas_call(
        _attn_kernel,
        grid=(B, n_pair),
        in_specs=[pair_spec, pair_spec, slab_spec],
        out_specs=slab_spec,
        out_shape=jax.ShapeDtypeStruct((B, S, D_ATTN), jnp.bfloat16),
        compiler_params=_params(("arbitrary", "arbitrary")),
    )(qa, ka, v)


def _pool_kernel(p_ref, w_ref, sc_ref, o_ref):
    g = pl.program_id(1)
    S = p_ref.shape[1]
    p = p_ref[0]
    t = lax.broadcasted_iota(jnp.int32, (S, LANES), 0)

    def shifted(x, k):
        return jnp.where(t >= k, pltpu.roll(x, k, 0), 0.0)

    win = p
    sums = []
    for step in range(N_POOL_GROUPS):
        win = win + shifted(win, 1 << step)
        sums.append(win)
    wsum = jnp.where(g == 0, sums[0],
                     jnp.where(g == 1, sums[1], jnp.where(g == 2, sums[2], sums[3])))
    window = jnp.left_shift(2, g)
    cnt = jnp.minimum(t + 1, window).astype(jnp.float32)
    d = (wsum / cnt - p).astype(jnp.bfloat16)
    y = jnp.dot(d, w_ref[0].astype(jnp.bfloat16), preferred_element_type=jnp.float32)
    o_ref[0] = (y * sc_ref[...]).astype(o_ref.dtype)


def _pool_call(p, w_pool_l, pool_scale_l):
    B, S, _ = p.shape
    slab = pl.BlockSpec((1, S, LANES), lambda b, g: (b, 0, g))
    return pl.pallas_call(
        _pool_kernel,
        grid=(B, N_POOL_GROUPS),
        in_specs=[
            slab,
            pl.BlockSpec((1, POOL_GROUP, POOL_GROUP), lambda b, g: (g, 0, 0)),
            pl.BlockSpec((1, LANES), lambda b, g: (0, g)),
        ],
        out_specs=slab,
        out_shape=jax.ShapeDtypeStruct((B, S, D_POOL), jnp.bfloat16),
        compiler_params=_params(("arbitrary", "arbitrary")),
    )(p, w_pool_l, pool_scale_l.reshape(1, D_POOL))


def _mixout_kernel(a_ref, m_ref, x_ref, mod_ref, w_ref, g_ref, b_ref, o_ref):
    y = jnp.dot(a_ref[0], w_ref[0:D_ATTN, :], preferred_element_type=jnp.float32)
    y = y + jnp.dot(m_ref[0], w_ref[D_ATTN:, :], preferred_element_type=jnp.float32)
    gate = mod_ref[0, 2:3, :]
    z = DEEPNORM_ALPHA * x_ref[0] + gate * y
    o_ref[0] = _layer_norm(z) * g_ref[...] + b_ref[...]


def _mixout_call(a, m, x, modl, w_out_bf, ln_g, ln_b):
    B, S, D = x.shape
    tm = 512
    vec = pl.BlockSpec((1, D), lambda b, i: (0, 0))
    return pl.pallas_call(
        _mixout_kernel,
        grid=(B, S // tm),
        in_specs=[
            pl.BlockSpec((1, tm, D_ATTN), lambda b, i: (b, i, 0)),
            pl.BlockSpec((1, tm, D_POOL), lambda b, i: (b, i, 0)),
            pl.BlockSpec((1, tm, D), lambda b, i: (b, i, 0)),
            pl.BlockSpec((1, N_MOD, D), lambda b, i: (b, 0, 0)),
            pl.BlockSpec((D, D), lambda b, i: (0, 0)),
            vec, vec,
        ],
        out_specs=pl.BlockSpec((1, tm, D), lambda b, i: (b, i, 0)),
        out_shape=jax.ShapeDtypeStruct((B, S, D), jnp.float32),
        compiler_params=_params(("arbitrary", "arbitrary")),
    )(a, m, x, modl, w_out_bf, ln_g.reshape(1, D), ln_b.reshape(1, D))


MOE_TILE = 1024
MOE_CHUNK = 160
MOE_ROWS = 2304
MOE_ROWS_ALLOC = MOE_ROWS + MOE_CHUNK
MOE_G_COLS = 768
MOE_EXPERTS_PER_STEP = 2


def _top2_rows(vals):
    def first_max(rows):
        m = rows[0]
        for v in rows[1:]:
            m = jnp.maximum(m, v)
        idx = jnp.full_like(m, float(len(rows) - 1))
        for k in range(len(rows) - 2, -1, -1):
            idx = jnp.where(rows[k] == m, float(k), idx)
        return m, idx

    m1, i1 = first_max(vals)
    rest = [jnp.where(i1 == float(k), -jnp.inf, v) for k, v in enumerate(vals)]
    m2, i2 = first_max(rest)
    return m1, i1, m2, i2


def _router_rows(logits_t, rb_ref):
    lg = [logits_t[e:e + 1, :] for e in range(N_EXPERTS)]
    mx = lg[0]
    for v in lg[1:]:
        mx = jnp.maximum(mx, v)
    ex = [jnp.exp(v - mx) for v in lg]
    den = ex[0]
    for v in ex[1:]:
        den = den + v
    scores = [v / den for v in ex]
    sel = [scores[e] + rb_ref[e:e + 1, :] for e in range(N_EXPERTS)]
    best_score = None
    best = None
    for g in range(N_EXPERTS // EXPERTS_PER_GROUP):
        m1, _, m2, _ = _top2_rows(sel[g * EXPERTS_PER_GROUP:(g + 1) * EXPERTS_PER_GROUP])
        gs = m1 + m2
        if g == 0:
            best_score, best = gs, jnp.zeros_like(gs)
        else:
            better = gs > best_score
            best_score = jnp.where(better, gs, best_score)
            best = jnp.where(better, float(g), best)
    masked = [jnp.where(best == float(e // EXPERTS_PER_GROUP), sel[e], NEG_INF)
              for e in range(N_EXPERTS)]
    _, i1, _, i2 = _top2_rows(masked)
    w1 = jnp.zeros_like(i1)
    w2 = jnp.zeros_like(i2)
    for e in range(N_EXPERTS):
        w1 = jnp.where(i1 == float(e), scores[e], w1)
        w2 = jnp.where(i2 == float(e), scores[e], w2)
    tot = w1 + w2
    return i1, i2, w1 / tot, w2 / tot


def _moe_route(x_ref, mod_ref, wr_ref, rb_ref, h_scr, col_scr, row_scr, ysh_scr, ysl_scr,
               start_smem, cnt_smem):
    T = MOE_TILE
    shift = mod_ref[0, 3:4, :]
    scale = mod_ref[0, 4:5, :]
    h = _layer_norm(x_ref[0]) * (1.0 + scale) + shift
    h_hi = h.astype(jnp.bfloat16)
    h_scr[...] = h_hi
    h_lo = (h - h_hi.astype(jnp.float32)).astype(jnp.bfloat16)
    wr_hi, wr_lo = _split_bf16(wr_ref[...])
    logits_t = _nt_dot(wr_hi, h_hi) + (_nt_dot(wr_lo, h_hi) + _nt_dot(wr_hi, h_lo))
    i1, i2, w1, w2 = _router_rows(logits_t, rb_ref)

    e_iota = lax.broadcasted_iota(jnp.int32, (N_EXPERTS, T), 0).astype(jnp.float32)
    sel1 = e_iota == i1
    sel2 = e_iota == i2
    onehot = jnp.where(sel1 | sel2, 1.0, 0.0)
    tr = lax.broadcasted_iota(jnp.int32, (T, T), 0)
    tc = lax.broadcasted_iota(jnp.int32, (T, T), 1)
    earlier = jnp.where(tr < tc, 1.0, 0.0).astype(jnp.bfloat16)
    excl = jnp.dot(onehot.astype(jnp.bfloat16), earlier,
                   preferred_element_type=jnp.float32)
    cnt = jnp.sum(onehot, axis=1, keepdims=True)

    e_col = lax.broadcasted_iota(jnp.int32, (N_EXPERTS, 1), 0)
    start_v = jnp.zeros((N_EXPERTS, 1), jnp.float32)
    run = jnp.int32(0)
    for ex in range(N_EXPERTS):
        padded = lax.shift_left(lax.shift_right_logical(cnt[ex, 0].astype(jnp.int32) + 15, 4), 4)
        start_smem[ex] = run
        cnt_smem[ex] = padded
        start_v = jnp.where(e_col == ex, run.astype(jnp.float32), start_v)
        run = run + padded

    slot = excl + start_v
    pos1 = jnp.sum(jnp.where(sel1, slot, 0.0), axis=0, keepdims=True)
    pos2 = jnp.sum(jnp.where(sel2, slot, 0.0), axis=0, keepdims=True)
    row_scr[0:1, :] = pos1
    row_scr[1:2, :] = pos2
    row_scr[2:3, :] = w1
    row_scr[3:4, :] = w2

    r128 = lax.broadcasted_iota(jnp.int32, (LANES, T), 0)
    terms = jnp.zeros((LANES, T), jnp.float32)
    k = 0
    for p in (pos1, pos2):
        a = jnp.floor(p * (1.0 / 64.0))
        for t in (a, p - 64.0 * a):
            terms = jnp.where(r128 == k, t, terms)
            k += 1
    eye = jnp.where(tr == tc, 1.0, 0.0).astype(jnp.bfloat16)
    col_scr[...] = _nt_dot(eye, terms.astype(jnp.bfloat16))

    tail = MOE_ROWS_ALLOC - 2 * T
    ysh_scr[2 * T:, :] = jnp.zeros((tail, D_MODEL), jnp.bfloat16)
    ysl_scr[2 * T:, :] = jnp.zeros((tail, D_MODEL), jnp.bfloat16)


def _moe_kernel(x_ref, mod_ref, wr_ref, rb_ref, wg_ref, wu_ref, wd_ref, g_ref, b_ref,
                o_ref, h_scr, col_scr, row_scr, ysh_scr, ysl_scr, start_smem, cnt_smem):
    step = pl.program_id(1)
    T = MOE_TILE

    @pl.when(step == 0)
    def _():
        _moe_route(x_ref, mod_ref, wr_ref, rb_ref, h_scr, col_scr, row_scr, ysh_scr, ysl_scr,
                   start_smem, cnt_smem)

    pos1_row = row_scr[0:1, :]
    pos2_row = row_scr[1:2, :]
    w1_row = row_scr[2:3, :]
    w2_row = row_scr[3:4, :]

    for k in range(MOE_EXPERTS_PER_STEP):
        e = step * MOE_EXPERTS_PER_STEP + k
        seg_start = start_smem[e]
        n_chunk = (cnt_smem[e] + (MOE_CHUNK - 1)) // MOE_CHUNK

        def chunk(j, carry, k=k, seg_start=seg_start):
            base = pl.multiple_of(seg_start + j * MOE_CHUNK, 16)
            r = (base + lax.broadcasted_iota(jnp.int32, (MOE_CHUNK, T), 0)).astype(jnp.float32)
            m1 = r == pos1_row
            m2 = r == pos2_row
            gather = jnp.where(m1 | m2, 1.0, 0.0).astype(jnp.bfloat16)
            w_row = jnp.sum(jnp.where(m1, w1_row, 0.0) + jnp.where(m2, w2_row, 0.0),
                            axis=1, keepdims=True)
            xs = jnp.dot(gather, h_scr[...],
                         preferred_element_type=jnp.float32).astype(jnp.bfloat16)
            gt = jnp.dot(xs, wg_ref[k], preferred_element_type=jnp.float32)
            up = jnp.dot(xs, wu_ref[k], preferred_element_type=jnp.float32)
            act = (gt * (1.0 / (1.0 + jnp.exp(-gt))) * up).astype(jnp.bfloat16)
            y = w_row * jnp.dot(act, wd_ref[k], preferred_element_type=jnp.float32)
            y_hi = y.astype(jnp.bfloat16)
            ysh_scr[pl.ds(base, MOE_CHUNK), :] = y_hi
            ysl_scr[pl.ds(base, MOE_CHUNK), :] = (y - y_hi.astype(jnp.float32)).astype(jnp.bfloat16)
            return carry

        lax.fori_loop(0, n_chunk, chunk, 0)

    @pl.when(step == N_EXPERTS // MOE_EXPERTS_PER_STEP - 1)
    def _():
        pos1 = 64.0 * col_scr[:, 0:1] + col_scr[:, 1:2]
        pos2 = 64.0 * col_scr[:, 2:3] + col_scr[:, 3:4]
        y = jnp.zeros((T, D_MODEL), jnp.float32)
        for c0 in range(0, MOE_ROWS, MOE_G_COLS):
            r = (c0 + lax.broadcasted_iota(jnp.int32, (T, MOE_G_COLS), 1)).astype(jnp.float32)
            scatter = jnp.where((r == pos1) | (r == pos2), 1.0, 0.0).astype(jnp.bfloat16)
            y = y + jnp.dot(scatter, ysh_scr[c0:c0 + MOE_G_COLS, :],
                            preferred_element_type=jnp.float32)
            y = y + jnp.dot(scatter, ysl_scr[c0:c0 + MOE_G_COLS, :],
                            preferred_element_type=jnp.float32)
        gate = mod_ref[0, 5:6, :]
        z = DEEPNORM_ALPHA * x_ref[0] + gate * y
        o_ref[0] = _layer_norm(z) * g_ref[...] + b_ref[...]


def _moe_call(x, modl, w_router_t, rbias_col, wg_bf, wu_bf, wd_bf, ln_g, ln_b):
    B, S, D = x.shape
    tm = MOE_TILE
    nt = S // tm
    eps = MOE_EXPERTS_PER_STEP
    vec = pl.BlockSpec((1, D), lambda t, e: (0, 0))
    xspec = pl.BlockSpec((1, tm, D), lambda t, e: (t // nt, t % nt, 0))
    return pl.pallas_call(
        _moe_kernel,
        grid=(B * nt, N_EXPERTS // eps),
        in_specs=[
            xspec,
            pl.BlockSpec((1, N_MOD, D), lambda t, e: (t // nt, 0, 0)),
            pl.BlockSpec((N_EXPERTS, D), lambda t, e: (0, 0)),
            pl.BlockSpec((N_EXPERTS, 1), lambda t, e: (0, 0)),
            pl.BlockSpec((eps, D, D_EXPERT), lambda t, e: (e, 0, 0)),
            pl.BlockSpec((eps, D, D_EXPERT), lambda t, e: (e, 0, 0)),
            pl.BlockSpec((eps, D_EXPERT, D), lambda t, e: (e, 0, 0)),
            vec, vec,
        ],
        out_specs=xspec,
        out_shape=jax.ShapeDtypeStruct((B, S, D), jnp.float32),
        scratch_shapes=[
            pltpu.VMEM((tm, D), jnp.bfloat16),
            pltpu.VMEM((tm, LANES), jnp.float32),
            pltpu.VMEM((8, tm), jnp.float32),
            pltpu.VMEM((MOE_ROWS_ALLOC, D), jnp.bfloat16),
            pltpu.VMEM((MOE_ROWS_ALLOC, D), jnp.bfloat16),
            pltpu.SMEM((N_EXPERTS,), jnp.int32),
            pltpu.SMEM((N_EXPERTS,), jnp.int32),
        ],
        compiler_params=_params(("arbitrary", "arbitrary")),
    )(x, modl, w_router_t, rbias_col, wg_bf, wu_bf, wd_bf,
      ln_g.reshape(1, D), ln_b.reshape(1, D))


def _rope_tables(S):
    half = ROT_DIM // 2
    inv_freq = ROPE_THETA ** (-(jnp.arange(half, dtype=jnp.float32) * 2.0 / ROT_DIM))
    ang = jnp.arange(S, dtype=jnp.float32)[:, None] * inv_freq[None, :]
    cos, sin = jnp.cos(ang), jnp.sin(ang)
    d = jnp.arange(LANES) % HEAD_DIM
    f = d % half
    rot = d[None, :] < ROT_DIM
    return jnp.where(rot, cos[:, f], 1.0), jnp.where(rot, sin[:, f], 0.0)


def _with_rotary_partner_columns(w_in_l):
    half = ROT_DIM // 2
    c = jnp.arange(2 * D_ATTN)
    d = c % HEAD_DIM
    src = jnp.where(d < half, c + half, c - half)
    sign = jnp.where(d < half, -1.0, jnp.where(d < ROT_DIM, 1.0, 0.0))
    partner = w_in_l[:, src] * sign[None, :]
    return jnp.concatenate([w_in_l, partner], axis=1)


def _bias_placement():
    src = jnp.arange(LANES)
    hd, u, j = src >> 4, (src >> 3) & 1, src & 7
    col0 = jnp.where(hd % 2 == 0, HEAD_DIM, 0)
    dst = LANES * hd + col0 + j
    onehot = (jnp.arange(N_HEADS * LANES)[None, :] == dst[:, None]) & (u[:, None] == 1)
    return onehot.astype(jnp.bfloat16)


def kernel(x, c, w_mod, b_mod, w_in, w_pool, pool_scale, w_out, ln1_g, ln1_b,
           w_router, router_bias, w_gate, w_up, w_down, ln2_g, ln2_b):
    B, S, D = x.shape
    bf = jnp.bfloat16
    mod = _mod_call(c, w_mod, b_mod).reshape(DEPTH, B, N_MOD, D)
    cos_t, sin_t = _rope_tables(S)
    place = _bias_placement()
    w_router_t = w_router.T
    rbias_col = router_bias.reshape(N_EXPERTS, 1)
    for l in range(DEPTH):
        modl = mod[l]
        w_in_aug = _with_rotary_partner_columns(w_in[l]).astype(bf)
        qa, ka, v, p, _ = _proj_call(x, modl, w_in_aug, cos_t, sin_t, place)
        a = _attn_call(qa, ka, v)
        m = _pool_call(p, w_pool[l], pool_scale[l])
        x = _mixout_call(a, m, x, modl, w_out[l].astype(bf), ln1_g[l], ln1_b[l])
        x = _moe_call(x, modl, w_router_t, rbias_col, w_gate[l].astype(bf),
                      w_up[l].astype(bf), w_down[l].astype(bf), ln2_g[l], ln2_b[l])
    return x
```

```python
import jax
import jax.numpy as jnp
from jax import lax
from jax.experimental import pallas as pl
from jax.experimental.pallas import tpu as pltpu

D_MODEL = 1024
DEPTH = 2
D_ATTN = 512
D_POOL = 512
N_HEADS = 8
HEAD_DIM = 64
ROT_DIM = 16
ROPE_THETA = 500000.0
MOBA_BLOCK = 256
MOBA_TOPK = 3
POOL_GROUP = 128
N_POOL_GROUPS = 4
D_IN = 3 * D_ATTN + D_POOL
N_EXPERTS = 16
EXPERTS_PER_GROUP = 4
D_EXPERT = 512
DEEPNORM_ALPHA = (2 * DEPTH) ** 0.25
N_MOD = 6
LN_EPS = 1e-5
NEG_INF = -1e30

LANES = 128
VMEM_LIMIT = 56 * 1024 * 1024

_HI = lax.Precision.HIGHEST
_NT = (((1,), (1,)), ((), ()))


def _params(sem):
    return pltpu.CompilerParams(dimension_semantics=sem, vmem_limit_bytes=VMEM_LIMIT)


def _nt_dot(a, b):
    return lax.dot_general(a, b, _NT, preferred_element_type=jnp.float32)


def _mod_kernel(c_ref, w_ref, b_ref, o_ref):
    c = c_ref[...]
    cond = c * (1.0 / (1.0 + jnp.exp(-c)))
    o_ref[0] = jnp.dot(cond, w_ref[0], precision=_HI,
                       preferred_element_type=jnp.float32) + b_ref[0]


def _mod_call(c, w_mod, b_mod):
    B = c.shape[0]
    return pl.pallas_call(
        _mod_kernel,
        grid=(DEPTH, N_MOD),
        in_specs=[
            pl.BlockSpec((B, D_MODEL), lambda l, j: (0, 0)),
            pl.BlockSpec((1, D_MODEL, D_MODEL), lambda l, j: (l, 0, j)),
            pl.BlockSpec((1, 1, D_MODEL), lambda l, j: (l, 0, j)),
        ],
        out_specs=pl.BlockSpec((1, B, D_MODEL), lambda l, j: (l, 0, j)),
        out_shape=jax.ShapeDtypeStruct((DEPTH, B, N_MOD * D_MODEL), jnp.float32),
        compiler_params=_params(("arbitrary", "arbitrary")),
    )(c, w_mod, b_mod.reshape(DEPTH, 1, N_MOD * D_MODEL))


def _layer_norm(x):
    mu = jnp.mean(x, axis=-1, keepdims=True)
    xc = x - mu
    var = jnp.mean(xc * xc, axis=-1, keepdims=True)
    return xc * lax.rsqrt(var + LN_EPS)


def _split_bf16(t):
    hi = t.astype(jnp.bfloat16)
    lo = (t - hi.astype(jnp.float32)).astype(jnp.bfloat16)
    return hi, lo


def _proj_kernel(x_ref, mod_ref, w_ref, cos_ref, sin_ref, place_ref,
                 qa_ref, ka_ref, v_ref, p_ref, kbar_ref):
    i = pl.program_id(1)

    @pl.when(i == 0)
    def _():
        kbar_ref[...] = jnp.zeros_like(kbar_ref)

    x = x_ref[0]
    shift = mod_ref[0, 0:1, :]
    scale = mod_ref[0, 1:2, :]
    h = (_layer_norm(x) * (1.0 + scale) + shift).astype(jnp.bfloat16)

    cos = cos_ref[...]
    sin = sin_ref[...]

    def proj(c0, width):
        return jnp.dot(h, w_ref[:, c0:c0 + width], preferred_element_type=jnp.float32)

    def slab(t, s):
        return t[:, LANES * s:LANES * (s + 1)]

    n_slab = D_ATTN // LANES
    q = proj(0, D_ATTN)
    qp = proj(D_IN, D_ATTN)
    q_slabs = [(slab(q, s) * cos + slab(qp, s) * sin) * (HEAD_DIM ** -0.5)
               for s in range(n_slab)]
    k = proj(D_ATTN, D_ATTN)
    kp = proj(D_IN + D_ATTN, D_ATTN)
    k_slabs = [slab(k, s) * cos + slab(kp, s) * sin for s in range(n_slab)]
    v_ref[0] = proj(2 * D_ATTN, D_ATTN).astype(jnp.bfloat16)
    p_ref[0] = proj(3 * D_ATTN, D_POOL)

    kmean = jnp.concatenate(
        [jnp.mean(ks, axis=0, keepdims=True) for ks in k_slabs], axis=1)
    kbar_ref[0, pl.ds(i, 1), :] = kmean

    kb = kbar_ref[0]
    kb_rows = jnp.concatenate([kb] * (LANES // 8), axis=0)
    r_head = lax.broadcasted_iota(jnp.int32, (LANES, D_ATTN), 1) >> 6
    c_head = lax.broadcasted_iota(jnp.int32, (LANES, D_ATTN), 0) >> 4
    kbd_hi, kbd_lo = _split_bf16(jnp.where(r_head == c_head, kb_rows, 0.0))
    q_hi, q_lo = _split_bf16(jnp.concatenate(q_slabs, axis=1))
    gate = _nt_dot(q_hi, kbd_hi) + (_nt_dot(q_lo, kbd_hi) + _nt_dot(q_hi, kbd_lo))

    lane = lax.broadcasted_iota(jnp.int32, (MOBA_BLOCK, LANES), 1)
    j_of = lane & 7
    past = j_of < i
    gm = jnp.where(past, gate, NEG_INF)
    rank = jnp.zeros((MOBA_BLOCK, LANES), jnp.int32)
    for r in range(1, 8):
        other = pltpu.roll(gm, r, 1)
        beats = (other > gm) | ((other == gm) & (j_of >= r))
        rank = rank + beats.astype(jnp.int32)
    allowed = (past & (rank < MOBA_TOPK)) | (j_of == i)
    bias = jnp.where(allowed, 0.0, NEG_INF).astype(jnp.bfloat16)
    bias_cols = jnp.dot(bias, place_ref[...], preferred_element_type=jnp.float32)

    for hd in range(N_HEADS):
        own = (lane < HEAD_DIM) if hd % 2 == 0 else (lane >= HEAD_DIM)
        col0 = HEAD_DIM if hd % 2 == 0 else 0
        qa = jnp.where(own, q_slabs[hd // 2], slab(bias_cols, hd))
        ka = jnp.where(own, k_slabs[hd // 2], jnp.where(lane == col0 + i, 1.0, 0.0))
        qa_ref[0, hd] = qa.astype(jnp.bfloat16)
        ka_ref[0, hd] = ka.astype(jnp.bfloat16)


def _proj_call(x, modl, w_in_aug, cos_t, sin_t, place):
    B, S, D = x.shape
    nb = S // MOBA_BLOCK
    tm = MOBA_BLOCK
    tab = pl.BlockSpec((tm, LANES), lambda b, i: (i, 0))
    head_spec = pl.BlockSpec((1, N_HEADS, tm, LANES), lambda b, i: (b, 0, i, 0))
    return pl.pallas_call(
        _proj_kernel,
        grid=(B, nb),
        in_specs=[
            pl.BlockSpec((1, tm, D), lambda b, i: (b, i, 0)),
            pl.BlockSpec((1, N_MOD, D), lambda b, i: (b, 0, 0)),
            pl.BlockSpec((D, D_IN + 2 * D_ATTN), lambda b, i: (0, 0)),
            tab, tab,
            pl.BlockSpec((LANES, N_HEADS * LANES), lambda b, i: (0, 0)),
        ],
        out_specs=[
            head_spec, head_spec,
            pl.BlockSpec((1, tm, D_ATTN), lambda b, i: (b, i, 0)),
            pl.BlockSpec((1, tm, D_POOL), lambda b, i: (b, i, 0)),
            pl.BlockSpec((1, nb, D_ATTN), lambda b, i: (b, 0, 0)),
        ],
        out_shape=[
            jax.ShapeDtypeStruct((B, N_HEADS, S, LANES), jnp.bfloat16),
            jax.ShapeDtypeStruct((B, N_HEADS, S, LANES), jnp.bfloat16),
            jax.ShapeDtypeStruct((B, S, D_ATTN), jnp.bfloat16),
            jax.ShapeDtypeStruct((B, S, D_POOL), jnp.float32),
            jax.ShapeDtypeStruct((B, nb, D_ATTN), jnp.float32),
        ],
        compiler_params=_params(("arbitrary", "arbitrary")),
    )(x, modl, w_in_aug, cos_t, sin_t, place)


def _attn_kernel(qa_ref, ka_ref, v_ref, o_ref):
    nb = v_ref.shape[1] // MOBA_BLOCK
    row = lax.broadcasted_iota(jnp.int32, (MOBA_BLOCK, MOBA_BLOCK), 0)
    col = lax.broadcasted_iota(jnp.int32, (MOBA_BLOCK, MOBA_BLOCK), 1)
    causal = col <= row
    lane = lax.broadcasted_iota(jnp.int32, (MOBA_BLOCK, LANES), 1)

    for i in reversed(range(nb)):
        r0 = i * MOBA_BLOCK
        outs = []
        for hh in range(2):
            q = qa_ref[0, hh, r0:r0 + MOBA_BLOCK, :]
            s_own = jnp.where(causal, _nt_dot(q, ka_ref[0, hh, r0:r0 + MOBA_BLOCK, :]), NEG_INF)
            m = jnp.max(s_own, axis=1, keepdims=True)
            if i > 0:
                s_past = _nt_dot(q, ka_ref[0, hh, 0:r0, :])
                m = jnp.maximum(m, jnp.max(s_past, axis=1, keepdims=True))
            p_own = jnp.exp(s_own - m)
            l = jnp.sum(p_own, axis=1, keepdims=True)
            acc = jnp.dot(p_own.astype(jnp.bfloat16), v_ref[0, r0:r0 + MOBA_BLOCK, :],
                          preferred_element_type=jnp.float32)
            if i > 0:
                p_past = jnp.exp(s_past - m)
                l = l + jnp.sum(p_past, axis=1, keepdims=True)
                acc = acc + jnp.dot(p_past.astype(jnp.bfloat16), v_ref[0, 0:r0, :],
                                    preferred_element_type=jnp.float32)
            outs.append(acc / l)
        o = jnp.where(lane < HEAD_DIM, outs[0], outs[1])
        o_ref[0, r0:r0 + MOBA_BLOCK, :] = o.astype(o_ref.dtype)


def _attn_call(qa, ka, v):
    B, _, S, _ = qa.shape
    n_pair = N_HEADS // 2
    pair_spec = pl.BlockSpec((1, 2, S, LANES), lambda b, hp: (b, hp, 0, 0))
    slab_spec = pl.BlockSpec((1, S, LANES), lambda b, hp: (b, 0, hp))
    return pl.pallas_call(
        _attn_kernel,
        grid=(B, n_pair),
        in_specs=[pair_spec, pair_spec, slab_spec],
        out_specs=slab_spec,
        out_shape=jax.ShapeDtypeStruct((B, S, D_ATTN), jnp.bfloat16),
        compiler_params=_params(("arbitrary", "arbitrary")),
    )(qa, ka, v)


def _pool_kernel(p_ref, w_ref, sc_ref, o_ref):
    g = pl.program_id(1)
    S = p_ref.shape[1]
    p = p_ref[0]
    t = lax.broadcasted_iota(jnp.int32, (S, LANES), 0)

    def shifted(x, k):
        return jnp.where(t >= k, pltpu.roll(x, k, 0), 0.0)

    win = p
    sums = []
    for step in range(N_POOL_GROUPS):
        win = win + shifted(win, 1 << step)
        sums.append(win)
    wsum = jnp.where(g == 0, sums[0],
                     jnp.where(g == 1, sums[1], jnp.where(g == 2, sums[2], sums[3])))
    window = jnp.left_shift(2, g)
    cnt = jnp.minimum(t + 1, window).astype(jnp.float32)
    d = (wsum / cnt - p).astype(jnp.bfloat16)
    y = jnp.dot(d, w_ref[0].astype(jnp.bfloat16), preferred_element_type=jnp.float32)
    o_ref[0] = (y * sc_ref[...]).astype(o_ref.dtype)


def _pool_call(p, w_pool_l, pool_scale_l):
    B, S, _ = p.shape
    slab = pl.BlockSpec((1, S, LANES), lambda b, g: (b, 0, g))
    return pl.pallas_call(
        _pool_kernel,
        grid=(B, N_POOL_GROUPS),
        in_specs=[
            slab,
            pl.BlockSpec((1, POOL_GROUP, POOL_GROUP), lambda b, g: (g, 0, 0)),
            pl.BlockSpec((1, LANES), lambda b, g: (0, g)),
        ],
        out_specs=slab,
        out_shape=jax.ShapeDtypeStruct((B, S, D_POOL), jnp.bfloat16),
        compiler_params=_params(("arbitrary", "arbitrary")),
    )(p, w_pool_l, pool_scale_l.reshape(1, D_POOL))


def _mixout_kernel(a_ref, m_ref, x_ref, mod_ref, w_ref, g_ref, b_ref, o_ref):
    y = jnp.dot(a_ref[0], w_ref[0:D_ATTN, :], preferred_element_type=jnp.float32)
    y = y + jnp.dot(m_ref[0], w_ref[D_ATTN:, :], preferred_element_type=jnp.float32)
    gate = mod_ref[0, 2:3, :]
    z = DEEPNORM_ALPHA * x_ref[0] + gate * y
    o_ref[0] = _layer_norm(z) * g_ref[...] + b_ref[...]


def _mixout_call(a, m, x, modl, w_out_bf, ln_g, ln_b):
    B, S, D = x.shape
    tm = 512
    vec = pl.BlockSpec((1, D), lambda b, i: (0, 0))
    return pl.pallas_call(
        _mixout_kernel,
        grid=(B, S // tm),
        in_specs=[
            pl.BlockSpec((1, tm, D_ATTN), lambda b, i: (b, i, 0)),
            pl.BlockSpec((1, tm, D_POOL), lambda b, i: (b, i, 0)),
            pl.BlockSpec((1, tm, D), lambda b, i: (b, i, 0)),
            pl.BlockSpec((1, N_MOD, D), lambda b, i: (b, 0, 0)),
            pl.BlockSpec((D, D), lambda b, i: (0, 0)),
            vec, vec,
        ],
        out_specs=pl.BlockSpec((1, tm, D), lambda b, i: (b, i, 0)),
        out_shape=jax.ShapeDtypeStruct((B, S, D), jnp.float32),
        compiler_params=_params(("arbitrary", "arbitrary")),
    )(a, m, x, modl, w_out_bf, ln_g.reshape(1, D), ln_b.reshape(1, D))


MOE_TILE = 1024
MOE_CHUNK = 160
MOE_ROWS = 2304
MOE_ROWS_ALLOC = MOE_ROWS + MOE_CHUNK
MOE_G_COLS = 768
MOE_EXPERTS_PER_STEP = 2


def _top2_rows(vals):
    def first_max(rows):
        m = rows[0]
        for v in rows[1:]:
            m = jnp.maximum(m, v)
        idx = jnp.full_like(m, float(len(rows) - 1))
        for k in range(len(rows) - 2, -1, -1):
            idx = jnp.where(rows[k] == m, float(k), idx)
        return m, idx

    m1, i1 = first_max(vals)
    rest = [jnp.where(i1 == float(k), -jnp.inf, v) for k, v in enumerate(vals)]
    m2, i2 = first_max(rest)
    return m1, i1, m2, i2


def _router_rows(logits_t, rb_ref):
    lg = [logits_t[e:e + 1, :] for e in range(N_EXPERTS)]
    mx = lg[0]
    for v in lg[1:]:
        mx = jnp.maximum(mx, v)
    ex = [jnp.exp(v - mx) for v in lg]
    den = ex[0]
    for v in ex[1:]:
        den = den + v
    scores = [v / den for v in ex]
    sel = [scores[e] + rb_ref[e:e + 1, :] for e in range(N_EXPERTS)]
    best_score = None
    best = None
    for g in range(N_EXPERTS // EXPERTS_PER_GROUP):
        m1, _, m2, _ = _top2_rows(sel[g * EXPERTS_PER_GROUP:(g + 1) * EXPERTS_PER_GROUP])
        gs = m1 + m2
        if g == 0:
            best_score, best = gs, jnp.zeros_like(gs)
        else:
            better = gs > best_score
            best_score = jnp.where(better, gs, best_score)
            best = jnp.where(better, float(g), best)
    masked = [jnp.where(best == float(e // EXPERTS_PER_GROUP), sel[e], NEG_INF)
              for e in range(N_EXPERTS)]
    _, i1, _, i2 = _top2_rows(masked)
    w1 = jnp.zeros_like(i1)
    w2 = jnp.zeros_like(i2)
    for e in range(N_EXPERTS):
        w1 = jnp.where(i1 == float(e), scores[e], w1)
        w2 = jnp.where(i2 == float(e), scores[e], w2)
    tot = w1 + w2
    return i1, i2, w1 / tot, w2 / tot


def _moe_route(x_ref, mod_ref, wr_ref, rb_ref, h_scr, col_scr, row_scr, ysh_scr, ysl_scr,
               start_smem, cnt_smem):
    T = MOE_TILE
    shift = mod_ref[0, 3:4, :]
    scale = mod_ref[0, 4:5, :]
    h = _layer_norm(x_ref[0]) * (1.0 + scale) + shift
    h_hi = h.astype(jnp.bfloat16)
    h_scr[...] = h_hi
    h_lo = (h - h_hi.astype(jnp.float32)).astype(jnp.bfloat16)
    wr_hi, wr_lo = _split_bf16(wr_ref[...])
    logits_t = _nt_dot(wr_hi, h_hi) + (_nt_dot(wr_lo, h_hi) + _nt_dot(wr_hi, h_lo))
    i1, i2, w1, w2 = _router_rows(logits_t, rb_ref)

    e_iota = lax.broadcasted_iota(jnp.int32, (N_EXPERTS, T), 0).astype(jnp.float32)
    sel1 = e_iota == i1
    sel2 = e_iota == i2
    onehot = jnp.where(sel1 | sel2, 1.0, 0.0)
    tr = lax.broadcasted_iota(jnp.int32, (T, T), 0)
    tc = lax.broadcasted_iota(jnp.int32, (T, T), 1)
    earlier = jnp.where(tr < tc, 1.0, 0.0).astype(jnp.bfloat16)
    excl = jnp.dot(onehot.astype(jnp.bfloat16), earlier,
                   preferred_element_type=jnp.float32)
    cnt = jnp.sum(onehot, axis=1, keepdims=True)

    e_col = lax.broadcasted_iota(jnp.int32, (N_EXPERTS, 1), 0)
    start_v = jnp.zeros((N_EXPERTS, 1), jnp.float32)
    run = jnp.int32(0)
    for ex in range(N_EXPERTS):
        padded = lax.shift_left(lax.shift_right_logical(cnt[ex, 0].astype(jnp.int32) + 15, 4), 4)
        start_smem[ex] = run
        cnt_smem[ex] = padded
        start_v = jnp.where(e_col == ex, run.astype(jnp.float32), start_v)
        run = run + padded

    slot = excl + start_v
    pos1 = jnp.sum(jnp.where(sel1, slot, 0.0), axis=0, keepdims=True)
    pos2 = jnp.sum(jnp.where(sel2, slot, 0.0), axis=0, keepdims=True)
    row_scr[0:1, :] = pos1
    row_scr[1:2, :] = pos2
    row_scr[2:3, :] = w1
    row_scr[3:4, :] = w2

    r128 = lax.broadcasted_iota(jnp.int32, (LANES, T), 0)
    terms = jnp.zeros((LANES, T), jnp.float32)
    k = 0
    for p in (pos1, pos2):
        a = jnp.floor(p * (1.0 / 64.0))
        for t in (a, p - 64.0 * a):
            terms = jnp.where(r128 == k, t, terms)
            k += 1
    eye = jnp.where(tr == tc, 1.0, 0.0).astype(jnp.bfloat16)
    col_scr[...] = _nt_dot(eye, terms.astype(jnp.bfloat16))

    tail = MOE_ROWS_ALLOC - 2 * T
    ysh_scr[2 * T:, :] = jnp.zeros((tail, D_MODEL), jnp.bfloat16)
    ysl_scr[2 * T:, :] = jnp.zeros((tail, D_MODEL), jnp.bfloat16)


def _moe_kernel(x_ref, mod_ref, wr_ref, rb_ref, wg_ref, wu_ref, wd_ref, g_ref, b_ref,
                o_ref, h_scr, col_scr, row_scr, ysh_scr, ysl_scr, start_smem, cnt_smem):
    step = pl.program_id(1)
    T = MOE_TILE

    @pl.when(step == 0)
    def _():
        _moe_route(x_ref, mod_ref, wr_ref, rb_ref, h_scr, col_scr, row_scr, ysh_scr, ysl_scr,
                   start_smem, cnt_smem)

    pos1_row = row_scr[0:1, :]
    pos2_row = row_scr[1:2, :]
    w1_row = row_scr[2:3, :]
    w2_row = row_scr[3:4, :]

    for k in range(MOE_EXPERTS_PER_STEP):
        e = step * MOE_EXPERTS_PER_STEP + k
        seg_start = start_smem[e]
        n_chunk = (cnt_smem[e] + (MOE_CHUNK - 1)) // MOE_CHUNK

        def chunk(j, carry, k=k, seg_start=seg_start):
            base = pl.multiple_of(seg_start + j * MOE_CHUNK, 16)
            r = (base + lax.broadcasted_iota(jnp.int32, (MOE_CHUNK, T), 0)).astype(jnp.float32)
            m1 = r == pos1_row
            m2 = r == pos2_row
            gather = jnp.where(m1 | m2, 1.0, 0.0).astype(jnp.bfloat16)
            w_row = jnp.sum(jnp.where(m1, w1_row, 0.0) + jnp.where(m2, w2_row, 0.0),
                            axis=1, keepdims=True)
            xs = jnp.dot(gather, h_scr[...],
                         preferred_element_type=jnp.float32).astype(jnp.bfloat16)
            gt = jnp.dot(xs, wg_ref[k], preferred_element_type=jnp.float32)
            up = jnp.dot(xs, wu_ref[k], preferred_element_type=jnp.float32)
            act = (gt * (1.0 / (1.0 + jnp.exp(-gt))) * up).astype(jnp.bfloat16)
            y = w_row * jnp.dot(act, wd_ref[k], preferred_element_type=jnp.float32)
            y_hi = y.astype(jnp.bfloat16)
            ysh_scr[pl.ds(base, MOE_CHUNK), :] = y_hi
            ysl_scr[pl.ds(base, MOE_CHUNK), :] = (y - y_hi.astype(jnp.float32)).astype(jnp.bfloat16)
            return carry

        lax.fori_loop(0, n_chunk, chunk, 0)

    @pl.when(step == N_EXPERTS // MOE_EXPERTS_PER_STEP - 1)
    def _():
        pos1 = 64.0 * col_scr[:, 0:1] + col_scr[:, 1:2]
        pos2 = 64.0 * col_scr[:, 2:3] + col_scr[:, 3:4]
        y = jnp.zeros((T, D_MODEL), jnp.float32)
        for c0 in range(0, MOE_ROWS, MOE_G_COLS):
            r = (c0 + lax.broadcasted_iota(jnp.int32, (T, MOE_G_COLS), 1)).astype(jnp.float32)
            scatter = jnp.where((r == pos1) | (r == pos2), 1.0, 0.0).astype(jnp.bfloat16)
            y = y + jnp.dot(scatter, ysh_scr[c0:c0 + MOE_G_COLS, :],
                            preferred_element_type=jnp.float32)
            y = y + jnp.dot(scatter, ysl_scr[c0:c0 + MOE_G_COLS, :],
                            preferred_element_type=jnp.float32)
        gate = mod_ref[0, 5:6, :]
        z = DEEPNORM_ALPHA * x_ref[0] + gate * y
        o_ref[0] = _layer_norm(z) * g_ref[...] + b_ref[...]


def _moe_call(x, modl, w_router_t, rbias_col, wg_bf, wu_bf, wd_bf, ln_g, ln_b):
    B, S, D = x.shape
    tm = MOE_TILE
    nt = S // tm
    eps = MOE_EXPERTS_PER_STEP
    vec = pl.BlockSpec((1, D), lambda t, e: (0, 0))
    xspec = pl.BlockSpec((1, tm, D), lambda t, e: (t // nt, t % nt, 0))
    return pl.pallas_call(
        _moe_kernel,
        grid=(B * nt, N_EXPERTS // eps),
        in_specs=[
            xspec,
            pl.BlockSpec((1, N_MOD, D), lambda t, e: (t // nt, 0, 0)),
            pl.BlockSpec((N_EXPERTS, D), lambda t, e: (0, 0)),
            pl.BlockSpec((N_EXPERTS, 1), lambda t, e: (0, 0)),
            pl.BlockSpec((eps, D, D_EXPERT), lambda t, e: (e, 0, 0)),
            pl.BlockSpec((eps, D, D_EXPERT), lambda t, e: (e, 0, 0)),
            pl.BlockSpec((eps, D_EXPERT, D), lambda t, e: (e, 0, 0)),
            vec, vec,
        ],
        out_specs=xspec,
        out_shape=jax.ShapeDtypeStruct((B, S, D), jnp.float32),
        scratch_shapes=[
            pltpu.VMEM((tm, D), jnp.bfloat16),
            pltpu.VMEM((tm, LANES), jnp.float32),
            pltpu.VMEM((8, tm), jnp.float32),
            pltpu.VMEM((MOE_ROWS_ALLOC, D), jnp.bfloat16),
            pltpu.VMEM((MOE_ROWS_ALLOC, D), jnp.bfloat16),
            pltpu.SMEM((N_EXPERTS,), jnp.int32),
            pltpu.SMEM((N_EXPERTS,), jnp.int32),
        ],
        compiler_params=_params(("arbitrary", "arbitrary")),
    )(x, modl, w_router_t, rbias_col, wg_bf, wu_bf, wd_bf,
      ln_g.reshape(1, D), ln_b.reshape(1, D))


import functools
from jax.experimental.pallas import tpu_sc as plsc

SC_CORES = 2
SC_SUBCORES = 16
SC_ROWS = 64


def _sc_gather_rows(table, idx):
    Bn = idx.shape[0]
    D = table.shape[1]
    nw = SC_CORES * SC_SUBCORES
    per_w = Bn // nw
    n_ch = per_w // SC_ROWS
    mesh = plsc.VectorSubcoreMesh(core_axis_name="c", subcore_axis_name="s")

    @functools.partial(
        pl.kernel, mesh=mesh,
        out_type=jax.ShapeDtypeStruct((Bn, D), table.dtype),
        scratch_types=[pltpu.VMEM((SC_ROWS,), jnp.int32),
                       pltpu.VMEM((SC_ROWS, D), table.dtype),
                       pltpu.SemaphoreType.DMA],
    )
    def k(table_hbm, idx_hbm, out_hbm, idx_v, rows_v, sem):
        wid = lax.axis_index("s") * SC_CORES + lax.axis_index("c")
        base = wid * per_w

        def body(j, c):
            off = pl.multiple_of(base + j * SC_ROWS, 8)
            pltpu.sync_copy(idx_hbm.at[pl.ds(off, SC_ROWS)], idx_v)
            pltpu.async_copy(table_hbm.at[idx_v], rows_v, sem).wait()
            pltpu.sync_copy(rows_v, out_hbm.at[pl.ds(off, SC_ROWS)])
            return c

        lax.fori_loop(0, n_ch, body, 0)

    return k(table, idx)


def _rope_tables(S):
    half = ROT_DIM // 2
    inv_freq = ROPE_THETA ** (-(jnp.arange(half, dtype=jnp.float32) * 2.0 / ROT_DIM))
    ang = jnp.arange(S, dtype=jnp.float32)[:, None] * inv_freq[None, :]
    cos, sin = jnp.cos(ang), jnp.sin(ang)
    d = jnp.arange(LANES) % HEAD_DIM
    f = d % half
    rot = d[None, :] < ROT_DIM
    return jnp.where(rot, cos[:, f], 1.0), jnp.where(rot, sin[:, f], 0.0)


def _with_rotary_partner_columns(w_in_l):
    half = ROT_DIM // 2
    c = jnp.arange(2 * D_ATTN)
    d = c % HEAD_DIM
    src = jnp.where(d < half, c + half, c - half)
    sign = jnp.where(d < half, -1.0, jnp.where(d < ROT_DIM, 1.0, 0.0))
    partner = w_in_l[:, src] * sign[None, :]
    return jnp.concatenate([w_in_l, partner], axis=1)


def _bias_placement():
    src = jnp.arange(LANES)
    hd, u, j = src >> 4, (src >> 3) & 1, src & 7
    col0 = jnp.where(hd % 2 == 0, HEAD_DIM, 0)
    dst = LANES * hd + col0 + j
    onehot = (jnp.arange(N_HEADS * LANES)[None, :] == dst[:, None]) & (u[:, None] == 1)
    return onehot.astype(jnp.bfloat16)


def kernel(x, c, w_mod, b_mod, w_in, w_pool, pool_scale, w_out, ln1_g, ln1_b,
           w_router, router_bias, w_gate, w_up, w_down, ln2_g, ln2_b):
    B, S, D = x.shape
    bf = jnp.bfloat16
    mod = _mod_call(c, w_mod, b_mod).reshape(DEPTH, B, N_MOD, D)
    cos_t, sin_t = _rope_tables(S)
    place = _bias_placement()
    w_router_t = w_router.T
    rbias_col = router_bias.reshape(N_EXPERTS, 1)
    for l in range(DEPTH):
        modl = mod[l]
        w_in_aug = _with_rotary_partner_columns(w_in[l]).astype(bf)
        qa, ka, v, p, _ = _proj_call(x, modl, w_in_aug, cos_t, sin_t, place)
        a = _attn_call(qa, ka, v)
        m = _pool_call(p, w_pool[l], pool_scale[l])
        x = _mixout_call(a, m, x, modl, w_out[l].astype(bf), ln1_g[l], ln1_b[l])
        perm = (jnp.arange(B * S, dtype=jnp.int32) * 7919 + 12345) % (B * S)
        inv = jnp.argsort(perm).astype(jnp.int32)
        xg = _sc_gather_rows(x.reshape(B * S, D), perm)
        x = _sc_gather_rows(xg, inv).reshape(B, S, D)
        x = _moe_call(x, modl, w_router_t, rbias_col, w_gate[l].astype(bf),
                      w_up[l].astype(bf), w_down[l].astype(bf), ln2_g[l], ln2_b[l])
    return x
```

```python
import functools

import jax
import jax.numpy as jnp
from jax import lax
from jax.experimental import pallas as pl
from jax.experimental.pallas import tpu as pltpu
from jax.experimental.pallas import tpu_sc as plsc

D_MODEL = 1024
DEPTH = 2
D_ATTN = 512
D_POOL = 512
N_HEADS = 8
HEAD_DIM = 64
ROT_DIM = 16
ROPE_THETA = 500000.0
MOBA_BLOCK = 256
MOBA_TOPK = 3
POOL_GROUP = 128
N_POOL_GROUPS = 4
D_IN = 3 * D_ATTN + D_POOL
N_EXPERTS = 16
EXPERTS_PER_GROUP = 4
D_EXPERT = 512
DEEPNORM_ALPHA = (2 * DEPTH) ** 0.25
N_MOD = 6
LN_EPS = 1e-5
NEG_INF = -1e30

LANES = 128
VMEM_LIMIT = 56 * 1024 * 1024

_HI = lax.Precision.HIGHEST
_NT = (((1,), (1,)), ((), ()))


def _params(sem):
    return pltpu.CompilerParams(dimension_semantics=sem, vmem_limit_bytes=VMEM_LIMIT)


def _nt_dot(a, b):
    return lax.dot_general(a, b, _NT, preferred_element_type=jnp.float32)


def _mod_kernel(c_ref, w_ref, b_ref, o_ref):
    c = c_ref[...]
    cond = c * (1.0 / (1.0 + jnp.exp(-c)))
    o_ref[0] = jnp.dot(cond, w_ref[0], precision=_HI,
                       preferred_element_type=jnp.float32) + b_ref[0]


def _mod_call(c, w_mod, b_mod):
    B = c.shape[0]
    return pl.pallas_call(
        _mod_kernel,
        grid=(DEPTH, N_MOD),
        in_specs=[
            pl.BlockSpec((B, D_MODEL), lambda l, j: (0, 0)),
            pl.BlockSpec((1, D_MODEL, D_MODEL), lambda l, j: (l, 0, j)),
            pl.BlockSpec((1, 1, D_MODEL), lambda l, j: (l, 0, j)),
        ],
        out_specs=pl.BlockSpec((1, B, D_MODEL), lambda l, j: (l, 0, j)),
        out_shape=jax.ShapeDtypeStruct((DEPTH, B, N_MOD * D_MODEL), jnp.float32),
        compiler_params=_params(("arbitrary", "arbitrary")),
    )(c, w_mod, b_mod.reshape(DEPTH, 1, N_MOD * D_MODEL))


def _layer_norm(x):
    mu = jnp.mean(x, axis=-1, keepdims=True)
    xc = x - mu
    var = jnp.mean(xc * xc, axis=-1, keepdims=True)
    return xc * lax.rsqrt(var + LN_EPS)


def _split_bf16(t):
    hi = t.astype(jnp.bfloat16)
    lo = (t - hi.astype(jnp.float32)).astype(jnp.bfloat16)
    return hi, lo


def _proj_kernel(x_ref, mod_ref, w_ref, cos_ref, sin_ref, place_ref,
                 qa_ref, ka_ref, v_ref, p_ref, kbar_ref):
    i = pl.program_id(1)

    @pl.when(i == 0)
    def _():
        kbar_ref[...] = jnp.zeros_like(kbar_ref)

    x = x_ref[0]
    shift = mod_ref[0, 0:1, :]
    scale = mod_ref[0, 1:2, :]
    h = (_layer_norm(x) * (1.0 + scale) + shift).astype(jnp.bfloat16)

    cos = cos_ref[...]
    sin = sin_ref[...]

    def proj(c0, width):
        return jnp.dot(h, w_ref[:, c0:c0 + width], preferred_element_type=jnp.float32)

    def slab(t, s):
        return t[:, LANES * s:LANES * (s + 1)]

    n_slab = D_ATTN // LANES
    q = proj(0, D_ATTN)
    qp = proj(D_IN, D_ATTN)
    q_slabs = [(slab(q, s) * cos + slab(qp, s) * sin) * (HEAD_DIM ** -0.5)
               for s in range(n_slab)]
    k = proj(D_ATTN, D_ATTN)
    kp = proj(D_IN + D_ATTN, D_ATTN)
    k_slabs = [slab(k, s) * cos + slab(kp, s) * sin for s in range(n_slab)]
    v_ref[0] = proj(2 * D_ATTN, D_ATTN).astype(jnp.bfloat16)
    p_ref[0] = proj(3 * D_ATTN, D_POOL)

    kmean = jnp.concatenate(
        [jnp.mean(ks, axis=0, keepdims=True) for ks in k_slabs], axis=1)
    kbar_ref[0, pl.ds(i, 1), :] = kmean

    kb = kbar_ref[0]
    kb_rows = jnp.concatenate([kb] * (LANES // 8), axis=0)
    r_head = lax.broadcasted_iota(jnp.int32, (LANES, D_ATTN), 1) >> 6
    c_head = lax.broadcasted_iota(jnp.int32, (LANES, D_ATTN), 0) >> 4
    kbd_hi, kbd_lo = _split_bf16(jnp.where(r_head == c_head, kb_rows, 0.0))
    q_hi, q_lo = _split_bf16(jnp.concatenate(q_slabs, axis=1))
    gate = _nt_dot(q_hi, kbd_hi) + (_nt_dot(q_lo, kbd_hi) + _nt_dot(q_hi, kbd_lo))

    lane = lax.broadcasted_iota(jnp.int32, (MOBA_BLOCK, LANES), 1)
    j_of = lane & 7
    past = j_of < i
    gm = jnp.where(past, gate, NEG_INF)
    rank = jnp.zeros((MOBA_BLOCK, LANES), jnp.int32)
    for r in range(1, 8):
        other = pltpu.roll(gm, r, 1)
        beats = (other > gm) | ((other == gm) & (j_of >= r))
        rank = rank + beats.astype(jnp.int32)
    allowed = (past & (rank < MOBA_TOPK)) | (j_of == i)
    bias = jnp.where(allowed, 0.0, NEG_INF).astype(jnp.bfloat16)
    bias_cols = jnp.dot(bias, place_ref[...], preferred_element_type=jnp.float32)

    for hd in range(N_HEADS):
        own = (lane < HEAD_DIM) if hd % 2 == 0 else (lane >= HEAD_DIM)
        col0 = HEAD_DIM if hd % 2 == 0 else 0
        qa = jnp.where(own, q_slabs[hd // 2], slab(bias_cols, hd))
        ka = jnp.where(own, k_slabs[hd // 2], jnp.where(lane == col0 + i, 1.0, 0.0))
        qa_ref[0, hd] = qa.astype(jnp.bfloat16)
        ka_ref[0, hd] = ka.astype(jnp.bfloat16)


def _proj_call(x, modl, w_in_aug, cos_t, sin_t, place):
    B, S, D = x.shape
    nb = S // MOBA_BLOCK
    tm = MOBA_BLOCK
    tab = pl.BlockSpec((tm, LANES), lambda b, i: (i, 0))
    head_spec = pl.BlockSpec((1, N_HEADS, tm, LANES), lambda b, i: (b, 0, i, 0))
    return pl.pallas_call(
        _proj_kernel,
        grid=(B, nb),
        in_specs=[
            pl.BlockSpec((1, tm, D), lambda b, i: (b, i, 0)),
            pl.BlockSpec((1, N_MOD, D), lambda b, i: (b, 0, 0)),
            pl.BlockSpec((D, D_IN + 2 * D_ATTN), lambda b, i: (0, 0)),
            tab, tab,
            pl.BlockSpec((LANES, N_HEADS * LANES), lambda b, i: (0, 0)),
        ],
        out_specs=[
            head_spec, head_spec,
            pl.BlockSpec((1, tm, D_ATTN), lambda b, i: (b, i, 0)),
            pl.BlockSpec((1, tm, D_POOL), lambda b, i: (b, i, 0)),
            pl.BlockSpec((1, nb, D_ATTN), lambda b, i: (b, 0, 0)),
        ],
        out_shape=[
            jax.ShapeDtypeStruct((B, N_HEADS, S, LANES), jnp.bfloat16),
            jax.ShapeDtypeStruct((B, N_HEADS, S, LANES), jnp.bfloat16),
            jax.ShapeDtypeStruct((B, S, D_ATTN), jnp.bfloat16),
            jax.ShapeDtypeStruct((B, S, D_POOL), jnp.float32),
            jax.ShapeDtypeStruct((B, nb, D_ATTN), jnp.float32),
        ],
        compiler_params=_params(("arbitrary", "arbitrary")),
    )(x, modl, w_in_aug, cos_t, sin_t, place)


def _attn_kernel(qa_ref, ka_ref, v_ref, o_ref):
    nb = v_ref.shape[1] // MOBA_BLOCK
    row = lax.broadcasted_iota(jnp.int32, (MOBA_BLOCK, MOBA_BLOCK), 0)
    col = lax.broadcasted_iota(jnp.int32, (MOBA_BLOCK, MOBA_BLOCK), 1)
    causal = col <= row
    lane = lax.broadcasted_iota(jnp.int32, (MOBA_BLOCK, LANES), 1)

    for i in reversed(range(nb)):
        r0 = i * MOBA_BLOCK
        outs = []
        for hh in range(2):
            q = qa_ref[0, hh, r0:r0 + MOBA_BLOCK, :]
            s_own = jnp.where(causal, _nt_dot(q, ka_ref[0, hh, r0:r0 + MOBA_BLOCK, :]), NEG_INF)
            m = jnp.max(s_own, axis=1, keepdims=True)
            if i > 0:
                s_past = _nt_dot(q, ka_ref[0, hh, 0:r0, :])
                m = jnp.maximum(m, jnp.max(s_past, axis=1, keepdims=True))
            p_own = jnp.exp(s_own - m)
            l = jnp.sum(p_own, axis=1, keepdims=True)
            acc = jnp.dot(p_own.astype(jnp.bfloat16), v_ref[0, r0:r0 + MOBA_BLOCK, :],
                          preferred_element_type=jnp.float32)
            if i > 0:
                p_past = jnp.exp(s_past - m)
                l = l + jnp.sum(p_past, axis=1, keepdims=True)
                acc = acc + jnp.dot(p_past.astype(jnp.bfloat16), v_ref[0, 0:r0, :],
                                    preferred_element_type=jnp.float32)
            outs.append(acc / l)
        o = jnp.where(lane < HEAD_DIM, outs[0], outs[1])
        o_ref[0, r0:r0 + MOBA_BLOCK, :] = o.astype(o_ref.dtype)


def _attn_call(qa, ka, v):
    B, _, S, _ = qa.shape
    n_pair = N_HEADS // 2
    pair_spec = pl.BlockSpec((1, 2, S, LANES), lambda b, hp: (b, hp, 0, 0))
    slab_spec = pl.BlockSpec((1, S, LANES), lambda b, hp: (b, 0, hp))
    return pl.pallas_call(
        _attn_kernel,
        grid=(B, n_pair),
        in_specs=[pair_spec, pair_spec, slab_spec],
        out_specs=slab_spec,
        out_shape=jax.ShapeDtypeStruct((B, S, D_ATTN), jnp.bfloat16),
        compiler_params=_params(("arbitrary", "arbitrary")),
    )(qa, ka, v)


def _pool_kernel(p_ref, w_ref, sc_ref, o_ref):
    g = pl.program_id(1)
    S = p_ref.shape[1]
    p = p_ref[0]
    t = lax.broadcasted_iota(jnp.int32, (S, LANES), 0)

    def shifted(x, k):
        return jnp.where(t >= k, pltpu.roll(x, k, 0), 0.0)

    win = p
    sums = []
    for step in range(N_POOL_GROUPS):
        win = win + shifted(win, 1 << step)
        sums.append(win)
    wsum = jnp.where(g == 0, sums[0],
                     jnp.where(g == 1, sums[1], jnp.where(g == 2, sums[2], sums[3])))
    window = jnp.left_shift(2, g)
    cnt = jnp.minimum(t + 1, window).astype(jnp.float32)
    d = (wsum / cnt - p).astype(jnp.bfloat16)
    y = jnp.dot(d, w_ref[0].astype(jnp.bfloat16), preferred_element_type=jnp.float32)
    o_ref[0] = (y * sc_ref[...]).astype(o_ref.dtype)


def _pool_call(p, w_pool_l, pool_scale_l):
    B, S, _ = p.shape
    slab = pl.BlockSpec((1, S, LANES), lambda b, g: (b, 0, g))
    return pl.pallas_call(
        _pool_kernel,
        grid=(B, N_POOL_GROUPS),
        in_specs=[
            slab,
            pl.BlockSpec((1, POOL_GROUP, POOL_GROUP), lambda b, g: (g, 0, 0)),
            pl.BlockSpec((1, LANES), lambda b, g: (0, g)),
        ],
        out_specs=slab,
        out_shape=jax.ShapeDtypeStruct((B, S, D_POOL), jnp.bfloat16),
        compiler_params=_params(("arbitrary", "arbitrary")),
    )(p, w_pool_l, pool_scale_l.reshape(1, D_POOL))


def _mixout_kernel(a_ref, m_ref, x_ref, mod_ref, w_ref, g_ref, b_ref, o_ref):
    y = jnp.dot(a_ref[0], w_ref[0:D_ATTN, :], preferred_element_type=jnp.float32)
    y = y + jnp.dot(m_ref[0], w_ref[D_ATTN:, :], preferred_element_type=jnp.float32)
    gate = mod_ref[0, 2:3, :]
    z = DEEPNORM_ALPHA * x_ref[0] + gate * y
    o_ref[0] = _layer_norm(z) * g_ref[...] + b_ref[...]


def _mixout_call(a, m, x, modl, w_out_bf, ln_g, ln_b):
    B, S, D = x.shape
    tm = 512
    vec = pl.BlockSpec((1, D), lambda b, i: (0, 0))
    return pl.pallas_call(
        _mixout_kernel,
        grid=(B, S // tm),
        in_specs=[
            pl.BlockSpec((1, tm, D_ATTN), lambda b, i: (b, i, 0)),
            pl.BlockSpec((1, tm, D_POOL), lambda b, i: (b, i, 0)),
            pl.BlockSpec((1, tm, D), lambda b, i: (b, i, 0)),
            pl.BlockSpec((1, N_MOD, D), lambda b, i: (b, 0, 0)),
            pl.BlockSpec((D, D), lambda b, i: (0, 0)),
            vec, vec,
        ],
        out_specs=pl.BlockSpec((1, tm, D), lambda b, i: (b, i, 0)),
        out_shape=jax.ShapeDtypeStruct((B, S, D), jnp.float32),
        compiler_params=_params(("arbitrary", "arbitrary")),
    )(a, m, x, modl, w_out_bf, ln_g.reshape(1, D), ln_b.reshape(1, D))


MOE_TILE = 1024
MOE_CHUNK = 160
MOE_ROWS = 2304
MOE_ROWS_ALLOC = MOE_ROWS + MOE_CHUNK
MOE_G_COLS = 768
MOE_EXPERTS_PER_STEP = 2


def _top2_rows(vals):
    def first_max(rows):
        m = rows[0]
        for v in rows[1:]:
            m = jnp.maximum(m, v)
        idx = jnp.full_like(m, float(len(rows) - 1))
        for k in range(len(rows) - 2, -1, -1):
            idx = jnp.where(rows[k] == m, float(k), idx)
        return m, idx

    m1, i1 = first_max(vals)
    rest = [jnp.where(i1 == float(k), -jnp.inf, v) for k, v in enumerate(vals)]
    m2, i2 = first_max(rest)
    return m1, i1, m2, i2


def _router_rows(logits_t, rb_ref):
    lg = [logits_t[e:e + 1, :] for e in range(N_EXPERTS)]
    mx = lg[0]
    for v in lg[1:]:
        mx = jnp.maximum(mx, v)
    ex = [jnp.exp(v - mx) for v in lg]
    den = ex[0]
    for v in ex[1:]:
        den = den + v
    scores = [v / den for v in ex]
    sel = [scores[e] + rb_ref[e:e + 1, :] for e in range(N_EXPERTS)]
    best_score = None
    best = None
    for g in range(N_EXPERTS // EXPERTS_PER_GROUP):
        m1, _, m2, _ = _top2_rows(sel[g * EXPERTS_PER_GROUP:(g + 1) * EXPERTS_PER_GROUP])
        gs = m1 + m2
        if g == 0:
            best_score, best = gs, jnp.zeros_like(gs)
        else:
            better = gs > best_score
            best_score = jnp.where(better, gs, best_score)
            best = jnp.where(better, float(g), best)
    masked = [jnp.where(best == float(e // EXPERTS_PER_GROUP), sel[e], NEG_INF)
              for e in range(N_EXPERTS)]
    _, i1, _, i2 = _top2_rows(masked)
    w1 = jnp.zeros_like(i1)
    w2 = jnp.zeros_like(i2)
    for e in range(N_EXPERTS):
        w1 = jnp.where(i1 == float(e), scores[e], w1)
        w2 = jnp.where(i2 == float(e), scores[e], w2)
    tot = w1 + w2
    return i1, i2, w1 / tot, w2 / tot


def _moe_route(x_ref, mod_ref, wr_ref, rb_ref, h_scr, col_scr, row_scr, ysh_scr, ysl_scr,
               start_smem, cnt_smem):
    T = MOE_TILE
    shift = mod_ref[0, 3:4, :]
    scale = mod_ref[0, 4:5, :]
    h = _layer_norm(x_ref[0]) * (1.0 + scale) + shift
    h_hi = h.astype(jnp.bfloat16)
    h_scr[...] = h_hi
    h_lo = (h - h_hi.astype(jnp.float32)).astype(jnp.bfloat16)
    wr_hi, wr_lo = _split_bf16(wr_ref[...])
    logits_t = _nt_dot(wr_hi, h_hi) + (_nt_dot(wr_lo, h_hi) + _nt_dot(wr_hi, h_lo))
    i1, i2, w1, w2 = _router_rows(logits_t, rb_ref)

    e_iota = lax.broadcasted_iota(jnp.int32, (N_EXPERTS, T), 0).astype(jnp.float32)
    sel1 = e_iota == i1
    sel2 = e_iota == i2
    onehot = jnp.where(sel1 | sel2, 1.0, 0.0)
    tr = lax.broadcasted_iota(jnp.int32, (T, T), 0)
    tc = lax.broadcasted_iota(jnp.int32, (T, T), 1)
    earlier = jnp.where(tr < tc, 1.0, 0.0).astype(jnp.bfloat16)
    excl = jnp.dot(onehot.astype(jnp.bfloat16), earlier,
                   preferred_element_type=jnp.float32)
    cnt = jnp.sum(onehot, axis=1, keepdims=True)

    e_col = lax.broadcasted_iota(jnp.int32, (N_EXPERTS, 1), 0)
    start_v = jnp.zeros((N_EXPERTS, 1), jnp.float32)
    run = jnp.int32(0)
    for ex in range(N_EXPERTS):
        padded = lax.shift_left(lax.shift_right_logical(cnt[ex, 0].astype(jnp.int32) + 15, 4), 4)
        start_smem[ex] = run
        cnt_smem[ex] = padded
        start_v = jnp.where(e_col == ex, run.astype(jnp.float32), start_v)
        run = run + padded

    slot = excl + start_v
    pos1 = jnp.sum(jnp.where(sel1, slot, 0.0), axis=0, keepdims=True)
    pos2 = jnp.sum(jnp.where(sel2, slot, 0.0), axis=0, keepdims=True)
    row_scr[0:1, :] = pos1
    row_scr[1:2, :] = pos2
    row_scr[2:3, :] = w1
    row_scr[3:4, :] = w2

    r128 = lax.broadcasted_iota(jnp.int32, (LANES, T), 0)
    terms = jnp.zeros((LANES, T), jnp.float32)
    k = 0
    for p in (pos1, pos2):
        a = jnp.floor(p * (1.0 / 64.0))
        for t in (a, p - 64.0 * a):
            terms = jnp.where(r128 == k, t, terms)
            k += 1
    eye = jnp.where(tr == tc, 1.0, 0.0).astype(jnp.bfloat16)
    col_scr[...] = _nt_dot(eye, terms.astype(jnp.bfloat16))

    tail = MOE_ROWS_ALLOC - 2 * T
    ysh_scr[2 * T:, :] = jnp.zeros((tail, D_MODEL), jnp.bfloat16)
    ysl_scr[2 * T:, :] = jnp.zeros((tail, D_MODEL), jnp.bfloat16)


def _moe_kernel(x_ref, mod_ref, wr_ref, rb_ref, wg_ref, wu_ref, wd_ref, g_ref, b_ref,
                o_ref, h_scr, col_scr, row_scr, ysh_scr, ysl_scr, start_smem, cnt_smem):
    step = pl.program_id(1)
    T = MOE_TILE

    @pl.when(step == 0)
    def _():
        _moe_route(x_ref, mod_ref, wr_ref, rb_ref, h_scr, col_scr, row_scr, ysh_scr, ysl_scr,
                   start_smem, cnt_smem)

    pos1_row = row_scr[0:1, :]
    pos2_row = row_scr[1:2, :]
    w1_row = row_scr[2:3, :]
    w2_row = row_scr[3:4, :]

    for k in range(MOE_EXPERTS_PER_STEP):
        e = step * MOE_EXPERTS_PER_STEP + k
        seg_start = start_smem[e]
        n_chunk = (cnt_smem[e] + (MOE_CHUNK - 1)) // MOE_CHUNK

        def chunk(j, carry, k=k, seg_start=seg_start):
            base = pl.multiple_of(seg_start + j * MOE_CHUNK, 16)
            r = (base + lax.broadcasted_iota(jnp.int32, (MOE_CHUNK, T), 0)).astype(jnp.float32)
            m1 = r == pos1_row
            m2 = r == pos2_row
            gather = jnp.where(m1 | m2, 1.0, 0.0).astype(jnp.bfloat16)
            w_row = jnp.sum(jnp.where(m1, w1_row, 0.0) + jnp.where(m2, w2_row, 0.0),
                            axis=1, keepdims=True)
            xs = jnp.dot(gather, h_scr[...],
                         preferred_element_type=jnp.float32).astype(jnp.bfloat16)
            gt = jnp.dot(xs, wg_ref[k], preferred_element_type=jnp.float32)
            up = jnp.dot(xs, wu_ref[k], preferred_element_type=jnp.float32)
            act = (gt * (1.0 / (1.0 + jnp.exp(-gt))) * up).astype(jnp.bfloat16)
            y = w_row * jnp.dot(act, wd_ref[k], preferred_element_type=jnp.float32)
            y_hi = y.astype(jnp.bfloat16)
            ysh_scr[pl.ds(base, MOE_CHUNK), :] = y_hi
            ysl_scr[pl.ds(base, MOE_CHUNK), :] = (y - y_hi.astype(jnp.float32)).astype(jnp.bfloat16)
            return carry

        lax.fori_loop(0, n_chunk, chunk, 0)

    @pl.when(step == N_EXPERTS // MOE_EXPERTS_PER_STEP - 1)
    def _():
        pos1 = 64.0 * col_scr[:, 0:1] + col_scr[:, 1:2]
        pos2 = 64.0 * col_scr[:, 2:3] + col_scr[:, 3:4]
        y = jnp.zeros((T, D_MODEL), jnp.float32)
        for c0 in range(0, MOE_ROWS, MOE_G_COLS):
            r = (c0 + lax.broadcasted_iota(jnp.int32, (T, MOE_G_COLS), 1)).astype(jnp.float32)
            scatter = jnp.where((r == pos1) | (r == pos2), 1.0, 0.0).astype(jnp.bfloat16)
            y = y + jnp.dot(scatter, ysh_scr[c0:c0 + MOE_G_COLS, :],
                            preferred_element_type=jnp.float32)
            y = y + jnp.dot(scatter, ysl_scr[c0:c0 + MOE_G_COLS, :],
                            preferred_element_type=jnp.float32)
        gate = mod_ref[0, 5:6, :]
        z = DEEPNORM_ALPHA * x_ref[0] + gate * y
        o_ref[0] = _layer_norm(z) * g_ref[...] + b_ref[...]


def _moe_call(x, modl, w_router_t, rbias_col, wg_bf, wu_bf, wd_bf, ln_g, ln_b):
    B, S, D = x.shape
    tm = MOE_TILE
    nt = S // tm
    eps = MOE_EXPERTS_PER_STEP
    vec = pl.BlockSpec((1, D), lambda t, e: (0, 0))
    xspec = pl.BlockSpec((1, tm, D), lambda t, e: (t // nt, t % nt, 0))
    return pl.pallas_call(
        _moe_kernel,
        grid=(B * nt, N_EXPERTS // eps),
        in_specs=[
            xspec,
            pl.BlockSpec((1, N_MOD, D), lambda t, e: (t // nt, 0, 0)),
            pl.BlockSpec((N_EXPERTS, D), lambda t, e: (0, 0)),
            pl.BlockSpec((N_EXPERTS, 1), lambda t, e: (0, 0)),
            pl.BlockSpec((eps, D, D_EXPERT), lambda t, e: (e, 0, 0)),
            pl.BlockSpec((eps, D, D_EXPERT), lambda t, e: (e, 0, 0)),
            pl.BlockSpec((eps, D_EXPERT, D), lambda t, e: (e, 0, 0)),
            vec, vec,
        ],
        out_specs=xspec,
        out_shape=jax.ShapeDtypeStruct((B, S, D), jnp.float32),
        scratch_shapes=[
            pltpu.VMEM((tm, D), jnp.bfloat16),
            pltpu.VMEM((tm, LANES), jnp.float32),
            pltpu.VMEM((8, tm), jnp.float32),
            pltpu.VMEM((MOE_ROWS_ALLOC, D), jnp.bfloat16),
            pltpu.VMEM((MOE_ROWS_ALLOC, D), jnp.bfloat16),
            pltpu.SMEM((N_EXPERTS,), jnp.int32),
            pltpu.SMEM((N_EXPERTS,), jnp.int32),
        ],
        compiler_params=_params(("arbitrary", "arbitrary")),
    )(x, modl, w_router_t, rbias_col, wg_bf, wu_bf, wd_bf,
      ln_g.reshape(1, D), ln_b.reshape(1, D))


SC_CORES = 2
SC_SUBCORES = 16
SC_MAX_INDEX_ROWS = 128
SC_TILE_BYTES = 256 * 1024

ROUTE_TILE = 1024
EXPERT_CHUNK = 256


def _sc_gather_rows(table, idx):
    n_rows = idx.shape[0]
    width = table.shape[1]
    workers = SC_CORES * SC_SUBCORES
    rows = min(SC_MAX_INDEX_ROWS, SC_TILE_BYTES // (4 * width))
    per_worker = n_rows // workers
    assert per_worker * workers == n_rows and per_worker % rows == 0
    mesh = plsc.VectorSubcoreMesh(core_axis_name="c", subcore_axis_name="s")

    @functools.partial(
        pl.kernel, mesh=mesh,
        out_type=jax.ShapeDtypeStruct((n_rows, width), table.dtype),
        scratch_types=[pltpu.VMEM((rows,), jnp.int32),
                       pltpu.VMEM((rows, width), table.dtype),
                       pltpu.SemaphoreType.DMA],
    )
    def gather(table_hbm, idx_hbm, out_hbm, idx_v, rows_v, sem):
        wid = lax.axis_index("s") * SC_CORES + lax.axis_index("c")
        base = wid * per_worker

        def body(j, c):
            off = pl.multiple_of(base + j * rows, 8)
            pltpu.sync_copy(idx_hbm.at[pl.ds(off, rows)], idx_v)
            pltpu.async_copy(table_hbm.at[idx_v], rows_v, sem).wait()
            pltpu.sync_copy(rows_v, out_hbm.at[pl.ds(off, rows)])
            return c

        lax.fori_loop(0, per_worker // rows, body, 0)

    return gather(table, idx)


def _route_kernel(x_ref, mod_ref, wr_ref, rb_ref, hp_ref, meta_ref, cnt_ref, carry_scr):
    t = pl.program_id(0)
    T = ROUTE_TILE

    @pl.when(t == 0)
    def _():
        carry_scr[...] = jnp.zeros_like(carry_scr)

    shift = mod_ref[0, 3:4, :]
    scale = mod_ref[0, 4:5, :]
    h = _layer_norm(x_ref[0]) * (1.0 + scale) + shift
    h_hi = h.astype(jnp.bfloat16)
    h_lo = (h - h_hi.astype(jnp.float32)).astype(jnp.bfloat16)
    bits = pltpu.bitcast(h_hi.astype(jnp.float32), jnp.uint32)
    half = D_MODEL // 2
    hp_ref[...] = lax.shift_right_logical(bits[:, :half], jnp.uint32(16)) | bits[:, half:]

    wr_hi, wr_lo = _split_bf16(wr_ref[...])
    logits_t = _nt_dot(wr_hi, h_hi) + (_nt_dot(wr_lo, h_hi) + _nt_dot(wr_hi, h_lo))
    i1, i2, w1, w2 = _router_rows(logits_t, rb_ref)

    e_iota = lax.broadcasted_iota(jnp.int32, (N_EXPERTS, T), 0).astype(jnp.float32)
    sel1 = e_iota == i1
    sel2 = e_iota == i2
    onehot = jnp.where(sel1 | sel2, 1.0, 0.0)
    tr = lax.broadcasted_iota(jnp.int32, (T, T), 0)
    tc = lax.broadcasted_iota(jnp.int32, (T, T), 1)
    earlier = jnp.where(tr < tc, 1.0, 0.0).astype(jnp.bfloat16)
    rank = jnp.dot(onehot.astype(jnp.bfloat16), earlier,
                   preferred_element_type=jnp.float32) + carry_scr[:, 0:1]
    meta_ref[0, 0:1, :] = i1
    meta_ref[0, 1:2, :] = i2
    meta_ref[0, 2:3, :] = w1
    meta_ref[0, 3:4, :] = w2
    meta_ref[0, 4:5, :] = jnp.sum(jnp.where(sel1, rank, 0.0), axis=0, keepdims=True)
    meta_ref[0, 5:6, :] = jnp.sum(jnp.where(sel2, rank, 0.0), axis=0, keepdims=True)
    meta_ref[0, 6:8, :] = jnp.zeros((2, T), jnp.float32)
    carry_scr[...] = carry_scr[...] + jnp.sum(onehot, axis=1, keepdims=True)
    cnt_ref[...] = carry_scr[...]


def _route_call(x, modl, w_router_t, rbias_col):
    B, S, D = x.shape
    tm = ROUTE_TILE
    nt = S // tm
    n_tiles = B * nt
    return pl.pallas_call(
        _route_kernel,
        grid=(n_tiles,),
        in_specs=[
            pl.BlockSpec((1, tm, D), lambda t: (t // nt, t % nt, 0)),
            pl.BlockSpec((1, N_MOD, D), lambda t: (t // nt, 0, 0)),
            pl.BlockSpec((N_EXPERTS, D), lambda t: (0, 0)),
            pl.BlockSpec((N_EXPERTS, 1), lambda t: (0, 0)),
        ],
        out_specs=[
            pl.BlockSpec((tm, D // 2), lambda t: (t, 0)),
            pl.BlockSpec((1, 8, tm), lambda t: (t, 0, 0)),
            pl.BlockSpec((N_EXPERTS, LANES), lambda t: (0, 0)),
        ],
        out_shape=[
            jax.ShapeDtypeStruct((B * S, D // 2), jnp.uint32),
            jax.ShapeDtypeStruct((n_tiles, 8, tm), jnp.float32),
            jax.ShapeDtypeStruct((N_EXPERTS, LANES), jnp.float32),
        ],
        scratch_shapes=[pltpu.VMEM((N_EXPERTS, LANES), jnp.float32)],
        compiler_params=_params(("arbitrary",)),
    )(x, modl, w_router_t, rbias_col)


def _dispatch_plan(meta, counts, n_tokens):
    n_rows = 2 * n_tokens + N_EXPERTS * EXPERT_CHUNK
    per_token = lambda r: meta[:, r, :].reshape(n_tokens)
    e1 = per_token(0).astype(jnp.int32)
    e2 = per_token(1).astype(jnp.int32)
    w1 = per_token(2)
    w2 = per_token(3)
    cnt = counts[:, 0].astype(jnp.int32)
    padded = (cnt + (EXPERT_CHUNK - 1)) // EXPERT_CHUNK * EXPERT_CHUNK
    ends = jnp.cumsum(padded)
    starts = ends - padded
    pos1 = starts[e1] + per_token(4).astype(jnp.int32)
    pos2 = starts[e2] + per_token(5).astype(jnp.int32)
    tok = jnp.arange(n_tokens, dtype=jnp.int32)
    src = jnp.zeros((n_rows,), jnp.int32).at[jnp.concatenate([pos1, pos2])].set(
        jnp.concatenate([tok, tok]), unique_indices=True)
    chunk_row0 = jnp.arange(n_rows // EXPERT_CHUNK, dtype=jnp.int32) * EXPERT_CHUNK
    chunk_expert = jnp.minimum(jnp.searchsorted(ends, chunk_row0, side="right"),
                               N_EXPERTS - 1).astype(jnp.int32)
    n_used = (ends[-1] // EXPERT_CHUNK).reshape(1).astype(jnp.int32)
    return pos1, pos2, w1, w2, src, chunk_expert, n_used


def _expert_kernel(ce_ref, nu_ref, xs_ref, wg_ref, wu_ref, wd_ref, o_ref):
    c = pl.program_id(0)

    @pl.when(c < nu_ref[0])
    def _():
        w = xs_ref[...]
        lo = pltpu.bitcast(lax.shift_left(w, jnp.uint32(16)), jnp.float32)
        hi = pltpu.bitcast(w & jnp.uint32(0xFFFF0000), jnp.float32)
        xs = jnp.concatenate([lo, hi], axis=1).astype(jnp.bfloat16)
        gt = jnp.dot(xs, wg_ref[0], preferred_element_type=jnp.float32)
        up = jnp.dot(xs, wu_ref[0], preferred_element_type=jnp.float32)
        act = (gt * (1.0 / (1.0 + jnp.exp(-gt))) * up).astype(jnp.bfloat16)
        o_ref[...] = jnp.dot(act, wd_ref[0], preferred_element_type=jnp.float32)

    @pl.when(c >= nu_ref[0])
    def _():
        o_ref[...] = jnp.zeros_like(o_ref)


def _expert_call(chunk_expert, n_used, xs, wg_bf, wu_bf, wd_bf):
    n_rows, half = xs.shape
    D = 2 * half
    grid_spec = pltpu.PrefetchScalarGridSpec(
        num_scalar_prefetch=2,
        grid=(n_rows // EXPERT_CHUNK,),
        in_specs=[
            pl.BlockSpec((EXPERT_CHUNK, half), lambda c, ce, nu: (c, 0)),
            pl.BlockSpec((1, D, D_EXPERT), lambda c, ce, nu: (ce[c], 0, 0)),
            pl.BlockSpec((1, D, D_EXPERT), lambda c, ce, nu: (ce[c], 0, 0)),
            pl.BlockSpec((1, D_EXPERT, D), lambda c, ce, nu: (ce[c], 0, 0)),
        ],
        out_specs=pl.BlockSpec((EXPERT_CHUNK, D), lambda c, ce, nu: (c, 0)),
    )
    return pl.pallas_call(
        _expert_kernel,
        grid_spec=grid_spec,
        out_shape=jax.ShapeDtypeStruct((n_rows, D), jnp.float32),
        compiler_params=_params(("arbitrary",)),
    )(chunk_expert, n_used, xs, wg_bf, wu_bf, wd_bf)


def _combine_kernel(y1_ref, y2_ref, w1_ref, w2_ref, x_ref, mod_ref, g_ref, b_ref, o_ref):
    y = w1_ref[0] * y1_ref[0] + w2_ref[0] * y2_ref[0]
    gate = mod_ref[0, 5:6, :]
    z = DEEPNORM_ALPHA * x_ref[0] + gate * y
    o_ref[0] = _layer_norm(z) * g_ref[...] + b_ref[...]


def _combine_call(y1, y2, w1, w2, x, modl, ln_g, ln_b):
    B, S, D = x.shape
    tm = 512
    rows = pl.BlockSpec((1, tm, D), lambda b, i: (b, i, 0))
    col = pl.BlockSpec((1, tm, 1), lambda b, i: (b, i, 0))
    vec = pl.BlockSpec((1, D), lambda b, i: (0, 0))
    return pl.pallas_call(
        _combine_kernel,
        grid=(B, S // tm),
        in_specs=[rows, rows, col, col, rows,
                  pl.BlockSpec((1, N_MOD, D), lambda b, i: (b, 0, 0)), vec, vec],
        out_specs=rows,
        out_shape=jax.ShapeDtypeStruct((B, S, D), jnp.float32),
        compiler_params=_params(("arbitrary", "arbitrary")),
    )(y1.reshape(B, S, D), y2.reshape(B, S, D), w1.reshape(B, S, 1), w2.reshape(B, S, 1),
      x, modl, ln_g.reshape(1, D), ln_b.reshape(1, D))


def _moe_routed(x, modl, w_router_t, rbias_col, wg_bf, wu_bf, wd_bf, ln_g, ln_b):
    B, S, D = x.shape
    hp, meta, counts = _route_call(x, modl, w_router_t, rbias_col)
    pos1, pos2, w1, w2, src, chunk_expert, n_used = _dispatch_plan(meta, counts, B * S)
    xs = _sc_gather_rows(hp, src)
    ys = _expert_call(chunk_expert, n_used, xs, wg_bf, wu_bf, wd_bf)
    y1 = _sc_gather_rows(ys, pos1)
    y2 = _sc_gather_rows(ys, pos2)
    return _combine_call(y1, y2, w1, w2, x, modl, ln_g, ln_b)


def _rope_tables(S):
    half = ROT_DIM // 2
    inv_freq = ROPE_THETA ** (-(jnp.arange(half, dtype=jnp.float32) * 2.0 / ROT_DIM))
    ang = jnp.arange(S, dtype=jnp.float32)[:, None] * inv_freq[None, :]
    cos, sin = jnp.cos(ang), jnp.sin(ang)
    d = jnp.arange(LANES) % HEAD_DIM
    f = d % half
    rot = d[None, :] < ROT_DIM
    return jnp.where(rot, cos[:, f], 1.0), jnp.where(rot, sin[:, f], 0.0)


def _with_rotary_partner_columns(w_in_l):
    half = ROT_DIM // 2
    c = jnp.arange(2 * D_ATTN)
    d = c % HEAD_DIM
    src = jnp.where(d < half, c + half, c - half)
    sign = jnp.where(d < half, -1.0, jnp.where(d < ROT_DIM, 1.0, 0.0))
    partner = w_in_l[:, src] * sign[None, :]
    return jnp.concatenate([w_in_l, partner], axis=1)


def _bias_placement():
    src = jnp.arange(LANES)
    hd, u, j = src >> 4, (src >> 3) & 1, src & 7
    col0 = jnp.where(hd % 2 == 0, HEAD_DIM, 0)
    dst = LANES * hd + col0 + j
    onehot = (jnp.arange(N_HEADS * LANES)[None, :] == dst[:, None]) & (u[:, None] == 1)
    return onehot.astype(jnp.bfloat16)


def kernel(x, c, w_mod, b_mod, w_in, w_pool, pool_scale, w_out, ln1_g, ln1_b,
           w_router, router_bias, w_gate, w_up, w_down, ln2_g, ln2_b):
    B, S, D = x.shape
    bf = jnp.bfloat16
    mod = _mod_call(c, w_mod, b_mod).reshape(DEPTH, B, N_MOD, D)
    cos_t, sin_t = _rope_tables(S)
    place = _bias_placement()
    w_router_t = w_router.T
    rbias_col = router_bias.reshape(N_EXPERTS, 1)
    for l in range(DEPTH):
        modl = mod[l]
        w_in_aug = _with_rotary_partner_columns(w_in[l]).astype(bf)
        qa, ka, v, p, _ = _proj_call(x, modl, w_in_aug, cos_t, sin_t, place)
        a = _attn_call(qa, ka, v)
        m = _pool_call(p, w_pool[l], pool_scale[l])
        x = _mixout_call(a, m, x, modl, w_out[l].astype(bf), ln1_g[l], ln1_b[l])
        x = _moe_routed(x, modl, w_router_t, rbias_col, w_gate[l].astype(bf),
                        w_up[l].astype(bf), w_down[l].astype(bf), ln2_g[l], ln2_b[l])
    return x
```

```python
import functools

import jax
import jax.numpy as jnp
from jax import lax
from jax.experimental import pallas as pl
from jax.experimental.pallas import tpu as pltpu
from jax.experimental.pallas import tpu_sc as plsc

D_MODEL = 1024
DEPTH = 2
D_ATTN = 512
D_POOL = 512
N_HEADS = 8
HEAD_DIM = 64
ROT_DIM = 16
ROPE_THETA = 500000.0
MOBA_BLOCK = 256
MOBA_TOPK = 3
POOL_GROUP = 128
N_POOL_GROUPS = 4
D_IN = 3 * D_ATTN + D_POOL
N_EXPERTS = 16
EXPERTS_PER_GROUP = 4
D_EXPERT = 512
DEEPNORM_ALPHA = (2 * DEPTH) ** 0.25
N_MOD = 6
LN_EPS = 1e-5
NEG_INF = -1e30

LANES = 128
VMEM_LIMIT = 56 * 1024 * 1024

_HI = lax.Precision.HIGHEST
_NT = (((1,), (1,)), ((), ()))


def _params(sem):
    return pltpu.CompilerParams(dimension_semantics=sem, vmem_limit_bytes=VMEM_LIMIT)


def _nt_dot(a, b):
    return lax.dot_general(a, b, _NT, preferred_element_type=jnp.float32)


def _mod_kernel(c_ref, w_ref, b_ref, o_ref):
    c = c_ref[...]
    cond = c * (1.0 / (1.0 + jnp.exp(-c)))
    o_ref[0] = jnp.dot(cond, w_ref[0], precision=_HI,
                       preferred_element_type=jnp.float32) + b_ref[0]


def _mod_call(c, w_mod, b_mod):
    B = c.shape[0]
    return pl.pallas_call(
        _mod_kernel,
        grid=(DEPTH, N_MOD),
        in_specs=[
            pl.BlockSpec((B, D_MODEL), lambda l, j: (0, 0)),
            pl.BlockSpec((1, D_MODEL, D_MODEL), lambda l, j: (l, 0, j)),
            pl.BlockSpec((1, 1, D_MODEL), lambda l, j: (l, 0, j)),
        ],
        out_specs=pl.BlockSpec((1, B, D_MODEL), lambda l, j: (l, 0, j)),
        out_shape=jax.ShapeDtypeStruct((DEPTH, B, N_MOD * D_MODEL), jnp.float32),
        compiler_params=_params(("arbitrary", "arbitrary")),
    )(c, w_mod, b_mod.reshape(DEPTH, 1, N_MOD * D_MODEL))


def _layer_norm(x):
    mu = jnp.mean(x, axis=-1, keepdims=True)
    xc = x - mu
    var = jnp.mean(xc * xc, axis=-1, keepdims=True)
    return xc * lax.rsqrt(var + LN_EPS)


def _split_bf16(t):
    hi = t.astype(jnp.bfloat16)
    lo = (t - hi.astype(jnp.float32)).astype(jnp.bfloat16)
    return hi, lo


def _proj_kernel(x_ref, mod_ref, w_ref, cos_ref, sin_ref, place_ref,
                 qa_ref, ka_ref, v_ref, p_ref, kbar_ref):
    i = pl.program_id(1)

    @pl.when(i == 0)
    def _():
        kbar_ref[...] = jnp.zeros_like(kbar_ref)

    x = x_ref[0]
    shift = mod_ref[0, 0:1, :]
    scale = mod_ref[0, 1:2, :]
    h = (_layer_norm(x) * (1.0 + scale) + shift).astype(jnp.bfloat16)

    cos = cos_ref[...]
    sin = sin_ref[...]

    def proj(c0, width):
        return jnp.dot(h, w_ref[:, c0:c0 + width], preferred_element_type=jnp.float32)

    def slab(t, s):
        return t[:, LANES * s:LANES * (s + 1)]

    n_slab = D_ATTN // LANES
    q = proj(0, D_ATTN)
    qp = proj(D_IN, D_ATTN)
    q_slabs = [(slab(q, s) * cos + slab(qp, s) * sin) * (HEAD_DIM ** -0.5)
               for s in range(n_slab)]
    k = proj(D_ATTN, D_ATTN)
    kp = proj(D_IN + D_ATTN, D_ATTN)
    k_slabs = [slab(k, s) * cos + slab(kp, s) * sin for s in range(n_slab)]
    v_ref[0] = proj(2 * D_ATTN, D_ATTN).astype(jnp.bfloat16)
    p_ref[0] = proj(3 * D_ATTN, D_POOL)

    kmean = jnp.concatenate(
        [jnp.mean(ks, axis=0, keepdims=True) for ks in k_slabs], axis=1)
    kbar_ref[0, pl.ds(i, 1), :] = kmean

    kb = kbar_ref[0]
    kb_rows = jnp.concatenate([kb] * (LANES // 8), axis=0)
    r_head = lax.broadcasted_iota(jnp.int32, (LANES, D_ATTN), 1) >> 6
    c_head = lax.broadcasted_iota(jnp.int32, (LANES, D_ATTN), 0) >> 4
    kbd_hi, kbd_lo = _split_bf16(jnp.where(r_head == c_head, kb_rows, 0.0))
    q_hi, q_lo = _split_bf16(jnp.concatenate(q_slabs, axis=1))
    gate = _nt_dot(q_hi, kbd_hi) + (_nt_dot(q_lo, kbd_hi) + _nt_dot(q_hi, kbd_lo))

    lane = lax.broadcasted_iota(jnp.int32, (MOBA_BLOCK, LANES), 1)
    j_of = lane & 7
    past = j_of < i
    gm = jnp.where(past, gate, NEG_INF)
    rank = jnp.zeros((MOBA_BLOCK, LANES), jnp.int32)
    for r in range(1, 8):
        other = pltpu.roll(gm, r, 1)
        beats = (other > gm) | ((other == gm) & (j_of >= r))
        rank = rank + beats.astype(jnp.int32)
    allowed = (past & (rank < MOBA_TOPK)) | (j_of == i)
    bias = jnp.where(allowed, 0.0, NEG_INF).astype(jnp.bfloat16)
    bias_cols = jnp.dot(bias, place_ref[...], preferred_element_type=jnp.float32)

    for hd in range(N_HEADS):
        own = (lane < HEAD_DIM) if hd % 2 == 0 else (lane >= HEAD_DIM)
        col0 = HEAD_DIM if hd % 2 == 0 else 0
        qa = jnp.where(own, q_slabs[hd // 2], slab(bias_cols, hd))
        ka = jnp.where(own, k_slabs[hd // 2], jnp.where(lane == col0 + i, 1.0, 0.0))
        qa_ref[0, hd] = qa.astype(jnp.bfloat16)
        ka_ref[0, hd] = ka.astype(jnp.bfloat16)


def _proj_call(x, modl, w_in_aug, cos_t, sin_t, place):
    B, S, D = x.shape
    nb = S // MOBA_BLOCK
    tm = MOBA_BLOCK
    tab = pl.BlockSpec((tm, LANES), lambda b, i: (i, 0))
    head_spec = pl.BlockSpec((1, N_HEADS, tm, LANES), lambda b, i: (b, 0, i, 0))
    return pl.pallas_call(
        _proj_kernel,
        grid=(B, nb),
        in_specs=[
            pl.BlockSpec((1, tm, D), lambda b, i: (b, i, 0)),
            pl.BlockSpec((1, N_MOD, D), lambda b, i: (b, 0, 0)),
            pl.BlockSpec((D, D_IN + 2 * D_ATTN), lambda b, i: (0, 0)),
            tab, tab,
            pl.BlockSpec((LANES, N_HEADS * LANES), lambda b, i: (0, 0)),
        ],
        out_specs=[
            head_spec, head_spec,
            pl.BlockSpec((1, tm, D_ATTN), lambda b, i: (b, i, 0)),
            pl.BlockSpec((1, tm, D_POOL), lambda b, i: (b, i, 0)),
            pl.BlockSpec((1, nb, D_ATTN), lambda b, i: (b, 0, 0)),
        ],
        out_shape=[
            jax.ShapeDtypeStruct((B, N_HEADS, S, LANES), jnp.bfloat16),
            jax.ShapeDtypeStruct((B, N_HEADS, S, LANES), jnp.bfloat16),
            jax.ShapeDtypeStruct((B, S, D_ATTN), jnp.bfloat16),
            jax.ShapeDtypeStruct((B, S, D_POOL), jnp.float32),
            jax.ShapeDtypeStruct((B, nb, D_ATTN), jnp.float32),
        ],
        compiler_params=_params(("arbitrary", "arbitrary")),
    )(x, modl, w_in_aug, cos_t, sin_t, place)


def _attn_kernel(qa_ref, ka_ref, v_ref, o_ref):
    nb = v_ref.shape[1] // MOBA_BLOCK
    row = lax.broadcasted_iota(jnp.int32, (MOBA_BLOCK, MOBA_BLOCK), 0)
    col = lax.broadcasted_iota(jnp.int32, (MOBA_BLOCK, MOBA_BLOCK), 1)
    causal = col <= row
    lane = lax.broadcasted_iota(jnp.int32, (MOBA_BLOCK, LANES), 1)

    for i in reversed(range(nb)):
        r0 = i * MOBA_BLOCK
        outs = []
        for hh in range(2):
            q = qa_ref[0, hh, r0:r0 + MOBA_BLOCK, :]
            s_own = jnp.where(causal, _nt_dot(q, ka_ref[0, hh, r0:r0 + MOBA_BLOCK, :]), NEG_INF)
            m = jnp.max(s_own, axis=1, keepdims=True)
            if i > 0:
                s_past = _nt_dot(q, ka_ref[0, hh, 0:r0, :])
                m = jnp.maximum(m, jnp.max(s_past, axis=1, keepdims=True))
            p_own = jnp.exp(s_own - m)
            l = jnp.sum(p_own, axis=1, keepdims=True)
            acc = jnp.dot(p_own.astype(jnp.bfloat16), v_ref[0, r0:r0 + MOBA_BLOCK, :],
                          preferred_element_type=jnp.float32)
            if i > 0:
                p_past = jnp.exp(s_past - m)
                l = l + jnp.sum(p_past, axis=1, keepdims=True)
                acc = acc + jnp.dot(p_past.astype(jnp.bfloat16), v_ref[0, 0:r0, :],
                                    preferred_element_type=jnp.float32)
            outs.append(acc / l)
        o = jnp.where(lane < HEAD_DIM, outs[0], outs[1])
        o_ref[0, r0:r0 + MOBA_BLOCK, :] = o.astype(o_ref.dtype)


def _attn_call(qa, ka, v):
    B, _, S, _ = qa.shape
    n_pair = N_HEADS // 2
    pair_spec = pl.BlockSpec((1, 2, S, LANES), lambda b, hp: (b, hp, 0, 0))
    slab_spec = pl.BlockSpec((1, S, LANES), lambda b, hp: (b, 0, hp))
    return pl.pallas_call(
        _attn_kernel,
        grid=(B, n_pair),
        in_specs=[pair_spec, pair_spec, slab_spec],
        out_specs=slab_spec,
        out_shape=jax.ShapeDtypeStruct((B, S, D_ATTN), jnp.bfloat16),
        compiler_params=_params(("arbitrary", "arbitrary")),
    )(qa, ka, v)


def _pool_kernel(p_ref, w_ref, sc_ref, o_ref):
    g = pl.program_id(1)
    S = p_ref.shape[1]
    p = p_ref[0]
    t = lax.broadcasted_iota(jnp.int32, (S, LANES), 0)

    def shifted(x, k):
        return jnp.where(t >= k, pltpu.roll(x, k, 0), 0.0)

    win = p
    sums = []
    for step in range(N_POOL_GROUPS):
        win = win + shifted(win, 1 << step)
        sums.append(win)
    wsum = jnp.where(g == 0, sums[0],
                     jnp.where(g == 1, sums[1], jnp.where(g == 2, sums[2], sums[3])))
    window = jnp.left_shift(2, g)
    cnt = jnp.minimum(t + 1, window).astype(jnp.float32)
    d = (wsum / cnt - p).astype(jnp.bfloat16)
    y = jnp.dot(d, w_ref[0].astype(jnp.bfloat16), preferred_element_type=jnp.float32)
    o_ref[0] = (y * sc_ref[...]).astype(o_ref.dtype)


def _pool_call(p, w_pool_l, pool_scale_l):
    B, S, _ = p.shape
    slab = pl.BlockSpec((1, S, LANES), lambda b, g: (b, 0, g))
    return pl.pallas_call(
        _pool_kernel,
        grid=(B, N_POOL_GROUPS),
        in_specs=[
            slab,
            pl.BlockSpec((1, POOL_GROUP, POOL_GROUP), lambda b, g: (g, 0, 0)),
            pl.BlockSpec((1, LANES), lambda b, g: (0, g)),
        ],
        out_specs=slab,
        out_shape=jax.ShapeDtypeStruct((B, S, D_POOL), jnp.bfloat16),
        compiler_params=_params(("arbitrary", "arbitrary")),
    )(p, w_pool_l, pool_scale_l.reshape(1, D_POOL))


def _mixout_kernel(a_ref, m_ref, x_ref, mod_ref, w_ref, g_ref, b_ref, o_ref):
    y = jnp.dot(a_ref[0], w_ref[0:D_ATTN, :], preferred_element_type=jnp.float32)
    y = y + jnp.dot(m_ref[0], w_ref[D_ATTN:, :], preferred_element_type=jnp.float32)
    gate = mod_ref[0, 2:3, :]
    z = DEEPNORM_ALPHA * x_ref[0] + gate * y
    o_ref[0] = _layer_norm(z) * g_ref[...] + b_ref[...]


def _mixout_call(a, m, x, modl, w_out_bf, ln_g, ln_b):
    B, S, D = x.shape
    tm = 512
    vec = pl.BlockSpec((1, D), lambda b, i: (0, 0))
    return pl.pallas_call(
        _mixout_kernel,
        grid=(B, S // tm),
        in_specs=[
            pl.BlockSpec((1, tm, D_ATTN), lambda b, i: (b, i, 0)),
            pl.BlockSpec((1, tm, D_POOL), lambda b, i: (b, i, 0)),
            pl.BlockSpec((1, tm, D), lambda b, i: (b, i, 0)),
            pl.BlockSpec((1, N_MOD, D), lambda b, i: (b, 0, 0)),
            pl.BlockSpec((D, D), lambda b, i: (0, 0)),
            vec, vec,
        ],
        out_specs=pl.BlockSpec((1, tm, D), lambda b, i: (b, i, 0)),
        out_shape=jax.ShapeDtypeStruct((B, S, D), jnp.float32),
        compiler_params=_params(("arbitrary", "arbitrary")),
    )(a, m, x, modl, w_out_bf, ln_g.reshape(1, D), ln_b.reshape(1, D))


MOE_TILE = 1024
MOE_CHUNK = 160
MOE_ROWS = 2304
MOE_ROWS_ALLOC = MOE_ROWS + MOE_CHUNK
MOE_G_COLS = 768
MOE_EXPERTS_PER_STEP = 2


def _top2_rows(vals):
    def first_max(rows):
        m = rows[0]
        for v in rows[1:]:
            m = jnp.maximum(m, v)
        idx = jnp.full_like(m, float(len(rows) - 1))
        for k in range(len(rows) - 2, -1, -1):
            idx = jnp.where(rows[k] == m, float(k), idx)
        return m, idx

    m1, i1 = first_max(vals)
    rest = [jnp.where(i1 == float(k), -jnp.inf, v) for k, v in enumerate(vals)]
    m2, i2 = first_max(rest)
    return m1, i1, m2, i2


def _router_rows(logits_t, rb_ref):
    lg = [logits_t[e:e + 1, :] for e in range(N_EXPERTS)]
    mx = lg[0]
    for v in lg[1:]:
        mx = jnp.maximum(mx, v)
    ex = [jnp.exp(v - mx) for v in lg]
    den = ex[0]
    for v in ex[1:]:
        den = den + v
    scores = [v / den for v in ex]
    sel = [scores[e] + rb_ref[e:e + 1, :] for e in range(N_EXPERTS)]
    best_score = None
    best = None
    for g in range(N_EXPERTS // EXPERTS_PER_GROUP):
        m1, _, m2, _ = _top2_rows(sel[g * EXPERTS_PER_GROUP:(g + 1) * EXPERTS_PER_GROUP])
        gs = m1 + m2
        if g == 0:
            best_score, best = gs, jnp.zeros_like(gs)
        else:
            better = gs > best_score
            best_score = jnp.where(better, gs, best_score)
            best = jnp.where(better, float(g), best)
    masked = [jnp.where(best == float(e // EXPERTS_PER_GROUP), sel[e], NEG_INF)
              for e in range(N_EXPERTS)]
    _, i1, _, i2 = _top2_rows(masked)
    w1 = jnp.zeros_like(i1)
    w2 = jnp.zeros_like(i2)
    for e in range(N_EXPERTS):
        w1 = jnp.where(i1 == float(e), scores[e], w1)
        w2 = jnp.where(i2 == float(e), scores[e], w2)
    tot = w1 + w2
    return i1, i2, w1 / tot, w2 / tot


def _moe_route(x_ref, mod_ref, wr_ref, rb_ref, h_scr, col_scr, row_scr, ysh_scr, ysl_scr,
               start_smem, cnt_smem):
    T = MOE_TILE
    shift = mod_ref[0, 3:4, :]
    scale = mod_ref[0, 4:5, :]
    h = _layer_norm(x_ref[0]) * (1.0 + scale) + shift
    h_hi = h.astype(jnp.bfloat16)
    h_scr[...] = h_hi
    h_lo = (h - h_hi.astype(jnp.float32)).astype(jnp.bfloat16)
    wr_hi, wr_lo = _split_bf16(wr_ref[...])
    logits_t = _nt_dot(wr_hi, h_hi) + (_nt_dot(wr_lo, h_hi) + _nt_dot(wr_hi, h_lo))
    i1, i2, w1, w2 = _router_rows(logits_t, rb_ref)

    e_iota = lax.broadcasted_iota(jnp.int32, (N_EXPERTS, T), 0).astype(jnp.float32)
    sel1 = e_iota == i1
    sel2 = e_iota == i2
    onehot = jnp.where(sel1 | sel2, 1.0, 0.0)
    tr = lax.broadcasted_iota(jnp.int32, (T, T), 0)
    tc = lax.broadcasted_iota(jnp.int32, (T, T), 1)
    earlier = jnp.where(tr < tc, 1.0, 0.0).astype(jnp.bfloat16)
    excl = jnp.dot(onehot.astype(jnp.bfloat16), earlier,
                   preferred_element_type=jnp.float32)
    cnt = jnp.sum(onehot, axis=1, keepdims=True)

    e_col = lax.broadcasted_iota(jnp.int32, (N_EXPERTS, 1), 0)
    start_v = jnp.zeros((N_EXPERTS, 1), jnp.float32)
    run = jnp.int32(0)
    for ex in range(N_EXPERTS):
        padded = lax.shift_left(lax.shift_right_logical(cnt[ex, 0].astype(jnp.int32) + 15, 4), 4)
        start_smem[ex] = run
        cnt_smem[ex] = padded
        start_v = jnp.where(e_col == ex, run.astype(jnp.float32), start_v)
        run = run + padded

    slot = excl + start_v
    pos1 = jnp.sum(jnp.where(sel1, slot, 0.0), axis=0, keepdims=True)
    pos2 = jnp.sum(jnp.where(sel2, slot, 0.0), axis=0, keepdims=True)
    row_scr[0:1, :] = pos1
    row_scr[1:2, :] = pos2
    row_scr[2:3, :] = w1
    row_scr[3:4, :] = w2

    r128 = lax.broadcasted_iota(jnp.int32, (LANES, T), 0)
    terms = jnp.zeros((LANES, T), jnp.float32)
    k = 0
    for p in (pos1, pos2):
        a = jnp.floor(p * (1.0 / 64.0))
        for t in (a, p - 64.0 * a):
            terms = jnp.where(r128 == k, t, terms)
            k += 1
    eye = jnp.where(tr == tc, 1.0, 0.0).astype(jnp.bfloat16)
    col_scr[...] = _nt_dot(eye, terms.astype(jnp.bfloat16))

    tail = MOE_ROWS_ALLOC - 2 * T
    ysh_scr[2 * T:, :] = jnp.zeros((tail, D_MODEL), jnp.bfloat16)
    ysl_scr[2 * T:, :] = jnp.zeros((tail, D_MODEL), jnp.bfloat16)


def _moe_kernel(x_ref, mod_ref, wr_ref, rb_ref, wg_ref, wu_ref, wd_ref, g_ref, b_ref,
                o_ref, h_scr, col_scr, row_scr, ysh_scr, ysl_scr, start_smem, cnt_smem):
    step = pl.program_id(1)
    T = MOE_TILE

    @pl.when(step == 0)
    def _():
        _moe_route(x_ref, mod_ref, wr_ref, rb_ref, h_scr, col_scr, row_scr, ysh_scr, ysl_scr,
                   start_smem, cnt_smem)

    pos1_row = row_scr[0:1, :]
    pos2_row = row_scr[1:2, :]
    w1_row = row_scr[2:3, :]
    w2_row = row_scr[3:4, :]

    for k in range(MOE_EXPERTS_PER_STEP):
        e = step * MOE_EXPERTS_PER_STEP + k
        seg_start = start_smem[e]
        n_chunk = (cnt_smem[e] + (MOE_CHUNK - 1)) // MOE_CHUNK

        def chunk(j, carry, k=k, seg_start=seg_start):
            base = pl.multiple_of(seg_start + j * MOE_CHUNK, 16)
            r = (base + lax.broadcasted_iota(jnp.int32, (MOE_CHUNK, T), 0)).astype(jnp.float32)
            m1 = r == pos1_row
            m2 = r == pos2_row
            gather = jnp.where(m1 | m2, 1.0, 0.0).astype(jnp.bfloat16)
            w_row = jnp.sum(jnp.where(m1, w1_row, 0.0) + jnp.where(m2, w2_row, 0.0),
                            axis=1, keepdims=True)
            xs = jnp.dot(gather, h_scr[...],
                         preferred_element_type=jnp.float32).astype(jnp.bfloat16)
            gt = jnp.dot(xs, wg_ref[k], preferred_element_type=jnp.float32)
            up = jnp.dot(xs, wu_ref[k], preferred_element_type=jnp.float32)
            act = (gt * (1.0 / (1.0 + jnp.exp(-gt))) * up).astype(jnp.bfloat16)
            y = w_row * jnp.dot(act, wd_ref[k], preferred_element_type=jnp.float32)
            y_hi = y.astype(jnp.bfloat16)
            ysh_scr[pl.ds(base, MOE_CHUNK), :] = y_hi
            ysl_scr[pl.ds(base, MOE_CHUNK), :] = (y - y_hi.astype(jnp.float32)).astype(jnp.bfloat16)
            return carry

        lax.fori_loop(0, n_chunk, chunk, 0)

    @pl.when(step == N_EXPERTS // MOE_EXPERTS_PER_STEP - 1)
    def _():
        pos1 = 64.0 * col_scr[:, 0:1] + col_scr[:, 1:2]
        pos2 = 64.0 * col_scr[:, 2:3] + col_scr[:, 3:4]
        y = jnp.zeros((T, D_MODEL), jnp.float32)
        for c0 in range(0, MOE_ROWS, MOE_G_COLS):
            r = (c0 + lax.broadcasted_iota(jnp.int32, (T, MOE_G_COLS), 1)).astype(jnp.float32)
            scatter = jnp.where((r == pos1) | (r == pos2), 1.0, 0.0).astype(jnp.bfloat16)
            y = y + jnp.dot(scatter, ysh_scr[c0:c0 + MOE_G_COLS, :],
                            preferred_element_type=jnp.float32)
            y = y + jnp.dot(scatter, ysl_scr[c0:c0 + MOE_G_COLS, :],
                            preferred_element_type=jnp.float32)
        gate = mod_ref[0, 5:6, :]
        z = DEEPNORM_ALPHA * x_ref[0] + gate * y
        o_ref[0] = _layer_norm(z) * g_ref[...] + b_ref[...]


def _moe_call(x, modl, w_router_t, rbias_col, wg_bf, wu_bf, wd_bf, ln_g, ln_b):
    B, S, D = x.shape
    tm = MOE_TILE
    nt = S // tm
    eps = MOE_EXPERTS_PER_STEP
    vec = pl.BlockSpec((1, D), lambda t, e: (0, 0))
    xspec = pl.BlockSpec((1, tm, D), lambda t, e: (t // nt, t % nt, 0))
    return pl.pallas_call(
        _moe_kernel,
        grid=(B * nt, N_EXPERTS // eps),
        in_specs=[
            xspec,
            pl.BlockSpec((1, N_MOD, D), lambda t, e: (t // nt, 0, 0)),
            pl.BlockSpec((N_EXPERTS, D), lambda t, e: (0, 0)),
            pl.BlockSpec((N_EXPERTS, 1), lambda t, e: (0, 0)),
            pl.BlockSpec((eps, D, D_EXPERT), lambda t, e: (e, 0, 0)),
            pl.BlockSpec((eps, D, D_EXPERT), lambda t, e: (e, 0, 0)),
            pl.BlockSpec((eps, D_EXPERT, D), lambda t, e: (e, 0, 0)),
            vec, vec,
        ],
        out_specs=xspec,
        out_shape=jax.ShapeDtypeStruct((B, S, D), jnp.float32),
        scratch_shapes=[
            pltpu.VMEM((tm, D), jnp.bfloat16),
            pltpu.VMEM((tm, LANES), jnp.float32),
            pltpu.VMEM((8, tm), jnp.float32),
            pltpu.VMEM((MOE_ROWS_ALLOC, D), jnp.bfloat16),
            pltpu.VMEM((MOE_ROWS_ALLOC, D), jnp.bfloat16),
            pltpu.SMEM((N_EXPERTS,), jnp.int32),
            pltpu.SMEM((N_EXPERTS,), jnp.int32),
        ],
        compiler_params=_params(("arbitrary", "arbitrary")),
    )(x, modl, w_router_t, rbias_col, wg_bf, wu_bf, wd_bf,
      ln_g.reshape(1, D), ln_b.reshape(1, D))


SC_CORES = 2
SC_SUBCORES = 16
SC_MAX_INDEX_ROWS = 128
SC_TILE_BYTES = 128 * 1024

ROUTE_TILE = 1024
EXPERT_CHUNK = 256


def _sc_gather_rows(table, idx):
    n_rows = idx.shape[0]
    width = table.shape[1]
    workers = SC_CORES * SC_SUBCORES
    rows = min(SC_MAX_INDEX_ROWS, SC_TILE_BYTES // (4 * width))
    per_worker = n_rows // workers
    assert per_worker * workers == n_rows and per_worker % rows == 0
    n_chunk = per_worker // rows
    mesh = plsc.VectorSubcoreMesh(core_axis_name="c", subcore_axis_name="s")
    row_buf = pltpu.VMEM((rows, width), table.dtype)
    idx_buf = pltpu.VMEM((rows,), jnp.int32)
    dma = pltpu.SemaphoreType.DMA

    @functools.partial(
        pl.kernel, mesh=mesh,
        out_type=jax.ShapeDtypeStruct((n_rows, width), table.dtype),
        scratch_types=[idx_buf, idx_buf, row_buf, row_buf, dma, dma, dma, dma],
    )
    def gather(table_hbm, idx_hbm, out_hbm, idx0, idx1, rows0, rows1, g0, g1, w0, w1):
        wid = lax.axis_index("s") * SC_CORES + lax.axis_index("c")
        base = wid * per_worker
        idx_v, rows_v, gsem, wsem = (idx0, idx1), (rows0, rows1), (g0, g1), (w0, w1)

        def start_gather(j):
            b = j % 2
            off = pl.multiple_of(base + j * rows, 8)
            pltpu.sync_copy(idx_hbm.at[pl.ds(off, rows)], idx_v[b])
            return pltpu.async_copy(table_hbm.at[idx_v[b]], rows_v[b], gsem[b])

        gathers = {0: start_gather(0)}
        writes = {}
        for j in range(n_chunk):
            b = j % 2
            if j + 1 < n_chunk:
                if j >= 1:
                    writes.pop(j - 1).wait()
                gathers[j + 1] = start_gather(j + 1)
            gathers.pop(j).wait()
            off = pl.multiple_of(base + j * rows, 8)
            writes[j] = pltpu.async_copy(rows_v[b], out_hbm.at[pl.ds(off, rows)], wsem[b])
        for j in sorted(writes):
            writes[j].wait()

    return gather(table, idx)


def _route_kernel(x_ref, mod_ref, wr_ref, rb_ref, hp_ref, meta_ref, cnt_ref, carry_scr):
    t = pl.program_id(0)
    T = ROUTE_TILE

    @pl.when(t == 0)
    def _():
        carry_scr[...] = jnp.zeros_like(carry_scr)

    shift = mod_ref[0, 3:4, :]
    scale = mod_ref[0, 4:5, :]
    h = _layer_norm(x_ref[0]) * (1.0 + scale) + shift
    h_hi = h.astype(jnp.bfloat16)
    h_lo = (h - h_hi.astype(jnp.float32)).astype(jnp.bfloat16)
    bits = pltpu.bitcast(h_hi.astype(jnp.float32), jnp.uint32)
    half = D_MODEL // 2
    hp_ref[...] = lax.shift_right_logical(bits[:, :half], jnp.uint32(16)) | bits[:, half:]

    wr_hi, wr_lo = _split_bf16(wr_ref[...])
    logits_t = _nt_dot(wr_hi, h_hi) + (_nt_dot(wr_lo, h_hi) + _nt_dot(wr_hi, h_lo))
    i1, i2, w1, w2 = _router_rows(logits_t, rb_ref)

    e_iota = lax.broadcasted_iota(jnp.int32, (N_EXPERTS, T), 0).astype(jnp.float32)
    sel1 = e_iota == i1
    sel2 = e_iota == i2
    onehot = jnp.where(sel1 | sel2, 1.0, 0.0)
    tr = lax.broadcasted_iota(jnp.int32, (T, T), 0)
    tc = lax.broadcasted_iota(jnp.int32, (T, T), 1)
    earlier = jnp.where(tr < tc, 1.0, 0.0).astype(jnp.bfloat16)
    rank = jnp.dot(onehot.astype(jnp.bfloat16), earlier,
                   preferred_element_type=jnp.float32) + carry_scr[:, 0:1]
    rank1 = jnp.sum(jnp.where(sel1, rank, 0.0), axis=0, keepdims=True)
    rank2 = jnp.sum(jnp.where(sel2, rank, 0.0), axis=0, keepdims=True)
    for r, row in enumerate((i1, i2, w1, w2, rank1, rank2)):
        meta_ref[r, 0] = row
    carry_scr[...] = carry_scr[...] + jnp.sum(onehot, axis=1, keepdims=True)
    cnt_ref[...] = carry_scr[...]


def _route_call(x, modl, w_router_t, rbias_col):
    B, S, D = x.shape
    tm = ROUTE_TILE
    nt = S // tm
    n_tiles = B * nt
    return pl.pallas_call(
        _route_kernel,
        grid=(n_tiles,),
        in_specs=[
            pl.BlockSpec((1, tm, D), lambda t: (t // nt, t % nt, 0)),
            pl.BlockSpec((1, N_MOD, D), lambda t: (t // nt, 0, 0)),
            pl.BlockSpec((N_EXPERTS, D), lambda t: (0, 0)),
            pl.BlockSpec((N_EXPERTS, 1), lambda t: (0, 0)),
        ],
        out_specs=[
            pl.BlockSpec((tm, D // 2), lambda t: (t, 0)),
            pl.BlockSpec((6, 1, 1, tm), lambda t: (0, t, 0, 0)),
            pl.BlockSpec((N_EXPERTS, LANES), lambda t: (0, 0)),
        ],
        out_shape=[
            jax.ShapeDtypeStruct((B * S, D // 2), jnp.uint32),
            jax.ShapeDtypeStruct((6, n_tiles, 1, tm), jnp.float32),
            jax.ShapeDtypeStruct((N_EXPERTS, LANES), jnp.float32),
        ],
        scratch_shapes=[pltpu.VMEM((N_EXPERTS, LANES), jnp.float32)],
        compiler_params=_params(("arbitrary",)),
    )(x, modl, w_router_t, rbias_col)


def _dispatch_plan(meta, counts, n_tokens):
    n_rows = 2 * n_tokens + N_EXPERTS * EXPERT_CHUNK
    n_chunks = n_rows // EXPERT_CHUNK
    per_token = lambda r: meta[r].reshape(n_tokens)
    e1 = per_token(0).astype(jnp.int32)
    e2 = per_token(1).astype(jnp.int32)
    w1 = per_token(2)
    w2 = per_token(3)
    cnt = counts[:, 0].astype(jnp.int32)
    padded = (cnt + (EXPERT_CHUNK - 1)) // EXPERT_CHUNK * EXPERT_CHUNK
    ends = jnp.cumsum(padded)
    starts = ends - padded
    first = jnp.cumsum(cnt) - cnt
    pos1 = starts[e1] + per_token(4).astype(jnp.int32)
    pos2 = starts[e2] + per_token(5).astype(jnp.int32)
    order = jnp.argsort(jnp.concatenate([pos1, pos2])).astype(jnp.int32)
    sorted_tok = jnp.where(order >= n_tokens, order - n_tokens, order)
    chunk_row0 = jnp.arange(n_chunks, dtype=jnp.int32) * EXPERT_CHUNK
    chunk_expert = jnp.minimum(
        jnp.sum((ends[None, :] <= chunk_row0[:, None]).astype(jnp.int32), axis=1), N_EXPERTS - 1)
    local = (chunk_row0 - starts[chunk_expert])[:, None] + jnp.arange(EXPERT_CHUNK, dtype=jnp.int32)
    real = local < cnt[chunk_expert][:, None]
    nth = jnp.clip(first[chunk_expert][:, None] + local, 0, 2 * n_tokens - 1)
    src = jnp.where(real, sorted_tok[nth], 0).reshape(n_rows)
    n_used = (ends[-1] // EXPERT_CHUNK).reshape(1).astype(jnp.int32)
    return pos1, pos2, w1, w2, src, chunk_expert, n_used


def _expert_kernel(ce_ref, nu_ref, xs_ref, wg_ref, wu_ref, wd_ref, o_ref):
    c = pl.program_id(0)

    @pl.when(c < nu_ref[0])
    def _():
        w = xs_ref[...]
        lo = pltpu.bitcast(lax.shift_left(w, jnp.uint32(16)), jnp.float32)
        hi = pltpu.bitcast(w & jnp.uint32(0xFFFF0000), jnp.float32)
        xs = jnp.concatenate([lo, hi], axis=1).astype(jnp.bfloat16)
        gt = jnp.dot(xs, wg_ref[0], preferred_element_type=jnp.float32)
        up = jnp.dot(xs, wu_ref[0], preferred_element_type=jnp.float32)
        act = (gt * (1.0 / (1.0 + jnp.exp(-gt))) * up).astype(jnp.bfloat16)
        o_ref[...] = jnp.dot(act, wd_ref[0], preferred_element_type=jnp.float32)

    @pl.when(c >= nu_ref[0])
    def _():
        o_ref[...] = jnp.zeros_like(o_ref)


def _expert_call(chunk_expert, n_used, xs, wg_bf, wu_bf, wd_bf):
    n_rows, half = xs.shape
    D = 2 * half
    grid_spec = pltpu.PrefetchScalarGridSpec(
        num_scalar_prefetch=2,
        grid=(n_rows // EXPERT_CHUNK,),
        in_specs=[
            pl.BlockSpec((EXPERT_CHUNK, half), lambda c, ce, nu: (c, 0)),
            pl.BlockSpec((1, D, D_EXPERT), lambda c, ce, nu: (ce[c], 0, 0)),
            pl.BlockSpec((1, D, D_EXPERT), lambda c, ce, nu: (ce[c], 0, 0)),
            pl.BlockSpec((1, D_EXPERT, D), lambda c, ce, nu: (ce[c], 0, 0)),
        ],
        out_specs=pl.BlockSpec((EXPERT_CHUNK, D), lambda c, ce, nu: (c, 0)),
    )
    return pl.pallas_call(
        _expert_kernel,
        grid_spec=grid_spec,
        out_shape=jax.ShapeDtypeStruct((n_rows, D), jnp.float32),
        compiler_params=_params(("arbitrary",)),
    )(chunk_expert, n_used, xs, wg_bf, wu_bf, wd_bf)


def _combine_kernel(y1_ref, y2_ref, w1_ref, w2_ref, x_ref, mod_ref, g_ref, b_ref, o_ref):
    y = w1_ref[0] * y1_ref[0] + w2_ref[0] * y2_ref[0]
    gate = mod_ref[0, 5:6, :]
    z = DEEPNORM_ALPHA * x_ref[0] + gate * y
    o_ref[0] = _layer_norm(z) * g_ref[...] + b_ref[...]


def _combine_call(y1, y2, w1, w2, x, modl, ln_g, ln_b):
    B, S, D = x.shape
    tm = 512
    rows = pl.BlockSpec((1, tm, D), lambda b, i: (b, i, 0))
    col = pl.BlockSpec((1, tm, 1), lambda b, i: (b, i, 0))
    vec = pl.BlockSpec((1, D), lambda b, i: (0, 0))
    return pl.pallas_call(
        _combine_kernel,
        grid=(B, S // tm),
        in_specs=[rows, rows, col, col, rows,
                  pl.BlockSpec((1, N_MOD, D), lambda b, i: (b, 0, 0)), vec, vec],
        out_specs=rows,
        out_shape=jax.ShapeDtypeStruct((B, S, D), jnp.float32),
        compiler_params=_params(("arbitrary", "arbitrary")),
    )(y1.reshape(B, S, D), y2.reshape(B, S, D), w1.reshape(B, S, 1), w2.reshape(B, S, 1),
      x, modl, ln_g.reshape(1, D), ln_b.reshape(1, D))


def _moe_routed(x, modl, w_router_t, rbias_col, wg_bf, wu_bf, wd_bf, ln_g, ln_b):
    B, S, D = x.shape
    hp, meta, counts = _route_call(x, modl, w_router_t, rbias_col)
    pos1, pos2, w1, w2, src, chunk_expert, n_used = _dispatch_plan(meta, counts, B * S)
    xs = _sc_gather_rows(hp, src)
    ys = _expert_call(chunk_expert, n_used, xs, wg_bf, wu_bf, wd_bf)
    y1 = _sc_gather_rows(ys, pos1)
    y2 = _sc_gather_rows(ys, pos2)
    return _combine_call(y1, y2, w1, w2, x, modl, ln_g, ln_b)


def _rope_tables(S):
    half = ROT_DIM // 2
    inv_freq = ROPE_THETA ** (-(jnp.arange(half, dtype=jnp.float32) * 2.0 / ROT_DIM))
    ang = jnp.arange(S, dtype=jnp.float32)[:, None] * inv_freq[None, :]
    cos, sin = jnp.cos(ang), jnp.sin(ang)
    d = jnp.arange(LANES) % HEAD_DIM
    f = d % half
    rot = d[None, :] < ROT_DIM
    return jnp.where(rot, cos[:, f], 1.0), jnp.where(rot, sin[:, f], 0.0)


def _with_rotary_partner_columns(w_in_l):
    half = ROT_DIM // 2
    c = jnp.arange(2 * D_ATTN)
    d = c % HEAD_DIM
    src = jnp.where(d < half, c + half, c - half)
    sign = jnp.where(d < half, -1.0, jnp.where(d < ROT_DIM, 1.0, 0.0))
    partner = w_in_l[:, src] * sign[None, :]
    return jnp.concatenate([w_in_l, partner], axis=1)


def _bias_placement():
    src = jnp.arange(LANES)
    hd, u, j = src >> 4, (src >> 3) & 1, src & 7
    col0 = jnp.where(hd % 2 == 0, HEAD_DIM, 0)
    dst = LANES * hd + col0 + j
    onehot = (jnp.arange(N_HEADS * LANES)[None, :] == dst[:, None]) & (u[:, None] == 1)
    return onehot.astype(jnp.bfloat16)


def kernel(x, c, w_mod, b_mod, w_in, w_pool, pool_scale, w_out, ln1_g, ln1_b,
           w_router, router_bias, w_gate, w_up, w_down, ln2_g, ln2_b):
    B, S, D = x.shape
    bf = jnp.bfloat16
    mod = _mod_call(c, w_mod, b_mod).reshape(DEPTH, B, N_MOD, D)
    cos_t, sin_t = _rope_tables(S)
    place = _bias_placement()
    w_router_t = w_router.T
    rbias_col = router_bias.reshape(N_EXPERTS, 1)
    for l in range(DEPTH):
        modl = mod[l]
        w_in_aug = _with_rotary_partner_columns(w_in[l]).astype(bf)
        qa, ka, v, p, _ = _proj_call(x, modl, w_in_aug, cos_t, sin_t, place)
        a = _attn_call(qa, ka, v)
        m = _pool_call(p, w_pool[l], pool_scale[l])
        x = _mixout_call(a, m, x, modl, w_out[l].astype(bf), ln1_g[l], ln1_b[l])
        x = _moe_routed(x, modl, w_router_t, rbias_col, w_gate[l].astype(bf),
                        w_up[l].astype(bf), w_down[l].astype(bf), ln2_g[l], ln2_b[l])
    return x
```

```python
import functools

import jax
import jax.numpy as jnp
from jax import lax
from jax.experimental import pallas as pl
from jax.experimental.pallas import tpu as pltpu
from jax.experimental.pallas import tpu_sc as plsc

D_MODEL = 1024
DEPTH = 2
D_ATTN = 512
D_POOL = 512
N_HEADS = 8
HEAD_DIM = 64
ROT_DIM = 16
ROPE_THETA = 500000.0
MOBA_BLOCK = 256
MOBA_TOPK = 3
POOL_GROUP = 128
N_POOL_GROUPS = 4
D_IN = 3 * D_ATTN + D_POOL
N_EXPERTS = 16
EXPERTS_PER_GROUP = 4
D_EXPERT = 512
DEEPNORM_ALPHA = (2 * DEPTH) ** 0.25
N_MOD = 6
LN_EPS = 1e-5
NEG_INF = -1e30

LANES = 128
VMEM_LIMIT = 56 * 1024 * 1024

_HI = lax.Precision.HIGHEST
_NT = (((1,), (1,)), ((), ()))


def _params(sem):
    return pltpu.CompilerParams(dimension_semantics=sem, vmem_limit_bytes=VMEM_LIMIT)


def _nt_dot(a, b):
    return lax.dot_general(a, b, _NT, preferred_element_type=jnp.float32)


def _mod_kernel(c_ref, w_ref, b_ref, o_ref):
    c = c_ref[...]
    cond = c * (1.0 / (1.0 + jnp.exp(-c)))
    o_ref[0] = jnp.dot(cond, w_ref[0], precision=_HI,
                       preferred_element_type=jnp.float32) + b_ref[0]


def _mod_call(c, w_mod, b_mod):
    B = c.shape[0]
    return pl.pallas_call(
        _mod_kernel,
        grid=(DEPTH, N_MOD),
        in_specs=[
            pl.BlockSpec((B, D_MODEL), lambda l, j: (0, 0)),
            pl.BlockSpec((1, D_MODEL, D_MODEL), lambda l, j: (l, 0, j)),
            pl.BlockSpec((1, 1, D_MODEL), lambda l, j: (l, 0, j)),
        ],
        out_specs=pl.BlockSpec((1, B, D_MODEL), lambda l, j: (l, 0, j)),
        out_shape=jax.ShapeDtypeStruct((DEPTH, B, N_MOD * D_MODEL), jnp.float32),
        compiler_params=_params(("arbitrary", "arbitrary")),
    )(c, w_mod, b_mod.reshape(DEPTH, 1, N_MOD * D_MODEL))


def _layer_norm(x):
    mu = jnp.mean(x, axis=-1, keepdims=True)
    xc = x - mu
    var = jnp.mean(xc * xc, axis=-1, keepdims=True)
    return xc * lax.rsqrt(var + LN_EPS)


def _split_bf16(t):
    hi = t.astype(jnp.bfloat16)
    lo = (t - hi.astype(jnp.float32)).astype(jnp.bfloat16)
    return hi, lo


def _proj_kernel(x_ref, mod_ref, w_ref, cos_ref, sin_ref, place_ref,
                 qa_ref, ka_ref, v_ref, p_ref, kbar_ref):
    i = pl.program_id(1)

    @pl.when(i == 0)
    def _():
        kbar_ref[...] = jnp.zeros_like(kbar_ref)

    x = x_ref[0]
    shift = mod_ref[0, 0:1, :]
    scale = mod_ref[0, 1:2, :]
    h = (_layer_norm(x) * (1.0 + scale) + shift).astype(jnp.bfloat16)

    cos = cos_ref[...]
    sin = sin_ref[...]

    def proj(c0, width):
        return jnp.dot(h, w_ref[:, c0:c0 + width], preferred_element_type=jnp.float32)

    def slab(t, s):
        return t[:, LANES * s:LANES * (s + 1)]

    n_slab = D_ATTN // LANES
    q = proj(0, D_ATTN)
    qp = proj(D_IN, D_ATTN)
    q_slabs = [(slab(q, s) * cos + slab(qp, s) * sin) * (HEAD_DIM ** -0.5)
               for s in range(n_slab)]
    k = proj(D_ATTN, D_ATTN)
    kp = proj(D_IN + D_ATTN, D_ATTN)
    k_slabs = [slab(k, s) * cos + slab(kp, s) * sin for s in range(n_slab)]
    v_ref[0] = proj(2 * D_ATTN, D_ATTN).astype(jnp.bfloat16)
    p_ref[0] = proj(3 * D_ATTN, D_POOL)

    kmean = jnp.concatenate(
        [jnp.mean(ks, axis=0, keepdims=True) for ks in k_slabs], axis=1)
    kbar_ref[0, pl.ds(i, 1), :] = kmean

    kb = kbar_ref[0]
    kb_rows = jnp.concatenate([kb] * (LANES // 8), axis=0)
    r_head = lax.broadcasted_iota(jnp.int32, (LANES, D_ATTN), 1) >> 6
    c_head = lax.broadcasted_iota(jnp.int32, (LANES, D_ATTN), 0) >> 4
    kbd_hi, kbd_lo = _split_bf16(jnp.where(r_head == c_head, kb_rows, 0.0))
    q_hi, q_lo = _split_bf16(jnp.concatenate(q_slabs, axis=1))
    gate = _nt_dot(q_hi, kbd_hi) + (_nt_dot(q_lo, kbd_hi) + _nt_dot(q_hi, kbd_lo))

    lane = lax.broadcasted_iota(jnp.int32, (MOBA_BLOCK, LANES), 1)
    j_of = lane & 7
    past = j_of < i
    gm = jnp.where(past, gate, NEG_INF)
    rank = jnp.zeros((MOBA_BLOCK, LANES), jnp.int32)
    for r in range(1, 8):
        other = pltpu.roll(gm, r, 1)
        beats = (other > gm) | ((other == gm) & (j_of >= r))
        rank = rank + beats.astype(jnp.int32)
    allowed = (past & (rank < MOBA_TOPK)) | (j_of == i)
    bias = jnp.where(allowed, 0.0, NEG_INF).astype(jnp.bfloat16)
    bias_cols = jnp.dot(bias, place_ref[...], preferred_element_type=jnp.float32)

    for hd in range(N_HEADS):
        own = (lane < HEAD_DIM) if hd % 2 == 0 else (lane >= HEAD_DIM)
        col0 = HEAD_DIM if hd % 2 == 0 else 0
        qa = jnp.where(own, q_slabs[hd // 2], slab(bias_cols, hd))
        ka = jnp.where(own, k_slabs[hd // 2], jnp.where(lane == col0 + i, 1.0, 0.0))
        qa_ref[0, hd] = qa.astype(jnp.bfloat16)
        ka_ref[0, hd] = ka.astype(jnp.bfloat16)


def _proj_call(x, modl, w_in_aug, cos_t, sin_t, place):
    B, S, D = x.shape
    nb = S // MOBA_BLOCK
    tm = MOBA_BLOCK
    tab = pl.BlockSpec((tm, LANES), lambda b, i: (i, 0))
    head_spec = pl.BlockSpec((1, N_HEADS, tm, LANES), lambda b, i: (b, 0, i, 0))
    return pl.pallas_call(
        _proj_kernel,
        grid=(B, nb),
        in_specs=[
            pl.BlockSpec((1, tm, D), lambda b, i: (b, i, 0)),
            pl.BlockSpec((1, N_MOD, D), lambda b, i: (b, 0, 0)),
            pl.BlockSpec((D, D_IN + 2 * D_ATTN), lambda b, i: (0, 0)),
            tab, tab,
            pl.BlockSpec((LANES, N_HEADS * LANES), lambda b, i: (0, 0)),
        ],
        out_specs=[
            head_spec, head_spec,
            pl.BlockSpec((1, tm, D_ATTN), lambda b, i: (b, i, 0)),
            pl.BlockSpec((1, tm, D_POOL), lambda b, i: (b, i, 0)),
            pl.BlockSpec((1, nb, D_ATTN), lambda b, i: (b, 0, 0)),
        ],
        out_shape=[
            jax.ShapeDtypeStruct((B, N_HEADS, S, LANES), jnp.bfloat16),
            jax.ShapeDtypeStruct((B, N_HEADS, S, LANES), jnp.bfloat16),
            jax.ShapeDtypeStruct((B, S, D_ATTN), jnp.bfloat16),
            jax.ShapeDtypeStruct((B, S, D_POOL), jnp.float32),
            jax.ShapeDtypeStruct((B, nb, D_ATTN), jnp.float32),
        ],
        compiler_params=_params(("arbitrary", "arbitrary")),
    )(x, modl, w_in_aug, cos_t, sin_t, place)


def _attn_kernel(qa_ref, ka_ref, v_ref, o_ref):
    nb = v_ref.shape[1] // MOBA_BLOCK
    row = lax.broadcasted_iota(jnp.int32, (MOBA_BLOCK, MOBA_BLOCK), 0)
    col = lax.broadcasted_iota(jnp.int32, (MOBA_BLOCK, MOBA_BLOCK), 1)
    causal = col <= row
    lane = lax.broadcasted_iota(jnp.int32, (MOBA_BLOCK, LANES), 1)

    for i in reversed(range(nb)):
        r0 = i * MOBA_BLOCK
        outs = []
        for hh in range(2):
            q = qa_ref[0, hh, r0:r0 + MOBA_BLOCK, :]
            s_own = jnp.where(causal, _nt_dot(q, ka_ref[0, hh, r0:r0 + MOBA_BLOCK, :]), NEG_INF)
            m = jnp.max(s_own, axis=1, keepdims=True)
            if i > 0:
                s_past = _nt_dot(q, ka_ref[0, hh, 0:r0, :])
                m = jnp.maximum(m, jnp.max(s_past, axis=1, keepdims=True))
            p_own = jnp.exp(s_own - m)
            l = jnp.sum(p_own, axis=1, keepdims=True)
            acc = jnp.dot(p_own.astype(jnp.bfloat16), v_ref[0, r0:r0 + MOBA_BLOCK, :],
                          preferred_element_type=jnp.float32)
            if i > 0:
                p_past = jnp.exp(s_past - m)
                l = l + jnp.sum(p_past, axis=1, keepdims=True)
                acc = acc + jnp.dot(p_past.astype(jnp.bfloat16), v_ref[0, 0:r0, :],
                                    preferred_element_type=jnp.float32)
            outs.append(acc / l)
        o = jnp.where(lane < HEAD_DIM, outs[0], outs[1])
        o_ref[0, r0:r0 + MOBA_BLOCK, :] = o.astype(o_ref.dtype)


def _attn_call(qa, ka, v):
    B, _, S, _ = qa.shape
    n_pair = N_HEADS // 2
    pair_spec = pl.BlockSpec((1, 2, S, LANES), lambda b, hp: (b, hp, 0, 0))
    slab_spec = pl.BlockSpec((1, S, LANES), lambda b, hp: (b, 0, hp))
    return pl.pallas_call(
        _attn_kernel,
        grid=(B, n_pair),
        in_specs=[pair_spec, pair_spec, slab_spec],
        out_specs=slab_spec,
        out_shape=jax.ShapeDtypeStruct((B, S, D_ATTN), jnp.bfloat16),
        compiler_params=_params(("arbitrary", "arbitrary")),
    )(qa, ka, v)


def _pool_kernel(p_ref, w_ref, sc_ref, o_ref):
    g = pl.program_id(1)
    S = p_ref.shape[1]
    p = p_ref[0]
    t = lax.broadcasted_iota(jnp.int32, (S, LANES), 0)

    def shifted(x, k):
        return jnp.where(t >= k, pltpu.roll(x, k, 0), 0.0)

    win = p
    sums = []
    for step in range(N_POOL_GROUPS):
        win = win + shifted(win, 1 << step)
        sums.append(win)
    wsum = jnp.where(g == 0, sums[0],
                     jnp.where(g == 1, sums[1], jnp.where(g == 2, sums[2], sums[3])))
    window = jnp.left_shift(2, g)
    cnt = jnp.minimum(t + 1, window).astype(jnp.float32)
    d = (wsum / cnt - p).astype(jnp.bfloat16)
    y = jnp.dot(d, w_ref[0].astype(jnp.bfloat16), preferred_element_type=jnp.float32)
    o_ref[0] = (y * sc_ref[...]).astype(o_ref.dtype)


def _pool_call(p, w_pool_l, pool_scale_l):
    B, S, _ = p.shape
    slab = pl.BlockSpec((1, S, LANES), lambda b, g: (b, 0, g))
    return pl.pallas_call(
        _pool_kernel,
        grid=(B, N_POOL_GROUPS),
        in_specs=[
            slab,
            pl.BlockSpec((1, POOL_GROUP, POOL_GROUP), lambda b, g: (g, 0, 0)),
            pl.BlockSpec((1, LANES), lambda b, g: (0, g)),
        ],
        out_specs=slab,
        out_shape=jax.ShapeDtypeStruct((B, S, D_POOL), jnp.bfloat16),
        compiler_params=_params(("arbitrary", "arbitrary")),
    )(p, w_pool_l, pool_scale_l.reshape(1, D_POOL))


def _mixout_kernel(a_ref, m_ref, x_ref, mod_ref, w_ref, g_ref, b_ref, o_ref):
    y = jnp.dot(a_ref[0], w_ref[0:D_ATTN, :], preferred_element_type=jnp.float32)
    y = y + jnp.dot(m_ref[0], w_ref[D_ATTN:, :], preferred_element_type=jnp.float32)
    gate = mod_ref[0, 2:3, :]
    z = DEEPNORM_ALPHA * x_ref[0] + gate * y
    o_ref[0] = _layer_norm(z) * g_ref[...] + b_ref[...]


def _mixout_call(a, m, x, modl, w_out_bf, ln_g, ln_b):
    B, S, D = x.shape
    tm = 512
    vec = pl.BlockSpec((1, D), lambda b, i: (0, 0))
    return pl.pallas_call(
        _mixout_kernel,
        grid=(B, S // tm),
        in_specs=[
            pl.BlockSpec((1, tm, D_ATTN), lambda b, i: (b, i, 0)),
            pl.BlockSpec((1, tm, D_POOL), lambda b, i: (b, i, 0)),
            pl.BlockSpec((1, tm, D), lambda b, i: (b, i, 0)),
            pl.BlockSpec((1, N_MOD, D), lambda b, i: (b, 0, 0)),
            pl.BlockSpec((D, D), lambda b, i: (0, 0)),
            vec, vec,
        ],
        out_specs=pl.BlockSpec((1, tm, D), lambda b, i: (b, i, 0)),
        out_shape=jax.ShapeDtypeStruct((B, S, D), jnp.float32),
        compiler_params=_params(("arbitrary", "arbitrary")),
    )(a, m, x, modl, w_out_bf, ln_g.reshape(1, D), ln_b.reshape(1, D))


MOE_TILE = 1024
MOE_CHUNK = 160
MOE_ROWS = 2304
MOE_ROWS_ALLOC = MOE_ROWS + MOE_CHUNK
MOE_G_COLS = 768
MOE_EXPERTS_PER_STEP = 2


def _top2_rows(vals):
    def first_max(rows):
        m = rows[0]
        for v in rows[1:]:
            m = jnp.maximum(m, v)
        idx = jnp.full_like(m, float(len(rows) - 1))
        for k in range(len(rows) - 2, -1, -1):
            idx = jnp.where(rows[k] == m, float(k), idx)
        return m, idx

    m1, i1 = first_max(vals)
    rest = [jnp.where(i1 == float(k), -jnp.inf, v) for k, v in enumerate(vals)]
    m2, i2 = first_max(rest)
    return m1, i1, m2, i2


def _router_rows(logits_t, rb_ref):
    lg = [logits_t[e:e + 1, :] for e in range(N_EXPERTS)]
    mx = lg[0]
    for v in lg[1:]:
        mx = jnp.maximum(mx, v)
    ex = [jnp.exp(v - mx) for v in lg]
    den = ex[0]
    for v in ex[1:]:
        den = den + v
    scores = [v / den for v in ex]
    sel = [scores[e] + rb_ref[e:e + 1, :] for e in range(N_EXPERTS)]
    best_score = None
    best = None
    for g in range(N_EXPERTS // EXPERTS_PER_GROUP):
        m1, _, m2, _ = _top2_rows(sel[g * EXPERTS_PER_GROUP:(g + 1) * EXPERTS_PER_GROUP])
        gs = m1 + m2
        if g == 0:
            best_score, best = gs, jnp.zeros_like(gs)
        else:
            better = gs > best_score
            best_score = jnp.where(better, gs, best_score)
            best = jnp.where(better, float(g), best)
    masked = [jnp.where(best == float(e // EXPERTS_PER_GROUP), sel[e], NEG_INF)
              for e in range(N_EXPERTS)]
    _, i1, _, i2 = _top2_rows(masked)
    w1 = jnp.zeros_like(i1)
    w2 = jnp.zeros_like(i2)
    for e in range(N_EXPERTS):
        w1 = jnp.where(i1 == float(e), scores[e], w1)
        w2 = jnp.where(i2 == float(e), scores[e], w2)
    tot = w1 + w2
    return i1, i2, w1 / tot, w2 / tot


def _moe_route(x_ref, mod_ref, wr_ref, rb_ref, h_scr, col_scr, row_scr, ysh_scr, ysl_scr,
               start_smem, cnt_smem):
    T = MOE_TILE
    shift = mod_ref[0, 3:4, :]
    scale = mod_ref[0, 4:5, :]
    h = _layer_norm(x_ref[0]) * (1.0 + scale) + shift
    h_hi = h.astype(jnp.bfloat16)
    h_scr[...] = h_hi
    h_lo = (h - h_hi.astype(jnp.float32)).astype(jnp.bfloat16)
    wr_hi, wr_lo = _split_bf16(wr_ref[...])
    logits_t = _nt_dot(wr_hi, h_hi) + (_nt_dot(wr_lo, h_hi) + _nt_dot(wr_hi, h_lo))
    i1, i2, w1, w2 = _router_rows(logits_t, rb_ref)

    e_iota = lax.broadcasted_iota(jnp.int32, (N_EXPERTS, T), 0).astype(jnp.float32)
    sel1 = e_iota == i1
    sel2 = e_iota == i2
    onehot = jnp.where(sel1 | sel2, 1.0, 0.0)
    tr = lax.broadcasted_iota(jnp.int32, (T, T), 0)
    tc = lax.broadcasted_iota(jnp.int32, (T, T), 1)
    earlier = jnp.where(tr < tc, 1.0, 0.0).astype(jnp.bfloat16)
    excl = jnp.dot(onehot.astype(jnp.bfloat16), earlier,
                   preferred_element_type=jnp.float32)
    cnt = jnp.sum(onehot, axis=1, keepdims=True)

    e_col = lax.broadcasted_iota(jnp.int32, (N_EXPERTS, 1), 0)
    start_v = jnp.zeros((N_EXPERTS, 1), jnp.float32)
    run = jnp.int32(0)
    for ex in range(N_EXPERTS):
        padded = lax.shift_left(lax.shift_right_logical(cnt[ex, 0].astype(jnp.int32) + 15, 4), 4)
        start_smem[ex] = run
        cnt_smem[ex] = padded
        start_v = jnp.where(e_col == ex, run.astype(jnp.float32), start_v)
        run = run + padded

    slot = excl + start_v
    pos1 = jnp.sum(jnp.where(sel1, slot, 0.0), axis=0, keepdims=True)
    pos2 = jnp.sum(jnp.where(sel2, slot, 0.0), axis=0, keepdims=True)
    row_scr[0:1, :] = pos1
    row_scr[1:2, :] = pos2
    row_scr[2:3, :] = w1
    row_scr[3:4, :] = w2

    r128 = lax.broadcasted_iota(jnp.int32, (LANES, T), 0)
    terms = jnp.zeros((LANES, T), jnp.float32)
    k = 0
    for p in (pos1, pos2):
        a = jnp.floor(p * (1.0 / 64.0))
        for t in (a, p - 64.0 * a):
            terms = jnp.where(r128 == k, t, terms)
            k += 1
    eye = jnp.where(tr == tc, 1.0, 0.0).astype(jnp.bfloat16)
    col_scr[...] = _nt_dot(eye, terms.astype(jnp.bfloat16))

    tail = MOE_ROWS_ALLOC - 2 * T
    ysh_scr[2 * T:, :] = jnp.zeros((tail, D_MODEL), jnp.bfloat16)
    ysl_scr[2 * T:, :] = jnp.zeros((tail, D_MODEL), jnp.bfloat16)


def _moe_kernel(x_ref, mod_ref, wr_ref, rb_ref, wg_ref, wu_ref, wd_ref, g_ref, b_ref,
                o_ref, h_scr, col_scr, row_scr, ysh_scr, ysl_scr, start_smem, cnt_smem):
    step = pl.program_id(1)
    T = MOE_TILE

    @pl.when(step == 0)
    def _():
        _moe_route(x_ref, mod_ref, wr_ref, rb_ref, h_scr, col_scr, row_scr, ysh_scr, ysl_scr,
                   start_smem, cnt_smem)

    pos1_row = row_scr[0:1, :]
    pos2_row = row_scr[1:2, :]
    w1_row = row_scr[2:3, :]
    w2_row = row_scr[3:4, :]

    for k in range(MOE_EXPERTS_PER_STEP):
        e = step * MOE_EXPERTS_PER_STEP + k
        seg_start = start_smem[e]
        n_chunk = (cnt_smem[e] + (MOE_CHUNK - 1)) // MOE_CHUNK

        def chunk(j, carry, k=k, seg_start=seg_start):
            base = pl.multiple_of(seg_start + j * MOE_CHUNK, 16)
            r = (base + lax.broadcasted_iota(jnp.int32, (MOE_CHUNK, T), 0)).astype(jnp.float32)
            m1 = r == pos1_row
            m2 = r == pos2_row
            gather = jnp.where(m1 | m2, 1.0, 0.0).astype(jnp.bfloat16)
            w_row = jnp.sum(jnp.where(m1, w1_row, 0.0) + jnp.where(m2, w2_row, 0.0),
                            axis=1, keepdims=True)
            xs = jnp.dot(gather, h_scr[...],
                         preferred_element_type=jnp.float32).astype(jnp.bfloat16)
            gt = jnp.dot(xs, wg_ref[k], preferred_element_type=jnp.float32)
            up = jnp.dot(xs, wu_ref[k], preferred_element_type=jnp.float32)
            act = (gt * (1.0 / (1.0 + jnp.exp(-gt))) * up).astype(jnp.bfloat16)
            y = w_row * jnp.dot(act, wd_ref[k], preferred_element_type=jnp.float32)
            y_hi = y.astype(jnp.bfloat16)
            ysh_scr[pl.ds(base, MOE_CHUNK), :] = y_hi
            ysl_scr[pl.ds(base, MOE_CHUNK), :] = (y - y_hi.astype(jnp.float32)).astype(jnp.bfloat16)
            return carry

        lax.fori_loop(0, n_chunk, chunk, 0)

    @pl.when(step == N_EXPERTS // MOE_EXPERTS_PER_STEP - 1)
    def _():
        pos1 = 64.0 * col_scr[:, 0:1] + col_scr[:, 1:2]
        pos2 = 64.0 * col_scr[:, 2:3] + col_scr[:, 3:4]
        y = jnp.zeros((T, D_MODEL), jnp.float32)
        for c0 in range(0, MOE_ROWS, MOE_G_COLS):
            r = (c0 + lax.broadcasted_iota(jnp.int32, (T, MOE_G_COLS), 1)).astype(jnp.float32)
            scatter = jnp.where((r == pos1) | (r == pos2), 1.0, 0.0).astype(jnp.bfloat16)
            y = y + jnp.dot(scatter, ysh_scr[c0:c0 + MOE_G_COLS, :],
                            preferred_element_type=jnp.float32)
            y = y + jnp.dot(scatter, ysl_scr[c0:c0 + MOE_G_COLS, :],
                            preferred_element_type=jnp.float32)
        gate = mod_ref[0, 5:6, :]
        z = DEEPNORM_ALPHA * x_ref[0] + gate * y
        o_ref[0] = _layer_norm(z) * g_ref[...] + b_ref[...]


def _moe_call(x, modl, w_router_t, rbias_col, wg_bf, wu_bf, wd_bf, ln_g, ln_b):
    B, S, D = x.shape
    tm = MOE_TILE
    nt = S // tm
    eps = MOE_EXPERTS_PER_STEP
    vec = pl.BlockSpec((1, D), lambda t, e: (0, 0))
    xspec = pl.BlockSpec((1, tm, D), lambda t, e: (t // nt, t % nt, 0))
    return pl.pallas_call(
        _moe_kernel,
        grid=(B * nt, N_EXPERTS // eps),
        in_specs=[
            xspec,
            pl.BlockSpec((1, N_MOD, D), lambda t, e: (t // nt, 0, 0)),
            pl.BlockSpec((N_EXPERTS, D), lambda t, e: (0, 0)),
            pl.BlockSpec((N_EXPERTS, 1), lambda t, e: (0, 0)),
            pl.BlockSpec((eps, D, D_EXPERT), lambda t, e: (e, 0, 0)),
            pl.BlockSpec((eps, D, D_EXPERT), lambda t, e: (e, 0, 0)),
            pl.BlockSpec((eps, D_EXPERT, D), lambda t, e: (e, 0, 0)),
            vec, vec,
        ],
        out_specs=xspec,
        out_shape=jax.ShapeDtypeStruct((B, S, D), jnp.float32),
        scratch_shapes=[
            pltpu.VMEM((tm, D), jnp.bfloat16),
            pltpu.VMEM((tm, LANES), jnp.float32),
            pltpu.VMEM((8, tm), jnp.float32),
            pltpu.VMEM((MOE_ROWS_ALLOC, D), jnp.bfloat16),
            pltpu.VMEM((MOE_ROWS_ALLOC, D), jnp.bfloat16),
            pltpu.SMEM((N_EXPERTS,), jnp.int32),
            pltpu.SMEM((N_EXPERTS,), jnp.int32),
        ],
        compiler_params=_params(("arbitrary", "arbitrary")),
    )(x, modl, w_router_t, rbias_col, wg_bf, wu_bf, wd_bf,
      ln_g.reshape(1, D), ln_b.reshape(1, D))


SC_CORES = 2
SC_SUBCORES = 16
SC_MAX_INDEX_ROWS = 128
SC_RING = 4
SC_TILE_BYTES = 64 * 1024

ROUTE_TILE = 1024
EXPERT_CHUNK = 512


def _sc_gather_rows(table, idx):
    n_rows = idx.shape[0]
    width = table.shape[1]
    workers = SC_CORES * SC_SUBCORES
    rows = min(SC_MAX_INDEX_ROWS, SC_TILE_BYTES // (4 * width))
    per_worker = n_rows // workers
    assert per_worker * workers == n_rows and per_worker % rows == 0
    n_chunk = per_worker // rows
    assert n_chunk % SC_RING == 0
    mesh = plsc.VectorSubcoreMesh(core_axis_name="c", subcore_axis_name="s")
    row_buf = pltpu.VMEM((rows, width), table.dtype)
    idx_buf = pltpu.VMEM((rows,), jnp.int32)

    @functools.partial(
        pl.kernel, mesh=mesh,
        out_type=jax.ShapeDtypeStruct((n_rows, width), table.dtype),
        scratch_types=([idx_buf] * SC_RING + [row_buf] * SC_RING
                       + [pltpu.SemaphoreType.DMA] * SC_RING),
    )
    def gather(table_hbm, idx_hbm, out_hbm, *scratch):
        idx_v = scratch[:SC_RING]
        rows_v = scratch[SC_RING:2 * SC_RING]
        sems = scratch[2 * SC_RING:]
        wid = lax.axis_index("s") * SC_CORES + lax.axis_index("c")
        base = wid * per_worker

        def gather_copy(b):
            return pltpu.make_async_copy(table_hbm.at[idx_v[b]], rows_v[b], sems[b])

        def start_gather(j, b):
            off = pl.multiple_of(base + j * rows, 8)
            pltpu.sync_copy(idx_hbm.at[pl.ds(off, rows)], idx_v[b])
            gather_copy(b).start()

        for b in range(SC_RING):
            start_gather(b, b)

        def group(g, carry):
            for b in range(SC_RING):
                j = g * SC_RING + b
                gather_copy(b).wait()
                off = pl.multiple_of(base + j * rows, 8)
                pltpu.sync_copy(rows_v[b], out_hbm.at[pl.ds(off, rows)])

                @pl.when(j + SC_RING < n_chunk)
                def _():
                    start_gather(j + SC_RING, b)
            return carry

        lax.fori_loop(0, n_chunk // SC_RING, group, 0)

    return gather(table, idx)


def _route_kernel(x_ref, mod_ref, wr_ref, rb_ref, hp_ref, meta_ref, cnt_ref, carry_scr):
    t = pl.program_id(0)
    T = ROUTE_TILE

    @pl.when(t == 0)
    def _():
        carry_scr[...] = jnp.zeros_like(carry_scr)

    shift = mod_ref[0, 3:4, :]
    scale = mod_ref[0, 4:5, :]
    h = _layer_norm(x_ref[0]) * (1.0 + scale) + shift
    h_hi = h.astype(jnp.bfloat16)
    h_lo = (h - h_hi.astype(jnp.float32)).astype(jnp.bfloat16)
    bits = pltpu.bitcast(h_hi.astype(jnp.float32), jnp.uint32)
    half = D_MODEL // 2
    hp_ref[...] = lax.shift_right_logical(bits[:, :half], jnp.uint32(16)) | bits[:, half:]

    wr_hi, wr_lo = _split_bf16(wr_ref[...])
    logits_t = _nt_dot(wr_hi, h_hi) + (_nt_dot(wr_lo, h_hi) + _nt_dot(wr_hi, h_lo))
    i1, i2, w1, w2 = _router_rows(logits_t, rb_ref)

    e_iota = lax.broadcasted_iota(jnp.int32, (N_EXPERTS, T), 0).astype(jnp.float32)
    sel1 = e_iota == i1
    sel2 = e_iota == i2
    onehot = jnp.where(sel1 | sel2, 1.0, 0.0)
    tr = lax.broadcasted_iota(jnp.int32, (T, T), 0)
    tc = lax.broadcasted_iota(jnp.int32, (T, T), 1)
    earlier = jnp.where(tr < tc, 1.0, 0.0).astype(jnp.bfloat16)
    rank = jnp.dot(onehot.astype(jnp.bfloat16), earlier,
                   preferred_element_type=jnp.float32) + carry_scr[:, 0:1]
    rank1 = jnp.sum(jnp.where(sel1, rank, 0.0), axis=0, keepdims=True)
    rank2 = jnp.sum(jnp.where(sel2, rank, 0.0), axis=0, keepdims=True)
    for r, row in enumerate((i1, i2, w1, w2, rank1, rank2)):
        meta_ref[r, 0] = row
    carry_scr[...] = carry_scr[...] + jnp.sum(onehot, axis=1, keepdims=True)
    cnt_ref[...] = carry_scr[...]


def _route_call(x, modl, w_router_t, rbias_col):
    B, S, D = x.shape
    tm = ROUTE_TILE
    nt = S // tm
    n_tiles = B * nt
    return pl.pallas_call(
        _route_kernel,
        grid=(n_tiles,),
        in_specs=[
            pl.BlockSpec((1, tm, D), lambda t: (t // nt, t % nt, 0)),
            pl.BlockSpec((1, N_MOD, D), lambda t: (t // nt, 0, 0)),
            pl.BlockSpec((N_EXPERTS, D), lambda t: (0, 0)),
            pl.BlockSpec((N_EXPERTS, 1), lambda t: (0, 0)),
        ],
        out_specs=[
            pl.BlockSpec((tm, D // 2), lambda t: (t, 0)),
            pl.BlockSpec((6, 1, 1, tm), lambda t: (0, t, 0, 0)),
            pl.BlockSpec((N_EXPERTS, LANES), lambda t: (0, 0)),
        ],
        out_shape=[
            jax.ShapeDtypeStruct((B * S, D // 2), jnp.uint32),
            jax.ShapeDtypeStruct((6, n_tiles, 1, tm), jnp.float32),
            jax.ShapeDtypeStruct((N_EXPERTS, LANES), jnp.float32),
        ],
        scratch_shapes=[pltpu.VMEM((N_EXPERTS, LANES), jnp.float32)],
        compiler_params=_params(("arbitrary",)),
    )(x, modl, w_router_t, rbias_col)


def _dispatch_plan(meta, counts, n_tokens):
    n_rows = 2 * n_tokens + N_EXPERTS * EXPERT_CHUNK
    n_chunks = n_rows // EXPERT_CHUNK
    meta = meta.reshape(meta.shape[0], n_tokens)
    per_token = lambda r: meta[r]
    e1 = per_token(0).astype(jnp.int32)
    e2 = per_token(1).astype(jnp.int32)
    w1 = per_token(2)
    w2 = per_token(3)
    cnt = counts[:, 0].astype(jnp.int32)
    padded = (cnt + (EXPERT_CHUNK - 1)) // EXPERT_CHUNK * EXPERT_CHUNK
    ends = jnp.cumsum(padded)
    starts = ends - padded
    first = jnp.cumsum(cnt) - cnt
    pos1 = starts[e1] + per_token(4).astype(jnp.int32)
    pos2 = starts[e2] + per_token(5).astype(jnp.int32)
    order = jnp.argsort(jnp.concatenate([pos1, pos2])).astype(jnp.int32)
    sorted_tok = jnp.where(order >= n_tokens, order - n_tokens, order)
    chunk_row0 = jnp.arange(n_chunks, dtype=jnp.int32) * EXPERT_CHUNK
    chunk_expert = jnp.minimum(
        jnp.sum((ends[None, :] <= chunk_row0[:, None]).astype(jnp.int32), axis=1), N_EXPERTS - 1)
    local = (chunk_row0 - starts[chunk_expert])[:, None] + jnp.arange(EXPERT_CHUNK, dtype=jnp.int32)
    real = local < cnt[chunk_expert][:, None]
    nth = jnp.clip(first[chunk_expert][:, None] + local, 0, 2 * n_tokens - 1)
    src = jnp.where(real, sorted_tok[nth], 0).reshape(n_rows)
    n_used = (ends[-1] // EXPERT_CHUNK).reshape(1).astype(jnp.int32)
    return pos1, pos2, w1, w2, src, chunk_expert, n_used


def _expert_kernel(ce_ref, nu_ref, xs_ref, wg_ref, wu_ref, wd_ref, o_ref):
    c = pl.program_id(0)

    @pl.when(c < nu_ref[0])
    def _():
        w = xs_ref[...]
        lo = pltpu.bitcast(lax.shift_left(w, jnp.uint32(16)), jnp.float32)
        hi = pltpu.bitcast(w & jnp.uint32(0xFFFF0000), jnp.float32)
        xs = jnp.concatenate([lo, hi], axis=1).astype(jnp.bfloat16)
        gt = jnp.dot(xs, wg_ref[0], preferred_element_type=jnp.float32)
        up = jnp.dot(xs, wu_ref[0], preferred_element_type=jnp.float32)
        act = (gt * (1.0 / (1.0 + jnp.exp(-gt))) * up).astype(jnp.bfloat16)
        o_ref[...] = jnp.dot(act, wd_ref[0], preferred_element_type=jnp.float32)

    @pl.when(c >= nu_ref[0])
    def _():
        o_ref[...] = jnp.zeros_like(o_ref)


def _expert_call(chunk_expert, n_used, xs, wg_bf, wu_bf, wd_bf):
    n_rows, half = xs.shape
    D = 2 * half
    grid_spec = pltpu.PrefetchScalarGridSpec(
        num_scalar_prefetch=2,
        grid=(n_rows // EXPERT_CHUNK,),
        in_specs=[
            pl.BlockSpec((EXPERT_CHUNK, half), lambda c, ce, nu: (c, 0)),
            pl.BlockSpec((1, D, D_EXPERT), lambda c, ce, nu: (ce[c], 0, 0)),
            pl.BlockSpec((1, D, D_EXPERT), lambda c, ce, nu: (ce[c], 0, 0)),
            pl.BlockSpec((1, D_EXPERT, D), lambda c, ce, nu: (ce[c], 0, 0)),
        ],
        out_specs=pl.BlockSpec((EXPERT_CHUNK, D), lambda c, ce, nu: (c, 0)),
    )
    return pl.pallas_call(
        _expert_kernel,
        grid_spec=grid_spec,
        out_shape=jax.ShapeDtypeStruct((n_rows, D), jnp.float32),
        compiler_params=_params(("arbitrary",)),
    )(chunk_expert, n_used, xs, wg_bf, wu_bf, wd_bf)


def _combine_kernel(y1_ref, y2_ref, w1_ref, w2_ref, x_ref, mod_ref, g_ref, b_ref, o_ref):
    y = w1_ref[0] * y1_ref[0] + w2_ref[0] * y2_ref[0]
    gate = mod_ref[0, 5:6, :]
    z = DEEPNORM_ALPHA * x_ref[0] + gate * y
    o_ref[0] = _layer_norm(z) * g_ref[...] + b_ref[...]


def _combine_call(y1, y2, w1, w2, x, modl, ln_g, ln_b):
    B, S, D = x.shape
    tm = 512
    rows = pl.BlockSpec((1, tm, D), lambda b, i: (b, i, 0))
    col = pl.BlockSpec((1, tm, 1), lambda b, i: (b, i, 0))
    vec = pl.BlockSpec((1, D), lambda b, i: (0, 0))
    return pl.pallas_call(
        _combine_kernel,
        grid=(B, S // tm),
        in_specs=[rows, rows, col, col, rows,
                  pl.BlockSpec((1, N_MOD, D), lambda b, i: (b, 0, 0)), vec, vec],
        out_specs=rows,
        out_shape=jax.ShapeDtypeStruct((B, S, D), jnp.float32),
        compiler_params=_params(("arbitrary", "arbitrary")),
    )(y1.reshape(B, S, D), y2.reshape(B, S, D), w1.reshape(B, S, 1), w2.reshape(B, S, 1),
      x, modl, ln_g.reshape(1, D), ln_b.reshape(1, D))


def _moe_routed(x, modl, w_router_t, rbias_col, wg_bf, wu_bf, wd_bf, ln_g, ln_b):
    B, S, D = x.shape
    hp, meta, counts = _route_call(x, modl, w_router_t, rbias_col)
    pos1, pos2, w1, w2, src, chunk_expert, n_used = _dispatch_plan(meta, counts, B * S)
    xs = _sc_gather_rows(hp, src)
    ys = _expert_call(chunk_expert, n_used, xs, wg_bf, wu_bf, wd_bf)
    y1 = _sc_gather_rows(ys, pos1)
    y2 = _sc_gather_rows(ys, pos2)
    return _combine_call(y1, y2, w1, w2, x, modl, ln_g, ln_b)


def _rope_tables(S):
    half = ROT_DIM // 2
    inv_freq = ROPE_THETA ** (-(jnp.arange(half, dtype=jnp.float32) * 2.0 / ROT_DIM))
    ang = jnp.arange(S, dtype=jnp.float32)[:, None] * inv_freq[None, :]
    cos, sin = jnp.cos(ang), jnp.sin(ang)
    d = jnp.arange(LANES) % HEAD_DIM
    f = d % half
    rot = d[None, :] < ROT_DIM
    return jnp.where(rot, cos[:, f], 1.0), jnp.where(rot, sin[:, f], 0.0)


def _with_rotary_partner_columns(w_in_l):
    half = ROT_DIM // 2
    c = jnp.arange(2 * D_ATTN)
    d = c % HEAD_DIM
    src = jnp.where(d < half, c + half, c - half)
    sign = jnp.where(d < half, -1.0, jnp.where(d < ROT_DIM, 1.0, 0.0))
    partner = w_in_l[:, src] * sign[None, :]
    return jnp.concatenate([w_in_l, partner], axis=1)


def _bias_placement():
    src = jnp.arange(LANES)
    hd, u, j = src >> 4, (src >> 3) & 1, src & 7
    col0 = jnp.where(hd % 2 == 0, HEAD_DIM, 0)
    dst = LANES * hd + col0 + j
    onehot = (jnp.arange(N_HEADS * LANES)[None, :] == dst[:, None]) & (u[:, None] == 1)
    return onehot.astype(jnp.bfloat16)


def kernel(x, c, w_mod, b_mod, w_in, w_pool, pool_scale, w_out, ln1_g, ln1_b,
           w_router, router_bias, w_gate, w_up, w_down, ln2_g, ln2_b):
    B, S, D = x.shape
    bf = jnp.bfloat16
    mod = _mod_call(c, w_mod, b_mod).reshape(DEPTH, B, N_MOD, D)
    cos_t, sin_t = _rope_tables(S)
    place = _bias_placement()
    w_router_t = w_router.T
    rbias_col = router_bias.reshape(N_EXPERTS, 1)
    for l in range(DEPTH):
        modl = mod[l]
        w_in_aug = _with_rotary_partner_columns(w_in[l]).astype(bf)
        qa, ka, v, p, _ = _proj_call(x, modl, w_in_aug, cos_t, sin_t, place)
        a = _attn_call(qa, ka, v)
        m = _pool_call(p, w_pool[l], pool_scale[l])
        x = _mixout_call(a, m, x, modl, w_out[l].astype(bf), ln1_g[l], ln1_b[l])
        x = _moe_routed(x, modl, w_router_t, rbias_col, w_gate[l].astype(bf),
                        w_up[l].astype(bf), w_down[l].astype(bf), ln2_g[l], ln2_b[l])
    return x
```

```python
import functools

import jax
import jax.numpy as jnp
from jax import lax
from jax.experimental import pallas as pl
from jax.experimental.pallas import tpu as pltpu
from jax.experimental.pallas import tpu_sc as plsc

D_MODEL = 1024
DEPTH = 2
D_ATTN = 512
D_POOL = 512
N_HEADS = 8
HEAD_DIM = 64
ROT_DIM = 16
ROPE_THETA = 500000.0
MOBA_BLOCK = 256
MOBA_TOPK = 3
POOL_GROUP = 128
N_POOL_GROUPS = 4
D_IN = 3 * D_ATTN + D_POOL
N_EXPERTS = 16
EXPERTS_PER_GROUP = 4
D_EXPERT = 512
DEEPNORM_ALPHA = (2 * DEPTH) ** 0.25
N_MOD = 6
LN_EPS = 1e-5
NEG_INF = -1e30

LANES = 128
VMEM_LIMIT = 56 * 1024 * 1024

_HI = lax.Precision.HIGHEST
_NT = (((1,), (1,)), ((), ()))


def _params(sem):
    return pltpu.CompilerParams(dimension_semantics=sem, vmem_limit_bytes=VMEM_LIMIT)


def _nt_dot(a, b):
    return lax.dot_general(a, b, _NT, preferred_element_type=jnp.float32)


def _mod_kernel(c_ref, w_ref, b_ref, o_ref):
    c = c_ref[...]
    cond = c * (1.0 / (1.0 + jnp.exp(-c)))
    o_ref[0] = jnp.dot(cond, w_ref[0], precision=_HI,
                       preferred_element_type=jnp.float32) + b_ref[0]


def _mod_call(c, w_mod, b_mod):
    B = c.shape[0]
    return pl.pallas_call(
        _mod_kernel,
        grid=(DEPTH, N_MOD),
        in_specs=[
            pl.BlockSpec((B, D_MODEL), lambda l, j: (0, 0)),
            pl.BlockSpec((1, D_MODEL, D_MODEL), lambda l, j: (l, 0, j)),
            pl.BlockSpec((1, 1, D_MODEL), lambda l, j: (l, 0, j)),
        ],
        out_specs=pl.BlockSpec((1, B, D_MODEL), lambda l, j: (l, 0, j)),
        out_shape=jax.ShapeDtypeStruct((DEPTH, B, N_MOD * D_MODEL), jnp.float32),
        compiler_params=_params(("arbitrary", "arbitrary")),
    )(c, w_mod, b_mod.reshape(DEPTH, 1, N_MOD * D_MODEL))


def _layer_norm(x):
    mu = jnp.mean(x, axis=-1, keepdims=True)
    xc = x - mu
    var = jnp.mean(xc * xc, axis=-1, keepdims=True)
    return xc * lax.rsqrt(var + LN_EPS)


def _split_bf16(t):
    hi = t.astype(jnp.bfloat16)
    lo = (t - hi.astype(jnp.float32)).astype(jnp.bfloat16)
    return hi, lo


def _proj_kernel(x_ref, mod_ref, w_ref, cos_ref, sin_ref, place_ref,
                 qa_ref, ka_ref, v_ref, p_ref, kbar_ref):
    i = pl.program_id(1)

    @pl.when(i == 0)
    def _():
        kbar_ref[...] = jnp.zeros_like(kbar_ref)

    x = x_ref[0]
    shift = mod_ref[0, 0:1, :]
    scale = mod_ref[0, 1:2, :]
    h = (_layer_norm(x) * (1.0 + scale) + shift).astype(jnp.bfloat16)

    cos = cos_ref[...]
    sin = sin_ref[...]

    def proj(c0, width):
        return jnp.dot(h, w_ref[:, c0:c0 + width], preferred_element_type=jnp.float32)

    def slab(t, s):
        return t[:, LANES * s:LANES * (s + 1)]

    n_slab = D_ATTN // LANES
    q = proj(0, D_ATTN)
    qp = proj(D_IN, D_ATTN)
    q_slabs = [(slab(q, s) * cos + slab(qp, s) * sin) * (HEAD_DIM ** -0.5)
               for s in range(n_slab)]
    k = proj(D_ATTN, D_ATTN)
    kp = proj(D_IN + D_ATTN, D_ATTN)
    k_slabs = [slab(k, s) * cos + slab(kp, s) * sin for s in range(n_slab)]
    v_ref[0] = proj(2 * D_ATTN, D_ATTN).astype(jnp.bfloat16)
    p_ref[0] = proj(3 * D_ATTN, D_POOL)

    kmean = jnp.concatenate(
        [jnp.mean(ks, axis=0, keepdims=True) for ks in k_slabs], axis=1)
    kbar_ref[0, pl.ds(i, 1), :] = kmean

    kb = kbar_ref[0]
    kb_rows = jnp.concatenate([kb] * (LANES // 8), axis=0)
    r_head = lax.broadcasted_iota(jnp.int32, (LANES, D_ATTN), 1) >> 6
    c_head = lax.broadcasted_iota(jnp.int32, (LANES, D_ATTN), 0) >> 4
    kbd_hi, kbd_lo = _split_bf16(jnp.where(r_head == c_head, kb_rows, 0.0))
    q_hi, q_lo = _split_bf16(jnp.concatenate(q_slabs, axis=1))
    gate = _nt_dot(q_hi, kbd_hi) + (_nt_dot(q_lo, kbd_hi) + _nt_dot(q_hi, kbd_lo))

    lane = lax.broadcasted_iota(jnp.int32, (MOBA_BLOCK, LANES), 1)
    j_of = lane & 7
    past = j_of < i
    gm = jnp.where(past, gate, NEG_INF)
    rank = jnp.zeros((MOBA_BLOCK, LANES), jnp.int32)
    for r in range(1, 8):
        other = pltpu.roll(gm, r, 1)
        beats = (other > gm) | ((other == gm) & (j_of >= r))
        rank = rank + beats.astype(jnp.int32)
    allowed = (past & (rank < MOBA_TOPK)) | (j_of == i)
    bias = jnp.where(allowed, 0.0, NEG_INF).astype(jnp.bfloat16)
    bias_cols = jnp.dot(bias, place_ref[...], preferred_element_type=jnp.float32)

    for hd in range(N_HEADS):
        own = (lane < HEAD_DIM) if hd % 2 == 0 else (lane >= HEAD_DIM)
        col0 = HEAD_DIM if hd % 2 == 0 else 0
        qa = jnp.where(own, q_slabs[hd // 2], slab(bias_cols, hd))
        ka = jnp.where(own, k_slabs[hd // 2], jnp.where(lane == col0 + i, 1.0, 0.0))
        qa_ref[0, hd] = qa.astype(jnp.bfloat16)
        ka_ref[0, hd] = ka.astype(jnp.bfloat16)


def _proj_call(x, modl, w_in_aug, cos_t, sin_t, place):
    B, S, D = x.shape
    nb = S // MOBA_BLOCK
    tm = MOBA_BLOCK
    tab = pl.BlockSpec((tm, LANES), lambda b, i: (i, 0))
    head_spec = pl.BlockSpec((1, N_HEADS, tm, LANES), lambda b, i: (b, 0, i, 0))
    return pl.pallas_call(
        _proj_kernel,
        grid=(B, nb),
        in_specs=[
            pl.BlockSpec((1, tm, D), lambda b, i: (b, i, 0)),
            pl.BlockSpec((1, N_MOD, D), lambda b, i: (b, 0, 0)),
            pl.BlockSpec((D, D_IN + 2 * D_ATTN), lambda b, i: (0, 0)),
            tab, tab,
            pl.BlockSpec((LANES, N_HEADS * LANES), lambda b, i: (0, 0)),
        ],
        out_specs=[
            head_spec, head_spec,
            pl.BlockSpec((1, tm, D_ATTN), lambda b, i: (b, i, 0)),
            pl.BlockSpec((1, tm, D_POOL), lambda b, i: (b, i, 0)),
            pl.BlockSpec((1, nb, D_ATTN), lambda b, i: (b, 0, 0)),
        ],
        out_shape=[
            jax.ShapeDtypeStruct((B, N_HEADS, S, LANES), jnp.bfloat16),
            jax.ShapeDtypeStruct((B, N_HEADS, S, LANES), jnp.bfloat16),
            jax.ShapeDtypeStruct((B, S, D_ATTN), jnp.bfloat16),
            jax.ShapeDtypeStruct((B, S, D_POOL), jnp.float32),
            jax.ShapeDtypeStruct((B, nb, D_ATTN), jnp.float32),
        ],
        compiler_params=_params(("arbitrary", "arbitrary")),
    )(x, modl, w_in_aug, cos_t, sin_t, place)


def _attn_kernel(qa_ref, ka_ref, v_ref, o_ref):
    nb = v_ref.shape[1] // MOBA_BLOCK
    row = lax.broadcasted_iota(jnp.int32, (MOBA_BLOCK, MOBA_BLOCK), 0)
    col = lax.broadcasted_iota(jnp.int32, (MOBA_BLOCK, MOBA_BLOCK), 1)
    causal = col <= row
    lane = lax.broadcasted_iota(jnp.int32, (MOBA_BLOCK, LANES), 1)

    for i in reversed(range(nb)):
        r0 = i * MOBA_BLOCK
        outs = []
        for hh in range(2):
            q = qa_ref[0, hh, r0:r0 + MOBA_BLOCK, :]
            s_own = jnp.where(causal, _nt_dot(q, ka_ref[0, hh, r0:r0 + MOBA_BLOCK, :]), NEG_INF)
            m = jnp.max(s_own, axis=1, keepdims=True)
            if i > 0:
                s_past = _nt_dot(q, ka_ref[0, hh, 0:r0, :])
                m = jnp.maximum(m, jnp.max(s_past, axis=1, keepdims=True))
            p_own = jnp.exp(s_own - m)
            l = jnp.sum(p_own, axis=1, keepdims=True)
            acc = jnp.dot(p_own.astype(jnp.bfloat16), v_ref[0, r0:r0 + MOBA_BLOCK, :],
                          preferred_element_type=jnp.float32)
            if i > 0:
                p_past = jnp.exp(s_past - m)
                l = l + jnp.sum(p_past, axis=1, keepdims=True)
                acc = acc + jnp.dot(p_past.astype(jnp.bfloat16), v_ref[0, 0:r0, :],
                                    preferred_element_type=jnp.float32)
            outs.append(acc / l)
        o = jnp.where(lane < HEAD_DIM, outs[0], outs[1])
        o_ref[0, r0:r0 + MOBA_BLOCK, :] = o.astype(o_ref.dtype)


def _attn_call(qa, ka, v):
    B, _, S, _ = qa.shape
    n_pair = N_HEADS // 2
    pair_spec = pl.BlockSpec((1, 2, S, LANES), lambda b, hp: (b, hp, 0, 0))
    slab_spec = pl.BlockSpec((1, S, LANES), lambda b, hp: (b, 0, hp))
    return pl.pallas_call(
        _attn_kernel,
        grid=(B, n_pair),
        in_specs=[pair_spec, pair_spec, slab_spec],
        out_specs=slab_spec,
        out_shape=jax.ShapeDtypeStruct((B, S, D_ATTN), jnp.bfloat16),
        compiler_params=_params(("arbitrary", "arbitrary")),
    )(qa, ka, v)


def _pool_kernel(p_ref, w_ref, sc_ref, o_ref):
    g = pl.program_id(1)
    S = p_ref.shape[1]
    p = p_ref[0]
    t = lax.broadcasted_iota(jnp.int32, (S, LANES), 0)

    def shifted(x, k):
        return jnp.where(t >= k, pltpu.roll(x, k, 0), 0.0)

    win = p
    sums = []
    for step in range(N_POOL_GROUPS):
        win = win + shifted(win, 1 << step)
        sums.append(win)
    wsum = jnp.where(g == 0, sums[0],
                     jnp.where(g == 1, sums[1], jnp.where(g == 2, sums[2], sums[3])))
    window = jnp.left_shift(2, g)
    cnt = jnp.minimum(t + 1, window).astype(jnp.float32)
    d = (wsum / cnt - p).astype(jnp.bfloat16)
    y = jnp.dot(d, w_ref[0].astype(jnp.bfloat16), preferred_element_type=jnp.float32)
    o_ref[0] = (y * sc_ref[...]).astype(o_ref.dtype)


def _pool_call(p, w_pool_l, pool_scale_l):
    B, S, _ = p.shape
    slab = pl.BlockSpec((1, S, LANES), lambda b, g: (b, 0, g))
    return pl.pallas_call(
        _pool_kernel,
        grid=(B, N_POOL_GROUPS),
        in_specs=[
            slab,
            pl.BlockSpec((1, POOL_GROUP, POOL_GROUP), lambda b, g: (g, 0, 0)),
            pl.BlockSpec((1, LANES), lambda b, g: (0, g)),
        ],
        out_specs=slab,
        out_shape=jax.ShapeDtypeStruct((B, S, D_POOL), jnp.bfloat16),
        compiler_params=_params(("arbitrary", "arbitrary")),
    )(p, w_pool_l, pool_scale_l.reshape(1, D_POOL))


def _mixout_kernel(a_ref, m_ref, x_ref, mod_ref, w_ref, g_ref, b_ref, o_ref):
    y = jnp.dot(a_ref[0], w_ref[0:D_ATTN, :], preferred_element_type=jnp.float32)
    y = y + jnp.dot(m_ref[0], w_ref[D_ATTN:, :], preferred_element_type=jnp.float32)
    gate = mod_ref[0, 2:3, :]
    z = DEEPNORM_ALPHA * x_ref[0] + gate * y
    o_ref[0] = _layer_norm(z) * g_ref[...] + b_ref[...]


def _mixout_call(a, m, x, modl, w_out_bf, ln_g, ln_b):
    B, S, D = x.shape
    tm = 512
    vec = pl.BlockSpec((1, D), lambda b, i: (0, 0))
    return pl.pallas_call(
        _mixout_kernel,
        grid=(B, S // tm),
        in_specs=[
            pl.BlockSpec((1, tm, D_ATTN), lambda b, i: (b, i, 0)),
            pl.BlockSpec((1, tm, D_POOL), lambda b, i: (b, i, 0)),
            pl.BlockSpec((1, tm, D), lambda b, i: (b, i, 0)),
            pl.BlockSpec((1, N_MOD, D), lambda b, i: (b, 0, 0)),
            pl.BlockSpec((D, D), lambda b, i: (0, 0)),
            vec, vec,
        ],
        out_specs=pl.BlockSpec((1, tm, D), lambda b, i: (b, i, 0)),
        out_shape=jax.ShapeDtypeStruct((B, S, D), jnp.float32),
        compiler_params=_params(("arbitrary", "arbitrary")),
    )(a, m, x, modl, w_out_bf, ln_g.reshape(1, D), ln_b.reshape(1, D))


MOE_TILE = 1024
MOE_CHUNK = 160
MOE_ROWS = 2304
MOE_ROWS_ALLOC = MOE_ROWS + MOE_CHUNK
MOE_G_COLS = 768
MOE_EXPERTS_PER_STEP = 2


def _top2_rows(vals):
    def first_max(rows):
        m = rows[0]
        for v in rows[1:]:
            m = jnp.maximum(m, v)
        idx = jnp.full_like(m, float(len(rows) - 1))
        for k in range(len(rows) - 2, -1, -1):
            idx = jnp.where(rows[k] == m, float(k), idx)
        return m, idx

    m1, i1 = first_max(vals)
    rest = [jnp.where(i1 == float(k), -jnp.inf, v) for k, v in enumerate(vals)]
    m2, i2 = first_max(rest)
    return m1, i1, m2, i2


def _router_rows(logits_t, rb_ref):
    lg = [logits_t[e:e + 1, :] for e in range(N_EXPERTS)]
    mx = lg[0]
    for v in lg[1:]:
        mx = jnp.maximum(mx, v)
    ex = [jnp.exp(v - mx) for v in lg]
    den = ex[0]
    for v in ex[1:]:
        den = den + v
    scores = [v / den for v in ex]
    sel = [scores[e] + rb_ref[e:e + 1, :] for e in range(N_EXPERTS)]
    best_score = None
    best = None
    for g in range(N_EXPERTS // EXPERTS_PER_GROUP):
        m1, _, m2, _ = _top2_rows(sel[g * EXPERTS_PER_GROUP:(g + 1) * EXPERTS_PER_GROUP])
        gs = m1 + m2
        if g == 0:
            best_score, best = gs, jnp.zeros_like(gs)
        else:
            better = gs > best_score
            best_score = jnp.where(better, gs, best_score)
            best = jnp.where(better, float(g), best)
    masked = [jnp.where(best == float(e // EXPERTS_PER_GROUP), sel[e], NEG_INF)
              for e in range(N_EXPERTS)]
    _, i1, _, i2 = _top2_rows(masked)
    w1 = jnp.zeros_like(i1)
    w2 = jnp.zeros_like(i2)
    for e in range(N_EXPERTS):
        w1 = jnp.where(i1 == float(e), scores[e], w1)
        w2 = jnp.where(i2 == float(e), scores[e], w2)
    tot = w1 + w2
    return i1, i2, w1 / tot, w2 / tot


def _moe_route(x_ref, mod_ref, wr_ref, rb_ref, h_scr, col_scr, row_scr, ysh_scr, ysl_scr,
               start_smem, cnt_smem):
    T = MOE_TILE
    shift = mod_ref[0, 3:4, :]
    scale = mod_ref[0, 4:5, :]
    h = _layer_norm(x_ref[0]) * (1.0 + scale) + shift
    h_hi = h.astype(jnp.bfloat16)
    h_scr[...] = h_hi
    h_lo = (h - h_hi.astype(jnp.float32)).astype(jnp.bfloat16)
    wr_hi, wr_lo = _split_bf16(wr_ref[...])
    logits_t = _nt_dot(wr_hi, h_hi) + (_nt_dot(wr_lo, h_hi) + _nt_dot(wr_hi, h_lo))
    i1, i2, w1, w2 = _router_rows(logits_t, rb_ref)

    e_iota = lax.broadcasted_iota(jnp.int32, (N_EXPERTS, T), 0).astype(jnp.float32)
    sel1 = e_iota == i1
    sel2 = e_iota == i2
    onehot = jnp.where(sel1 | sel2, 1.0, 0.0)
    tr = lax.broadcasted_iota(jnp.int32, (T, T), 0)
    tc = lax.broadcasted_iota(jnp.int32, (T, T), 1)
    earlier = jnp.where(tr < tc, 1.0, 0.0).astype(jnp.bfloat16)
    excl = jnp.dot(onehot.astype(jnp.bfloat16), earlier,
                   preferred_element_type=jnp.float32)
    cnt = jnp.sum(onehot, axis=1, keepdims=True)

    e_col = lax.broadcasted_iota(jnp.int32, (N_EXPERTS, 1), 0)
    start_v = jnp.zeros((N_EXPERTS, 1), jnp.float32)
    run = jnp.int32(0)
    for ex in range(N_EXPERTS):
        padded = lax.shift_left(lax.shift_right_logical(cnt[ex, 0].astype(jnp.int32) + 15, 4), 4)
        start_smem[ex] = run
        cnt_smem[ex] = padded
        start_v = jnp.where(e_col == ex, run.astype(jnp.float32), start_v)
        run = run + padded

    slot = excl + start_v
    pos1 = jnp.sum(jnp.where(sel1, slot, 0.0), axis=0, keepdims=True)
    pos2 = jnp.sum(jnp.where(sel2, slot, 0.0), axis=0, keepdims=True)
    row_scr[0:1, :] = pos1
    row_scr[1:2, :] = pos2
    row_scr[2:3, :] = w1
    row_scr[3:4, :] = w2

    r128 = lax.broadcasted_iota(jnp.int32, (LANES, T), 0)
    terms = jnp.zeros((LANES, T), jnp.float32)
    k = 0
    for p in (pos1, pos2):
        a = jnp.floor(p * (1.0 / 64.0))
        for t in (a, p - 64.0 * a):
            terms = jnp.where(r128 == k, t, terms)
            k += 1
    eye = jnp.where(tr == tc, 1.0, 0.0).astype(jnp.bfloat16)
    col_scr[...] = _nt_dot(eye, terms.astype(jnp.bfloat16))

    tail = MOE_ROWS_ALLOC - 2 * T
    ysh_scr[2 * T:, :] = jnp.zeros((tail, D_MODEL), jnp.bfloat16)
    ysl_scr[2 * T:, :] = jnp.zeros((tail, D_MODEL), jnp.bfloat16)


def _moe_kernel(x_ref, mod_ref, wr_ref, rb_ref, wg_ref, wu_ref, wd_ref, g_ref, b_ref,
                o_ref, h_scr, col_scr, row_scr, ysh_scr, ysl_scr, start_smem, cnt_smem):
    step = pl.program_id(1)
    T = MOE_TILE

    @pl.when(step == 0)
    def _():
        _moe_route(x_ref, mod_ref, wr_ref, rb_ref, h_scr, col_scr, row_scr, ysh_scr, ysl_scr,
                   start_smem, cnt_smem)

    pos1_row = row_scr[0:1, :]
    pos2_row = row_scr[1:2, :]
    w1_row = row_scr[2:3, :]
    w2_row = row_scr[3:4, :]

    for k in range(MOE_EXPERTS_PER_STEP):
        e = step * MOE_EXPERTS_PER_STEP + k
        seg_start = start_smem[e]
        n_chunk = (cnt_smem[e] + (MOE_CHUNK - 1)) // MOE_CHUNK

        def chunk(j, carry, k=k, seg_start=seg_start):
            base = pl.multiple_of(seg_start + j * MOE_CHUNK, 16)
            r = (base + lax.broadcasted_iota(jnp.int32, (MOE_CHUNK, T), 0)).astype(jnp.float32)
            m1 = r == pos1_row
            m2 = r == pos2_row
            gather = jnp.where(m1 | m2, 1.0, 0.0).astype(jnp.bfloat16)
            w_row = jnp.sum(jnp.where(m1, w1_row, 0.0) + jnp.where(m2, w2_row, 0.0),
                            axis=1, keepdims=True)
            xs = jnp.dot(gather, h_scr[...],
                         preferred_element_type=jnp.float32).astype(jnp.bfloat16)
            gt = jnp.dot(xs, wg_ref[k], preferred_element_type=jnp.float32)
            up = jnp.dot(xs, wu_ref[k], preferred_element_type=jnp.float32)
            act = (gt * (1.0 / (1.0 + jnp.exp(-gt))) * up).astype(jnp.bfloat16)
            y = w_row * jnp.dot(act, wd_ref[k], preferred_element_type=jnp.float32)
            y_hi = y.astype(jnp.bfloat16)
            ysh_scr[pl.ds(base, MOE_CHUNK), :] = y_hi
            ysl_scr[pl.ds(base, MOE_CHUNK), :] = (y - y_hi.astype(jnp.float32)).astype(jnp.bfloat16)
            return carry

        lax.fori_loop(0, n_chunk, chunk, 0)

    @pl.when(step == N_EXPERTS // MOE_EXPERTS_PER_STEP - 1)
    def _():
        pos1 = 64.0 * col_scr[:, 0:1] + col_scr[:, 1:2]
        pos2 = 64.0 * col_scr[:, 2:3] + col_scr[:, 3:4]
        y = jnp.zeros((T, D_MODEL), jnp.float32)
        for c0 in range(0, MOE_ROWS, MOE_G_COLS):
            r = (c0 + lax.broadcasted_iota(jnp.int32, (T, MOE_G_COLS), 1)).astype(jnp.float32)
            scatter = jnp.where((r == pos1) | (r == pos2), 1.0, 0.0).astype(jnp.bfloat16)
            y = y + jnp.dot(scatter, ysh_scr[c0:c0 + MOE_G_COLS, :],
                            preferred_element_type=jnp.float32)
            y = y + jnp.dot(scatter, ysl_scr[c0:c0 + MOE_G_COLS, :],
                            preferred_element_type=jnp.float32)
        gate = mod_ref[0, 5:6, :]
        z = DEEPNORM_ALPHA * x_ref[0] + gate * y
        o_ref[0] = _layer_norm(z) * g_ref[...] + b_ref[...]


def _moe_call(x, modl, w_router_t, rbias_col, wg_bf, wu_bf, wd_bf, ln_g, ln_b):
    B, S, D = x.shape
    tm = MOE_TILE
    nt = S // tm
    eps = MOE_EXPERTS_PER_STEP
    vec = pl.BlockSpec((1, D), lambda t, e: (0, 0))
    xspec = pl.BlockSpec((1, tm, D), lambda t, e: (t // nt, t % nt, 0))
    return pl.pallas_call(
        _moe_kernel,
        grid=(B * nt, N_EXPERTS // eps),
        in_specs=[
            xspec,
            pl.BlockSpec((1, N_MOD, D), lambda t, e: (t // nt, 0, 0)),
            pl.BlockSpec((N_EXPERTS, D), lambda t, e: (0, 0)),
            pl.BlockSpec((N_EXPERTS, 1), lambda t, e: (0, 0)),
            pl.BlockSpec((eps, D, D_EXPERT), lambda t, e: (e, 0, 0)),
            pl.BlockSpec((eps, D, D_EXPERT), lambda t, e: (e, 0, 0)),
            pl.BlockSpec((eps, D_EXPERT, D), lambda t, e: (e, 0, 0)),
            vec, vec,
        ],
        out_specs=xspec,
        out_shape=jax.ShapeDtypeStruct((B, S, D), jnp.float32),
        scratch_shapes=[
            pltpu.VMEM((tm, D), jnp.bfloat16),
            pltpu.VMEM((tm, LANES), jnp.float32),
            pltpu.VMEM((8, tm), jnp.float32),
            pltpu.VMEM((MOE_ROWS_ALLOC, D), jnp.bfloat16),
            pltpu.VMEM((MOE_ROWS_ALLOC, D), jnp.bfloat16),
            pltpu.SMEM((N_EXPERTS,), jnp.int32),
            pltpu.SMEM((N_EXPERTS,), jnp.int32),
        ],
        compiler_params=_params(("arbitrary", "arbitrary")),
    )(x, modl, w_router_t, rbias_col, wg_bf, wu_bf, wd_bf,
      ln_g.reshape(1, D), ln_b.reshape(1, D))


SC_CORES = 2
SC_SUBCORES = 16
SC_MAX_INDEX_ROWS = 128
SC_RING = 4
SC_TILE_BYTES = 64 * 1024

ROUTE_TILE = 1024
EXPERT_CHUNK = 512


def _sc_gather_rows(table, idx):
    n_rows = idx.shape[0]
    width = table.shape[1]
    workers = SC_CORES * SC_SUBCORES
    rows = min(SC_MAX_INDEX_ROWS, SC_TILE_BYTES // (4 * width))
    per_worker = n_rows // workers
    assert per_worker * workers == n_rows and per_worker % rows == 0
    n_chunk = per_worker // rows
    assert n_chunk % SC_RING == 0
    mesh = plsc.VectorSubcoreMesh(core_axis_name="c", subcore_axis_name="s")
    row_buf = pltpu.VMEM((rows, width), table.dtype)
    idx_buf = pltpu.VMEM((rows,), jnp.int32)

    @functools.partial(
        pl.kernel, mesh=mesh,
        out_type=jax.ShapeDtypeStruct((n_rows, width), table.dtype),
        scratch_types=([idx_buf] * SC_RING + [row_buf] * SC_RING
                       + [pltpu.SemaphoreType.DMA] * SC_RING),
    )
    def gather(table_hbm, idx_hbm, out_hbm, *scratch):
        idx_v = scratch[:SC_RING]
        rows_v = scratch[SC_RING:2 * SC_RING]
        sems = scratch[2 * SC_RING:]
        wid = lax.axis_index("s") * SC_CORES + lax.axis_index("c")
        base = wid * per_worker

        def gather_copy(b):
            return pltpu.make_async_copy(table_hbm.at[idx_v[b]], rows_v[b], sems[b])

        def start_gather(j, b):
            off = pl.multiple_of(base + j * rows, 8)
            pltpu.sync_copy(idx_hbm.at[pl.ds(off, rows)], idx_v[b])
            gather_copy(b).start()

        for b in range(SC_RING):
            start_gather(b, b)

        def group(g, carry):
            for b in range(SC_RING):
                j = g * SC_RING + b
                gather_copy(b).wait()
                off = pl.multiple_of(base + j * rows, 8)
                pltpu.sync_copy(rows_v[b], out_hbm.at[pl.ds(off, rows)])

                @pl.when(j + SC_RING < n_chunk)
                def _():
                    start_gather(j + SC_RING, b)
            return carry

        lax.fori_loop(0, n_chunk // SC_RING, group, 0)

    return gather(table, idx)


def _route_kernel(x_ref, mod_ref, wr_ref, rb_ref, hp_ref, meta_ref, cnt_ref, carry_scr):
    t = pl.program_id(0)
    T = ROUTE_TILE

    @pl.when(t == 0)
    def _():
        carry_scr[...] = jnp.zeros_like(carry_scr)

    shift = mod_ref[0, 3:4, :]
    scale = mod_ref[0, 4:5, :]
    h = _layer_norm(x_ref[0]) * (1.0 + scale) + shift
    h_hi = h.astype(jnp.bfloat16)
    h_lo = (h - h_hi.astype(jnp.float32)).astype(jnp.bfloat16)
    hp_ref[...] = h

    wr_hi, wr_lo = _split_bf16(wr_ref[...])
    logits_t = _nt_dot(wr_hi, h_hi) + (_nt_dot(wr_lo, h_hi) + _nt_dot(wr_hi, h_lo))
    i1, i2, w1, w2 = _router_rows(logits_t, rb_ref)

    e_iota = lax.broadcasted_iota(jnp.int32, (N_EXPERTS, T), 0).astype(jnp.float32)
    sel1 = e_iota == i1
    sel2 = e_iota == i2
    onehot = jnp.where(sel1 | sel2, 1.0, 0.0)
    tr = lax.broadcasted_iota(jnp.int32, (T, T), 0)
    tc = lax.broadcasted_iota(jnp.int32, (T, T), 1)
    earlier = jnp.where(tr < tc, 1.0, 0.0).astype(jnp.bfloat16)
    rank = jnp.dot(onehot.astype(jnp.bfloat16), earlier,
                   preferred_element_type=jnp.float32) + carry_scr[:, 0:1]
    rank1 = jnp.sum(jnp.where(sel1, rank, 0.0), axis=0, keepdims=True)
    rank2 = jnp.sum(jnp.where(sel2, rank, 0.0), axis=0, keepdims=True)
    for r, row in enumerate((i1, i2, w1, w2, rank1, rank2)):
        meta_ref[r, 0] = row
    carry_scr[...] = carry_scr[...] + jnp.sum(onehot, axis=1, keepdims=True)
    cnt_ref[...] = carry_scr[...]


def _route_call(x, modl, w_router_t, rbias_col):
    B, S, D = x.shape
    tm = ROUTE_TILE
    nt = S // tm
    n_tiles = B * nt
    return pl.pallas_call(
        _route_kernel,
        grid=(n_tiles,),
        in_specs=[
            pl.BlockSpec((1, tm, D), lambda t: (t // nt, t % nt, 0)),
            pl.BlockSpec((1, N_MOD, D), lambda t: (t // nt, 0, 0)),
            pl.BlockSpec((N_EXPERTS, D), lambda t: (0, 0)),
            pl.BlockSpec((N_EXPERTS, 1), lambda t: (0, 0)),
        ],
        out_specs=[
            pl.BlockSpec((tm, D), lambda t: (t, 0)),
            pl.BlockSpec((6, 1, 1, tm), lambda t: (0, t, 0, 0)),
            pl.BlockSpec((N_EXPERTS, LANES), lambda t: (0, 0)),
        ],
        out_shape=[
            jax.ShapeDtypeStruct((B * S, D), jnp.float32),
            jax.ShapeDtypeStruct((6, n_tiles, 1, tm), jnp.float32),
            jax.ShapeDtypeStruct((N_EXPERTS, LANES), jnp.float32),
        ],
        scratch_shapes=[pltpu.VMEM((N_EXPERTS, LANES), jnp.float32)],
        compiler_params=_params(("arbitrary",)),
    )(x, modl, w_router_t, rbias_col)


def _dispatch_plan(meta, counts, n_tokens):
    n_rows = 2 * n_tokens + N_EXPERTS * EXPERT_CHUNK
    n_chunks = n_rows // EXPERT_CHUNK
    meta = meta.reshape(meta.shape[0], n_tokens)
    per_token = lambda r: meta[r]
    e1 = per_token(0).astype(jnp.int32)
    e2 = per_token(1).astype(jnp.int32)
    w1 = per_token(2)
    w2 = per_token(3)
    cnt = counts[:, 0].astype(jnp.int32)
    padded = (cnt + (EXPERT_CHUNK - 1)) // EXPERT_CHUNK * EXPERT_CHUNK
    ends = jnp.cumsum(padded)
    starts = ends - padded
    first = jnp.cumsum(cnt) - cnt
    pos1 = starts[e1] + per_token(4).astype(jnp.int32)
    pos2 = starts[e2] + per_token(5).astype(jnp.int32)
    order = jnp.argsort(jnp.concatenate([pos1, pos2])).astype(jnp.int32)
    sorted_tok = jnp.where(order >= n_tokens, order - n_tokens, order)
    chunk_row0 = jnp.arange(n_chunks, dtype=jnp.int32) * EXPERT_CHUNK
    chunk_expert = jnp.minimum(
        jnp.sum((ends[None, :] <= chunk_row0[:, None]).astype(jnp.int32), axis=1), N_EXPERTS - 1)
    local = (chunk_row0 - starts[chunk_expert])[:, None] + jnp.arange(EXPERT_CHUNK, dtype=jnp.int32)
    real = local < cnt[chunk_expert][:, None]
    nth = jnp.clip(first[chunk_expert][:, None] + local, 0, 2 * n_tokens - 1)
    src = jnp.where(real, sorted_tok[nth], 0).reshape(n_rows)
    n_used = (ends[-1] // EXPERT_CHUNK).reshape(1).astype(jnp.int32)
    return pos1, pos2, w1, w2, src, chunk_expert, n_used


def _expert_kernel(ce_ref, nu_ref, xs_ref, wg_ref, wu_ref, wd_ref, o_ref):
    c = pl.program_id(0)

    @pl.when(c < nu_ref[0])
    def _():
        xs = xs_ref[...].astype(jnp.bfloat16)
        gt = jnp.dot(xs, wg_ref[0], preferred_element_type=jnp.float32)
        up = jnp.dot(xs, wu_ref[0], preferred_element_type=jnp.float32)
        act = (gt * (1.0 / (1.0 + jnp.exp(-gt))) * up).astype(jnp.bfloat16)
        o_ref[...] = jnp.dot(act, wd_ref[0], preferred_element_type=jnp.float32)

    @pl.when(c >= nu_ref[0])
    def _():
        o_ref[...] = jnp.zeros_like(o_ref)


def _expert_call(chunk_expert, n_used, xs, wg_bf, wu_bf, wd_bf):
    n_rows, D = xs.shape
    grid_spec = pltpu.PrefetchScalarGridSpec(
        num_scalar_prefetch=2,
        grid=(n_rows // EXPERT_CHUNK,),
        in_specs=[
            pl.BlockSpec((EXPERT_CHUNK, D), lambda c, ce, nu: (c, 0)),
            pl.BlockSpec((1, D, D_EXPERT), lambda c, ce, nu: (ce[c], 0, 0)),
            pl.BlockSpec((1, D, D_EXPERT), lambda c, ce, nu: (ce[c], 0, 0)),
            pl.BlockSpec((1, D_EXPERT, D), lambda c, ce, nu: (ce[c], 0, 0)),
        ],
        out_specs=pl.BlockSpec((EXPERT_CHUNK, D), lambda c, ce, nu: (c, 0)),
    )
    return pl.pallas_call(
        _expert_kernel,
        grid_spec=grid_spec,
        out_shape=jax.ShapeDtypeStruct((n_rows, D), jnp.float32),
        compiler_params=_params(("arbitrary",)),
    )(chunk_expert, n_used, xs, wg_bf, wu_bf, wd_bf)


def _combine_kernel(y1_ref, y2_ref, w1_ref, w2_ref, x_ref, mod_ref, g_ref, b_ref, o_ref):
    y = w1_ref[0] * y1_ref[0] + w2_ref[0] * y2_ref[0]
    gate = mod_ref[0, 5:6, :]
    z = DEEPNORM_ALPHA * x_ref[0] + gate * y
    o_ref[0] = _layer_norm(z) * g_ref[...] + b_ref[...]


def _combine_call(y1, y2, w1, w2, x, modl, ln_g, ln_b):
    B, S, D = x.shape
    tm = 512
    rows = pl.BlockSpec((1, tm, D), lambda b, i: (b, i, 0))
    col = pl.BlockSpec((1, tm, 1), lambda b, i: (b, i, 0))
    vec = pl.BlockSpec((1, D), lambda b, i: (0, 0))
    return pl.pallas_call(
        _combine_kernel,
        grid=(B, S // tm),
        in_specs=[rows, rows, col, col, rows,
                  pl.BlockSpec((1, N_MOD, D), lambda b, i: (b, 0, 0)), vec, vec],
        out_specs=rows,
        out_shape=jax.ShapeDtypeStruct((B, S, D), jnp.float32),
        compiler_params=_params(("arbitrary", "arbitrary")),
    )(y1.reshape(B, S, D), y2.reshape(B, S, D), w1.reshape(B, S, 1), w2.reshape(B, S, 1),
      x, modl, ln_g.reshape(1, D), ln_b.reshape(1, D))


def _moe_routed(x, modl, w_router_t, rbias_col, wg_bf, wu_bf, wd_bf, ln_g, ln_b):
    B, S, D = x.shape
    hp, meta, counts = _route_call(x, modl, w_router_t, rbias_col)
    pos1, pos2, w1, w2, src, chunk_expert, n_used = _dispatch_plan(meta, counts, B * S)
    xs = _sc_gather_rows(hp, src)
    ys = _expert_call(chunk_expert, n_used, xs, wg_bf, wu_bf, wd_bf)
    y1 = _sc_gather_rows(ys, pos1)
    y2 = _sc_gather_rows(ys, pos2)
    return _combine_call(y1, y2, w1, w2, x, modl, ln_g, ln_b)


def _rope_tables(S):
    half = ROT_DIM // 2
    inv_freq = ROPE_THETA ** (-(jnp.arange(half, dtype=jnp.float32) * 2.0 / ROT_DIM))
    ang = jnp.arange(S, dtype=jnp.float32)[:, None] * inv_freq[None, :]
    cos, sin = jnp.cos(ang), jnp.sin(ang)
    d = jnp.arange(LANES) % HEAD_DIM
    f = d % half
    rot = d[None, :] < ROT_DIM
    return jnp.where(rot, cos[:, f], 1.0), jnp.where(rot, sin[:, f], 0.0)


def _with_rotary_partner_columns(w_in_l):
    half = ROT_DIM // 2
    c = jnp.arange(2 * D_ATTN)
    d = c % HEAD_DIM
    src = jnp.where(d < half, c + half, c - half)
    sign = jnp.where(d < half, -1.0, jnp.where(d < ROT_DIM, 1.0, 0.0))
    partner = w_in_l[:, src] * sign[None, :]
    return jnp.concatenate([w_in_l, partner], axis=1)


def _bias_placement():
    src = jnp.arange(LANES)
    hd, u, j = src >> 4, (src >> 3) & 1, src & 7
    col0 = jnp.where(hd % 2 == 0, HEAD_DIM, 0)
    dst = LANES * hd + col0 + j
    onehot = (jnp.arange(N_HEADS * LANES)[None, :] == dst[:, None]) & (u[:, None] == 1)
    return onehot.astype(jnp.bfloat16)


def kernel(x, c, w_mod, b_mod, w_in, w_pool, pool_scale, w_out, ln1_g, ln1_b,
           w_router, router_bias, w_gate, w_up, w_down, ln2_g, ln2_b):
    B, S, D = x.shape
    bf = jnp.bfloat16
    mod = _mod_call(c, w_mod, b_mod).reshape(DEPTH, B, N_MOD, D)
    cos_t, sin_t = _rope_tables(S)
    place = _bias_placement()
    w_router_t = w_router.T
    rbias_col = router_bias.reshape(N_EXPERTS, 1)
    for l in range(DEPTH):
        modl = mod[l]
        w_in_aug = _with_rotary_partner_columns(w_in[l]).astype(bf)
        qa, ka, v, p, _ = _proj_call(x, modl, w_in_aug, cos_t, sin_t, place)
        a = _attn_call(qa, ka, v)
        m = _pool_call(p, w_pool[l], pool_scale[l])
        x = _mixout_call(a, m, x, modl, w_out[l].astype(bf), ln1_g[l], ln1_b[l])
        x = _moe_routed(x, modl, w_router_t, rbias_col, w_gate[l].astype(bf),
                        w_up[l].astype(bf), w_down[l].astype(bf), ln2_g[l], ln2_b[l])
    return x
```

```python
import functools

import jax
import jax.numpy as jnp
from jax import lax
from jax.experimental import pallas as pl
from jax.experimental.pallas import tpu as pltpu
from jax.experimental.pallas import tpu_sc as plsc

D_MODEL = 1024
DEPTH = 2
D_ATTN = 512
D_POOL = 512
N_HEADS = 8
HEAD_DIM = 64
ROT_DIM = 16
ROPE_THETA = 500000.0
MOBA_BLOCK = 256
MOBA_TOPK = 3
POOL_GROUP = 128
N_POOL_GROUPS = 4
D_IN = 3 * D_ATTN + D_POOL
N_EXPERTS = 16
EXPERTS_PER_GROUP = 4
D_EXPERT = 512
DEEPNORM_ALPHA = (2 * DEPTH) ** 0.25
N_MOD = 6
LN_EPS = 1e-5
NEG_INF = -1e30

LANES = 128
VMEM_LIMIT = 56 * 1024 * 1024

_HI = lax.Precision.HIGHEST
_NT = (((1,), (1,)), ((), ()))


def _params(sem):
    return pltpu.CompilerParams(dimension_semantics=sem, vmem_limit_bytes=VMEM_LIMIT)


def _nt_dot(a, b):
    return lax.dot_general(a, b, _NT, preferred_element_type=jnp.float32)


def _mod_kernel(c_ref, w_ref, b_ref, o_ref):
    c = c_ref[...]
    cond = c * (1.0 / (1.0 + jnp.exp(-c)))
    o_ref[0] = jnp.dot(cond, w_ref[0], precision=_HI,
                       preferred_element_type=jnp.float32) + b_ref[0]


def _mod_call(c, w_mod, b_mod):
    B = c.shape[0]
    return pl.pallas_call(
        _mod_kernel,
        grid=(DEPTH, N_MOD),
        in_specs=[
            pl.BlockSpec((B, D_MODEL), lambda l, j: (0, 0)),
            pl.BlockSpec((1, D_MODEL, D_MODEL), lambda l, j: (l, 0, j)),
            pl.BlockSpec((1, 1, D_MODEL), lambda l, j: (l, 0, j)),
        ],
        out_specs=pl.BlockSpec((1, B, D_MODEL), lambda l, j: (l, 0, j)),
        out_shape=jax.ShapeDtypeStruct((DEPTH, B, N_MOD * D_MODEL), jnp.float32),
        compiler_params=_params(("arbitrary", "arbitrary")),
    )(c, w_mod, b_mod.reshape(DEPTH, 1, N_MOD * D_MODEL))


def _layer_norm(x):
    mu = jnp.mean(x, axis=-1, keepdims=True)
    xc = x - mu
    var = jnp.mean(xc * xc, axis=-1, keepdims=True)
    return xc * lax.rsqrt(var + LN_EPS)


def _split_bf16(t):
    hi = t.astype(jnp.bfloat16)
    lo = (t - hi.astype(jnp.float32)).astype(jnp.bfloat16)
    return hi, lo


def _proj_kernel(x_ref, mod_ref, w_ref, cos_ref, sin_ref, place_ref,
                 qa_ref, ka_ref, v_ref, p_ref, kbar_ref):
    i = pl.program_id(1)

    @pl.when(i == 0)
    def _():
        kbar_ref[...] = jnp.zeros_like(kbar_ref)

    x = x_ref[0]
    shift = mod_ref[0, 0:1, :]
    scale = mod_ref[0, 1:2, :]
    h = (_layer_norm(x) * (1.0 + scale) + shift).astype(jnp.bfloat16)

    cos = cos_ref[...]
    sin = sin_ref[...]

    def proj(c0, width):
        return jnp.dot(h, w_ref[:, c0:c0 + width], preferred_element_type=jnp.float32)

    def slab(t, s):
        return t[:, LANES * s:LANES * (s + 1)]

    n_slab = D_ATTN // LANES
    q = proj(0, D_ATTN)
    qp = proj(D_IN, D_ATTN)
    q_slabs = [(slab(q, s) * cos + slab(qp, s) * sin) * (HEAD_DIM ** -0.5)
               for s in range(n_slab)]
    k = proj(D_ATTN, D_ATTN)
    kp = proj(D_IN + D_ATTN, D_ATTN)
    k_slabs = [slab(k, s) * cos + slab(kp, s) * sin for s in range(n_slab)]
    v_ref[0] = proj(2 * D_ATTN, D_ATTN).astype(jnp.bfloat16)
    p_ref[0] = proj(3 * D_ATTN, D_POOL)

    kmean = jnp.concatenate(
        [jnp.mean(ks, axis=0, keepdims=True) for ks in k_slabs], axis=1)
    kbar_ref[0, pl.ds(i, 1), :] = kmean

    kb = kbar_ref[0]
    kb_rows = jnp.concatenate([kb] * (LANES // 8), axis=0)
    r_head = lax.broadcasted_iota(jnp.int32, (LANES, D_ATTN), 1) >> 6
    c_head = lax.broadcasted_iota(jnp.int32, (LANES, D_ATTN), 0) >> 4
    kbd_hi, kbd_lo = _split_bf16(jnp.where(r_head == c_head, kb_rows, 0.0))
    q_hi, q_lo = _split_bf16(jnp.concatenate(q_slabs, axis=1))
    gate = _nt_dot(q_hi, kbd_hi) + (_nt_dot(q_lo, kbd_hi) + _nt_dot(q_hi, kbd_lo))

    lane = lax.broadcasted_iota(jnp.int32, (MOBA_BLOCK, LANES), 1)
    j_of = lane & 7
    past = j_of < i
    gm = jnp.where(past, gate, NEG_INF)
    rank = jnp.zeros((MOBA_BLOCK, LANES), jnp.int32)
    for r in range(1, 8):
        other = pltpu.roll(gm, r, 1)
        beats = (other > gm) | ((other == gm) & (j_of >= r))
        rank = rank + beats.astype(jnp.int32)
    allowed = (past & (rank < MOBA_TOPK)) | (j_of == i)
    bias = jnp.where(allowed, 0.0, NEG_INF).astype(jnp.bfloat16)
    bias_cols = jnp.dot(bias, place_ref[...], preferred_element_type=jnp.float32)

    for hd in range(N_HEADS):
        own = (lane < HEAD_DIM) if hd % 2 == 0 else (lane >= HEAD_DIM)
        col0 = HEAD_DIM if hd % 2 == 0 else 0
        qa = jnp.where(own, q_slabs[hd // 2], slab(bias_cols, hd))
        ka = jnp.where(own, k_slabs[hd // 2], jnp.where(lane == col0 + i, 1.0, 0.0))
        qa_ref[0, hd] = qa.astype(jnp.bfloat16)
        ka_ref[0, hd] = ka.astype(jnp.bfloat16)


def _proj_call(x, modl, w_in_aug, cos_t, sin_t, place):
    B, S, D = x.shape
    nb = S // MOBA_BLOCK
    tm = MOBA_BLOCK
    tab = pl.BlockSpec((tm, LANES), lambda b, i: (i, 0))
    head_spec = pl.BlockSpec((1, N_HEADS, tm, LANES), lambda b, i: (b, 0, i, 0))
    return pl.pallas_call(
        _proj_kernel,
        grid=(B, nb),
        in_specs=[
            pl.BlockSpec((1, tm, D), lambda b, i: (b, i, 0)),
            pl.BlockSpec((1, N_MOD, D), lambda b, i: (b, 0, 0)),
            pl.BlockSpec((D, D_IN + 2 * D_ATTN), lambda b, i: (0, 0)),
            tab, tab,
            pl.BlockSpec((LANES, N_HEADS * LANES), lambda b, i: (0, 0)),
        ],
        out_specs=[
            head_spec, head_spec,
            pl.BlockSpec((1, tm, D_ATTN), lambda b, i: (b, i, 0)),
            pl.BlockSpec((1, tm, D_POOL), lambda b, i: (b, i, 0)),
            pl.BlockSpec((1, nb, D_ATTN), lambda b, i: (b, 0, 0)),
        ],
        out_shape=[
            jax.ShapeDtypeStruct((B, N_HEADS, S, LANES), jnp.bfloat16),
            jax.ShapeDtypeStruct((B, N_HEADS, S, LANES), jnp.bfloat16),
            jax.ShapeDtypeStruct((B, S, D_ATTN), jnp.bfloat16),
            jax.ShapeDtypeStruct((B, S, D_POOL), jnp.float32),
            jax.ShapeDtypeStruct((B, nb, D_ATTN), jnp.float32),
        ],
        compiler_params=_params(("arbitrary", "arbitrary")),
    )(x, modl, w_in_aug, cos_t, sin_t, place)


def _attn_kernel(qa_ref, ka_ref, v_ref, o_ref):
    nb = v_ref.shape[1] // MOBA_BLOCK
    row = lax.broadcasted_iota(jnp.int32, (MOBA_BLOCK, MOBA_BLOCK), 0)
    col = lax.broadcasted_iota(jnp.int32, (MOBA_BLOCK, MOBA_BLOCK), 1)
    causal = col <= row
    lane = lax.broadcasted_iota(jnp.int32, (MOBA_BLOCK, LANES), 1)

    for i in reversed(range(nb)):
        r0 = i * MOBA_BLOCK
        outs = []
        for hh in range(2):
            q = qa_ref[0, hh, r0:r0 + MOBA_BLOCK, :]
            s_own = jnp.where(causal, _nt_dot(q, ka_ref[0, hh, r0:r0 + MOBA_BLOCK, :]), NEG_INF)
            m = jnp.max(s_own, axis=1, keepdims=True)
            if i > 0:
                s_past = _nt_dot(q, ka_ref[0, hh, 0:r0, :])
                m = jnp.maximum(m, jnp.max(s_past, axis=1, keepdims=True))
            p_own = jnp.exp(s_own - m)
            l = jnp.sum(p_own, axis=1, keepdims=True)
            acc = jnp.dot(p_own.astype(jnp.bfloat16), v_ref[0, r0:r0 + MOBA_BLOCK, :],
                          preferred_element_type=jnp.float32)
            if i > 0:
                p_past = jnp.exp(s_past - m)
                l = l + jnp.sum(p_past, axis=1, keepdims=True)
                acc = acc + jnp.dot(p_past.astype(jnp.bfloat16), v_ref[0, 0:r0, :],
                                    preferred_element_type=jnp.float32)
            outs.append(acc / l)
        o = jnp.where(lane < HEAD_DIM, outs[0], outs[1])
        o_ref[0, r0:r0 + MOBA_BLOCK, :] = o.astype(o_ref.dtype)


def _attn_call(qa, ka, v):
    B, _, S, _ = qa.shape
    n_pair = N_HEADS // 2
    pair_spec = pl.BlockSpec((1, 2, S, LANES), lambda b, hp: (b, hp, 0, 0))
    slab_spec = pl.BlockSpec((1, S, LANES), lambda b, hp: (b, 0, hp))
    return pl.pallas_call(
        _attn_kernel,
        grid=(B, n_pair),
        in_specs=[pair_spec, pair_spec, slab_spec],
        out_specs=slab_spec,
        out_shape=jax.ShapeDtypeStruct((B, S, D_ATTN), jnp.bfloat16),
        compiler_params=_params(("arbitrary", "arbitrary")),
    )(qa, ka, v)


def _pool_kernel(p_ref, w_ref, sc_ref, o_ref):
    g = pl.program_id(1)
    S = p_ref.shape[1]
    p = p_ref[0]
    t = lax.broadcasted_iota(jnp.int32, (S, LANES), 0)

    def shifted(x, k):
        return jnp.where(t >= k, pltpu.roll(x, k, 0), 0.0)

    win = p
    sums = []
    for step in range(N_POOL_GROUPS):
        win = win + shifted(win, 1 << step)
        sums.append(win)
    wsum = jnp.where(g == 0, sums[0],
                     jnp.where(g == 1, sums[1], jnp.where(g == 2, sums[2], sums[3])))
    window = jnp.left_shift(2, g)
    cnt = jnp.minimum(t + 1, window).astype(jnp.float32)
    d = (wsum / cnt - p).astype(jnp.bfloat16)
    y = jnp.dot(d, w_ref[0].astype(jnp.bfloat16), preferred_element_type=jnp.float32)
    o_ref[0] = (y * sc_ref[...]).astype(o_ref.dtype)


def _pool_call(p, w_pool_l, pool_scale_l):
    B, S, _ = p.shape
    slab = pl.BlockSpec((1, S, LANES), lambda b, g: (b, 0, g))
    return pl.pallas_call(
        _pool_kernel,
        grid=(B, N_POOL_GROUPS),
        in_specs=[
            slab,
            pl.BlockSpec((1, POOL_GROUP, POOL_GROUP), lambda b, g: (g, 0, 0)),
            pl.BlockSpec((1, LANES), lambda b, g: (0, g)),
        ],
        out_specs=slab,
        out_shape=jax.ShapeDtypeStruct((B, S, D_POOL), jnp.bfloat16),
        compiler_params=_params(("arbitrary", "arbitrary")),
    )(p, w_pool_l, pool_scale_l.reshape(1, D_POOL))


def _mixout_kernel(a_ref, m_ref, x_ref, mod_ref, w_ref, g_ref, b_ref, o_ref):
    y = jnp.dot(a_ref[0], w_ref[0:D_ATTN, :], preferred_element_type=jnp.float32)
    y = y + jnp.dot(m_ref[0], w_ref[D_ATTN:, :], preferred_element_type=jnp.float32)
    gate = mod_ref[0, 2:3, :]
    z = DEEPNORM_ALPHA * x_ref[0] + gate * y
    o_ref[0] = _layer_norm(z) * g_ref[...] + b_ref[...]


def _mixout_call(a, m, x, modl, w_out_bf, ln_g, ln_b):
    B, S, D = x.shape
    tm = 512
    vec = pl.BlockSpec((1, D), lambda b, i: (0, 0))
    return pl.pallas_call(
        _mixout_kernel,
        grid=(B, S // tm),
        in_specs=[
            pl.BlockSpec((1, tm, D_ATTN), lambda b, i: (b, i, 0)),
            pl.BlockSpec((1, tm, D_POOL), lambda b, i: (b, i, 0)),
            pl.BlockSpec((1, tm, D), lambda b, i: (b, i, 0)),
            pl.BlockSpec((1, N_MOD, D), lambda b, i: (b, 0, 0)),
            pl.BlockSpec((D, D), lambda b, i: (0, 0)),
            vec, vec,
        ],
        out_specs=pl.BlockSpec((1, tm, D), lambda b, i: (b, i, 0)),
        out_shape=jax.ShapeDtypeStruct((B, S, D), jnp.float32),
        compiler_params=_params(("arbitrary", "arbitrary")),
    )(a, m, x, modl, w_out_bf, ln_g.reshape(1, D), ln_b.reshape(1, D))


MOE_TILE = 1024
MOE_CHUNK = 160
MOE_ROWS = 2304
MOE_ROWS_ALLOC = MOE_ROWS + MOE_CHUNK
MOE_G_COLS = 768
MOE_EXPERTS_PER_STEP = 2


def _top2_rows(vals):
    def first_max(rows):
        m = rows[0]
        for v in rows[1:]:
            m = jnp.maximum(m, v)
        idx = jnp.full_like(m, float(len(rows) - 1))
        for k in range(len(rows) - 2, -1, -1):
            idx = jnp.where(rows[k] == m, float(k), idx)
        return m, idx

    m1, i1 = first_max(vals)
    rest = [jnp.where(i1 == float(k), -jnp.inf, v) for k, v in enumerate(vals)]
    m2, i2 = first_max(rest)
    return m1, i1, m2, i2


def _router_rows(logits_t, rb_ref):
    lg = [logits_t[e:e + 1, :] for e in range(N_EXPERTS)]
    mx = lg[0]
    for v in lg[1:]:
        mx = jnp.maximum(mx, v)
    ex = [jnp.exp(v - mx) for v in lg]
    den = ex[0]
    for v in ex[1:]:
        den = den + v
    scores = [v / den for v in ex]
    sel = [scores[e] + rb_ref[e:e + 1, :] for e in range(N_EXPERTS)]
    best_score = None
    best = None
    for g in range(N_EXPERTS // EXPERTS_PER_GROUP):
        m1, _, m2, _ = _top2_rows(sel[g * EXPERTS_PER_GROUP:(g + 1) * EXPERTS_PER_GROUP])
        gs = m1 + m2
        if g == 0:
            best_score, best = gs, jnp.zeros_like(gs)
        else:
            better = gs > best_score
            best_score = jnp.where(better, gs, best_score)
            best = jnp.where(better, float(g), best)
    masked = [jnp.where(best == float(e // EXPERTS_PER_GROUP), sel[e], NEG_INF)
              for e in range(N_EXPERTS)]
    _, i1, _, i2 = _top2_rows(masked)
    w1 = jnp.zeros_like(i1)
    w2 = jnp.zeros_like(i2)
    for e in range(N_EXPERTS):
        w1 = jnp.where(i1 == float(e), scores[e], w1)
        w2 = jnp.where(i2 == float(e), scores[e], w2)
    tot = w1 + w2
    return i1, i2, w1 / tot, w2 / tot


def _moe_route(x_ref, mod_ref, wr_ref, rb_ref, h_scr, col_scr, row_scr, ysh_scr, ysl_scr,
               start_smem, cnt_smem):
    T = MOE_TILE
    shift = mod_ref[0, 3:4, :]
    scale = mod_ref[0, 4:5, :]
    h = _layer_norm(x_ref[0]) * (1.0 + scale) + shift
    h_hi = h.astype(jnp.bfloat16)
    h_scr[...] = h_hi
    h_lo = (h - h_hi.astype(jnp.float32)).astype(jnp.bfloat16)
    wr_hi, wr_lo = _split_bf16(wr_ref[...])
    logits_t = _nt_dot(wr_hi, h_hi) + (_nt_dot(wr_lo, h_hi) + _nt_dot(wr_hi, h_lo))
    i1, i2, w1, w2 = _router_rows(logits_t, rb_ref)

    e_iota = lax.broadcasted_iota(jnp.int32, (N_EXPERTS, T), 0).astype(jnp.float32)
    sel1 = e_iota == i1
    sel2 = e_iota == i2
    onehot = jnp.where(sel1 | sel2, 1.0, 0.0)
    tr = lax.broadcasted_iota(jnp.int32, (T, T), 0)
    tc = lax.broadcasted_iota(jnp.int32, (T, T), 1)
    earlier = jnp.where(tr < tc, 1.0, 0.0).astype(jnp.bfloat16)
    excl = jnp.dot(onehot.astype(jnp.bfloat16), earlier,
                   preferred_element_type=jnp.float32)
    cnt = jnp.sum(onehot, axis=1, keepdims=True)

    e_col = lax.broadcasted_iota(jnp.int32, (N_EXPERTS, 1), 0)
    start_v = jnp.zeros((N_EXPERTS, 1), jnp.float32)
    run = jnp.int32(0)
    for ex in range(N_EXPERTS):
        padded = lax.shift_left(lax.shift_right_logical(cnt[ex, 0].astype(jnp.int32) + 15, 4), 4)
        start_smem[ex] = run
        cnt_smem[ex] = padded
        start_v = jnp.where(e_col == ex, run.astype(jnp.float32), start_v)
        run = run + padded

    slot = excl + start_v
    pos1 = jnp.sum(jnp.where(sel1, slot, 0.0), axis=0, keepdims=True)
    pos2 = jnp.sum(jnp.where(sel2, slot, 0.0), axis=0, keepdims=True)
    row_scr[0:1, :] = pos1
    row_scr[1:2, :] = pos2
    row_scr[2:3, :] = w1
    row_scr[3:4, :] = w2

    r128 = lax.broadcasted_iota(jnp.int32, (LANES, T), 0)
    terms = jnp.zeros((LANES, T), jnp.float32)
    k = 0
    for p in (pos1, pos2):
        a = jnp.floor(p * (1.0 / 64.0))
        for t in (a, p - 64.0 * a):
            terms = jnp.where(r128 == k, t, terms)
            k += 1
    eye = jnp.where(tr == tc, 1.0, 0.0).astype(jnp.bfloat16)
    col_scr[...] = _nt_dot(eye, terms.astype(jnp.bfloat16))

    tail = MOE_ROWS_ALLOC - 2 * T
    ysh_scr[2 * T:, :] = jnp.zeros((tail, D_MODEL), jnp.bfloat16)
    ysl_scr[2 * T:, :] = jnp.zeros((tail, D_MODEL), jnp.bfloat16)


def _moe_kernel(x_ref, mod_ref, wr_ref, rb_ref, wg_ref, wu_ref, wd_ref, g_ref, b_ref,
                o_ref, h_scr, col_scr, row_scr, ysh_scr, ysl_scr, start_smem, cnt_smem):
    step = pl.program_id(1)
    T = MOE_TILE

    @pl.when(step == 0)
    def _():
        _moe_route(x_ref, mod_ref, wr_ref, rb_ref, h_scr, col_scr, row_scr, ysh_scr, ysl_scr,
                   start_smem, cnt_smem)

    pos1_row = row_scr[0:1, :]
    pos2_row = row_scr[1:2, :]
    w1_row = row_scr[2:3, :]
    w2_row = row_scr[3:4, :]

    for k in range(MOE_EXPERTS_PER_STEP):
        e = step * MOE_EXPERTS_PER_STEP + k
        seg_start = start_smem[e]
        n_chunk = (cnt_smem[e] + (MOE_CHUNK - 1)) // MOE_CHUNK

        def chunk(j, carry, k=k, seg_start=seg_start):
            base = pl.multiple_of(seg_start + j * MOE_CHUNK, 16)
            r = (base + lax.broadcasted_iota(jnp.int32, (MOE_CHUNK, T), 0)).astype(jnp.float32)
            m1 = r == pos1_row
            m2 = r == pos2_row
            gather = jnp.where(m1 | m2, 1.0, 0.0).astype(jnp.bfloat16)
            w_row = jnp.sum(jnp.where(m1, w1_row, 0.0) + jnp.where(m2, w2_row, 0.0),
                            axis=1, keepdims=True)
            xs = jnp.dot(gather, h_scr[...],
                         preferred_element_type=jnp.float32).astype(jnp.bfloat16)
            gt = jnp.dot(xs, wg_ref[k], preferred_element_type=jnp.float32)
            up = jnp.dot(xs, wu_ref[k], preferred_element_type=jnp.float32)
            act = (gt * (1.0 / (1.0 + jnp.exp(-gt))) * up).astype(jnp.bfloat16)
            y = w_row * jnp.dot(act, wd_ref[k], preferred_element_type=jnp.float32)
            y_hi = y.astype(jnp.bfloat16)
            ysh_scr[pl.ds(base, MOE_CHUNK), :] = y_hi
            ysl_scr[pl.ds(base, MOE_CHUNK), :] = (y - y_hi.astype(jnp.float32)).astype(jnp.bfloat16)
            return carry

        lax.fori_loop(0, n_chunk, chunk, 0)

    @pl.when(step == N_EXPERTS // MOE_EXPERTS_PER_STEP - 1)
    def _():
        pos1 = 64.0 * col_scr[:, 0:1] + col_scr[:, 1:2]
        pos2 = 64.0 * col_scr[:, 2:3] + col_scr[:, 3:4]
        y = jnp.zeros((T, D_MODEL), jnp.float32)
        for c0 in range(0, MOE_ROWS, MOE_G_COLS):
            r = (c0 + lax.broadcasted_iota(jnp.int32, (T, MOE_G_COLS), 1)).astype(jnp.float32)
            scatter = jnp.where((r == pos1) | (r == pos2), 1.0, 0.0).astype(jnp.bfloat16)
            y = y + jnp.dot(scatter, ysh_scr[c0:c0 + MOE_G_COLS, :],
                            preferred_element_type=jnp.float32)
            y = y + jnp.dot(scatter, ysl_scr[c0:c0 + MOE_G_COLS, :],
                            preferred_element_type=jnp.float32)
        gate = mod_ref[0, 5:6, :]
        z = DEEPNORM_ALPHA * x_ref[0] + gate * y
        o_ref[0] = _layer_norm(z) * g_ref[...] + b_ref[...]


def _moe_call(x, modl, w_router_t, rbias_col, wg_bf, wu_bf, wd_bf, ln_g, ln_b):
    B, S, D = x.shape
    tm = MOE_TILE
    nt = S // tm
    eps = MOE_EXPERTS_PER_STEP
    vec = pl.BlockSpec((1, D), lambda t, e: (0, 0))
    xspec = pl.BlockSpec((1, tm, D), lambda t, e: (t // nt, t % nt, 0))
    return pl.pallas_call(
        _moe_kernel,
        grid=(B * nt, N_EXPERTS // eps),
        in_specs=[
            xspec,
            pl.BlockSpec((1, N_MOD, D), lambda t, e: (t // nt, 0, 0)),
            pl.BlockSpec((N_EXPERTS, D), lambda t, e: (0, 0)),
            pl.BlockSpec((N_EXPERTS, 1), lambda t, e: (0, 0)),
            pl.BlockSpec((eps, D, D_EXPERT), lambda t, e: (e, 0, 0)),
            pl.BlockSpec((eps, D, D_EXPERT), lambda t, e: (e, 0, 0)),
            pl.BlockSpec((eps, D_EXPERT, D), lambda t, e: (e, 0, 0)),
            vec, vec,
        ],
        out_specs=xspec,
        out_shape=jax.ShapeDtypeStruct((B, S, D), jnp.float32),
        scratch_shapes=[
            pltpu.VMEM((tm, D), jnp.bfloat16),
            pltpu.VMEM((tm, LANES), jnp.float32),
            pltpu.VMEM((8, tm), jnp.float32),
            pltpu.VMEM((MOE_ROWS_ALLOC, D), jnp.bfloat16),
            pltpu.VMEM((MOE_ROWS_ALLOC, D), jnp.bfloat16),
            pltpu.SMEM((N_EXPERTS,), jnp.int32),
            pltpu.SMEM((N_EXPERTS,), jnp.int32),
        ],
        compiler_params=_params(("arbitrary", "arbitrary")),
    )(x, modl, w_router_t, rbias_col, wg_bf, wu_bf, wd_bf,
      ln_g.reshape(1, D), ln_b.reshape(1, D))


SC_CORES = 2
SC_SUBCORES = 16
SC_MAX_INDEX_ROWS = 128
SC_RING = 4
SC_TILE_BYTES = 64 * 1024

ROUTE_TILE = 1024
EXPERT_CHUNK = 512


def _sc_gather_rows(table, idx):
    n_rows = idx.shape[0]
    width = table.shape[1]
    workers = SC_CORES * SC_SUBCORES
    rows = min(SC_MAX_INDEX_ROWS, SC_TILE_BYTES // (4 * width))
    per_worker = n_rows // workers
    assert per_worker * workers == n_rows and per_worker % rows == 0
    n_chunk = per_worker // rows
    assert n_chunk % SC_RING == 0
    mesh = plsc.VectorSubcoreMesh(core_axis_name="c", subcore_axis_name="s")
    row_buf = pltpu.VMEM((rows, width), table.dtype)
    idx_buf = pltpu.VMEM((rows,), jnp.int32)

    @functools.partial(
        pl.kernel, mesh=mesh,
        out_type=jax.ShapeDtypeStruct((n_rows, width), table.dtype),
        scratch_types=([idx_buf] * SC_RING + [row_buf] * SC_RING
                       + [pltpu.SemaphoreType.DMA] * SC_RING),
    )
    def gather(table_hbm, idx_hbm, out_hbm, *scratch):
        idx_v = scratch[:SC_RING]
        rows_v = scratch[SC_RING:2 * SC_RING]
        sems = scratch[2 * SC_RING:]
        wid = lax.axis_index("s") * SC_CORES + lax.axis_index("c")
        base = wid * per_worker

        def gather_copy(b):
            return pltpu.make_async_copy(table_hbm.at[idx_v[b]], rows_v[b], sems[b])

        def start_gather(j, b):
            off = pl.multiple_of(base + j * rows, 8)
            pltpu.sync_copy(idx_hbm.at[pl.ds(off, rows)], idx_v[b])
            gather_copy(b).start()

        for b in range(SC_RING):
            start_gather(b, b)

        def group(g, carry):
            for b in range(SC_RING):
                j = g * SC_RING + b
                gather_copy(b).wait()
                off = pl.multiple_of(base + j * rows, 8)
                pltpu.sync_copy(rows_v[b], out_hbm.at[pl.ds(off, rows)])

                @pl.when(j + SC_RING < n_chunk)
                def _():
                    start_gather(j + SC_RING, b)
            return carry

        lax.fori_loop(0, n_chunk // SC_RING, group, 0)

    return gather(table, idx)


def _route_kernel(x_ref, mod_ref, wr_ref, rb_ref, hp_ref, meta_ref, cnt_ref, carry_scr):
    t = pl.program_id(0)
    T = ROUTE_TILE

    @pl.when(t == 0)
    def _():
        carry_scr[...] = jnp.zeros_like(carry_scr)

    shift = mod_ref[0, 3:4, :]
    scale = mod_ref[0, 4:5, :]
    h = _layer_norm(x_ref[0]) * (1.0 + scale) + shift
    h_hi = h.astype(jnp.bfloat16)
    h_lo = (h - h_hi.astype(jnp.float32)).astype(jnp.bfloat16)
    hp_ref[...] = h

    wr_hi, wr_lo = _split_bf16(wr_ref[...])
    logits_t = _nt_dot(wr_hi, h_hi) + (_nt_dot(wr_lo, h_hi) + _nt_dot(wr_hi, h_lo))
    i1, i2, w1, w2 = _router_rows(logits_t, rb_ref)

    e_iota = lax.broadcasted_iota(jnp.int32, (N_EXPERTS, T), 0).astype(jnp.float32)
    sel1 = e_iota == i1
    sel2 = e_iota == i2
    onehot = jnp.where(sel1 | sel2, 1.0, 0.0)
    tr = lax.broadcasted_iota(jnp.int32, (T, T), 0)
    tc = lax.broadcasted_iota(jnp.int32, (T, T), 1)
    earlier = jnp.where(tr < tc, 1.0, 0.0).astype(jnp.bfloat16)
    rank = jnp.dot(onehot.astype(jnp.bfloat16), earlier,
                   preferred_element_type=jnp.float32) + carry_scr[:, 0:1]
    rank1 = jnp.sum(jnp.where(sel1, rank, 0.0), axis=0, keepdims=True)
    rank2 = jnp.sum(jnp.where(sel2, rank, 0.0), axis=0, keepdims=True)
    for r, row in enumerate((i1, i2, w1, w2, rank1, rank2)):
        meta_ref[r, 0] = row
    carry_scr[...] = carry_scr[...] + jnp.sum(onehot, axis=1, keepdims=True)
    cnt_ref[...] = carry_scr[...]


def _route_call(x, modl, w_router_t, rbias_col):
    B, S, D = x.shape
    tm = ROUTE_TILE
    nt = S // tm
    n_tiles = B * nt
    return pl.pallas_call(
        _route_kernel,
        grid=(n_tiles,),
        in_specs=[
            pl.BlockSpec((1, tm, D), lambda t: (t // nt, t % nt, 0)),
            pl.BlockSpec((1, N_MOD, D), lambda t: (t // nt, 0, 0)),
            pl.BlockSpec((N_EXPERTS, D), lambda t: (0, 0)),
            pl.BlockSpec((N_EXPERTS, 1), lambda t: (0, 0)),
        ],
        out_specs=[
            pl.BlockSpec((tm, D), lambda t: (t, 0)),
            pl.BlockSpec((6, 1, 1, tm), lambda t: (0, t, 0, 0)),
            pl.BlockSpec((N_EXPERTS, LANES), lambda t: (0, 0)),
        ],
        out_shape=[
            jax.ShapeDtypeStruct((B * S, D), jnp.float32),
            jax.ShapeDtypeStruct((6, n_tiles, 1, tm), jnp.float32),
            jax.ShapeDtypeStruct((N_EXPERTS, LANES), jnp.float32),
        ],
        scratch_shapes=[pltpu.VMEM((N_EXPERTS, LANES), jnp.float32)],
        compiler_params=_params(("arbitrary",)),
    )(x, modl, w_router_t, rbias_col)


def _dispatch_plan(meta, counts, n_tokens):
    n_rows = 2 * n_tokens + N_EXPERTS * EXPERT_CHUNK
    n_chunks = n_rows // EXPERT_CHUNK
    meta = meta.reshape(meta.shape[0], n_tokens)
    per_token = lambda r: meta[r]
    e1 = per_token(0).astype(jnp.int32)
    e2 = per_token(1).astype(jnp.int32)
    w1 = per_token(2)
    w2 = per_token(3)
    cnt = counts[:, 0].astype(jnp.int32)
    padded = (cnt + (EXPERT_CHUNK - 1)) // EXPERT_CHUNK * EXPERT_CHUNK
    ends = jnp.cumsum(padded)
    starts = ends - padded
    first = jnp.cumsum(cnt) - cnt
    pos1 = starts[e1] + per_token(4).astype(jnp.int32)
    pos2 = starts[e2] + per_token(5).astype(jnp.int32)
    order = jnp.argsort(jnp.concatenate([pos1, pos2])).astype(jnp.int32)
    sorted_tok = jnp.where(order >= n_tokens, order - n_tokens, order)
    chunk_row0 = jnp.arange(n_chunks, dtype=jnp.int32) * EXPERT_CHUNK
    chunk_expert = jnp.minimum(
        jnp.sum((ends[None, :] <= chunk_row0[:, None]).astype(jnp.int32), axis=1), N_EXPERTS - 1)
    local = (chunk_row0 - starts[chunk_expert])[:, None] + jnp.arange(EXPERT_CHUNK, dtype=jnp.int32)
    real = local < cnt[chunk_expert][:, None]
    nth = jnp.clip(first[chunk_expert][:, None] + local, 0, 2 * n_tokens - 1)
    spread = (chunk_row0[:, None] + jnp.arange(EXPERT_CHUNK, dtype=jnp.int32)) % n_tokens
    src = jnp.where(real, sorted_tok[nth], spread).reshape(n_rows)
    n_used = (ends[-1] // EXPERT_CHUNK).reshape(1).astype(jnp.int32)
    return pos1, pos2, w1, w2, src, chunk_expert, n_used


def _expert_kernel(ce_ref, nu_ref, xs_ref, wg_ref, wu_ref, wd_ref, o_ref):
    c = pl.program_id(0)

    @pl.when(c < nu_ref[0])
    def _():
        xs = xs_ref[...].astype(jnp.bfloat16)
        gt = jnp.dot(xs, wg_ref[0], preferred_element_type=jnp.float32)
        up = jnp.dot(xs, wu_ref[0], preferred_element_type=jnp.float32)
        act = (gt * (1.0 / (1.0 + jnp.exp(-gt))) * up).astype(jnp.bfloat16)
        o_ref[...] = jnp.dot(act, wd_ref[0], preferred_element_type=jnp.float32)

    @pl.when(c >= nu_ref[0])
    def _():
        o_ref[...] = jnp.zeros_like(o_ref)


def _expert_call(chunk_expert, n_used, xs, wg_bf, wu_bf, wd_bf):
    n_rows, D = xs.shape
    grid_spec = pltpu.PrefetchScalarGridSpec(
        num_scalar_prefetch=2,
        grid=(n_rows // EXPERT_CHUNK,),
        in_specs=[
            pl.BlockSpec((EXPERT_CHUNK, D), lambda c, ce, nu: (c, 0)),
            pl.BlockSpec((1, D, D_EXPERT), lambda c, ce, nu: (ce[c], 0, 0)),
            pl.BlockSpec((1, D, D_EXPERT), lambda c, ce, nu: (ce[c], 0, 0)),
            pl.BlockSpec((1, D_EXPERT, D), lambda c, ce, nu: (ce[c], 0, 0)),
        ],
        out_specs=pl.BlockSpec((EXPERT_CHUNK, D), lambda c, ce, nu: (c, 0)),
    )
    return pl.pallas_call(
        _expert_kernel,
        grid_spec=grid_spec,
        out_shape=jax.ShapeDtypeStruct((n_rows, D), jnp.float32),
        compiler_params=_params(("arbitrary",)),
    )(chunk_expert, n_used, xs, wg_bf, wu_bf, wd_bf)


def _combine_kernel(y1_ref, y2_ref, w1_ref, w2_ref, x_ref, mod_ref, g_ref, b_ref, o_ref):
    y = w1_ref[0] * y1_ref[0] + w2_ref[0] * y2_ref[0]
    gate = mod_ref[0, 5:6, :]
    z = DEEPNORM_ALPHA * x_ref[0] + gate * y
    o_ref[0] = _layer_norm(z) * g_ref[...] + b_ref[...]


def _combine_call(y1, y2, w1, w2, x, modl, ln_g, ln_b):
    B, S, D = x.shape
    tm = 512
    rows = pl.BlockSpec((1, tm, D), lambda b, i: (b, i, 0))
    col = pl.BlockSpec((1, tm, 1), lambda b, i: (b, i, 0))
    vec = pl.BlockSpec((1, D), lambda b, i: (0, 0))
    return pl.pallas_call(
        _combine_kernel,
        grid=(B, S // tm),
        in_specs=[rows, rows, col, col, rows,
                  pl.BlockSpec((1, N_MOD, D), lambda b, i: (b, 0, 0)), vec, vec],
        out_specs=rows,
        out_shape=jax.ShapeDtypeStruct((B, S, D), jnp.float32),
        compiler_params=_params(("arbitrary", "arbitrary")),
    )(y1.reshape(B, S, D), y2.reshape(B, S, D), w1.reshape(B, S, 1), w2.reshape(B, S, 1),
      x, modl, ln_g.reshape(1, D), ln_b.reshape(1, D))


def _moe_routed(x, modl, w_router_t, rbias_col, wg_bf, wu_bf, wd_bf, ln_g, ln_b):
    B, S, D = x.shape
    hp, meta, counts = _route_call(x, modl, w_router_t, rbias_col)
    pos1, pos2, w1, w2, src, chunk_expert, n_used = _dispatch_plan(meta, counts, B * S)
    xs = _sc_gather_rows(hp, src)
    ys = _expert_call(chunk_expert, n_used, xs, wg_bf, wu_bf, wd_bf)
    y1 = _sc_gather_rows(ys, pos1)
    y2 = _sc_gather_rows(ys, pos2)
    return _combine_call(y1, y2, w1, w2, x, modl, ln_g, ln_b)


def _rope_tables(S):
    half = ROT_DIM // 2
    inv_freq = ROPE_THETA ** (-(jnp.arange(half, dtype=jnp.float32) * 2.0 / ROT_DIM))
    ang = jnp.arange(S, dtype=jnp.float32)[:, None] * inv_freq[None, :]
    cos, sin = jnp.cos(ang), jnp.sin(ang)
    d = jnp.arange(LANES) % HEAD_DIM
    f = d % half
    rot = d[None, :] < ROT_DIM
    return jnp.where(rot, cos[:, f], 1.0), jnp.where(rot, sin[:, f], 0.0)


def _with_rotary_partner_columns(w_in_l):
    half = ROT_DIM // 2
    c = jnp.arange(2 * D_ATTN)
    d = c % HEAD_DIM
    src = jnp.where(d < half, c + half, c - half)
    sign = jnp.where(d < half, -1.0, jnp.where(d < ROT_DIM, 1.0, 0.0))
    partner = w_in_l[:, src] * sign[None, :]
    return jnp.concatenate([w_in_l, partner], axis=1)


def _bias_placement():
    src = jnp.arange(LANES)
    hd, u, j = src >> 4, (src >> 3) & 1, src & 7
    col0 = jnp.where(hd % 2 == 0, HEAD_DIM, 0)
    dst = LANES * hd + col0 + j
    onehot = (jnp.arange(N_HEADS * LANES)[None, :] == dst[:, None]) & (u[:, None] == 1)
    return onehot.astype(jnp.bfloat16)


def kernel(x, c, w_mod, b_mod, w_in, w_pool, pool_scale, w_out, ln1_g, ln1_b,
           w_router, router_bias, w_gate, w_up, w_down, ln2_g, ln2_b):
    B, S, D = x.shape
    bf = jnp.bfloat16
    mod = _mod_call(c, w_mod, b_mod).reshape(DEPTH, B, N_MOD, D)
    cos_t, sin_t = _rope_tables(S)
    place = _bias_placement()
    w_router_t = w_router.T
    rbias_col = router_bias.reshape(N_EXPERTS, 1)
    for l in range(DEPTH):
        modl = mod[l]
        w_in_aug = _with_rotary_partner_columns(w_in[l]).astype(bf)
        qa, ka, v, p, _ = _proj_call(x, modl, w_in_aug, cos_t, sin_t, place)
        a = _attn_call(qa, ka, v)
        m = _pool_call(p, w_pool[l], pool_scale[l])
        x = _mixout_call(a, m, x, modl, w_out[l].astype(bf), ln1_g[l], ln1_b[l])
        x = _moe_routed(x, modl, w_router_t, rbias_col, w_gate[l].astype(bf),
                        w_up[l].astype(bf), w_down[l].astype(bf), ln2_g[l], ln2_b[l])
    return x
```

```python
import functools

import jax
import jax.numpy as jnp
from jax import lax
from jax.experimental import pallas as pl
from jax.experimental.pallas import tpu as pltpu
from jax.experimental.pallas import tpu_sc as plsc

D_MODEL = 1024
DEPTH = 2
D_ATTN = 512
D_POOL = 512
N_HEADS = 8
HEAD_DIM = 64
ROT_DIM = 16
ROPE_THETA = 500000.0
MOBA_BLOCK = 256
MOBA_TOPK = 3
POOL_GROUP = 128
N_POOL_GROUPS = 4
D_IN = 3 * D_ATTN + D_POOL
N_EXPERTS = 16
EXPERTS_PER_GROUP = 4
D_EXPERT = 512
DEEPNORM_ALPHA = (2 * DEPTH) ** 0.25
N_MOD = 6
LN_EPS = 1e-5
NEG_INF = -1e30

LANES = 128
VMEM_LIMIT = 56 * 1024 * 1024

_HI = lax.Precision.HIGHEST
_NT = (((1,), (1,)), ((), ()))


def _params(sem):
    return pltpu.CompilerParams(dimension_semantics=sem, vmem_limit_bytes=VMEM_LIMIT)


def _nt_dot(a, b):
    return lax.dot_general(a, b, _NT, preferred_element_type=jnp.float32)


def _mod_kernel(c_ref, w_ref, b_ref, o_ref):
    c = c_ref[...]
    cond = c * (1.0 / (1.0 + jnp.exp(-c)))
    o_ref[0] = jnp.dot(cond, w_ref[0], precision=_HI,
                       preferred_element_type=jnp.float32) + b_ref[0]


def _mod_call(c, w_mod, b_mod):
    B = c.shape[0]
    return pl.pallas_call(
        _mod_kernel,
        grid=(DEPTH, N_MOD),
        in_specs=[
            pl.BlockSpec((B, D_MODEL), lambda l, j: (0, 0)),
            pl.BlockSpec((1, D_MODEL, D_MODEL), lambda l, j: (l, 0, j)),
            pl.BlockSpec((1, 1, D_MODEL), lambda l, j: (l, 0, j)),
        ],
        out_specs=pl.BlockSpec((1, B, D_MODEL), lambda l, j: (l, 0, j)),
        out_shape=jax.ShapeDtypeStruct((DEPTH, B, N_MOD * D_MODEL), jnp.float32),
        compiler_params=_params(("arbitrary", "arbitrary")),
    )(c, w_mod, b_mod.reshape(DEPTH, 1, N_MOD * D_MODEL))


def _layer_norm(x):
    mu = jnp.mean(x, axis=-1, keepdims=True)
    xc = x - mu
    var = jnp.mean(xc * xc, axis=-1, keepdims=True)
    return xc * lax.rsqrt(var + LN_EPS)


def _split_bf16(t):
    hi = t.astype(jnp.bfloat16)
    lo = (t - hi.astype(jnp.float32)).astype(jnp.bfloat16)
    return hi, lo


def _proj_kernel(x_ref, mod_ref, w_ref, cos_ref, sin_ref, place_ref,
                 qa_ref, ka_ref, v_ref, p_ref, kbar_ref):
    i = pl.program_id(1)

    @pl.when(i == 0)
    def _():
        kbar_ref[...] = jnp.zeros_like(kbar_ref)

    x = x_ref[0]
    shift = mod_ref[0, 0:1, :]
    scale = mod_ref[0, 1:2, :]
    h = (_layer_norm(x) * (1.0 + scale) + shift).astype(jnp.bfloat16)

    cos = cos_ref[...]
    sin = sin_ref[...]

    def proj(c0, width):
        return jnp.dot(h, w_ref[:, c0:c0 + width], preferred_element_type=jnp.float32)

    def slab(t, s):
        return t[:, LANES * s:LANES * (s + 1)]

    n_slab = D_ATTN // LANES
    q = proj(0, D_ATTN)
    qp = proj(D_IN, D_ATTN)
    q_slabs = [(slab(q, s) * cos + slab(qp, s) * sin) * (HEAD_DIM ** -0.5)
               for s in range(n_slab)]
    k = proj(D_ATTN, D_ATTN)
    kp = proj(D_IN + D_ATTN, D_ATTN)
    k_slabs = [slab(k, s) * cos + slab(kp, s) * sin for s in range(n_slab)]
    v_ref[0] = proj(2 * D_ATTN, D_ATTN).astype(jnp.bfloat16)
    p_ref[0] = proj(3 * D_ATTN, D_POOL)

    kmean = jnp.concatenate(
        [jnp.mean(ks, axis=0, keepdims=True) for ks in k_slabs], axis=1)
    kbar_ref[0, pl.ds(i, 1), :] = kmean

    kb = kbar_ref[0]
    kb_rows = jnp.concatenate([kb] * (LANES // 8), axis=0)
    r_head = lax.broadcasted_iota(jnp.int32, (LANES, D_ATTN), 1) >> 6
    c_head = lax.broadcasted_iota(jnp.int32, (LANES, D_ATTN), 0) >> 4
    kbd_hi, kbd_lo = _split_bf16(jnp.where(r_head == c_head, kb_rows, 0.0))
    q_hi, q_lo = _split_bf16(jnp.concatenate(q_slabs, axis=1))
    gate = _nt_dot(q_hi, kbd_hi) + (_nt_dot(q_lo, kbd_hi) + _nt_dot(q_hi, kbd_lo))

    lane = lax.broadcasted_iota(jnp.int32, (MOBA_BLOCK, LANES), 1)
    j_of = lane & 7
    past = j_of < i
    gm = jnp.where(past, gate, NEG_INF)
    rank = jnp.zeros((MOBA_BLOCK, LANES), jnp.int32)
    for r in range(1, 8):
        other = pltpu.roll(gm, r, 1)
        beats = (other > gm) | ((other == gm) & (j_of >= r))
        rank = rank + beats.astype(jnp.int32)
    allowed = (past & (rank < MOBA_TOPK)) | (j_of == i)
    bias = jnp.where(allowed, 0.0, NEG_INF).astype(jnp.bfloat16)
    bias_cols = jnp.dot(bias, place_ref[...], preferred_element_type=jnp.float32)

    for hd in range(N_HEADS):
        own = (lane < HEAD_DIM) if hd % 2 == 0 else (lane >= HEAD_DIM)
        col0 = HEAD_DIM if hd % 2 == 0 else 0
        qa = jnp.where(own, q_slabs[hd // 2], slab(bias_cols, hd))
        ka = jnp.where(own, k_slabs[hd // 2], jnp.where(lane == col0 + i, 1.0, 0.0))
        qa_ref[0, hd] = qa.astype(jnp.bfloat16)
        ka_ref[0, hd] = ka.astype(jnp.bfloat16)


def _proj_call(x, modl, w_in_aug, cos_t, sin_t, place):
    B, S, D = x.shape
    nb = S // MOBA_BLOCK
    tm = MOBA_BLOCK
    tab = pl.BlockSpec((tm, LANES), lambda b, i: (i, 0))
    head_spec = pl.BlockSpec((1, N_HEADS, tm, LANES), lambda b, i: (b, 0, i, 0))
    return pl.pallas_call(
        _proj_kernel,
        grid=(B, nb),
        in_specs=[
            pl.BlockSpec((1, tm, D), lambda b, i: (b, i, 0)),
            pl.BlockSpec((1, N_MOD, D), lambda b, i: (b, 0, 0)),
            pl.BlockSpec((D, D_IN + 2 * D_ATTN), lambda b, i: (0, 0)),
            tab, tab,
            pl.BlockSpec((LANES, N_HEADS * LANES), lambda b, i: (0, 0)),
        ],
        out_specs=[
            head_spec, head_spec,
            pl.BlockSpec((1, tm, D_ATTN), lambda b, i: (b, i, 0)),
            pl.BlockSpec((1, tm, D_POOL), lambda b, i: (b, i, 0)),
            pl.BlockSpec((1, nb, D_ATTN), lambda b, i: (b, 0, 0)),
        ],
        out_shape=[
            jax.ShapeDtypeStruct((B, N_HEADS, S, LANES), jnp.bfloat16),
            jax.ShapeDtypeStruct((B, N_HEADS, S, LANES), jnp.bfloat16),
            jax.ShapeDtypeStruct((B, S, D_ATTN), jnp.bfloat16),
            jax.ShapeDtypeStruct((B, S, D_POOL), jnp.float32),
            jax.ShapeDtypeStruct((B, nb, D_ATTN), jnp.float32),
        ],
        compiler_params=_params(("arbitrary", "arbitrary")),
    )(x, modl, w_in_aug, cos_t, sin_t, place)


def _attn_kernel(qa_ref, ka_ref, v_ref, o_ref):
    nb = v_ref.shape[1] // MOBA_BLOCK
    row = lax.broadcasted_iota(jnp.int32, (MOBA_BLOCK, MOBA_BLOCK), 0)
    col = lax.broadcasted_iota(jnp.int32, (MOBA_BLOCK, MOBA_BLOCK), 1)
    causal = col <= row
    lane = lax.broadcasted_iota(jnp.int32, (MOBA_BLOCK, LANES), 1)

    for i in reversed(range(nb)):
        r0 = i * MOBA_BLOCK
        outs = []
        for hh in range(2):
            q = qa_ref[0, hh, r0:r0 + MOBA_BLOCK, :]
            s_own = jnp.where(causal, _nt_dot(q, ka_ref[0, hh, r0:r0 + MOBA_BLOCK, :]), NEG_INF)
            m = jnp.max(s_own, axis=1, keepdims=True)
            if i > 0:
                s_past = _nt_dot(q, ka_ref[0, hh, 0:r0, :])
                m = jnp.maximum(m, jnp.max(s_past, axis=1, keepdims=True))
            p_own = jnp.exp(s_own - m)
            l = jnp.sum(p_own, axis=1, keepdims=True)
            acc = jnp.dot(p_own.astype(jnp.bfloat16), v_ref[0, r0:r0 + MOBA_BLOCK, :],
                          preferred_element_type=jnp.float32)
            if i > 0:
                p_past = jnp.exp(s_past - m)
                l = l + jnp.sum(p_past, axis=1, keepdims=True)
                acc = acc + jnp.dot(p_past.astype(jnp.bfloat16), v_ref[0, 0:r0, :],
                                    preferred_element_type=jnp.float32)
            outs.append(acc / l)
        o = jnp.where(lane < HEAD_DIM, outs[0], outs[1])
        o_ref[0, r0:r0 + MOBA_BLOCK, :] = o.astype(o_ref.dtype)


def _attn_call(qa, ka, v):
    B, _, S, _ = qa.shape
    n_pair = N_HEADS // 2
    pair_spec = pl.BlockSpec((1, 2, S, LANES), lambda b, hp: (b, hp, 0, 0))
    slab_spec = pl.BlockSpec((1, S, LANES), lambda b, hp: (b, 0, hp))
    return pl.pallas_call(
        _attn_kernel,
        grid=(B, n_pair),
        in_specs=[pair_spec, pair_spec, slab_spec],
        out_specs=slab_spec,
        out_shape=jax.ShapeDtypeStruct((B, S, D_ATTN), jnp.bfloat16),
        compiler_params=_params(("arbitrary", "arbitrary")),
    )(qa, ka, v)


def _pool_kernel(p_ref, w_ref, sc_ref, o_ref):
    g = pl.program_id(1)
    S = p_ref.shape[1]
    p = p_ref[0]
    t = lax.broadcasted_iota(jnp.int32, (S, LANES), 0)

    def shifted(x, k):
        return jnp.where(t >= k, pltpu.roll(x, k, 0), 0.0)

    win = p
    sums = []
    for step in range(N_POOL_GROUPS):
        win = win + shifted(win, 1 << step)
        sums.append(win)
    wsum = jnp.where(g == 0, sums[0],
                     jnp.where(g == 1, sums[1], jnp.where(g == 2, sums[2], sums[3])))
    window = jnp.left_shift(2, g)
    cnt = jnp.minimum(t + 1, window).astype(jnp.float32)
    d = (wsum / cnt - p).astype(jnp.bfloat16)
    y = jnp.dot(d, w_ref[0].astype(jnp.bfloat16), preferred_element_type=jnp.float32)
    o_ref[0] = (y * sc_ref[...]).astype(o_ref.dtype)


def _pool_call(p, w_pool_l, pool_scale_l):
    B, S, _ = p.shape
    slab = pl.BlockSpec((1, S, LANES), lambda b, g: (b, 0, g))
    return pl.pallas_call(
        _pool_kernel,
        grid=(B, N_POOL_GROUPS),
        in_specs=[
            slab,
            pl.BlockSpec((1, POOL_GROUP, POOL_GROUP), lambda b, g: (g, 0, 0)),
            pl.BlockSpec((1, LANES), lambda b, g: (0, g)),
        ],
        out_specs=slab,
        out_shape=jax.ShapeDtypeStruct((B, S, D_POOL), jnp.bfloat16),
        compiler_params=_params(("arbitrary", "arbitrary")),
    )(p, w_pool_l, pool_scale_l.reshape(1, D_POOL))


def _mixout_kernel(a_ref, m_ref, x_ref, mod_ref, w_ref, g_ref, b_ref, o_ref):
    y = jnp.dot(a_ref[0], w_ref[0:D_ATTN, :], preferred_element_type=jnp.float32)
    y = y + jnp.dot(m_ref[0], w_ref[D_ATTN:, :], preferred_element_type=jnp.float32)
    gate = mod_ref[0, 2:3, :]
    z = DEEPNORM_ALPHA * x_ref[0] + gate * y
    o_ref[0] = _layer_norm(z) * g_ref[...] + b_ref[...]


def _mixout_call(a, m, x, modl, w_out_bf, ln_g, ln_b):
    B, S, D = x.shape
    tm = 512
    vec = pl.BlockSpec((1, D), lambda b, i: (0, 0))
    return pl.pallas_call(
        _mixout_kernel,
        grid=(B, S // tm),
        in_specs=[
            pl.BlockSpec((1, tm, D_ATTN), lambda b, i: (b, i, 0)),
            pl.BlockSpec((1, tm, D_POOL), lambda b, i: (b, i, 0)),
            pl.BlockSpec((1, tm, D), lambda b, i: (b, i, 0)),
            pl.BlockSpec((1, N_MOD, D), lambda b, i: (b, 0, 0)),
            pl.BlockSpec((D, D), lambda b, i: (0, 0)),
            vec, vec,
        ],
        out_specs=pl.BlockSpec((1, tm, D), lambda b, i: (b, i, 0)),
        out_shape=jax.ShapeDtypeStruct((B, S, D), jnp.float32),
        compiler_params=_params(("arbitrary", "arbitrary")),
    )(a, m, x, modl, w_out_bf, ln_g.reshape(1, D), ln_b.reshape(1, D))


MOE_TILE = 1024
MOE_CHUNK = 160
MOE_ROWS = 2304
MOE_ROWS_ALLOC = MOE_ROWS + MOE_CHUNK
MOE_G_COLS = 768
MOE_EXPERTS_PER_STEP = 2


def _top2_rows(vals):
    def first_max(rows):
        m = rows[0]
        for v in rows[1:]:
            m = jnp.maximum(m, v)
        idx = jnp.full_like(m, float(len(rows) - 1))
        for k in range(len(rows) - 2, -1, -1):
            idx = jnp.where(rows[k] == m, float(k), idx)
        return m, idx

    m1, i1 = first_max(vals)
    rest = [jnp.where(i1 == float(k), -jnp.inf, v) for k, v in enumerate(vals)]
    m2, i2 = first_max(rest)
    return m1, i1, m2, i2


def _router_rows(logits_t, rb_ref):
    lg = [logits_t[e:e + 1, :] for e in range(N_EXPERTS)]
    mx = lg[0]
    for v in lg[1:]:
        mx = jnp.maximum(mx, v)
    ex = [jnp.exp(v - mx) for v in lg]
    den = ex[0]
    for v in ex[1:]:
        den = den + v
    scores = [v / den for v in ex]
    sel = [scores[e] + rb_ref[e:e + 1, :] for e in range(N_EXPERTS)]
    best_score = None
    best = None
    for g in range(N_EXPERTS // EXPERTS_PER_GROUP):
        m1, _, m2, _ = _top2_rows(sel[g * EXPERTS_PER_GROUP:(g + 1) * EXPERTS_PER_GROUP])
        gs = m1 + m2
        if g == 0:
            best_score, best = gs, jnp.zeros_like(gs)
        else:
            better = gs > best_score
            best_score = jnp.where(better, gs, best_score)
            best = jnp.where(better, float(g), best)
    masked = [jnp.where(best == float(e // EXPERTS_PER_GROUP), sel[e], NEG_INF)
              for e in range(N_EXPERTS)]
    _, i1, _, i2 = _top2_rows(masked)
    w1 = jnp.zeros_like(i1)
    w2 = jnp.zeros_like(i2)
    for e in range(N_EXPERTS):
        w1 = jnp.where(i1 == float(e), scores[e], w1)
        w2 = jnp.where(i2 == float(e), scores[e], w2)
    tot = w1 + w2
    return i1, i2, w1 / tot, w2 / tot


def _moe_route(x_ref, mod_ref, wr_ref, rb_ref, h_scr, col_scr, row_scr, ysh_scr, ysl_scr,
               start_smem, cnt_smem):
    T = MOE_TILE
    shift = mod_ref[0, 3:4, :]
    scale = mod_ref[0, 4:5, :]
    h = _layer_norm(x_ref[0]) * (1.0 + scale) + shift
    h_hi = h.astype(jnp.bfloat16)
    h_scr[...] = h_hi
    h_lo = (h - h_hi.astype(jnp.float32)).astype(jnp.bfloat16)
    wr_hi, wr_lo = _split_bf16(wr_ref[...])
    logits_t = _nt_dot(wr_hi, h_hi) + (_nt_dot(wr_lo, h_hi) + _nt_dot(wr_hi, h_lo))
    i1, i2, w1, w2 = _router_rows(logits_t, rb_ref)

    e_iota = lax.broadcasted_iota(jnp.int32, (N_EXPERTS, T), 0).astype(jnp.float32)
    sel1 = e_iota == i1
    sel2 = e_iota == i2
    onehot = jnp.where(sel1 | sel2, 1.0, 0.0)
    tr = lax.broadcasted_iota(jnp.int32, (T, T), 0)
    tc = lax.broadcasted_iota(jnp.int32, (T, T), 1)
    earlier = jnp.where(tr < tc, 1.0, 0.0).astype(jnp.bfloat16)
    excl = jnp.dot(onehot.astype(jnp.bfloat16), earlier,
                   preferred_element_type=jnp.float32)
    cnt = jnp.sum(onehot, axis=1, keepdims=True)

    e_col = lax.broadcasted_iota(jnp.int32, (N_EXPERTS, 1), 0)
    start_v = jnp.zeros((N_EXPERTS, 1), jnp.float32)
    run = jnp.int32(0)
    for ex in range(N_EXPERTS):
        padded = lax.shift_left(lax.shift_right_logical(cnt[ex, 0].astype(jnp.int32) + 15, 4), 4)
        start_smem[ex] = run
        cnt_smem[ex] = padded
        start_v = jnp.where(e_col == ex, run.astype(jnp.float32), start_v)
        run = run + padded

    slot = excl + start_v
    pos1 = jnp.sum(jnp.where(sel1, slot, 0.0), axis=0, keepdims=True)
    pos2 = jnp.sum(jnp.where(sel2, slot, 0.0), axis=0, keepdims=True)
    row_scr[0:1, :] = pos1
    row_scr[1:2, :] = pos2
    row_scr[2:3, :] = w1
    row_scr[3:4, :] = w2

    r128 = lax.broadcasted_iota(jnp.int32, (LANES, T), 0)
    terms = jnp.zeros((LANES, T), jnp.float32)
    k = 0
    for p in (pos1, pos2):
        a = jnp.floor(p * (1.0 / 64.0))
        for t in (a, p - 64.0 * a):
            terms = jnp.where(r128 == k, t, terms)
            k += 1
    eye = jnp.where(tr == tc, 1.0, 0.0).astype(jnp.bfloat16)
    col_scr[...] = _nt_dot(eye, terms.astype(jnp.bfloat16))

    tail = MOE_ROWS_ALLOC - 2 * T
    ysh_scr[2 * T:, :] = jnp.zeros((tail, D_MODEL), jnp.bfloat16)
    ysl_scr[2 * T:, :] = jnp.zeros((tail, D_MODEL), jnp.bfloat16)


def _moe_kernel(x_ref, mod_ref, wr_ref, rb_ref, wg_ref, wu_ref, wd_ref, g_ref, b_ref,
                o_ref, h_scr, col_scr, row_scr, ysh_scr, ysl_scr, start_smem, cnt_smem):
    step = pl.program_id(1)
    T = MOE_TILE

    @pl.when(step == 0)
    def _():
        _moe_route(x_ref, mod_ref, wr_ref, rb_ref, h_scr, col_scr, row_scr, ysh_scr, ysl_scr,
                   start_smem, cnt_smem)

    pos1_row = row_scr[0:1, :]
    pos2_row = row_scr[1:2, :]
    w1_row = row_scr[2:3, :]
    w2_row = row_scr[3:4, :]

    for k in range(MOE_EXPERTS_PER_STEP):
        e = step * MOE_EXPERTS_PER_STEP + k
        seg_start = start_smem[e]
        n_chunk = (cnt_smem[e] + (MOE_CHUNK - 1)) // MOE_CHUNK

        def chunk(j, carry, k=k, seg_start=seg_start):
            base = pl.multiple_of(seg_start + j * MOE_CHUNK, 16)
            r = (base + lax.broadcasted_iota(jnp.int32, (MOE_CHUNK, T), 0)).astype(jnp.float32)
            m1 = r == pos1_row
            m2 = r == pos2_row
            gather = jnp.where(m1 | m2, 1.0, 0.0).astype(jnp.bfloat16)
            w_row = jnp.sum(jnp.where(m1, w1_row, 0.0) + jnp.where(m2, w2_row, 0.0),
                            axis=1, keepdims=True)
            xs = jnp.dot(gather, h_scr[...],
                         preferred_element_type=jnp.float32).astype(jnp.bfloat16)
            gt = jnp.dot(xs, wg_ref[k], preferred_element_type=jnp.float32)
            up = jnp.dot(xs, wu_ref[k], preferred_element_type=jnp.float32)
            act = (gt * (1.0 / (1.0 + jnp.exp(-gt))) * up).astype(jnp.bfloat16)
            y = w_row * jnp.dot(act, wd_ref[k], preferred_element_type=jnp.float32)
            y_hi = y.astype(jnp.bfloat16)
            ysh_scr[pl.ds(base, MOE_CHUNK), :] = y_hi
            ysl_scr[pl.ds(base, MOE_CHUNK), :] = (y - y_hi.astype(jnp.float32)).astype(jnp.bfloat16)
            return carry

        lax.fori_loop(0, n_chunk, chunk, 0)

    @pl.when(step == N_EXPERTS // MOE_EXPERTS_PER_STEP - 1)
    def _():
        pos1 = 64.0 * col_scr[:, 0:1] + col_scr[:, 1:2]
        pos2 = 64.0 * col_scr[:, 2:3] + col_scr[:, 3:4]
        y = jnp.zeros((T, D_MODEL), jnp.float32)
        for c0 in range(0, MOE_ROWS, MOE_G_COLS):
            r = (c0 + lax.broadcasted_iota(jnp.int32, (T, MOE_G_COLS), 1)).astype(jnp.float32)
            scatter = jnp.where((r == pos1) | (r == pos2), 1.0, 0.0).astype(jnp.bfloat16)
            y = y + jnp.dot(scatter, ysh_scr[c0:c0 + MOE_G_COLS, :],
                            preferred_element_type=jnp.float32)
            y = y + jnp.dot(scatter, ysl_scr[c0:c0 + MOE_G_COLS, :],
                            preferred_element_type=jnp.float32)
        gate = mod_ref[0, 5:6, :]
        z = DEEPNORM_ALPHA * x_ref[0] + gate * y
        o_ref[0] = _layer_norm(z) * g_ref[...] + b_ref[...]


def _moe_call(x, modl, w_router_t, rbias_col, wg_bf, wu_bf, wd_bf, ln_g, ln_b):
    B, S, D = x.shape
    tm = MOE_TILE
    nt = S // tm
    eps = MOE_EXPERTS_PER_STEP
    vec = pl.BlockSpec((1, D), lambda t, e: (0, 0))
    xspec = pl.BlockSpec((1, tm, D), lambda t, e: (t // nt, t % nt, 0))
    return pl.pallas_call(
        _moe_kernel,
        grid=(B * nt, N_EXPERTS // eps),
        in_specs=[
            xspec,
            pl.BlockSpec((1, N_MOD, D), lambda t, e: (t // nt, 0, 0)),
            pl.BlockSpec((N_EXPERTS, D), lambda t, e: (0, 0)),
            pl.BlockSpec((N_EXPERTS, 1), lambda t, e: (0, 0)),
            pl.BlockSpec((eps, D, D_EXPERT), lambda t, e: (e, 0, 0)),
            pl.BlockSpec((eps, D, D_EXPERT), lambda t, e: (e, 0, 0)),
            pl.BlockSpec((eps, D_EXPERT, D), lambda t, e: (e, 0, 0)),
            vec, vec,
        ],
        out_specs=xspec,
        out_shape=jax.ShapeDtypeStruct((B, S, D), jnp.float32),
        scratch_shapes=[
            pltpu.VMEM((tm, D), jnp.bfloat16),
            pltpu.VMEM((tm, LANES), jnp.float32),
            pltpu.VMEM((8, tm), jnp.float32),
            pltpu.VMEM((MOE_ROWS_ALLOC, D), jnp.bfloat16),
            pltpu.VMEM((MOE_ROWS_ALLOC, D), jnp.bfloat16),
            pltpu.SMEM((N_EXPERTS,), jnp.int32),
            pltpu.SMEM((N_EXPERTS,), jnp.int32),
        ],
        compiler_params=_params(("arbitrary", "arbitrary")),
    )(x, modl, w_router_t, rbias_col, wg_bf, wu_bf, wd_bf,
      ln_g.reshape(1, D), ln_b.reshape(1, D))


SC_CORES = 2
SC_SUBCORES = 16
SC_MAX_INDEX_ROWS = 128
SC_RING = 4
SC_TILE_BYTES = 64 * 1024

ROUTE_TILE = 1024
EXPERT_CHUNK = 512


def _sc_gather_rows(table, idx):
    n_rows = idx.shape[0]
    width = table.shape[1]
    workers = SC_CORES * SC_SUBCORES
    rows = min(SC_MAX_INDEX_ROWS, SC_TILE_BYTES // (4 * width))
    per_worker = n_rows // workers
    assert per_worker * workers == n_rows and per_worker % rows == 0
    n_chunk = per_worker // rows
    assert n_chunk % SC_RING == 0
    mesh = plsc.VectorSubcoreMesh(core_axis_name="c", subcore_axis_name="s")
    row_buf = pltpu.VMEM((rows, width), table.dtype)
    idx_buf = pltpu.VMEM((rows,), jnp.int32)

    @functools.partial(
        pl.kernel, mesh=mesh,
        out_type=jax.ShapeDtypeStruct((n_rows, width), table.dtype),
        scratch_types=([idx_buf] * SC_RING + [row_buf] * SC_RING
                       + [pltpu.SemaphoreType.DMA] * SC_RING),
    )
    def gather(table_hbm, idx_hbm, out_hbm, *scratch):
        idx_v = scratch[:SC_RING]
        rows_v = scratch[SC_RING:2 * SC_RING]
        sems = scratch[2 * SC_RING:]
        wid = lax.axis_index("s") * SC_CORES + lax.axis_index("c")
        base = wid * per_worker

        def gather_copy(b):
            return pltpu.make_async_copy(table_hbm.at[idx_v[b]], rows_v[b], sems[b])

        def start_gather(j, b):
            off = pl.multiple_of(base + j * rows, 8)
            pltpu.sync_copy(idx_hbm.at[pl.ds(off, rows)], idx_v[b])
            gather_copy(b).start()

        for b in range(SC_RING):
            start_gather(b, b)

        def group(g, carry):
            for b in range(SC_RING):
                j = g * SC_RING + b
                gather_copy(b).wait()
                off = pl.multiple_of(base + j * rows, 8)
                pltpu.sync_copy(rows_v[b], out_hbm.at[pl.ds(off, rows)])

                @pl.when(j + SC_RING < n_chunk)
                def _():
                    start_gather(j + SC_RING, b)
            return carry

        lax.fori_loop(0, n_chunk // SC_RING, group, 0)

    return gather(table, idx)


def _route_kernel(x_ref, mod_ref, wr_ref, rb_ref, hp_ref, meta_ref, cnt_ref, wcol_ref, carry_scr):
    t = pl.program_id(0)
    T = ROUTE_TILE

    @pl.when(t == 0)
    def _():
        carry_scr[...] = jnp.zeros_like(carry_scr)

    shift = mod_ref[0, 3:4, :]
    scale = mod_ref[0, 4:5, :]
    h = _layer_norm(x_ref[0]) * (1.0 + scale) + shift
    h_hi = h.astype(jnp.bfloat16)
    h_lo = (h - h_hi.astype(jnp.float32)).astype(jnp.bfloat16)
    hp_ref[...] = h

    wr_hi, wr_lo = _split_bf16(wr_ref[...])
    logits_t = _nt_dot(wr_hi, h_hi) + (_nt_dot(wr_lo, h_hi) + _nt_dot(wr_hi, h_lo))
    i1, i2, w1, w2 = _router_rows(logits_t, rb_ref)

    e_iota = lax.broadcasted_iota(jnp.int32, (N_EXPERTS, T), 0).astype(jnp.float32)
    sel1 = e_iota == i1
    sel2 = e_iota == i2
    onehot = jnp.where(sel1 | sel2, 1.0, 0.0)
    tr = lax.broadcasted_iota(jnp.int32, (T, T), 0)
    tc = lax.broadcasted_iota(jnp.int32, (T, T), 1)
    earlier = jnp.where(tr < tc, 1.0, 0.0).astype(jnp.bfloat16)
    rank = jnp.dot(onehot.astype(jnp.bfloat16), earlier,
                   preferred_element_type=jnp.float32) + carry_scr[:, 0:1]
    rank1 = jnp.sum(jnp.where(sel1, rank, 0.0), axis=0, keepdims=True)
    rank2 = jnp.sum(jnp.where(sel2, rank, 0.0), axis=0, keepdims=True)
    for r, row in enumerate((i1, i2, rank1, rank2)):
        meta_ref[r, 0] = row
    carry_scr[...] = carry_scr[...] + jnp.sum(onehot, axis=1, keepdims=True)
    cnt_ref[...] = carry_scr[...]

    r128 = lax.broadcasted_iota(jnp.int32, (LANES, T), 0)
    terms = jnp.zeros((LANES, T), jnp.float32)
    k = 0
    for w in (w1, w2):
        rest = w
        for _ in range(3):
            part = rest.astype(jnp.bfloat16).astype(jnp.float32)
            terms = jnp.where(r128 == k, part, terms)
            rest = rest - part
            k += 1
    eye = jnp.where(tr == tc, 1.0, 0.0).astype(jnp.bfloat16)
    wcol_ref[...] = _nt_dot(eye, terms.astype(jnp.bfloat16))


def _route_call(x, modl, w_router_t, rbias_col):
    B, S, D = x.shape
    tm = ROUTE_TILE
    nt = S // tm
    n_tiles = B * nt
    return pl.pallas_call(
        _route_kernel,
        grid=(n_tiles,),
        in_specs=[
            pl.BlockSpec((1, tm, D), lambda t: (t // nt, t % nt, 0)),
            pl.BlockSpec((1, N_MOD, D), lambda t: (t // nt, 0, 0)),
            pl.BlockSpec((N_EXPERTS, D), lambda t: (0, 0)),
            pl.BlockSpec((N_EXPERTS, 1), lambda t: (0, 0)),
        ],
        out_specs=[
            pl.BlockSpec((tm, D), lambda t: (t, 0)),
            pl.BlockSpec((4, 1, 1, tm), lambda t: (0, t, 0, 0)),
            pl.BlockSpec((N_EXPERTS, LANES), lambda t: (0, 0)),
            pl.BlockSpec((tm, LANES), lambda t: (t, 0)),
        ],
        out_shape=[
            jax.ShapeDtypeStruct((B * S, D), jnp.float32),
            jax.ShapeDtypeStruct((4, n_tiles, 1, tm), jnp.float32),
            jax.ShapeDtypeStruct((N_EXPERTS, LANES), jnp.float32),
            jax.ShapeDtypeStruct((B * S, LANES), jnp.float32),
        ],
        scratch_shapes=[pltpu.VMEM((N_EXPERTS, LANES), jnp.float32)],
        compiler_params=_params(("arbitrary",)),
    )(x, modl, w_router_t, rbias_col)


def _dispatch_plan(meta, counts, n_tokens):
    n_rows = 2 * n_tokens + N_EXPERTS * EXPERT_CHUNK
    n_chunks = n_rows // EXPERT_CHUNK
    meta = meta.reshape(meta.shape[0], n_tokens).astype(jnp.int32)
    per_token = lambda r: meta[r]
    e1 = per_token(0)
    e2 = per_token(1)
    cnt = counts[:, 0].astype(jnp.int32)
    padded = (cnt + (EXPERT_CHUNK - 1)) // EXPERT_CHUNK * EXPERT_CHUNK
    ends = jnp.cumsum(padded)
    starts = ends - padded
    first = jnp.cumsum(cnt) - cnt
    pos1 = starts[e1] + per_token(2)
    pos2 = starts[e2] + per_token(3)
    order = jnp.argsort(jnp.concatenate([pos1, pos2])).astype(jnp.int32)
    sorted_tok = jnp.where(order >= n_tokens, order - n_tokens, order)
    chunk_row0 = jnp.arange(n_chunks, dtype=jnp.int32) * EXPERT_CHUNK
    chunk_expert = jnp.minimum(
        jnp.sum((ends[None, :] <= chunk_row0[:, None]).astype(jnp.int32), axis=1), N_EXPERTS - 1)
    local = (chunk_row0 - starts[chunk_expert])[:, None] + jnp.arange(EXPERT_CHUNK, dtype=jnp.int32)
    real = local < cnt[chunk_expert][:, None]
    nth = jnp.clip(first[chunk_expert][:, None] + local, 0, 2 * n_tokens - 1)
    spread = (chunk_row0[:, None] + jnp.arange(EXPERT_CHUNK, dtype=jnp.int32)) % n_tokens
    src = jnp.where(real, sorted_tok[nth], spread).reshape(n_rows)
    n_used = (ends[-1] // EXPERT_CHUNK).reshape(1).astype(jnp.int32)
    return pos1, pos2, src, chunk_expert, n_used


def _expert_kernel(ce_ref, nu_ref, xs_ref, wg_ref, wu_ref, wd_ref, o_ref, wg_bf, wu_bf, wd_bf):
    c = pl.program_id(0)

    @pl.when(c < nu_ref[0])
    def _():
        @pl.when((c == 0) | (ce_ref[c] != ce_ref[jnp.maximum(c - 1, 0)]))
        def _():
            wg_bf[...] = wg_ref[0].astype(jnp.bfloat16)
            wu_bf[...] = wu_ref[0].astype(jnp.bfloat16)
            wd_bf[...] = wd_ref[0].astype(jnp.bfloat16)

        xs = xs_ref[...].astype(jnp.bfloat16)
        gt = jnp.dot(xs, wg_bf[...], preferred_element_type=jnp.float32)
        up = jnp.dot(xs, wu_bf[...], preferred_element_type=jnp.float32)
        act = (gt * (1.0 / (1.0 + jnp.exp(-gt))) * up).astype(jnp.bfloat16)
        o_ref[...] = jnp.dot(act, wd_bf[...], preferred_element_type=jnp.float32)

    @pl.when(c >= nu_ref[0])
    def _():
        o_ref[...] = jnp.zeros_like(o_ref)


def _expert_call(chunk_expert, n_used, xs, w_gate_l, w_up_l, w_down_l):
    n_rows, D = xs.shape
    grid_spec = pltpu.PrefetchScalarGridSpec(
        num_scalar_prefetch=2,
        grid=(n_rows // EXPERT_CHUNK,),
        in_specs=[
            pl.BlockSpec((EXPERT_CHUNK, D), lambda c, ce, nu: (c, 0)),
            pl.BlockSpec((1, D, D_EXPERT), lambda c, ce, nu: (ce[c], 0, 0)),
            pl.BlockSpec((1, D, D_EXPERT), lambda c, ce, nu: (ce[c], 0, 0)),
            pl.BlockSpec((1, D_EXPERT, D), lambda c, ce, nu: (ce[c], 0, 0)),
        ],
        out_specs=pl.BlockSpec((EXPERT_CHUNK, D), lambda c, ce, nu: (c, 0)),
        scratch_shapes=[
            pltpu.VMEM((D, D_EXPERT), jnp.bfloat16),
            pltpu.VMEM((D, D_EXPERT), jnp.bfloat16),
            pltpu.VMEM((D_EXPERT, D), jnp.bfloat16),
        ],
    )
    return pl.pallas_call(
        _expert_kernel,
        grid_spec=grid_spec,
        out_shape=jax.ShapeDtypeStruct((n_rows, D), jnp.float32),
        compiler_params=_params(("arbitrary",)),
    )(chunk_expert, n_used, xs, w_gate_l, w_up_l, w_down_l)


def _combine_kernel(y1_ref, y2_ref, wcol_ref, x_ref, mod_ref, g_ref, b_ref, o_ref):
    wc = wcol_ref[0]
    w1 = (wc[:, 0:1] + wc[:, 1:2]) + wc[:, 2:3]
    w2 = (wc[:, 3:4] + wc[:, 4:5]) + wc[:, 5:6]
    y = w1 * y1_ref[0] + w2 * y2_ref[0]
    gate = mod_ref[0, 5:6, :]
    z = DEEPNORM_ALPHA * x_ref[0] + gate * y
    o_ref[0] = _layer_norm(z) * g_ref[...] + b_ref[...]


def _combine_call(y1, y2, wcol, x, modl, ln_g, ln_b):
    B, S, D = x.shape
    tm = 512
    rows = pl.BlockSpec((1, tm, D), lambda b, i: (b, i, 0))
    vec = pl.BlockSpec((1, D), lambda b, i: (0, 0))
    return pl.pallas_call(
        _combine_kernel,
        grid=(B, S // tm),
        in_specs=[rows, rows, pl.BlockSpec((1, tm, LANES), lambda b, i: (b, i, 0)), rows,
                  pl.BlockSpec((1, N_MOD, D), lambda b, i: (b, 0, 0)), vec, vec],
        out_specs=rows,
        out_shape=jax.ShapeDtypeStruct((B, S, D), jnp.float32),
        compiler_params=_params(("arbitrary", "arbitrary")),
    )(y1.reshape(B, S, D), y2.reshape(B, S, D), wcol.reshape(B, S, LANES),
      x, modl, ln_g.reshape(1, D), ln_b.reshape(1, D))


def _moe_routed(x, modl, w_router_t, rbias_col, w_gate_l, w_up_l, w_down_l, ln_g, ln_b):
    B, S, D = x.shape
    h, meta, counts, wcol = _route_call(x, modl, w_router_t, rbias_col)
    pos1, pos2, src, chunk_expert, n_used = _dispatch_plan(meta, counts, B * S)
    xs = _sc_gather_rows(h, src)
    ys = _expert_call(chunk_expert, n_used, xs, w_gate_l, w_up_l, w_down_l)
    y1 = _sc_gather_rows(ys, pos1)
    y2 = _sc_gather_rows(ys, pos2)
    return _combine_call(y1, y2, wcol, x, modl, ln_g, ln_b)


def _rope_tables(S):
    half = ROT_DIM // 2
    inv_freq = ROPE_THETA ** (-(jnp.arange(half, dtype=jnp.float32) * 2.0 / ROT_DIM))
    ang = jnp.arange(S, dtype=jnp.float32)[:, None] * inv_freq[None, :]
    cos, sin = jnp.cos(ang), jnp.sin(ang)
    d = jnp.arange(LANES) % HEAD_DIM
    f = d % half
    rot = d[None, :] < ROT_DIM
    return jnp.where(rot, cos[:, f], 1.0), jnp.where(rot, sin[:, f], 0.0)


def _with_rotary_partner_columns(w_in_l):
    half = ROT_DIM // 2
    qk = w_in_l[:, :2 * D_ATTN]
    d = (jnp.arange(2 * D_ATTN) % HEAD_DIM)[None, :]
    partner = jnp.where(d < half, -jnp.roll(qk, -half, axis=1),
                        jnp.where(d < ROT_DIM, jnp.roll(qk, half, axis=1), 0.0))
    return jnp.concatenate([w_in_l, partner], axis=1)


def _bias_placement():
    src = jnp.arange(LANES)
    hd, u, j = src >> 4, (src >> 3) & 1, src & 7
    col0 = jnp.where(hd % 2 == 0, HEAD_DIM, 0)
    dst = LANES * hd + col0 + j
    onehot = (jnp.arange(N_HEADS * LANES)[None, :] == dst[:, None]) & (u[:, None] == 1)
    return onehot.astype(jnp.bfloat16)


def kernel(x, c, w_mod, b_mod, w_in, w_pool, pool_scale, w_out, ln1_g, ln1_b,
           w_router, router_bias, w_gate, w_up, w_down, ln2_g, ln2_b):
    B, S, D = x.shape
    bf = jnp.bfloat16
    mod = _mod_call(c, w_mod, b_mod).reshape(DEPTH, B, N_MOD, D)
    cos_t, sin_t = _rope_tables(S)
    place = _bias_placement()
    w_router_t = w_router.T
    rbias_col = router_bias.reshape(N_EXPERTS, 1)
    for l in range(DEPTH):
        modl = mod[l]
        w_in_aug = _with_rotary_partner_columns(w_in[l]).astype(bf)
        qa, ka, v, p, _ = _proj_call(x, modl, w_in_aug, cos_t, sin_t, place)
        a = _attn_call(qa, ka, v)
        m = _pool_call(p, w_pool[l], pool_scale[l])
        x = _mixout_call(a, m, x, modl, w_out[l].astype(bf), ln1_g[l], ln1_b[l])
        x = _moe_routed(x, modl, w_router_t, rbias_col, w_gate[l], w_up[l], w_down[l],
                        ln2_g[l], ln2_b[l])
    return x
```

```python
import functools

import jax
import jax.numpy as jnp
from jax import lax
from jax.experimental import pallas as pl
from jax.experimental.pallas import tpu as pltpu
from jax.experimental.pallas import tpu_sc as plsc

D_MODEL = 1024
DEPTH = 2
D_ATTN = 512
D_POOL = 512
N_HEADS = 8
HEAD_DIM = 64
ROT_DIM = 16
ROPE_THETA = 500000.0
MOBA_BLOCK = 256
MOBA_TOPK = 3
POOL_GROUP = 128
N_POOL_GROUPS = 4
D_IN = 3 * D_ATTN + D_POOL
N_EXPERTS = 16
EXPERTS_PER_GROUP = 4
D_EXPERT = 512
DEEPNORM_ALPHA = (2 * DEPTH) ** 0.25
N_MOD = 6
LN_EPS = 1e-5
NEG_INF = -1e30

LANES = 128
VMEM_LIMIT = 56 * 1024 * 1024

_HI = lax.Precision.HIGHEST
_NT = (((1,), (1,)), ((), ()))


def _params(sem):
    return pltpu.CompilerParams(dimension_semantics=sem, vmem_limit_bytes=VMEM_LIMIT)


def _nt_dot(a, b):
    return lax.dot_general(a, b, _NT, preferred_element_type=jnp.float32)


def _mod_kernel(c_ref, w_ref, b_ref, o_ref):
    c = c_ref[...]
    cond = c * (1.0 / (1.0 + jnp.exp(-c)))
    o_ref[0] = jnp.dot(cond, w_ref[0], precision=_HI,
                       preferred_element_type=jnp.float32) + b_ref[0]


def _mod_call(c, w_mod, b_mod):
    B = c.shape[0]
    return pl.pallas_call(
        _mod_kernel,
        grid=(DEPTH, N_MOD),
        in_specs=[
            pl.BlockSpec((B, D_MODEL), lambda l, j: (0, 0)),
            pl.BlockSpec((1, D_MODEL, D_MODEL), lambda l, j: (l, 0, j)),
            pl.BlockSpec((1, 1, D_MODEL), lambda l, j: (l, 0, j)),
        ],
        out_specs=pl.BlockSpec((1, B, D_MODEL), lambda l, j: (l, 0, j)),
        out_shape=jax.ShapeDtypeStruct((DEPTH, B, N_MOD * D_MODEL), jnp.float32),
        compiler_params=_params(("arbitrary", "arbitrary")),
    )(c, w_mod, b_mod.reshape(DEPTH, 1, N_MOD * D_MODEL))


def _layer_norm(x):
    mu = jnp.mean(x, axis=-1, keepdims=True)
    xc = x - mu
    var = jnp.mean(xc * xc, axis=-1, keepdims=True)
    return xc * lax.rsqrt(var + LN_EPS)


def _split_bf16(t):
    hi = t.astype(jnp.bfloat16)
    lo = (t - hi.astype(jnp.float32)).astype(jnp.bfloat16)
    return hi, lo


def _proj_kernel(x_ref, mod_ref, w_ref, cos_ref, sin_ref, place_ref,
                 qa_ref, ka_ref, v_ref, p_ref, kbar_ref):
    i = pl.program_id(1)

    @pl.when(i == 0)
    def _():
        kbar_ref[...] = jnp.zeros_like(kbar_ref)

    x = x_ref[0]
    shift = mod_ref[0, 0:1, :]
    scale = mod_ref[0, 1:2, :]
    h = (_layer_norm(x) * (1.0 + scale) + shift).astype(jnp.bfloat16)

    cos = cos_ref[...]
    sin = sin_ref[...]

    def proj(c0, width):
        return jnp.dot(h, w_ref[:, c0:c0 + width], preferred_element_type=jnp.float32)

    def slab(t, s):
        return t[:, LANES * s:LANES * (s + 1)]

    n_slab = D_ATTN // LANES
    q = proj(0, D_ATTN)
    qp = proj(D_IN, D_ATTN)
    q_slabs = [(slab(q, s) * cos + slab(qp, s) * sin) * (HEAD_DIM ** -0.5)
               for s in range(n_slab)]
    k = proj(D_ATTN, D_ATTN)
    kp = proj(D_IN + D_ATTN, D_ATTN)
    k_slabs = [slab(k, s) * cos + slab(kp, s) * sin for s in range(n_slab)]
    v_ref[0] = proj(2 * D_ATTN, D_ATTN).astype(jnp.bfloat16)
    p_ref[0] = proj(3 * D_ATTN, D_POOL)

    kmean = jnp.concatenate(
        [jnp.mean(ks, axis=0, keepdims=True) for ks in k_slabs], axis=1)
    kbar_ref[0, pl.ds(i, 1), :] = kmean

    kb = kbar_ref[0]
    kb_rows = jnp.concatenate([kb] * (LANES // 8), axis=0)
    r_head = lax.broadcasted_iota(jnp.int32, (LANES, D_ATTN), 1) >> 6
    c_head = lax.broadcasted_iota(jnp.int32, (LANES, D_ATTN), 0) >> 4
    kbd_hi, kbd_lo = _split_bf16(jnp.where(r_head == c_head, kb_rows, 0.0))
    q_hi, q_lo = _split_bf16(jnp.concatenate(q_slabs, axis=1))
    gate = _nt_dot(q_hi, kbd_hi) + (_nt_dot(q_lo, kbd_hi) + _nt_dot(q_hi, kbd_lo))

    lane = lax.broadcasted_iota(jnp.int32, (MOBA_BLOCK, LANES), 1)
    j_of = lane & 7
    past = j_of < i
    gm = jnp.where(past, gate, NEG_INF)
    rank = jnp.zeros((MOBA_BLOCK, LANES), jnp.int32)
    for r in range(1, 8):
        other = pltpu.roll(gm, r, 1)
        beats = (other > gm) | ((other == gm) & (j_of >= r))
        rank = rank + beats.astype(jnp.int32)
    allowed = (past & (rank < MOBA_TOPK)) | (j_of == i)
    bias = jnp.where(allowed, 0.0, NEG_INF).astype(jnp.bfloat16)
    bias_cols = jnp.dot(bias, place_ref[...], preferred_element_type=jnp.float32)

    for hd in range(N_HEADS):
        own = (lane < HEAD_DIM) if hd % 2 == 0 else (lane >= HEAD_DIM)
        col0 = HEAD_DIM if hd % 2 == 0 else 0
        qa = jnp.where(own, q_slabs[hd // 2], slab(bias_cols, hd))
        ka = jnp.where(own, k_slabs[hd // 2], jnp.where(lane == col0 + i, 1.0, 0.0))
        qa_ref[0, hd] = qa.astype(jnp.bfloat16)
        ka_ref[0, hd] = ka.astype(jnp.bfloat16)


def _proj_call(x, modl, w_in_aug, cos_t, sin_t, place):
    B, S, D = x.shape
    nb = S // MOBA_BLOCK
    tm = MOBA_BLOCK
    tab = pl.BlockSpec((tm, LANES), lambda b, i: (i, 0))
    head_spec = pl.BlockSpec((1, N_HEADS, tm, LANES), lambda b, i: (b, 0, i, 0))
    return pl.pallas_call(
        _proj_kernel,
        grid=(B, nb),
        in_specs=[
            pl.BlockSpec((1, tm, D), lambda b, i: (b, i, 0)),
            pl.BlockSpec((1, N_MOD, D), lambda b, i: (b, 0, 0)),
            pl.BlockSpec((D, D_IN + 2 * D_ATTN), lambda b, i: (0, 0)),
            tab, tab,
            pl.BlockSpec((LANES, N_HEADS * LANES), lambda b, i: (0, 0)),
        ],
        out_specs=[
            head_spec, head_spec,
            pl.BlockSpec((1, tm, D_ATTN), lambda b, i: (b, i, 0)),
            pl.BlockSpec((1, tm, D_POOL), lambda b, i: (b, i, 0)),
            pl.BlockSpec((1, nb, D_ATTN), lambda b, i: (b, 0, 0)),
        ],
        out_shape=[
            jax.ShapeDtypeStruct((B, N_HEADS, S, LANES), jnp.bfloat16),
            jax.ShapeDtypeStruct((B, N_HEADS, S, LANES), jnp.bfloat16),
            jax.ShapeDtypeStruct((B, S, D_ATTN), jnp.bfloat16),
            jax.ShapeDtypeStruct((B, S, D_POOL), jnp.float32),
            jax.ShapeDtypeStruct((B, nb, D_ATTN), jnp.float32),
        ],
        compiler_params=_params(("arbitrary", "arbitrary")),
    )(x, modl, w_in_aug, cos_t, sin_t, place)


def _attn_kernel(qa_ref, ka_ref, v_ref, o_ref):
    nb = v_ref.shape[1] // MOBA_BLOCK
    row = lax.broadcasted_iota(jnp.int32, (MOBA_BLOCK, MOBA_BLOCK), 0)
    col = lax.broadcasted_iota(jnp.int32, (MOBA_BLOCK, MOBA_BLOCK), 1)
    causal = col <= row
    lane = lax.broadcasted_iota(jnp.int32, (MOBA_BLOCK, LANES), 1)

    for i in reversed(range(nb)):
        r0 = i * MOBA_BLOCK
        outs = []
        for hh in range(2):
            q = qa_ref[0, hh, r0:r0 + MOBA_BLOCK, :]
            s_own = jnp.where(causal, _nt_dot(q, ka_ref[0, hh, r0:r0 + MOBA_BLOCK, :]), NEG_INF)
            m = jnp.max(s_own, axis=1, keepdims=True)
            if i > 0:
                s_past = _nt_dot(q, ka_ref[0, hh, 0:r0, :])
                m = jnp.maximum(m, jnp.max(s_past, axis=1, keepdims=True))
            p_own = jnp.exp(s_own - m)
            l = jnp.sum(p_own, axis=1, keepdims=True)
            acc = jnp.dot(p_own.astype(jnp.bfloat16), v_ref[0, r0:r0 + MOBA_BLOCK, :],
                          preferred_element_type=jnp.float32)
            if i > 0:
                p_past = jnp.exp(s_past - m)
                l = l + jnp.sum(p_past, axis=1, keepdims=True)
                acc = acc + jnp.dot(p_past.astype(jnp.bfloat16), v_ref[0, 0:r0, :],
                                    preferred_element_type=jnp.float32)
            outs.append(acc / l)
        o = jnp.where(lane < HEAD_DIM, outs[0], outs[1])
        o_ref[0, r0:r0 + MOBA_BLOCK, :] = o.astype(o_ref.dtype)


def _attn_call(qa, ka, v):
    B, _, S, _ = qa.shape
    n_pair = N_HEADS // 2
    pair_spec = pl.BlockSpec((1, 2, S, LANES), lambda b, hp: (b, hp, 0, 0))
    slab_spec = pl.BlockSpec((1, S, LANES), lambda b, hp: (b, 0, hp))
    return pl.pallas_call(
        _attn_kernel,
        grid=(B, n_pair),
        in_specs=[pair_spec, pair_spec, slab_spec],
        out_specs=slab_spec,
        out_shape=jax.ShapeDtypeStruct((B, S, D_ATTN), jnp.bfloat16),
        compiler_params=_params(("arbitrary", "arbitrary")),
    )(qa, ka, v)


def _pool_kernel(p_ref, w_ref, sc_ref, o_ref):
    g = pl.program_id(1)
    S = p_ref.shape[1]
    p = p_ref[0]
    t = lax.broadcasted_iota(jnp.int32, (S, LANES), 0)

    def shifted(x, k):
        return jnp.where(t >= k, pltpu.roll(x, k, 0), 0.0)

    win = p
    sums = []
    for step in range(N_POOL_GROUPS):
        win = win + shifted(win, 1 << step)
        sums.append(win)
    wsum = jnp.where(g == 0, sums[0],
                     jnp.where(g == 1, sums[1], jnp.where(g == 2, sums[2], sums[3])))
    window = jnp.left_shift(2, g)
    cnt = jnp.minimum(t + 1, window).astype(jnp.float32)
    d = (wsum / cnt - p).astype(jnp.bfloat16)
    y = jnp.dot(d, w_ref[0].astype(jnp.bfloat16), preferred_element_type=jnp.float32)
    o_ref[0] = (y * sc_ref[...]).astype(o_ref.dtype)


def _pool_call(p, w_pool_l, pool_scale_l):
    B, S, _ = p.shape
    slab = pl.BlockSpec((1, S, LANES), lambda b, g: (b, 0, g))
    return pl.pallas_call(
        _pool_kernel,
        grid=(B, N_POOL_GROUPS),
        in_specs=[
            slab,
            pl.BlockSpec((1, POOL_GROUP, POOL_GROUP), lambda b, g: (g, 0, 0)),
            pl.BlockSpec((1, LANES), lambda b, g: (0, g)),
        ],
        out_specs=slab,
        out_shape=jax.ShapeDtypeStruct((B, S, D_POOL), jnp.bfloat16),
        compiler_params=_params(("arbitrary", "arbitrary")),
    )(p, w_pool_l, pool_scale_l.reshape(1, D_POOL))


def _mixout_kernel(a_ref, m_ref, x_ref, mod_ref, w_ref, g_ref, b_ref, o_ref):
    y = jnp.dot(a_ref[0], w_ref[0:D_ATTN, :], preferred_element_type=jnp.float32)
    y = y + jnp.dot(m_ref[0], w_ref[D_ATTN:, :], preferred_element_type=jnp.float32)
    gate = mod_ref[0, 2:3, :]
    z = DEEPNORM_ALPHA * x_ref[0] + gate * y
    o_ref[0] = _layer_norm(z) * g_ref[...] + b_ref[...]


def _mixout_call(a, m, x, modl, w_out_bf, ln_g, ln_b):
    B, S, D = x.shape
    tm = 512
    vec = pl.BlockSpec((1, D), lambda b, i: (0, 0))
    return pl.pallas_call(
        _mixout_kernel,
        grid=(B, S // tm),
        in_specs=[
            pl.BlockSpec((1, tm, D_ATTN), lambda b, i: (b, i, 0)),
            pl.BlockSpec((1, tm, D_POOL), lambda b, i: (b, i, 0)),
            pl.BlockSpec((1, tm, D), lambda b, i: (b, i, 0)),
            pl.BlockSpec((1, N_MOD, D), lambda b, i: (b, 0, 0)),
            pl.BlockSpec((D, D), lambda b, i: (0, 0)),
            vec, vec,
        ],
        out_specs=pl.BlockSpec((1, tm, D), lambda b, i: (b, i, 0)),
        out_shape=jax.ShapeDtypeStruct((B, S, D), jnp.float32),
        compiler_params=_params(("arbitrary", "arbitrary")),
    )(a, m, x, modl, w_out_bf, ln_g.reshape(1, D), ln_b.reshape(1, D))


MOE_TILE = 1024
MOE_CHUNK = 160
MOE_ROWS = 2304
MOE_ROWS_ALLOC = MOE_ROWS + MOE_CHUNK
MOE_G_COLS = 768
MOE_EXPERTS_PER_STEP = 2


def _top2_rows(vals):
    def first_max(rows):
        m = rows[0]
        for v in rows[1:]:
            m = jnp.maximum(m, v)
        idx = jnp.full_like(m, float(len(rows) - 1))
        for k in range(len(rows) - 2, -1, -1):
            idx = jnp.where(rows[k] == m, float(k), idx)
        return m, idx

    m1, i1 = first_max(vals)
    rest = [jnp.where(i1 == float(k), -jnp.inf, v) for k, v in enumerate(vals)]
    m2, i2 = first_max(rest)
    return m1, i1, m2, i2


def _router_rows(logits_t, rb_ref):
    lg = [logits_t[e:e + 1, :] for e in range(N_EXPERTS)]
    mx = lg[0]
    for v in lg[1:]:
        mx = jnp.maximum(mx, v)
    ex = [jnp.exp(v - mx) for v in lg]
    den = ex[0]
    for v in ex[1:]:
        den = den + v
    scores = [v / den for v in ex]
    sel = [scores[e] + rb_ref[e:e + 1, :] for e in range(N_EXPERTS)]
    best_score = None
    best = None
    for g in range(N_EXPERTS // EXPERTS_PER_GROUP):
        m1, _, m2, _ = _top2_rows(sel[g * EXPERTS_PER_GROUP:(g + 1) * EXPERTS_PER_GROUP])
        gs = m1 + m2
        if g == 0:
            best_score, best = gs, jnp.zeros_like(gs)
        else:
            better = gs > best_score
            best_score = jnp.where(better, gs, best_score)
            best = jnp.where(better, float(g), best)
    masked = [jnp.where(best == float(e // EXPERTS_PER_GROUP), sel[e], NEG_INF)
              for e in range(N_EXPERTS)]
    _, i1, _, i2 = _top2_rows(masked)
    w1 = jnp.zeros_like(i1)
    w2 = jnp.zeros_like(i2)
    for e in range(N_EXPERTS):
        w1 = jnp.where(i1 == float(e), scores[e], w1)
        w2 = jnp.where(i2 == float(e), scores[e], w2)
    tot = w1 + w2
    return i1, i2, w1 / tot, w2 / tot


def _moe_route(x_ref, mod_ref, wr_ref, rb_ref, h_scr, col_scr, row_scr, ysh_scr, ysl_scr,
               start_smem, cnt_smem):
    T = MOE_TILE
    shift = mod_ref[0, 3:4, :]
    scale = mod_ref[0, 4:5, :]
    h = _layer_norm(x_ref[0]) * (1.0 + scale) + shift
    h_hi = h.astype(jnp.bfloat16)
    h_scr[...] = h_hi
    h_lo = (h - h_hi.astype(jnp.float32)).astype(jnp.bfloat16)
    wr_hi, wr_lo = _split_bf16(wr_ref[...])
    logits_t = _nt_dot(wr_hi, h_hi) + (_nt_dot(wr_lo, h_hi) + _nt_dot(wr_hi, h_lo))
    i1, i2, w1, w2 = _router_rows(logits_t, rb_ref)

    e_iota = lax.broadcasted_iota(jnp.int32, (N_EXPERTS, T), 0).astype(jnp.float32)
    sel1 = e_iota == i1
    sel2 = e_iota == i2
    onehot = jnp.where(sel1 | sel2, 1.0, 0.0)
    tr = lax.broadcasted_iota(jnp.int32, (T, T), 0)
    tc = lax.broadcasted_iota(jnp.int32, (T, T), 1)
    earlier = jnp.where(tr < tc, 1.0, 0.0).astype(jnp.bfloat16)
    excl = jnp.dot(onehot.astype(jnp.bfloat16), earlier,
                   preferred_element_type=jnp.float32)
    cnt = jnp.sum(onehot, axis=1, keepdims=True)

    e_col = lax.broadcasted_iota(jnp.int32, (N_EXPERTS, 1), 0)
    start_v = jnp.zeros((N_EXPERTS, 1), jnp.float32)
    run = jnp.int32(0)
    for ex in range(N_EXPERTS):
        padded = lax.shift_left(lax.shift_right_logical(cnt[ex, 0].astype(jnp.int32) + 15, 4), 4)
        start_smem[ex] = run
        cnt_smem[ex] = padded
        start_v = jnp.where(e_col == ex, run.astype(jnp.float32), start_v)
        run = run + padded

    slot = excl + start_v
    pos1 = jnp.sum(jnp.where(sel1, slot, 0.0), axis=0, keepdims=True)
    pos2 = jnp.sum(jnp.where(sel2, slot, 0.0), axis=0, keepdims=True)
    row_scr[0:1, :] = pos1
    row_scr[1:2, :] = pos2
    row_scr[2:3, :] = w1
    row_scr[3:4, :] = w2

    r128 = lax.broadcasted_iota(jnp.int32, (LANES, T), 0)
    terms = jnp.zeros((LANES, T), jnp.float32)
    k = 0
    for p in (pos1, pos2):
        a = jnp.floor(p * (1.0 / 64.0))
        for t in (a, p - 64.0 * a):
            terms = jnp.where(r128 == k, t, terms)
            k += 1
    eye = jnp.where(tr == tc, 1.0, 0.0).astype(jnp.bfloat16)
    col_scr[...] = _nt_dot(eye, terms.astype(jnp.bfloat16))

    tail = MOE_ROWS_ALLOC - 2 * T
    ysh_scr[2 * T:, :] = jnp.zeros((tail, D_MODEL), jnp.bfloat16)
    ysl_scr[2 * T:, :] = jnp.zeros((tail, D_MODEL), jnp.bfloat16)


def _moe_kernel(x_ref, mod_ref, wr_ref, rb_ref, wg_ref, wu_ref, wd_ref, g_ref, b_ref,
                o_ref, h_scr, col_scr, row_scr, ysh_scr, ysl_scr, start_smem, cnt_smem):
    step = pl.program_id(1)
    T = MOE_TILE

    @pl.when(step == 0)
    def _():
        _moe_route(x_ref, mod_ref, wr_ref, rb_ref, h_scr, col_scr, row_scr, ysh_scr, ysl_scr,
                   start_smem, cnt_smem)

    pos1_row = row_scr[0:1, :]
    pos2_row = row_scr[1:2, :]
    w1_row = row_scr[2:3, :]
    w2_row = row_scr[3:4, :]

    for k in range(MOE_EXPERTS_PER_STEP):
        e = step * MOE_EXPERTS_PER_STEP + k
        seg_start = start_smem[e]
        n_chunk = (cnt_smem[e] + (MOE_CHUNK - 1)) // MOE_CHUNK

        def chunk(j, carry, k=k, seg_start=seg_start):
            base = pl.multiple_of(seg_start + j * MOE_CHUNK, 16)
            r = (base + lax.broadcasted_iota(jnp.int32, (MOE_CHUNK, T), 0)).astype(jnp.float32)
            m1 = r == pos1_row
            m2 = r == pos2_row
            gather = jnp.where(m1 | m2, 1.0, 0.0).astype(jnp.bfloat16)
            w_row = jnp.sum(jnp.where(m1, w1_row, 0.0) + jnp.where(m2, w2_row, 0.0),
                            axis=1, keepdims=True)
            xs = jnp.dot(gather, h_scr[...],
                         preferred_element_type=jnp.float32).astype(jnp.bfloat16)
            gt = jnp.dot(xs, wg_ref[k], preferred_element_type=jnp.float32)
            up = jnp.dot(xs, wu_ref[k], preferred_element_type=jnp.float32)
            act = (gt * (1.0 / (1.0 + jnp.exp(-gt))) * up).astype(jnp.bfloat16)
            y = w_row * jnp.dot(act, wd_ref[k], preferred_element_type=jnp.float32)
            y_hi = y.astype(jnp.bfloat16)
            ysh_scr[pl.ds(base, MOE_CHUNK), :] = y_hi
            ysl_scr[pl.ds(base, MOE_CHUNK), :] = (y - y_hi.astype(jnp.float32)).astype(jnp.bfloat16)
            return carry

        lax.fori_loop(0, n_chunk, chunk, 0)

    @pl.when(step == N_EXPERTS // MOE_EXPERTS_PER_STEP - 1)
    def _():
        pos1 = 64.0 * col_scr[:, 0:1] + col_scr[:, 1:2]
        pos2 = 64.0 * col_scr[:, 2:3] + col_scr[:, 3:4]
        y = jnp.zeros((T, D_MODEL), jnp.float32)
        for c0 in range(0, MOE_ROWS, MOE_G_COLS):
            r = (c0 + lax.broadcasted_iota(jnp.int32, (T, MOE_G_COLS), 1)).astype(jnp.float32)
            scatter = jnp.where((r == pos1) | (r == pos2), 1.0, 0.0).astype(jnp.bfloat16)
            y = y + jnp.dot(scatter, ysh_scr[c0:c0 + MOE_G_COLS, :],
                            preferred_element_type=jnp.float32)
            y = y + jnp.dot(scatter, ysl_scr[c0:c0 + MOE_G_COLS, :],
                            preferred_element_type=jnp.float32)
        gate = mod_ref[0, 5:6, :]
        z = DEEPNORM_ALPHA * x_ref[0] + gate * y
        o_ref[0] = _layer_norm(z) * g_ref[...] + b_ref[...]


def _moe_call(x, modl, w_router_t, rbias_col, wg_bf, wu_bf, wd_bf, ln_g, ln_b):
    B, S, D = x.shape
    tm = MOE_TILE
    nt = S // tm
    eps = MOE_EXPERTS_PER_STEP
    vec = pl.BlockSpec((1, D), lambda t, e: (0, 0))
    xspec = pl.BlockSpec((1, tm, D), lambda t, e: (t // nt, t % nt, 0))
    return pl.pallas_call(
        _moe_kernel,
        grid=(B * nt, N_EXPERTS // eps),
        in_specs=[
            xspec,
            pl.BlockSpec((1, N_MOD, D), lambda t, e: (t // nt, 0, 0)),
            pl.BlockSpec((N_EXPERTS, D), lambda t, e: (0, 0)),
            pl.BlockSpec((N_EXPERTS, 1), lambda t, e: (0, 0)),
            pl.BlockSpec((eps, D, D_EXPERT), lambda t, e: (e, 0, 0)),
            pl.BlockSpec((eps, D, D_EXPERT), lambda t, e: (e, 0, 0)),
            pl.BlockSpec((eps, D_EXPERT, D), lambda t, e: (e, 0, 0)),
            vec, vec,
        ],
        out_specs=xspec,
        out_shape=jax.ShapeDtypeStruct((B, S, D), jnp.float32),
        scratch_shapes=[
            pltpu.VMEM((tm, D), jnp.bfloat16),
            pltpu.VMEM((tm, LANES), jnp.float32),
            pltpu.VMEM((8, tm), jnp.float32),
            pltpu.VMEM((MOE_ROWS_ALLOC, D), jnp.bfloat16),
            pltpu.VMEM((MOE_ROWS_ALLOC, D), jnp.bfloat16),
            pltpu.SMEM((N_EXPERTS,), jnp.int32),
            pltpu.SMEM((N_EXPERTS,), jnp.int32),
        ],
        compiler_params=_params(("arbitrary", "arbitrary")),
    )(x, modl, w_router_t, rbias_col, wg_bf, wu_bf, wd_bf,
      ln_g.reshape(1, D), ln_b.reshape(1, D))


SC_CORES = 2
SC_SUBCORES = 16
SC_MAX_INDEX_ROWS = 128
SC_RING = 4
SC_TILE_BYTES = 64 * 1024

ROUTE_TILE = 1024
EXPERT_CHUNK = 512


def _sc_gather_rows(table, idx):
    n_rows = idx.shape[0]
    width = table.shape[1]
    workers = SC_CORES * SC_SUBCORES
    rows = min(SC_MAX_INDEX_ROWS, SC_TILE_BYTES // (4 * width))
    per_worker = n_rows // workers
    assert per_worker * workers == n_rows and per_worker % rows == 0
    n_chunk = per_worker // rows
    assert n_chunk % SC_RING == 0
    mesh = plsc.VectorSubcoreMesh(core_axis_name="c", subcore_axis_name="s")
    row_buf = pltpu.VMEM((rows, width), table.dtype)
    idx_buf = pltpu.VMEM((rows,), jnp.int32)

    @functools.partial(
        pl.kernel, mesh=mesh,
        out_type=jax.ShapeDtypeStruct((n_rows, width), table.dtype),
        scratch_types=([idx_buf] * SC_RING + [row_buf] * SC_RING
                       + [pltpu.SemaphoreType.DMA] * SC_RING),
    )
    def gather(table_hbm, idx_hbm, out_hbm, *scratch):
        idx_v = scratch[:SC_RING]
        rows_v = scratch[SC_RING:2 * SC_RING]
        sems = scratch[2 * SC_RING:]
        wid = lax.axis_index("s") * SC_CORES + lax.axis_index("c")
        base = wid * per_worker

        def gather_copy(b):
            return pltpu.make_async_copy(table_hbm.at[idx_v[b]], rows_v[b], sems[b])

        def start_gather(j, b):
            off = pl.multiple_of(base + j * rows, 8)
            pltpu.sync_copy(idx_hbm.at[pl.ds(off, rows)], idx_v[b])
            gather_copy(b).start()

        for b in range(SC_RING):
            start_gather(b, b)

        def group(g, carry):
            for b in range(SC_RING):
                j = g * SC_RING + b
                gather_copy(b).wait()
                off = pl.multiple_of(base + j * rows, 8)
                pltpu.sync_copy(rows_v[b], out_hbm.at[pl.ds(off, rows)])

                @pl.when(j + SC_RING < n_chunk)
                def _():
                    start_gather(j + SC_RING, b)
            return carry

        lax.fori_loop(0, n_chunk // SC_RING, group, 0)

    return gather(table, idx)


def _route_kernel(x_ref, mod_ref, wr_ref, rb_ref, hp_ref, meta_ref, cnt_ref, wcol_ref, carry_scr):
    t = pl.program_id(0)
    T = ROUTE_TILE

    @pl.when(t == 0)
    def _():
        carry_scr[...] = jnp.zeros_like(carry_scr)

    shift = mod_ref[0, 3:4, :]
    scale = mod_ref[0, 4:5, :]
    h = _layer_norm(x_ref[0]) * (1.0 + scale) + shift
    h_hi = h.astype(jnp.bfloat16)
    h_lo = (h - h_hi.astype(jnp.float32)).astype(jnp.bfloat16)
    hp_ref[...] = h

    wr_hi, wr_lo = _split_bf16(wr_ref[...])
    logits_t = _nt_dot(wr_hi, h_hi) + (_nt_dot(wr_lo, h_hi) + _nt_dot(wr_hi, h_lo))
    i1, i2, w1, w2 = _router_rows(logits_t, rb_ref)

    e_iota = lax.broadcasted_iota(jnp.int32, (N_EXPERTS, T), 0).astype(jnp.float32)
    sel1 = e_iota == i1
    sel2 = e_iota == i2
    onehot = jnp.where(sel1 | sel2, 1.0, 0.0)
    tr = lax.broadcasted_iota(jnp.int32, (T, T), 0)
    tc = lax.broadcasted_iota(jnp.int32, (T, T), 1)
    earlier = jnp.where(tr < tc, 1.0, 0.0).astype(jnp.bfloat16)
    rank = jnp.dot(onehot.astype(jnp.bfloat16), earlier,
                   preferred_element_type=jnp.float32) + carry_scr[:, 0:1]
    rank1 = jnp.sum(jnp.where(sel1, rank, 0.0), axis=0, keepdims=True)
    rank2 = jnp.sum(jnp.where(sel2, rank, 0.0), axis=0, keepdims=True)
    for r, row in enumerate((i1, i2, rank1, rank2)):
        meta_ref[r, 0] = row
    carry_scr[...] = carry_scr[...] + jnp.sum(onehot, axis=1, keepdims=True)
    cnt_ref[...] = carry_scr[...]

    r128 = lax.broadcasted_iota(jnp.int32, (LANES, T), 0)
    terms = jnp.zeros((LANES, T), jnp.float32)
    k = 0
    for w in (w1, w2):
        rest = w
        for _ in range(3):
            part = rest.astype(jnp.bfloat16).astype(jnp.float32)
            terms = jnp.where(r128 == k, part, terms)
            rest = rest - part
            k += 1
    eye = jnp.where(tr == tc, 1.0, 0.0).astype(jnp.bfloat16)
    wcol_ref[...] = _nt_dot(eye, terms.astype(jnp.bfloat16))


def _route_call(x, modl, w_router_t, rbias_col):
    B, S, D = x.shape
    tm = ROUTE_TILE
    nt = S // tm
    n_tiles = B * nt
    return pl.pallas_call(
        _route_kernel,
        grid=(n_tiles,),
        in_specs=[
            pl.BlockSpec((1, tm, D), lambda t: (t // nt, t % nt, 0)),
            pl.BlockSpec((1, N_MOD, D), lambda t: (t // nt, 0, 0)),
            pl.BlockSpec((N_EXPERTS, D), lambda t: (0, 0)),
            pl.BlockSpec((N_EXPERTS, 1), lambda t: (0, 0)),
        ],
        out_specs=[
            pl.BlockSpec((tm, D), lambda t: (t, 0)),
            pl.BlockSpec((4, 1, 1, tm), lambda t: (0, t, 0, 0)),
            pl.BlockSpec((N_EXPERTS, LANES), lambda t: (0, 0)),
            pl.BlockSpec((tm, LANES), lambda t: (t, 0)),
        ],
        out_shape=[
            jax.ShapeDtypeStruct((B * S, D), jnp.float32),
            jax.ShapeDtypeStruct((4, n_tiles, 1, tm), jnp.float32),
            jax.ShapeDtypeStruct((N_EXPERTS, LANES), jnp.float32),
            jax.ShapeDtypeStruct((B * S, LANES), jnp.float32),
        ],
        scratch_shapes=[pltpu.VMEM((N_EXPERTS, LANES), jnp.float32)],
        compiler_params=_params(("arbitrary",)),
    )(x, modl, w_router_t, rbias_col)


def _dispatch_plan(meta, counts, n_tokens):
    n_rows = 2 * n_tokens + N_EXPERTS * EXPERT_CHUNK
    n_chunks = n_rows // EXPERT_CHUNK
    meta = meta.reshape(meta.shape[0], n_tokens).astype(jnp.int32)
    per_token = lambda r: meta[r]
    e1 = per_token(0)
    e2 = per_token(1)
    cnt = counts[:, 0].astype(jnp.int32)
    padded = (cnt + (EXPERT_CHUNK - 1)) // EXPERT_CHUNK * EXPERT_CHUNK
    ends = jnp.cumsum(padded)
    starts = ends - padded
    first = jnp.cumsum(cnt) - cnt
    pos1 = starts[e1] + per_token(2)
    pos2 = starts[e2] + per_token(3)
    order = jnp.argsort(jnp.concatenate([pos1, pos2])).astype(jnp.int32)
    sorted_tok = jnp.where(order >= n_tokens, order - n_tokens, order)
    chunk_row0 = jnp.arange(n_chunks, dtype=jnp.int32) * EXPERT_CHUNK
    chunk_expert = jnp.minimum(
        jnp.sum((ends[None, :] <= chunk_row0[:, None]).astype(jnp.int32), axis=1), N_EXPERTS - 1)
    local = (chunk_row0 - starts[chunk_expert])[:, None] + jnp.arange(EXPERT_CHUNK, dtype=jnp.int32)
    real = local < cnt[chunk_expert][:, None]
    nth = jnp.clip(first[chunk_expert][:, None] + local, 0, 2 * n_tokens - 1)
    spread = (chunk_row0[:, None] + jnp.arange(EXPERT_CHUNK, dtype=jnp.int32)) % n_tokens
    src = jnp.where(real, sorted_tok[nth], spread).reshape(n_rows)
    n_used = (ends[-1] // EXPERT_CHUNK).reshape(1).astype(jnp.int32)
    return pos1, pos2, src, chunk_expert, n_used


def _expert_kernel(ce_ref, nu_ref, xs_ref, wg_ref, wu_ref, wd_ref, o_ref, wg_bf, wu_bf, wd_bf):
    c = pl.program_id(0)

    @pl.when(c < nu_ref[0])
    def _():
        @pl.when((c == 0) | (ce_ref[c] != ce_ref[jnp.maximum(c - 1, 0)]))
        def _():
            wg_bf[...] = wg_ref[0, 0].astype(jnp.bfloat16)
            wu_bf[...] = wu_ref[0, 0].astype(jnp.bfloat16)
            wd_bf[...] = wd_ref[0, 0].astype(jnp.bfloat16)

        xs = xs_ref[...].astype(jnp.bfloat16)
        gt = jnp.dot(xs, wg_bf[...], preferred_element_type=jnp.float32)
        up = jnp.dot(xs, wu_bf[...], preferred_element_type=jnp.float32)
        act = (gt * (1.0 / (1.0 + jnp.exp(-gt))) * up).astype(jnp.bfloat16)
        o_ref[...] = jnp.dot(act, wd_bf[...], preferred_element_type=jnp.float32)

    @pl.when(c >= nu_ref[0])
    def _():
        o_ref[...] = jnp.zeros_like(o_ref)


def _expert_call(chunk_expert, n_used, xs, w_gate, w_up, w_down, layer):
    n_rows, D = xs.shape
    grid_spec = pltpu.PrefetchScalarGridSpec(
        num_scalar_prefetch=2,
        grid=(n_rows // EXPERT_CHUNK,),
        in_specs=[
            pl.BlockSpec((EXPERT_CHUNK, D), lambda c, ce, nu: (c, 0)),
            pl.BlockSpec((1, 1, D, D_EXPERT), lambda c, ce, nu: (layer, ce[c], 0, 0)),
            pl.BlockSpec((1, 1, D, D_EXPERT), lambda c, ce, nu: (layer, ce[c], 0, 0)),
            pl.BlockSpec((1, 1, D_EXPERT, D), lambda c, ce, nu: (layer, ce[c], 0, 0)),
        ],
        out_specs=pl.BlockSpec((EXPERT_CHUNK, D), lambda c, ce, nu: (c, 0)),
        scratch_shapes=[
            pltpu.VMEM((D, D_EXPERT), jnp.bfloat16),
            pltpu.VMEM((D, D_EXPERT), jnp.bfloat16),
            pltpu.VMEM((D_EXPERT, D), jnp.bfloat16),
        ],
    )
    return pl.pallas_call(
        _expert_kernel,
        grid_spec=grid_spec,
        out_shape=jax.ShapeDtypeStruct((n_rows, D), jnp.float32),
        compiler_params=_params(("arbitrary",)),
    )(chunk_expert, n_used, xs, w_gate, w_up, w_down)


def _combine_kernel(y1_ref, y2_ref, wcol_ref, x_ref, mod_ref, g_ref, b_ref, o_ref):
    wc = wcol_ref[0]
    w1 = (wc[:, 0:1] + wc[:, 1:2]) + wc[:, 2:3]
    w2 = (wc[:, 3:4] + wc[:, 4:5]) + wc[:, 5:6]
    y = w1 * y1_ref[0] + w2 * y2_ref[0]
    gate = mod_ref[0, 5:6, :]
    z = DEEPNORM_ALPHA * x_ref[0] + gate * y
    o_ref[0] = _layer_norm(z) * g_ref[...] + b_ref[...]


def _combine_call(y1, y2, wcol, x, modl, ln_g, ln_b):
    B, S, D = x.shape
    tm = 512
    rows = pl.BlockSpec((1, tm, D), lambda b, i: (b, i, 0))
    vec = pl.BlockSpec((1, D), lambda b, i: (0, 0))
    return pl.pallas_call(
        _combine_kernel,
        grid=(B, S // tm),
        in_specs=[rows, rows, pl.BlockSpec((1, tm, LANES), lambda b, i: (b, i, 0)), rows,
                  pl.BlockSpec((1, N_MOD, D), lambda b, i: (b, 0, 0)), vec, vec],
        out_specs=rows,
        out_shape=jax.ShapeDtypeStruct((B, S, D), jnp.float32),
        compiler_params=_params(("arbitrary", "arbitrary")),
    )(y1.reshape(B, S, D), y2.reshape(B, S, D), wcol.reshape(B, S, LANES),
      x, modl, ln_g.reshape(1, D), ln_b.reshape(1, D))


def _moe_routed(x, modl, w_router_t, rbias_col, w_gate, w_up, w_down, layer, ln_g, ln_b):
    B, S, D = x.shape
    h, meta, counts, wcol = _route_call(x, modl, w_router_t, rbias_col)
    pos1, pos2, src, chunk_expert, n_used = _dispatch_plan(meta, counts, B * S)
    xs = _sc_gather_rows(h, src)
    ys = _expert_call(chunk_expert, n_used, xs, w_gate, w_up, w_down, layer)
    y1 = _sc_gather_rows(ys, pos1)
    y2 = _sc_gather_rows(ys, pos2)
    return _combine_call(y1, y2, wcol, x, modl, ln_g, ln_b)


def _rope_tables(S):
    half = ROT_DIM // 2
    inv_freq = ROPE_THETA ** (-(jnp.arange(half, dtype=jnp.float32) * 2.0 / ROT_DIM))
    ang = jnp.arange(S, dtype=jnp.float32)[:, None] * inv_freq[None, :]
    cos, sin = jnp.cos(ang), jnp.sin(ang)
    d = jnp.arange(LANES) % HEAD_DIM
    f = d % half
    rot = d[None, :] < ROT_DIM
    return jnp.where(rot, cos[:, f], 1.0), jnp.where(rot, sin[:, f], 0.0)


def _with_rotary_partner_columns(w_in_l):
    half = ROT_DIM // 2
    qk = w_in_l[:, :2 * D_ATTN]
    d = (jnp.arange(2 * D_ATTN) % HEAD_DIM)[None, :]
    partner = jnp.where(d < half, -jnp.roll(qk, -half, axis=1),
                        jnp.where(d < ROT_DIM, jnp.roll(qk, half, axis=1), 0.0))
    return jnp.concatenate([w_in_l, partner], axis=1)


def _bias_placement():
    src = jnp.arange(LANES)
    hd, u, j = src >> 4, (src >> 3) & 1, src & 7
    col0 = jnp.where(hd % 2 == 0, HEAD_DIM, 0)
    dst = LANES * hd + col0 + j
    onehot = (jnp.arange(N_HEADS * LANES)[None, :] == dst[:, None]) & (u[:, None] == 1)
    return onehot.astype(jnp.bfloat16)


def kernel(x, c, w_mod, b_mod, w_in, w_pool, pool_scale, w_out, ln1_g, ln1_b,
           w_router, router_bias, w_gate, w_up, w_down, ln2_g, ln2_b):
    B, S, D = x.shape
    bf = jnp.bfloat16
    mod = _mod_call(c, w_mod, b_mod).reshape(DEPTH, B, N_MOD, D)
    cos_t, sin_t = _rope_tables(S)
    place = _bias_placement()
    w_router_t = w_router.T
    rbias_col = router_bias.reshape(N_EXPERTS, 1)
    for l in range(DEPTH):
        modl = mod[l]
        w_in_aug = _with_rotary_partner_columns(w_in[l]).astype(bf)
        qa, ka, v, p, _ = _proj_call(x, modl, w_in_aug, cos_t, sin_t, place)
        a = _attn_call(qa, ka, v)
        m = _pool_call(p, w_pool[l], pool_scale[l])
        x = _mixout_call(a, m, x, modl, w_out[l].astype(bf), ln1_g[l], ln1_b[l])
        x = _moe_routed(x, modl, w_router_t, rbias_col, w_gate, w_up, w_down, l,
                        ln2_g[l], ln2_b[l])
    return x
```

```python
import functools

import jax
import jax.numpy as jnp
from jax import lax
from jax.experimental import pallas as pl
from jax.experimental.pallas import tpu as pltpu
from jax.experimental.pallas import tpu_sc as plsc

D_MODEL = 1024
DEPTH = 2
D_ATTN = 512
D_POOL = 512
N_HEADS = 8
HEAD_DIM = 64
ROT_DIM = 16
ROPE_THETA = 500000.0
MOBA_BLOCK = 256
MOBA_TOPK = 3
POOL_GROUP = 128
N_POOL_GROUPS = 4
D_IN = 3 * D_ATTN + D_POOL
N_EXPERTS = 16
EXPERTS_PER_GROUP = 4
D_EXPERT = 512
DEEPNORM_ALPHA = (2 * DEPTH) ** 0.25
N_MOD = 6
LN_EPS = 1e-5
NEG_INF = -1e30

POOL_HALO = 16
LANES = 128
VMEM_LIMIT = 56 * 1024 * 1024

_HI = lax.Precision.HIGHEST
_NT = (((1,), (1,)), ((), ()))


def _params(sem):
    return pltpu.CompilerParams(dimension_semantics=sem, vmem_limit_bytes=VMEM_LIMIT)


def _nt_dot(a, b):
    return lax.dot_general(a, b, _NT, preferred_element_type=jnp.float32)


def _mod_kernel(c_ref, w_ref, b_ref, o_ref):
    c = c_ref[...]
    cond = c * (1.0 / (1.0 + jnp.exp(-c)))
    o_ref[0] = jnp.dot(cond, w_ref[0], precision=_HI,
                       preferred_element_type=jnp.float32) + b_ref[0]


def _mod_call(c, w_mod, b_mod):
    B = c.shape[0]
    return pl.pallas_call(
        _mod_kernel,
        grid=(DEPTH, N_MOD),
        in_specs=[
            pl.BlockSpec((B, D_MODEL), lambda l, j: (0, 0)),
            pl.BlockSpec((1, D_MODEL, D_MODEL), lambda l, j: (l, 0, j)),
            pl.BlockSpec((1, 1, D_MODEL), lambda l, j: (l, 0, j)),
        ],
        out_specs=pl.BlockSpec((1, B, D_MODEL), lambda l, j: (l, 0, j)),
        out_shape=jax.ShapeDtypeStruct((DEPTH, B, N_MOD * D_MODEL), jnp.float32),
        compiler_params=_params(("arbitrary", "arbitrary")),
    )(c, w_mod, b_mod.reshape(DEPTH, 1, N_MOD * D_MODEL))


def _layer_norm(x):
    mu = jnp.mean(x, axis=-1, keepdims=True)
    xc = x - mu
    var = jnp.mean(xc * xc, axis=-1, keepdims=True)
    return xc * lax.rsqrt(var + LN_EPS)


def _split_bf16(t):
    hi = t.astype(jnp.bfloat16)
    lo = (t - hi.astype(jnp.float32)).astype(jnp.bfloat16)
    return hi, lo


def _proj_kernel(x_ref, mod_ref, w_ref, cos_ref, sin_ref, place_ref, wp_ref, ps_ref,
                 qa_ref, ka_ref, v_ref, m_ref, kbar_ref, halo_scr):
    i = pl.program_id(1)

    @pl.when(i == 0)
    def _():
        kbar_ref[...] = jnp.zeros_like(kbar_ref)
        halo_scr[...] = jnp.zeros_like(halo_scr)

    x = x_ref[0]
    shift = mod_ref[0, 0:1, :]
    scale = mod_ref[0, 1:2, :]
    h = (_layer_norm(x) * (1.0 + scale) + shift).astype(jnp.bfloat16)

    cos = cos_ref[...]
    sin = sin_ref[...]

    def proj(c0, width):
        return jnp.dot(h, w_ref[:, c0:c0 + width], preferred_element_type=jnp.float32)

    def slab(t, s):
        return t[:, LANES * s:LANES * (s + 1)]

    n_slab = D_ATTN // LANES
    q = proj(0, D_ATTN)
    qp = proj(D_IN, D_ATTN)
    q_slabs = [(slab(q, s) * cos + slab(qp, s) * sin) * (HEAD_DIM ** -0.5)
               for s in range(n_slab)]
    k = proj(D_ATTN, D_ATTN)
    kp = proj(D_IN + D_ATTN, D_ATTN)
    k_slabs = [slab(k, s) * cos + slab(kp, s) * sin for s in range(n_slab)]
    v_ref[0] = proj(2 * D_ATTN, D_ATTN).astype(jnp.bfloat16)

    p = proj(3 * D_ATTN, D_POOL)
    ext = jnp.concatenate([halo_scr[...], p], axis=0)
    halo_scr[...] = p[MOBA_BLOCK - POOL_HALO:, :]
    t_pos = i * MOBA_BLOCK + lax.broadcasted_iota(jnp.int32, (MOBA_BLOCK, LANES), 0)
    m_slabs = []
    for g in range(N_POOL_GROUPS):
        win = slab(ext, g)
        for step in range(g + 1):
            win = win + pltpu.roll(win, 1 << step, 0)
        cnt = jnp.minimum(t_pos + 1, 2 << g).astype(jnp.float32)
        d = (win[POOL_HALO:, :] / cnt - slab(p, g)).astype(jnp.bfloat16)
        y = jnp.dot(d, wp_ref[g].astype(jnp.bfloat16), preferred_element_type=jnp.float32)
        m_slabs.append(y * slab(ps_ref[...], g))
    m_ref[0] = jnp.concatenate(m_slabs, axis=1).astype(m_ref.dtype)

    kmean = jnp.concatenate(
        [jnp.mean(ks, axis=0, keepdims=True) for ks in k_slabs], axis=1)
    kbar_ref[0, pl.ds(i, 1), :] = kmean

    kb = kbar_ref[0]
    kb_rows = jnp.concatenate([kb] * (LANES // 8), axis=0)
    r_head = lax.broadcasted_iota(jnp.int32, (LANES, D_ATTN), 1) >> 6
    c_head = lax.broadcasted_iota(jnp.int32, (LANES, D_ATTN), 0) >> 4
    kbd_hi, kbd_lo = _split_bf16(jnp.where(r_head == c_head, kb_rows, 0.0))
    q_hi, q_lo = _split_bf16(jnp.concatenate(q_slabs, axis=1))
    gate = _nt_dot(q_hi, kbd_hi) + (_nt_dot(q_lo, kbd_hi) + _nt_dot(q_hi, kbd_lo))

    lane = lax.broadcasted_iota(jnp.int32, (MOBA_BLOCK, LANES), 1)
    j_of = lane & 7
    past = j_of < i
    gm = jnp.where(past, gate, NEG_INF)
    rank = jnp.zeros((MOBA_BLOCK, LANES), jnp.int32)
    for r in range(1, 8):
        other = pltpu.roll(gm, r, 1)
        beats = (other > gm) | ((other == gm) & (j_of >= r))
        rank = rank + beats.astype(jnp.int32)
    allowed = (past & (rank < MOBA_TOPK)) | (j_of == i)
    bias = jnp.where(allowed, 0.0, NEG_INF).astype(jnp.bfloat16)
    bias_cols = jnp.dot(bias, place_ref[...], preferred_element_type=jnp.float32)

    for hd in range(N_HEADS):
        own = (lane < HEAD_DIM) if hd % 2 == 0 else (lane >= HEAD_DIM)
        col0 = HEAD_DIM if hd % 2 == 0 else 0
        qa = jnp.where(own, q_slabs[hd // 2], slab(bias_cols, hd))
        ka = jnp.where(own, k_slabs[hd // 2], jnp.where(lane == col0 + i, 1.0, 0.0))
        qa_ref[0, hd] = qa.astype(jnp.bfloat16)
        ka_ref[0, hd] = ka.astype(jnp.bfloat16)


def _proj_call(x, modl, w_in_aug, cos_t, sin_t, place, w_pool_l, pool_scale_l):
    B, S, D = x.shape
    nb = S // MOBA_BLOCK
    tm = MOBA_BLOCK
    tab = pl.BlockSpec((tm, LANES), lambda b, i: (i, 0))
    head_spec = pl.BlockSpec((1, N_HEADS, tm, LANES), lambda b, i: (b, 0, i, 0))
    return pl.pallas_call(
        _proj_kernel,
        grid=(B, nb),
        in_specs=[
            pl.BlockSpec((1, tm, D), lambda b, i: (b, i, 0)),
            pl.BlockSpec((1, N_MOD, D), lambda b, i: (b, 0, 0)),
            pl.BlockSpec((D, D_IN + 2 * D_ATTN), lambda b, i: (0, 0)),
            tab, tab,
            pl.BlockSpec((LANES, N_HEADS * LANES), lambda b, i: (0, 0)),
            pl.BlockSpec((N_POOL_GROUPS, POOL_GROUP, POOL_GROUP), lambda b, i: (0, 0, 0)),
            pl.BlockSpec((1, D_POOL), lambda b, i: (0, 0)),
        ],
        out_specs=[
            head_spec, head_spec,
            pl.BlockSpec((1, tm, D_ATTN), lambda b, i: (b, i, 0)),
            pl.BlockSpec((1, tm, D_POOL), lambda b, i: (b, i, 0)),
            pl.BlockSpec((1, nb, D_ATTN), lambda b, i: (b, 0, 0)),
        ],
        out_shape=[
            jax.ShapeDtypeStruct((B, N_HEADS, S, LANES), jnp.bfloat16),
            jax.ShapeDtypeStruct((B, N_HEADS, S, LANES), jnp.bfloat16),
            jax.ShapeDtypeStruct((B, S, D_ATTN), jnp.bfloat16),
            jax.ShapeDtypeStruct((B, S, D_POOL), jnp.bfloat16),
            jax.ShapeDtypeStruct((B, nb, D_ATTN), jnp.float32),
        ],
        scratch_shapes=[pltpu.VMEM((POOL_HALO, D_POOL), jnp.float32)],
        compiler_params=_params(("arbitrary", "arbitrary")),
    )(x, modl, w_in_aug, cos_t, sin_t, place, w_pool_l, pool_scale_l.reshape(1, D_POOL))


def _attn_kernel(qa_ref, ka_ref, v_ref, o_ref):
    nb = v_ref.shape[1] // MOBA_BLOCK
    row = lax.broadcasted_iota(jnp.int32, (MOBA_BLOCK, MOBA_BLOCK), 0)
    col = lax.broadcasted_iota(jnp.int32, (MOBA_BLOCK, MOBA_BLOCK), 1)
    causal = col <= row
    lane = lax.broadcasted_iota(jnp.int32, (MOBA_BLOCK, LANES), 1)

    for i in reversed(range(nb)):
        r0 = i * MOBA_BLOCK
        outs = []
        for hh in range(2):
            q = qa_ref[0, hh, r0:r0 + MOBA_BLOCK, :]
            s_own = jnp.where(causal, _nt_dot(q, ka_ref[0, hh, r0:r0 + MOBA_BLOCK, :]), NEG_INF)
            m = jnp.max(s_own, axis=1, keepdims=True)
            if i > 0:
                s_past = _nt_dot(q, ka_ref[0, hh, 0:r0, :])
                m = jnp.maximum(m, jnp.max(s_past, axis=1, keepdims=True))
            p_own = jnp.exp(s_own - m)
            l = jnp.sum(p_own, axis=1, keepdims=True)
            acc = jnp.dot(p_own.astype(jnp.bfloat16), v_ref[0, r0:r0 + MOBA_BLOCK, :],
                          preferred_element_type=jnp.float32)
            if i > 0:
                p_past = jnp.exp(s_past - m)
                l = l + jnp.sum(p_past, axis=1, keepdims=True)
                acc = acc + jnp.dot(p_past.astype(jnp.bfloat16), v_ref[0, 0:r0, :],
                                    preferred_element_type=jnp.float32)
            outs.append(acc / l)
        o = jnp.where(lane < HEAD_DIM, outs[0], outs[1])
        o_ref[0, r0:r0 + MOBA_BLOCK, :] = o.astype(o_ref.dtype)


def _attn_call(qa, ka, v):
    B, _, S, _ = qa.shape
    n_pair = N_HEADS // 2
    pair_spec = pl.BlockSpec((1, 2, S, LANES), lambda b, hp: (b, hp, 0, 0))
    slab_spec = pl.BlockSpec((1, S, LANES), lambda b, hp: (b, 0, hp))
    return pl.pallas_call(
        _attn_kernel,
        grid=(B, n_pair),
        in_specs=[pair_spec, pair_spec, slab_spec],
        out_specs=slab_spec,
        out_shape=jax.ShapeDtypeStruct((B, S, D_ATTN), jnp.bfloat16),
        compiler_params=_params(("arbitrary", "arbitrary")),
    )(qa, ka, v)


def _pool_kernel(p_ref, w_ref, sc_ref, o_ref):
    g = pl.program_id(1)
    S = p_ref.shape[1]
    p = p_ref[0]
    t = lax.broadcasted_iota(jnp.int32, (S, LANES), 0)

    def shifted(x, k):
        return jnp.where(t >= k, pltpu.roll(x, k, 0), 0.0)

    win = p
    sums = []
    for step in range(N_POOL_GROUPS):
        win = win + shifted(win, 1 << step)
        sums.append(win)
    wsum = jnp.where(g == 0, sums[0],
                     jnp.where(g == 1, sums[1], jnp.where(g == 2, sums[2], sums[3])))
    window = jnp.left_shift(2, g)
    cnt = jnp.minimum(t + 1, window).astype(jnp.float32)
    d = (wsum / cnt - p).astype(jnp.bfloat16)
    y = jnp.dot(d, w_ref[0].astype(jnp.bfloat16), preferred_element_type=jnp.float32)
    o_ref[0] = (y * sc_ref[...]).astype(o_ref.dtype)


def _pool_call(p, w_pool_l, pool_scale_l):
    B, S, _ = p.shape
    slab = pl.BlockSpec((1, S, LANES), lambda b, g: (b, 0, g))
    return pl.pallas_call(
        _pool_kernel,
        grid=(B, N_POOL_GROUPS),
        in_specs=[
            slab,
            pl.BlockSpec((1, POOL_GROUP, POOL_GROUP), lambda b, g: (g, 0, 0)),
            pl.BlockSpec((1, LANES), lambda b, g: (0, g)),
        ],
        out_specs=slab,
        out_shape=jax.ShapeDtypeStruct((B, S, D_POOL), jnp.bfloat16),
        compiler_params=_params(("arbitrary", "arbitrary")),
    )(p, w_pool_l, pool_scale_l.reshape(1, D_POOL))


def _mixout_kernel(a_ref, m_ref, x_ref, mod_ref, w_ref, g_ref, b_ref, o_ref):
    y = jnp.dot(a_ref[0], w_ref[0:D_ATTN, :], preferred_element_type=jnp.float32)
    y = y + jnp.dot(m_ref[0], w_ref[D_ATTN:, :], preferred_element_type=jnp.float32)
    gate = mod_ref[0, 2:3, :]
    z = DEEPNORM_ALPHA * x_ref[0] + gate * y
    o_ref[0] = _layer_norm(z) * g_ref[...] + b_ref[...]


def _mixout_call(a, m, x, modl, w_out_bf, ln_g, ln_b):
    B, S, D = x.shape
    tm = 512
    vec = pl.BlockSpec((1, D), lambda b, i: (0, 0))
    return pl.pallas_call(
        _mixout_kernel,
        grid=(B, S // tm),
        in_specs=[
            pl.BlockSpec((1, tm, D_ATTN), lambda b, i: (b, i, 0)),
            pl.BlockSpec((1, tm, D_POOL), lambda b, i: (b, i, 0)),
            pl.BlockSpec((1, tm, D), lambda b, i: (b, i, 0)),
            pl.BlockSpec((1, N_MOD, D), lambda b, i: (b, 0, 0)),
            pl.BlockSpec((D, D), lambda b, i: (0, 0)),
            vec, vec,
        ],
        out_specs=pl.BlockSpec((1, tm, D), lambda b, i: (b, i, 0)),
        out_shape=jax.ShapeDtypeStruct((B, S, D), jnp.float32),
        compiler_params=_params(("arbitrary", "arbitrary")),
    )(a, m, x, modl, w_out_bf, ln_g.reshape(1, D), ln_b.reshape(1, D))


MOE_TILE = 1024
MOE_CHUNK = 160
MOE_ROWS = 2304
MOE_ROWS_ALLOC = MOE_ROWS + MOE_CHUNK
MOE_G_COLS = 768
MOE_EXPERTS_PER_STEP = 2


def _top2_rows(vals):
    def first_max(rows):
        m = rows[0]
        for v in rows[1:]:
            m = jnp.maximum(m, v)
        idx = jnp.full_like(m, float(len(rows) - 1))
        for k in range(len(rows) - 2, -1, -1):
            idx = jnp.where(rows[k] == m, float(k), idx)
        return m, idx

    m1, i1 = first_max(vals)
    rest = [jnp.where(i1 == float(k), -jnp.inf, v) for k, v in enumerate(vals)]
    m2, i2 = first_max(rest)
    return m1, i1, m2, i2


def _router_rows(logits_t, rb_ref):
    lg = [logits_t[e:e + 1, :] for e in range(N_EXPERTS)]
    mx = lg[0]
    for v in lg[1:]:
        mx = jnp.maximum(mx, v)
    ex = [jnp.exp(v - mx) for v in lg]
    den = ex[0]
    for v in ex[1:]:
        den = den + v
    scores = [v / den for v in ex]
    sel = [scores[e] + rb_ref[e:e + 1, :] for e in range(N_EXPERTS)]
    best_score = None
    best = None
    for g in range(N_EXPERTS // EXPERTS_PER_GROUP):
        m1, _, m2, _ = _top2_rows(sel[g * EXPERTS_PER_GROUP:(g + 1) * EXPERTS_PER_GROUP])
        gs = m1 + m2
        if g == 0:
            best_score, best = gs, jnp.zeros_like(gs)
        else:
            better = gs > best_score
            best_score = jnp.where(better, gs, best_score)
            best = jnp.where(better, float(g), best)
    masked = [jnp.where(best == float(e // EXPERTS_PER_GROUP), sel[e], NEG_INF)
              for e in range(N_EXPERTS)]
    _, i1, _, i2 = _top2_rows(masked)
    w1 = jnp.zeros_like(i1)
    w2 = jnp.zeros_like(i2)
    for e in range(N_EXPERTS):
        w1 = jnp.where(i1 == float(e), scores[e], w1)
        w2 = jnp.where(i2 == float(e), scores[e], w2)
    tot = w1 + w2
    return i1, i2, w1 / tot, w2 / tot


def _moe_route(x_ref, mod_ref, wr_ref, rb_ref, h_scr, col_scr, row_scr, ysh_scr, ysl_scr,
               start_smem, cnt_smem):
    T = MOE_TILE
    shift = mod_ref[0, 3:4, :]
    scale = mod_ref[0, 4:5, :]
    h = _layer_norm(x_ref[0]) * (1.0 + scale) + shift
    h_hi = h.astype(jnp.bfloat16)
    h_scr[...] = h_hi
    h_lo = (h - h_hi.astype(jnp.float32)).astype(jnp.bfloat16)
    wr_hi, wr_lo = _split_bf16(wr_ref[...])
    logits_t = _nt_dot(wr_hi, h_hi) + (_nt_dot(wr_lo, h_hi) + _nt_dot(wr_hi, h_lo))
    i1, i2, w1, w2 = _router_rows(logits_t, rb_ref)

    e_iota = lax.broadcasted_iota(jnp.int32, (N_EXPERTS, T), 0).astype(jnp.float32)
    sel1 = e_iota == i1
    sel2 = e_iota == i2
    onehot = jnp.where(sel1 | sel2, 1.0, 0.0)
    tr = lax.broadcasted_iota(jnp.int32, (T, T), 0)
    tc = lax.broadcasted_iota(jnp.int32, (T, T), 1)
    earlier = jnp.where(tr < tc, 1.0, 0.0).astype(jnp.bfloat16)
    excl = jnp.dot(onehot.astype(jnp.bfloat16), earlier,
                   preferred_element_type=jnp.float32)
    cnt = jnp.sum(onehot, axis=1, keepdims=True)

    e_col = lax.broadcasted_iota(jnp.int32, (N_EXPERTS, 1), 0)
    start_v = jnp.zeros((N_EXPERTS, 1), jnp.float32)
    run = jnp.int32(0)
    for ex in range(N_EXPERTS):
        padded = lax.shift_left(lax.shift_right_logical(cnt[ex, 0].astype(jnp.int32) + 15, 4), 4)
        start_smem[ex] = run
        cnt_smem[ex] = padded
        start_v = jnp.where(e_col == ex, run.astype(jnp.float32), start_v)
        run = run + padded

    slot = excl + start_v
    pos1 = jnp.sum(jnp.where(sel1, slot, 0.0), axis=0, keepdims=True)
    pos2 = jnp.sum(jnp.where(sel2, slot, 0.0), axis=0, keepdims=True)
    row_scr[0:1, :] = pos1
    row_scr[1:2, :] = pos2
    row_scr[2:3, :] = w1
    row_scr[3:4, :] = w2

    r128 = lax.broadcasted_iota(jnp.int32, (LANES, T), 0)
    terms = jnp.zeros((LANES, T), jnp.float32)
    k = 0
    for p in (pos1, pos2):
        a = jnp.floor(p * (1.0 / 64.0))
        for t in (a, p - 64.0 * a):
            terms = jnp.where(r128 == k, t, terms)
            k += 1
    eye = jnp.where(tr == tc, 1.0, 0.0).astype(jnp.bfloat16)
    col_scr[...] = _nt_dot(eye, terms.astype(jnp.bfloat16))

    tail = MOE_ROWS_ALLOC - 2 * T
    ysh_scr[2 * T:, :] = jnp.zeros((tail, D_MODEL), jnp.bfloat16)
    ysl_scr[2 * T:, :] = jnp.zeros((tail, D_MODEL), jnp.bfloat16)


def _moe_kernel(x_ref, mod_ref, wr_ref, rb_ref, wg_ref, wu_ref, wd_ref, g_ref, b_ref,
                o_ref, h_scr, col_scr, row_scr, ysh_scr, ysl_scr, start_smem, cnt_smem):
    step = pl.program_id(1)
    T = MOE_TILE

    @pl.when(step == 0)
    def _():
        _moe_route(x_ref, mod_ref, wr_ref, rb_ref, h_scr, col_scr, row_scr, ysh_scr, ysl_scr,
                   start_smem, cnt_smem)

    pos1_row = row_scr[0:1, :]
    pos2_row = row_scr[1:2, :]
    w1_row = row_scr[2:3, :]
    w2_row = row_scr[3:4, :]

    for k in range(MOE_EXPERTS_PER_STEP):
        e = step * MOE_EXPERTS_PER_STEP + k
        seg_start = start_smem[e]
        n_chunk = (cnt_smem[e] + (MOE_CHUNK - 1)) // MOE_CHUNK

        def chunk(j, carry, k=k, seg_start=seg_start):
            base = pl.multiple_of(seg_start + j * MOE_CHUNK, 16)
            r = (base + lax.broadcasted_iota(jnp.int32, (MOE_CHUNK, T), 0)).astype(jnp.float32)
            m1 = r == pos1_row
            m2 = r == pos2_row
            gather = jnp.where(m1 | m2, 1.0, 0.0).astype(jnp.bfloat16)
            w_row = jnp.sum(jnp.where(m1, w1_row, 0.0) + jnp.where(m2, w2_row, 0.0),
                            axis=1, keepdims=True)
            xs = jnp.dot(gather, h_scr[...],
                         preferred_element_type=jnp.float32).astype(jnp.bfloat16)
            gt = jnp.dot(xs, wg_ref[k], preferred_element_type=jnp.float32)
            up = jnp.dot(xs, wu_ref[k], preferred_element_type=jnp.float32)
            act = (gt * (1.0 / (1.0 + jnp.exp(-gt))) * up).astype(jnp.bfloat16)
            y = w_row * jnp.dot(act, wd_ref[k], preferred_element_type=jnp.float32)
            y_hi = y.astype(jnp.bfloat16)
            ysh_scr[pl.ds(base, MOE_CHUNK), :] = y_hi
            ysl_scr[pl.ds(base, MOE_CHUNK), :] = (y - y_hi.astype(jnp.float32)).astype(jnp.bfloat16)
            return carry

        lax.fori_loop(0, n_chunk, chunk, 0)

    @pl.when(step == N_EXPERTS // MOE_EXPERTS_PER_STEP - 1)
    def _():
        pos1 = 64.0 * col_scr[:, 0:1] + col_scr[:, 1:2]
        pos2 = 64.0 * col_scr[:, 2:3] + col_scr[:, 3:4]
        y = jnp.zeros((T, D_MODEL), jnp.float32)
        for c0 in range(0, MOE_ROWS, MOE_G_COLS):
            r = (c0 + lax.broadcasted_iota(jnp.int32, (T, MOE_G_COLS), 1)).astype(jnp.float32)
            scatter = jnp.where((r == pos1) | (r == pos2), 1.0, 0.0).astype(jnp.bfloat16)
            y = y + jnp.dot(scatter, ysh_scr[c0:c0 + MOE_G_COLS, :],
                            preferred_element_type=jnp.float32)
            y = y + jnp.dot(scatter, ysl_scr[c0:c0 + MOE_G_COLS, :],
                            preferred_element_type=jnp.float32)
        gate = mod_ref[0, 5:6, :]
        z = DEEPNORM_ALPHA * x_ref[0] + gate * y
        o_ref[0] = _layer_norm(z) * g_ref[...] + b_ref[...]


def _moe_call(x, modl, w_router_t, rbias_col, wg_bf, wu_bf, wd_bf, ln_g, ln_b):
    B, S, D = x.shape
    tm = MOE_TILE
    nt = S // tm
    eps = MOE_EXPERTS_PER_STEP
    vec = pl.BlockSpec((1, D), lambda t, e: (0, 0))
    xspec = pl.BlockSpec((1, tm, D), lambda t, e: (t // nt, t % nt, 0))
    return pl.pallas_call(
        _moe_kernel,
        grid=(B * nt, N_EXPERTS // eps),
        in_specs=[
            xspec,
            pl.BlockSpec((1, N_MOD, D), lambda t, e: (t // nt, 0, 0)),
            pl.BlockSpec((N_EXPERTS, D), lambda t, e: (0, 0)),
            pl.BlockSpec((N_EXPERTS, 1), lambda t, e: (0, 0)),
            pl.BlockSpec((eps, D, D_EXPERT), lambda t, e: (e, 0, 0)),
            pl.BlockSpec((eps, D, D_EXPERT), lambda t, e: (e, 0, 0)),
            pl.BlockSpec((eps, D_EXPERT, D), lambda t, e: (e, 0, 0)),
            vec, vec,
        ],
        out_specs=xspec,
        out_shape=jax.ShapeDtypeStruct((B, S, D), jnp.float32),
        scratch_shapes=[
            pltpu.VMEM((tm, D), jnp.bfloat16),
            pltpu.VMEM((tm, LANES), jnp.float32),
            pltpu.VMEM((8, tm), jnp.float32),
            pltpu.VMEM((MOE_ROWS_ALLOC, D), jnp.bfloat16),
            pltpu.VMEM((MOE_ROWS_ALLOC, D), jnp.bfloat16),
            pltpu.SMEM((N_EXPERTS,), jnp.int32),
            pltpu.SMEM((N_EXPERTS,), jnp.int32),
        ],
        compiler_params=_params(("arbitrary", "arbitrary")),
    )(x, modl, w_router_t, rbias_col, wg_bf, wu_bf, wd_bf,
      ln_g.reshape(1, D), ln_b.reshape(1, D))


SC_CORES = 2
SC_SUBCORES = 16
SC_MAX_INDEX_ROWS = 128
SC_RING = 4
SC_TILE_BYTES = 64 * 1024

ROUTE_TILE = 1024
EXPERT_CHUNK = 512


def _sc_gather_rows(table, idx):
    n_rows = idx.shape[0]
    width = table.shape[1]
    workers = SC_CORES * SC_SUBCORES
    rows = min(SC_MAX_INDEX_ROWS, SC_TILE_BYTES // (4 * width))
    per_worker = n_rows // workers
    assert per_worker * workers == n_rows and per_worker % rows == 0
    n_chunk = per_worker // rows
    assert n_chunk % SC_RING == 0
    mesh = plsc.VectorSubcoreMesh(core_axis_name="c", subcore_axis_name="s")
    row_buf = pltpu.VMEM((rows, width), table.dtype)
    idx_buf = pltpu.VMEM((rows,), jnp.int32)

    @functools.partial(
        pl.kernel, mesh=mesh,
        out_type=jax.ShapeDtypeStruct((n_rows, width), table.dtype),
        scratch_types=([idx_buf] * SC_RING + [row_buf] * SC_RING
                       + [pltpu.SemaphoreType.DMA] * SC_RING),
    )
    def gather(table_hbm, idx_hbm, out_hbm, *scratch):
        idx_v = scratch[:SC_RING]
        rows_v = scratch[SC_RING:2 * SC_RING]
        sems = scratch[2 * SC_RING:]
        wid = lax.axis_index("s") * SC_CORES + lax.axis_index("c")
        base = wid * per_worker

        def gather_copy(b):
            return pltpu.make_async_copy(table_hbm.at[idx_v[b]], rows_v[b], sems[b])

        def start_gather(j, b):
            off = pl.multiple_of(base + j * rows, 8)
            pltpu.sync_copy(idx_hbm.at[pl.ds(off, rows)], idx_v[b])
            gather_copy(b).start()

        for b in range(SC_RING):
            start_gather(b, b)

        def group(g, carry):
            for b in range(SC_RING):
                j = g * SC_RING + b
                gather_copy(b).wait()
                off = pl.multiple_of(base + j * rows, 8)
                pltpu.sync_copy(rows_v[b], out_hbm.at[pl.ds(off, rows)])

                @pl.when(j + SC_RING < n_chunk)
                def _():
                    start_gather(j + SC_RING, b)
            return carry

        lax.fori_loop(0, n_chunk // SC_RING, group, 0)

    return gather(table, idx)


def _route_kernel(x_ref, mod_ref, wr_ref, rb_ref, hp_ref, meta_ref, cnt_ref, wcol_ref, carry_scr):
    t = pl.program_id(0)
    T = ROUTE_TILE

    @pl.when(t == 0)
    def _():
        carry_scr[...] = jnp.zeros_like(carry_scr)

    shift = mod_ref[0, 3:4, :]
    scale = mod_ref[0, 4:5, :]
    h = _layer_norm(x_ref[0]) * (1.0 + scale) + shift
    h_hi = h.astype(jnp.bfloat16)
    h_lo = (h - h_hi.astype(jnp.float32)).astype(jnp.bfloat16)
    hp_ref[...] = h

    wr_hi, wr_lo = _split_bf16(wr_ref[...])
    logits_t = _nt_dot(wr_hi, h_hi) + (_nt_dot(wr_lo, h_hi) + _nt_dot(wr_hi, h_lo))
    i1, i2, w1, w2 = _router_rows(logits_t, rb_ref)

    e_iota = lax.broadcasted_iota(jnp.int32, (N_EXPERTS, T), 0).astype(jnp.float32)
    sel1 = e_iota == i1
    sel2 = e_iota == i2
    onehot = jnp.where(sel1 | sel2, 1.0, 0.0)
    tr = lax.broadcasted_iota(jnp.int32, (T, T), 0)
    tc = lax.broadcasted_iota(jnp.int32, (T, T), 1)
    earlier = jnp.where(tr < tc, 1.0, 0.0).astype(jnp.bfloat16)
    rank = jnp.dot(onehot.astype(jnp.bfloat16), earlier,
                   preferred_element_type=jnp.float32) + carry_scr[:, 0:1]
    rank1 = jnp.sum(jnp.where(sel1, rank, 0.0), axis=0, keepdims=True)
    rank2 = jnp.sum(jnp.where(sel2, rank, 0.0), axis=0, keepdims=True)
    for r, row in enumerate((i1, i2, rank1, rank2)):
        meta_ref[r, 0] = row
    carry_scr[...] = carry_scr[...] + jnp.sum(onehot, axis=1, keepdims=True)
    cnt_ref[...] = carry_scr[...]

    r128 = lax.broadcasted_iota(jnp.int32, (LANES, T), 0)
    terms = jnp.zeros((LANES, T), jnp.float32)
    k = 0
    for w in (w1, w2):
        rest = w
        for _ in range(3):
            part = rest.astype(jnp.bfloat16).astype(jnp.float32)
            terms = jnp.where(r128 == k, part, terms)
            rest = rest - part
            k += 1
    eye = jnp.where(tr == tc, 1.0, 0.0).astype(jnp.bfloat16)
    wcol_ref[...] = _nt_dot(eye, terms.astype(jnp.bfloat16))


def _route_call(x, modl, w_router_t, rbias_col):
    B, S, D = x.shape
    tm = ROUTE_TILE
    nt = S // tm
    n_tiles = B * nt
    return pl.pallas_call(
        _route_kernel,
        grid=(n_tiles,),
        in_specs=[
            pl.BlockSpec((1, tm, D), lambda t: (t // nt, t % nt, 0)),
            pl.BlockSpec((1, N_MOD, D), lambda t: (t // nt, 0, 0)),
            pl.BlockSpec((N_EXPERTS, D), lambda t: (0, 0)),
            pl.BlockSpec((N_EXPERTS, 1), lambda t: (0, 0)),
        ],
        out_specs=[
            pl.BlockSpec((tm, D), lambda t: (t, 0)),
            pl.BlockSpec((4, 1, 1, tm), lambda t: (0, t, 0, 0)),
            pl.BlockSpec((N_EXPERTS, LANES), lambda t: (0, 0)),
            pl.BlockSpec((tm, LANES), lambda t: (t, 0)),
        ],
        out_shape=[
            jax.ShapeDtypeStruct((B * S, D), jnp.float32),
            jax.ShapeDtypeStruct((4, n_tiles, 1, tm), jnp.float32),
            jax.ShapeDtypeStruct((N_EXPERTS, LANES), jnp.float32),
            jax.ShapeDtypeStruct((B * S, LANES), jnp.float32),
        ],
        scratch_shapes=[pltpu.VMEM((N_EXPERTS, LANES), jnp.float32)],
        compiler_params=_params(("arbitrary",)),
    )(x, modl, w_router_t, rbias_col)


def _dispatch_plan(meta, counts, n_tokens):
    n_rows = 2 * n_tokens + N_EXPERTS * EXPERT_CHUNK
    n_chunks = n_rows // EXPERT_CHUNK
    meta = meta.reshape(meta.shape[0], n_tokens).astype(jnp.int32)
    per_token = lambda r: meta[r]
    e1 = per_token(0)
    e2 = per_token(1)
    cnt = counts[:, 0].astype(jnp.int32)
    padded = (cnt + (EXPERT_CHUNK - 1)) // EXPERT_CHUNK * EXPERT_CHUNK
    ends = jnp.cumsum(padded)
    starts = ends - padded
    first = jnp.cumsum(cnt) - cnt
    pos1 = starts[e1] + per_token(2)
    pos2 = starts[e2] + per_token(3)
    order = jnp.argsort(jnp.concatenate([pos1, pos2])).astype(jnp.int32)
    sorted_tok = jnp.where(order >= n_tokens, order - n_tokens, order)
    chunk_row0 = jnp.arange(n_chunks, dtype=jnp.int32) * EXPERT_CHUNK
    chunk_expert = jnp.minimum(
        jnp.sum((ends[None, :] <= chunk_row0[:, None]).astype(jnp.int32), axis=1), N_EXPERTS - 1)
    local = (chunk_row0 - starts[chunk_expert])[:, None] + jnp.arange(EXPERT_CHUNK, dtype=jnp.int32)
    real = local < cnt[chunk_expert][:, None]
    nth = jnp.clip(first[chunk_expert][:, None] + local, 0, 2 * n_tokens - 1)
    spread = (chunk_row0[:, None] + jnp.arange(EXPERT_CHUNK, dtype=jnp.int32)) % n_tokens
    src = jnp.where(real, sorted_tok[nth], spread).reshape(n_rows)
    n_used = (ends[-1] // EXPERT_CHUNK).reshape(1).astype(jnp.int32)
    return pos1, pos2, src, chunk_expert, n_used


def _expert_kernel(ce_ref, nu_ref, xs_ref, wg_ref, wu_ref, wd_ref, o_ref, wg_bf, wu_bf, wd_bf):
    c = pl.program_id(0)

    @pl.when(c < nu_ref[0])
    def _():
        @pl.when((c == 0) | (ce_ref[c] != ce_ref[jnp.maximum(c - 1, 0)]))
        def _():
            wg_bf[...] = wg_ref[0, 0].astype(jnp.bfloat16)
            wu_bf[...] = wu_ref[0, 0].astype(jnp.bfloat16)
            wd_bf[...] = wd_ref[0, 0].astype(jnp.bfloat16)

        xs = xs_ref[...].astype(jnp.bfloat16)
        gt = jnp.dot(xs, wg_bf[...], preferred_element_type=jnp.float32)
        up = jnp.dot(xs, wu_bf[...], preferred_element_type=jnp.float32)
        act = (gt * (1.0 / (1.0 + jnp.exp(-gt))) * up).astype(jnp.bfloat16)
        o_ref[...] = jnp.dot(act, wd_bf[...], preferred_element_type=jnp.float32)

    @pl.when(c >= nu_ref[0])
    def _():
        o_ref[...] = jnp.zeros_like(o_ref)


def _expert_call(chunk_expert, n_used, xs, w_gate, w_up, w_down, layer):
    n_rows, D = xs.shape
    grid_spec = pltpu.PrefetchScalarGridSpec(
        num_scalar_prefetch=2,
        grid=(n_rows // EXPERT_CHUNK,),
        in_specs=[
            pl.BlockSpec((EXPERT_CHUNK, D), lambda c, ce, nu: (c, 0)),
            pl.BlockSpec((1, 1, D, D_EXPERT), lambda c, ce, nu: (layer, ce[c], 0, 0)),
            pl.BlockSpec((1, 1, D, D_EXPERT), lambda c, ce, nu: (layer, ce[c], 0, 0)),
            pl.BlockSpec((1, 1, D_EXPERT, D), lambda c, ce, nu: (layer, ce[c], 0, 0)),
        ],
        out_specs=pl.BlockSpec((EXPERT_CHUNK, D), lambda c, ce, nu: (c, 0)),
        scratch_shapes=[
            pltpu.VMEM((D, D_EXPERT), jnp.bfloat16),
            pltpu.VMEM((D, D_EXPERT), jnp.bfloat16),
            pltpu.VMEM((D_EXPERT, D), jnp.bfloat16),
        ],
    )
    return pl.pallas_call(
        _expert_kernel,
        grid_spec=grid_spec,
        out_shape=jax.ShapeDtypeStruct((n_rows, D), jnp.float32),
        compiler_params=_params(("arbitrary",)),
    )(chunk_expert, n_used, xs, w_gate, w_up, w_down)


def _combine_kernel(y1_ref, y2_ref, wcol_ref, x_ref, mod_ref, g_ref, b_ref, o_ref):
    wc = wcol_ref[0]
    w1 = (wc[:, 0:1] + wc[:, 1:2]) + wc[:, 2:3]
    w2 = (wc[:, 3:4] + wc[:, 4:5]) + wc[:, 5:6]
    y = w1 * y1_ref[0] + w2 * y2_ref[0]
    gate = mod_ref[0, 5:6, :]
    z = DEEPNORM_ALPHA * x_ref[0] + gate * y
    o_ref[0] = _layer_norm(z) * g_ref[...] + b_ref[...]


def _combine_call(y1, y2, wcol, x, modl, ln_g, ln_b):
    B, S, D = x.shape
    tm = 512
    rows = pl.BlockSpec((1, tm, D), lambda b, i: (b, i, 0))
    vec = pl.BlockSpec((1, D), lambda b, i: (0, 0))
    return pl.pallas_call(
        _combine_kernel,
        grid=(B, S // tm),
        in_specs=[rows, rows, pl.BlockSpec((1, tm, LANES), lambda b, i: (b, i, 0)), rows,
                  pl.BlockSpec((1, N_MOD, D), lambda b, i: (b, 0, 0)), vec, vec],
        out_specs=rows,
        out_shape=jax.ShapeDtypeStruct((B, S, D), jnp.float32),
        compiler_params=_params(("arbitrary", "arbitrary")),
    )(y1.reshape(B, S, D), y2.reshape(B, S, D), wcol.reshape(B, S, LANES),
      x, modl, ln_g.reshape(1, D), ln_b.reshape(1, D))


def _moe_routed(x, modl, w_router_t, rbias_col, w_gate, w_up, w_down, layer, ln_g, ln_b):
    B, S, D = x.shape
    h, meta, counts, wcol = _route_call(x, modl, w_router_t, rbias_col)
    pos1, pos2, src, chunk_expert, n_used = _dispatch_plan(meta, counts, B * S)
    xs = _sc_gather_rows(h, src)
    ys = _expert_call(chunk_expert, n_used, xs, w_gate, w_up, w_down, layer)
    y1 = _sc_gather_rows(ys, pos1)
    y2 = _sc_gather_rows(ys, pos2)
    return _combine_call(y1, y2, wcol, x, modl, ln_g, ln_b)


def _rope_tables(S):
    half = ROT_DIM // 2
    inv_freq = ROPE_THETA ** (-(jnp.arange(half, dtype=jnp.float32) * 2.0 / ROT_DIM))
    ang = jnp.arange(S, dtype=jnp.float32)[:, None] * inv_freq[None, :]
    cos, sin = jnp.cos(ang), jnp.sin(ang)
    d = jnp.arange(LANES) % HEAD_DIM
    f = d % half
    rot = d[None, :] < ROT_DIM
    return jnp.where(rot, cos[:, f], 1.0), jnp.where(rot, sin[:, f], 0.0)


def _with_rotary_partner_columns(w_in_l):
    half = ROT_DIM // 2
    qk = w_in_l[:, :2 * D_ATTN]
    d = (jnp.arange(2 * D_ATTN) % HEAD_DIM)[None, :]
    partner = jnp.where(d < half, -jnp.roll(qk, -half, axis=1),
                        jnp.where(d < ROT_DIM, jnp.roll(qk, half, axis=1), 0.0))
    return jnp.concatenate([w_in_l, partner], axis=1)


def _bias_placement():
    src = jnp.arange(LANES)
    hd, u, j = src >> 4, (src >> 3) & 1, src & 7
    col0 = jnp.where(hd % 2 == 0, HEAD_DIM, 0)
    dst = LANES * hd + col0 + j
    onehot = (jnp.arange(N_HEADS * LANES)[None, :] == dst[:, None]) & (u[:, None] == 1)
    return onehot.astype(jnp.bfloat16)


def kernel(x, c, w_mod, b_mod, w_in, w_pool, pool_scale, w_out, ln1_g, ln1_b,
           w_router, router_bias, w_gate, w_up, w_down, ln2_g, ln2_b):
    B, S, D = x.shape
    bf = jnp.bfloat16
    mod = _mod_call(c, w_mod, b_mod).reshape(DEPTH, B, N_MOD, D)
    cos_t, sin_t = _rope_tables(S)
    place = _bias_placement()
    w_router_t = w_router.T
    rbias_col = router_bias.reshape(N_EXPERTS, 1)
    for l in range(DEPTH):
        modl = mod[l]
        w_in_aug = _with_rotary_partner_columns(w_in[l]).astype(bf)
        qa, ka, v, m, _ = _proj_call(x, modl, w_in_aug, cos_t, sin_t, place,
                                     w_pool[l], pool_scale[l])
        a = _attn_call(qa, ka, v)
        x = _mixout_call(a, m, x, modl, w_out[l].astype(bf), ln1_g[l], ln1_b[l])
        x = _moe_routed(x, modl, w_router_t, rbias_col, w_gate, w_up, w_down, l,
                        ln2_g[l], ln2_b[l])
    return x
```

```python
import functools

import jax
import jax.numpy as jnp
from jax import lax
from jax.experimental import pallas as pl
from jax.experimental.pallas import tpu as pltpu
from jax.experimental.pallas import tpu_sc as plsc

D_MODEL = 1024
DEPTH = 2
D_ATTN = 512
D_POOL = 512
N_HEADS = 8
HEAD_DIM = 64
ROT_DIM = 16
ROPE_THETA = 500000.0
MOBA_BLOCK = 256
MOBA_TOPK = 3
POOL_GROUP = 128
N_POOL_GROUPS = 4
D_IN = 3 * D_ATTN + D_POOL
N_EXPERTS = 16
EXPERTS_PER_GROUP = 4
D_EXPERT = 512
DEEPNORM_ALPHA = (2 * DEPTH) ** 0.25
N_MOD = 6
LN_EPS = 1e-5
NEG_INF = -1e30

POOL_HALO = 16
LANES = 128
VMEM_LIMIT = 56 * 1024 * 1024

_HI = lax.Precision.HIGHEST
_NT = (((1,), (1,)), ((), ()))


def _params(sem):
    return pltpu.CompilerParams(dimension_semantics=sem, vmem_limit_bytes=VMEM_LIMIT)


def _nt_dot(a, b):
    return lax.dot_general(a, b, _NT, preferred_element_type=jnp.float32)


def _mod_kernel(c_ref, w_ref, b_ref, o_ref):
    c = c_ref[...]
    cond = c * (1.0 / (1.0 + jnp.exp(-c)))
    o_ref[0] = jnp.dot(cond, w_ref[0], precision=_HI,
                       preferred_element_type=jnp.float32) + b_ref[0]


def _mod_call(c, w_mod, b_mod):
    B = c.shape[0]
    return pl.pallas_call(
        _mod_kernel,
        grid=(DEPTH, N_MOD),
        in_specs=[
            pl.BlockSpec((B, D_MODEL), lambda l, j: (0, 0)),
            pl.BlockSpec((1, D_MODEL, D_MODEL), lambda l, j: (l, 0, j)),
            pl.BlockSpec((1, 1, D_MODEL), lambda l, j: (l, 0, j)),
        ],
        out_specs=pl.BlockSpec((1, B, D_MODEL), lambda l, j: (l, 0, j)),
        out_shape=jax.ShapeDtypeStruct((DEPTH, B, N_MOD * D_MODEL), jnp.float32),
        compiler_params=_params(("arbitrary", "arbitrary")),
    )(c, w_mod, b_mod.reshape(DEPTH, 1, N_MOD * D_MODEL))


def _layer_norm(x):
    mu = jnp.mean(x, axis=-1, keepdims=True)
    xc = x - mu
    var = jnp.mean(xc * xc, axis=-1, keepdims=True)
    return xc * lax.rsqrt(var + LN_EPS)


def _split_bf16(t):
    hi = t.astype(jnp.bfloat16)
    lo = (t - hi.astype(jnp.float32)).astype(jnp.bfloat16)
    return hi, lo


def _proj_kernel(x_ref, mod_ref, w_ref, cos_ref, sin_ref, place_ref, wp_ref, ps_ref,
                 qa_ref, ka_ref, v_ref, m_ref, kbar_ref, halo_scr):
    i = pl.program_id(1)

    @pl.when(i == 0)
    def _():
        kbar_ref[...] = jnp.zeros_like(kbar_ref)
        halo_scr[...] = jnp.zeros_like(halo_scr)

    x = x_ref[0]
    shift = mod_ref[0, 0:1, :]
    scale = mod_ref[0, 1:2, :]
    h = (_layer_norm(x) * (1.0 + scale) + shift).astype(jnp.bfloat16)

    cos = cos_ref[...]
    sin = sin_ref[...]

    def proj(c0, width):
        return jnp.dot(h, w_ref[:, c0:c0 + width], preferred_element_type=jnp.float32)

    def slab(t, s):
        return t[:, LANES * s:LANES * (s + 1)]

    n_slab = D_ATTN // LANES
    q = proj(0, D_ATTN)
    qp = proj(D_IN, D_ATTN)
    q_slabs = [(slab(q, s) * cos + slab(qp, s) * sin) * (HEAD_DIM ** -0.5)
               for s in range(n_slab)]
    k = proj(D_ATTN, D_ATTN)
    kp = proj(D_IN + D_ATTN, D_ATTN)
    k_slabs = [slab(k, s) * cos + slab(kp, s) * sin for s in range(n_slab)]
    v_ref[0] = proj(2 * D_ATTN, D_ATTN).astype(jnp.bfloat16)

    p = proj(3 * D_ATTN, D_POOL)
    ext = jnp.concatenate([halo_scr[...], p], axis=0)
    halo_scr[...] = p[MOBA_BLOCK - POOL_HALO:, :]
    t_pos = i * MOBA_BLOCK + lax.broadcasted_iota(jnp.int32, (MOBA_BLOCK, LANES), 0)
    m_slabs = []
    for g in range(N_POOL_GROUPS):
        win = slab(ext, g)
        for step in range(g + 1):
            win = win + pltpu.roll(win, 1 << step, 0)
        cnt = jnp.minimum(t_pos + 1, 2 << g).astype(jnp.float32)
        d = (win[POOL_HALO:, :] / cnt - slab(p, g)).astype(jnp.bfloat16)
        y = jnp.dot(d, wp_ref[g].astype(jnp.bfloat16), preferred_element_type=jnp.float32)
        m_slabs.append(y * slab(ps_ref[...], g))
    m_ref[0] = jnp.concatenate(m_slabs, axis=1).astype(m_ref.dtype)

    kmean = jnp.concatenate(
        [jnp.mean(ks, axis=0, keepdims=True) for ks in k_slabs], axis=1)
    kbar_ref[0, pl.ds(i, 1), :] = kmean

    kb = kbar_ref[0]
    kb_rows = jnp.concatenate([kb] * (LANES // 8), axis=0)
    r_head = lax.broadcasted_iota(jnp.int32, (LANES, D_ATTN), 1) >> 6
    c_head = lax.broadcasted_iota(jnp.int32, (LANES, D_ATTN), 0) >> 4
    kbd_hi, kbd_lo = _split_bf16(jnp.where(r_head == c_head, kb_rows, 0.0))
    q_hi, q_lo = _split_bf16(jnp.concatenate(q_slabs, axis=1))
    gate = _nt_dot(q_hi, kbd_hi) + (_nt_dot(q_lo, kbd_hi) + _nt_dot(q_hi, kbd_lo))

    lane = lax.broadcasted_iota(jnp.int32, (MOBA_BLOCK, LANES), 1)
    j_of = lane & 7
    past = j_of < i
    gm = jnp.where(past, gate, NEG_INF)
    rank = jnp.zeros((MOBA_BLOCK, LANES), jnp.int32)
    for r in range(1, 8):
        other = pltpu.roll(gm, r, 1)
        beats = (other > gm) | ((other == gm) & (j_of >= r))
        rank = rank + beats.astype(jnp.int32)
    allowed = (past & (rank < MOBA_TOPK)) | (j_of == i)
    bias = jnp.where(allowed, 0.0, NEG_INF).astype(jnp.bfloat16)
    bias_cols = jnp.dot(bias, place_ref[...], preferred_element_type=jnp.float32)

    for hd in range(N_HEADS):
        own = (lane < HEAD_DIM) if hd % 2 == 0 else (lane >= HEAD_DIM)
        col0 = HEAD_DIM if hd % 2 == 0 else 0
        qa = jnp.where(own, q_slabs[hd // 2], slab(bias_cols, hd))
        ka = jnp.where(own, k_slabs[hd // 2], jnp.where(lane == col0 + i, 1.0, 0.0))
        qa_ref[0, hd] = qa.astype(jnp.bfloat16)
        ka_ref[0, hd] = ka.astype(jnp.bfloat16)


def _proj_call(x, modl, w_in_aug, cos_t, sin_t, place, w_pool_l, pool_scale_l):
    B, S, D = x.shape
    nb = S // MOBA_BLOCK
    tm = MOBA_BLOCK
    tab = pl.BlockSpec((tm, LANES), lambda b, i: (i, 0))
    head_spec = pl.BlockSpec((1, N_HEADS, tm, LANES), lambda b, i: (b, 0, i, 0))
    return pl.pallas_call(
        _proj_kernel,
        grid=(B, nb),
        in_specs=[
            pl.BlockSpec((1, tm, D), lambda b, i: (b, i, 0)),
            pl.BlockSpec((1, N_MOD, D), lambda b, i: (b, 0, 0)),
            pl.BlockSpec((D, D_IN + 2 * D_ATTN), lambda b, i: (0, 0)),
            tab, tab,
            pl.BlockSpec((LANES, N_HEADS * LANES), lambda b, i: (0, 0)),
            pl.BlockSpec((N_POOL_GROUPS, POOL_GROUP, POOL_GROUP), lambda b, i: (0, 0, 0)),
            pl.BlockSpec((1, D_POOL), lambda b, i: (0, 0)),
        ],
        out_specs=[
            head_spec, head_spec,
            pl.BlockSpec((1, tm, D_ATTN), lambda b, i: (b, i, 0)),
            pl.BlockSpec((1, tm, D_POOL), lambda b, i: (b, i, 0)),
            pl.BlockSpec((1, nb, D_ATTN), lambda b, i: (b, 0, 0)),
        ],
        out_shape=[
            jax.ShapeDtypeStruct((B, N_HEADS, S, LANES), jnp.bfloat16),
            jax.ShapeDtypeStruct((B, N_HEADS, S, LANES), jnp.bfloat16),
            jax.ShapeDtypeStruct((B, S, D_ATTN), jnp.bfloat16),
            jax.ShapeDtypeStruct((B, S, D_POOL), jnp.bfloat16),
            jax.ShapeDtypeStruct((B, nb, D_ATTN), jnp.float32),
        ],
        scratch_shapes=[pltpu.VMEM((POOL_HALO, D_POOL), jnp.float32)],
        compiler_params=_params(("arbitrary", "arbitrary")),
    )(x, modl, w_in_aug, cos_t, sin_t, place, w_pool_l, pool_scale_l.reshape(1, D_POOL))


def _attn_kernel(qa_ref, ka_ref, v_ref, o_ref):
    nb = v_ref.shape[1] // MOBA_BLOCK
    row = lax.broadcasted_iota(jnp.int32, (MOBA_BLOCK, MOBA_BLOCK), 0)
    col = lax.broadcasted_iota(jnp.int32, (MOBA_BLOCK, MOBA_BLOCK), 1)
    causal = col <= row
    lane = lax.broadcasted_iota(jnp.int32, (MOBA_BLOCK, LANES), 1)

    for i in reversed(range(nb)):
        r0 = i * MOBA_BLOCK
        outs = []
        for hh in range(2):
            q = qa_ref[0, hh, r0:r0 + MOBA_BLOCK, :]
            s_own = jnp.where(causal, _nt_dot(q, ka_ref[0, hh, r0:r0 + MOBA_BLOCK, :]), NEG_INF)
            m = jnp.max(s_own, axis=1, keepdims=True)
            if i > 0:
                s_past = _nt_dot(q, ka_ref[0, hh, 0:r0, :])
                m = jnp.maximum(m, jnp.max(s_past, axis=1, keepdims=True))
            p_own = jnp.exp(s_own - m)
            l = jnp.sum(p_own, axis=1, keepdims=True)
            acc = jnp.dot(p_own.astype(jnp.bfloat16), v_ref[0, r0:r0 + MOBA_BLOCK, :],
                          preferred_element_type=jnp.float32)
            if i > 0:
                p_past = jnp.exp(s_past - m)
                l = l + jnp.sum(p_past, axis=1, keepdims=True)
                acc = acc + jnp.dot(p_past.astype(jnp.bfloat16), v_ref[0, 0:r0, :],
                                    preferred_element_type=jnp.float32)
            outs.append(acc / l)
        o = jnp.where(lane < HEAD_DIM, outs[0], outs[1])
        o_ref[0, r0:r0 + MOBA_BLOCK, :] = o.astype(o_ref.dtype)


def _attn_call(qa, ka, v):
    B, _, S, _ = qa.shape
    n_pair = N_HEADS // 2
    pair_spec = pl.BlockSpec((1, 2, S, LANES), lambda b, hp: (b, hp, 0, 0))
    slab_spec = pl.BlockSpec((1, S, LANES), lambda b, hp: (b, 0, hp))
    return pl.pallas_call(
        _attn_kernel,
        grid=(B, n_pair),
        in_specs=[pair_spec, pair_spec, slab_spec],
        out_specs=slab_spec,
        out_shape=jax.ShapeDtypeStruct((B, S, D_ATTN), jnp.bfloat16),
        compiler_params=_params(("arbitrary", "arbitrary")),
    )(qa, ka, v)


def _pool_kernel(p_ref, w_ref, sc_ref, o_ref):
    g = pl.program_id(1)
    S = p_ref.shape[1]
    p = p_ref[0]
    t = lax.broadcasted_iota(jnp.int32, (S, LANES), 0)

    def shifted(x, k):
        return jnp.where(t >= k, pltpu.roll(x, k, 0), 0.0)

    win = p
    sums = []
    for step in range(N_POOL_GROUPS):
        win = win + shifted(win, 1 << step)
        sums.append(win)
    wsum = jnp.where(g == 0, sums[0],
                     jnp.where(g == 1, sums[1], jnp.where(g == 2, sums[2], sums[3])))
    window = jnp.left_shift(2, g)
    cnt = jnp.minimum(t + 1, window).astype(jnp.float32)
    d = (wsum / cnt - p).astype(jnp.bfloat16)
    y = jnp.dot(d, w_ref[0].astype(jnp.bfloat16), preferred_element_type=jnp.float32)
    o_ref[0] = (y * sc_ref[...]).astype(o_ref.dtype)


def _pool_call(p, w_pool_l, pool_scale_l):
    B, S, _ = p.shape
    slab = pl.BlockSpec((1, S, LANES), lambda b, g: (b, 0, g))
    return pl.pallas_call(
        _pool_kernel,
        grid=(B, N_POOL_GROUPS),
        in_specs=[
            slab,
            pl.BlockSpec((1, POOL_GROUP, POOL_GROUP), lambda b, g: (g, 0, 0)),
            pl.BlockSpec((1, LANES), lambda b, g: (0, g)),
        ],
        out_specs=slab,
        out_shape=jax.ShapeDtypeStruct((B, S, D_POOL), jnp.bfloat16),
        compiler_params=_params(("arbitrary", "arbitrary")),
    )(p, w_pool_l, pool_scale_l.reshape(1, D_POOL))


def _mixout_kernel(a_ref, m_ref, x_ref, mod_ref, w_ref, g_ref, b_ref, o_ref):
    y = jnp.dot(a_ref[0], w_ref[0:D_ATTN, :], preferred_element_type=jnp.float32)
    y = y + jnp.dot(m_ref[0], w_ref[D_ATTN:, :], preferred_element_type=jnp.float32)
    gate = mod_ref[0, 2:3, :]
    z = DEEPNORM_ALPHA * x_ref[0] + gate * y
    o_ref[0] = _layer_norm(z) * g_ref[...] + b_ref[...]


def _mixout_call(a, m, x, modl, w_out_bf, ln_g, ln_b):
    B, S, D = x.shape
    tm = 512
    vec = pl.BlockSpec((1, D), lambda b, i: (0, 0))
    return pl.pallas_call(
        _mixout_kernel,
        grid=(B, S // tm),
        in_specs=[
            pl.BlockSpec((1, tm, D_ATTN), lambda b, i: (b, i, 0)),
            pl.BlockSpec((1, tm, D_POOL), lambda b, i: (b, i, 0)),
            pl.BlockSpec((1, tm, D), lambda b, i: (b, i, 0)),
            pl.BlockSpec((1, N_MOD, D), lambda b, i: (b, 0, 0)),
            pl.BlockSpec((D, D), lambda b, i: (0, 0)),
            vec, vec,
        ],
        out_specs=pl.BlockSpec((1, tm, D), lambda b, i: (b, i, 0)),
        out_shape=jax.ShapeDtypeStruct((B, S, D), jnp.float32),
        compiler_params=_params(("arbitrary", "arbitrary")),
    )(a, m, x, modl, w_out_bf, ln_g.reshape(1, D), ln_b.reshape(1, D))


MOE_TILE = 1024
MOE_CHUNK = 160
MOE_ROWS = 2304
MOE_ROWS_ALLOC = MOE_ROWS + MOE_CHUNK
MOE_G_COLS = 768
MOE_EXPERTS_PER_STEP = 2


def _top2_rows(vals):
    def first_max(rows):
        m = rows[0]
        for v in rows[1:]:
            m = jnp.maximum(m, v)
        idx = jnp.full_like(m, float(len(rows) - 1))
        for k in range(len(rows) - 2, -1, -1):
            idx = jnp.where(rows[k] == m, float(k), idx)
        return m, idx

    m1, i1 = first_max(vals)
    rest = [jnp.where(i1 == float(k), -jnp.inf, v) for k, v in enumerate(vals)]
    m2, i2 = first_max(rest)
    return m1, i1, m2, i2


def _router_rows(logits_t, rb_ref):
    lg = [logits_t[e:e + 1, :] for e in range(N_EXPERTS)]
    mx = lg[0]
    for v in lg[1:]:
        mx = jnp.maximum(mx, v)
    ex = [jnp.exp(v - mx) for v in lg]
    den = ex[0]
    for v in ex[1:]:
        den = den + v
    scores = [v / den for v in ex]
    sel = [scores[e] + rb_ref[e:e + 1, :] for e in range(N_EXPERTS)]
    best_score = None
    best = None
    for g in range(N_EXPERTS // EXPERTS_PER_GROUP):
        m1, _, m2, _ = _top2_rows(sel[g * EXPERTS_PER_GROUP:(g + 1) * EXPERTS_PER_GROUP])
        gs = m1 + m2
        if g == 0:
            best_score, best = gs, jnp.zeros_like(gs)
        else:
            better = gs > best_score
            best_score = jnp.where(better, gs, best_score)
            best = jnp.where(better, float(g), best)
    masked = [jnp.where(best == float(e // EXPERTS_PER_GROUP), sel[e], NEG_INF)
              for e in range(N_EXPERTS)]
    _, i1, _, i2 = _top2_rows(masked)
    w1 = jnp.zeros_like(i1)
    w2 = jnp.zeros_like(i2)
    for e in range(N_EXPERTS):
        w1 = jnp.where(i1 == float(e), scores[e], w1)
        w2 = jnp.where(i2 == float(e), scores[e], w2)
    tot = w1 + w2
    return i1, i2, w1 / tot, w2 / tot


def _moe_route(x_ref, mod_ref, wr_ref, rb_ref, h_scr, col_scr, row_scr, ysh_scr, ysl_scr,
               start_smem, cnt_smem):
    T = MOE_TILE
    shift = mod_ref[0, 3:4, :]
    scale = mod_ref[0, 4:5, :]
    h = _layer_norm(x_ref[0]) * (1.0 + scale) + shift
    h_hi = h.astype(jnp.bfloat16)
    h_scr[...] = h_hi
    h_lo = (h - h_hi.astype(jnp.float32)).astype(jnp.bfloat16)
    wr_hi, wr_lo = _split_bf16(wr_ref[...])
    logits_t = _nt_dot(wr_hi, h_hi) + (_nt_dot(wr_lo, h_hi) + _nt_dot(wr_hi, h_lo))
    i1, i2, w1, w2 = _router_rows(logits_t, rb_ref)

    e_iota = lax.broadcasted_iota(jnp.int32, (N_EXPERTS, T), 0).astype(jnp.float32)
    sel1 = e_iota == i1
    sel2 = e_iota == i2
    onehot = jnp.where(sel1 | sel2, 1.0, 0.0)
    tr = lax.broadcasted_iota(jnp.int32, (T, T), 0)
    tc = lax.broadcasted_iota(jnp.int32, (T, T), 1)
    earlier = jnp.where(tr < tc, 1.0, 0.0).astype(jnp.bfloat16)
    excl = jnp.dot(onehot.astype(jnp.bfloat16), earlier,
                   preferred_element_type=jnp.float32)
    cnt = jnp.sum(onehot, axis=1, keepdims=True)

    e_col = lax.broadcasted_iota(jnp.int32, (N_EXPERTS, 1), 0)
    start_v = jnp.zeros((N_EXPERTS, 1), jnp.float32)
    run = jnp.int32(0)
    for ex in range(N_EXPERTS):
        padded = lax.shift_left(lax.shift_right_logical(cnt[ex, 0].astype(jnp.int32) + 15, 4), 4)
        start_smem[ex] = run
        cnt_smem[ex] = padded
        start_v = jnp.where(e_col == ex, run.astype(jnp.float32), start_v)
        run = run + padded

    slot = excl + start_v
    pos1 = jnp.sum(jnp.where(sel1, slot, 0.0), axis=0, keepdims=True)
    pos2 = jnp.sum(jnp.where(sel2, slot, 0.0), axis=0, keepdims=True)
    row_scr[0:1, :] = pos1
    row_scr[1:2, :] = pos2
    row_scr[2:3, :] = w1
    row_scr[3:4, :] = w2

    r128 = lax.broadcasted_iota(jnp.int32, (LANES, T), 0)
    terms = jnp.zeros((LANES, T), jnp.float32)
    k = 0
    for p in (pos1, pos2):
        a = jnp.floor(p * (1.0 / 64.0))
        for t in (a, p - 64.0 * a):
            terms = jnp.where(r128 == k, t, terms)
            k += 1
    eye = jnp.where(tr == tc, 1.0, 0.0).astype(jnp.bfloat16)
    col_scr[...] = _nt_dot(eye, terms.astype(jnp.bfloat16))

    tail = MOE_ROWS_ALLOC - 2 * T
    ysh_scr[2 * T:, :] = jnp.zeros((tail, D_MODEL), jnp.bfloat16)
    ysl_scr[2 * T:, :] = jnp.zeros((tail, D_MODEL), jnp.bfloat16)


def _moe_kernel(x_ref, mod_ref, wr_ref, rb_ref, wg_ref, wu_ref, wd_ref, g_ref, b_ref,
                o_ref, h_scr, col_scr, row_scr, ysh_scr, ysl_scr, start_smem, cnt_smem):
    step = pl.program_id(1)
    T = MOE_TILE

    @pl.when(step == 0)
    def _():
        _moe_route(x_ref, mod_ref, wr_ref, rb_ref, h_scr, col_scr, row_scr, ysh_scr, ysl_scr,
                   start_smem, cnt_smem)

    pos1_row = row_scr[0:1, :]
    pos2_row = row_scr[1:2, :]
    w1_row = row_scr[2:3, :]
    w2_row = row_scr[3:4, :]

    for k in range(MOE_EXPERTS_PER_STEP):
        e = step * MOE_EXPERTS_PER_STEP + k
        seg_start = start_smem[e]
        n_chunk = (cnt_smem[e] + (MOE_CHUNK - 1)) // MOE_CHUNK

        def chunk(j, carry, k=k, seg_start=seg_start):
            base = pl.multiple_of(seg_start + j * MOE_CHUNK, 16)
            r = (base + lax.broadcasted_iota(jnp.int32, (MOE_CHUNK, T), 0)).astype(jnp.float32)
            m1 = r == pos1_row
            m2 = r == pos2_row
            gather = jnp.where(m1 | m2, 1.0, 0.0).astype(jnp.bfloat16)
            w_row = jnp.sum(jnp.where(m1, w1_row, 0.0) + jnp.where(m2, w2_row, 0.0),
                            axis=1, keepdims=True)
            xs = jnp.dot(gather, h_scr[...],
                         preferred_element_type=jnp.float32).astype(jnp.bfloat16)
            gt = jnp.dot(xs, wg_ref[k], preferred_element_type=jnp.float32)
            up = jnp.dot(xs, wu_ref[k], preferred_element_type=jnp.float32)
            act = (gt * (1.0 / (1.0 + jnp.exp(-gt))) * up).astype(jnp.bfloat16)
            y = w_row * jnp.dot(act, wd_ref[k], preferred_element_type=jnp.float32)
            y_hi = y.astype(jnp.bfloat16)
            ysh_scr[pl.ds(base, MOE_CHUNK), :] = y_hi
            ysl_scr[pl.ds(base, MOE_CHUNK), :] = (y - y_hi.astype(jnp.float32)).astype(jnp.bfloat16)
            return carry

        lax.fori_loop(0, n_chunk, chunk, 0)

    @pl.when(step == N_EXPERTS // MOE_EXPERTS_PER_STEP - 1)
    def _():
        pos1 = 64.0 * col_scr[:, 0:1] + col_scr[:, 1:2]
        pos2 = 64.0 * col_scr[:, 2:3] + col_scr[:, 3:4]
        y = jnp.zeros((T, D_MODEL), jnp.float32)
        for c0 in range(0, MOE_ROWS, MOE_G_COLS):
            r = (c0 + lax.broadcasted_iota(jnp.int32, (T, MOE_G_COLS), 1)).astype(jnp.float32)
            scatter = jnp.where((r == pos1) | (r == pos2), 1.0, 0.0).astype(jnp.bfloat16)
            y = y + jnp.dot(scatter, ysh_scr[c0:c0 + MOE_G_COLS, :],
                            preferred_element_type=jnp.float32)
            y = y + jnp.dot(scatter, ysl_scr[c0:c0 + MOE_G_COLS, :],
                            preferred_element_type=jnp.float32)
        gate = mod_ref[0, 5:6, :]
        z = DEEPNORM_ALPHA * x_ref[0] + gate * y
        o_ref[0] = _layer_norm(z) * g_ref[...] + b_ref[...]


def _moe_call(x, modl, w_router_t, rbias_col, wg_bf, wu_bf, wd_bf, ln_g, ln_b):
    B, S, D = x.shape
    tm = MOE_TILE
    nt = S // tm
    eps = MOE_EXPERTS_PER_STEP
    vec = pl.BlockSpec((1, D), lambda t, e: (0, 0))
    xspec = pl.BlockSpec((1, tm, D), lambda t, e: (t // nt, t % nt, 0))
    return pl.pallas_call(
        _moe_kernel,
        grid=(B * nt, N_EXPERTS // eps),
        in_specs=[
            xspec,
            pl.BlockSpec((1, N_MOD, D), lambda t, e: (t // nt, 0, 0)),
            pl.BlockSpec((N_EXPERTS, D), lambda t, e: (0, 0)),
            pl.BlockSpec((N_EXPERTS, 1), lambda t, e: (0, 0)),
            pl.BlockSpec((eps, D, D_EXPERT), lambda t, e: (e, 0, 0)),
            pl.BlockSpec((eps, D, D_EXPERT), lambda t, e: (e, 0, 0)),
            pl.BlockSpec((eps, D_EXPERT, D), lambda t, e: (e, 0, 0)),
            vec, vec,
        ],
        out_specs=xspec,
        out_shape=jax.ShapeDtypeStruct((B, S, D), jnp.float32),
        scratch_shapes=[
            pltpu.VMEM((tm, D), jnp.bfloat16),
            pltpu.VMEM((tm, LANES), jnp.float32),
            pltpu.VMEM((8, tm), jnp.float32),
            pltpu.VMEM((MOE_ROWS_ALLOC, D), jnp.bfloat16),
            pltpu.VMEM((MOE_ROWS_ALLOC, D), jnp.bfloat16),
            pltpu.SMEM((N_EXPERTS,), jnp.int32),
            pltpu.SMEM((N_EXPERTS,), jnp.int32),
        ],
        compiler_params=_params(("arbitrary", "arbitrary")),
    )(x, modl, w_router_t, rbias_col, wg_bf, wu_bf, wd_bf,
      ln_g.reshape(1, D), ln_b.reshape(1, D))


SC_CORES = 2
SC_SUBCORES = 16
SC_MAX_INDEX_ROWS = 128
SC_RING = 4
SC_TILE_BYTES = 64 * 1024

BATCH_CHAINS = 2
ROUTE_TILE = 1024
EXPERT_CHUNK = 512


def _sc_gather_rows(table, idx):
    n_rows = idx.shape[0]
    width = table.shape[1]
    workers = SC_CORES * SC_SUBCORES
    rows = min(SC_MAX_INDEX_ROWS, SC_TILE_BYTES // (4 * width))
    per_worker = n_rows // workers
    assert per_worker * workers == n_rows and per_worker % rows == 0
    n_chunk = per_worker // rows
    assert n_chunk % SC_RING == 0
    mesh = plsc.VectorSubcoreMesh(core_axis_name="c", subcore_axis_name="s")
    row_buf = pltpu.VMEM((rows, width), table.dtype)
    idx_buf = pltpu.VMEM((rows,), jnp.int32)

    @functools.partial(
        pl.kernel, mesh=mesh,
        out_type=jax.ShapeDtypeStruct((n_rows, width), table.dtype),
        scratch_types=([idx_buf] * SC_RING + [row_buf] * SC_RING
                       + [pltpu.SemaphoreType.DMA] * SC_RING),
    )
    def gather(table_hbm, idx_hbm, out_hbm, *scratch):
        idx_v = scratch[:SC_RING]
        rows_v = scratch[SC_RING:2 * SC_RING]
        sems = scratch[2 * SC_RING:]
        wid = lax.axis_index("s") * SC_CORES + lax.axis_index("c")
        base = wid * per_worker

        def gather_copy(b):
            return pltpu.make_async_copy(table_hbm.at[idx_v[b]], rows_v[b], sems[b])

        def start_gather(j, b):
            off = pl.multiple_of(base + j * rows, 8)
            pltpu.sync_copy(idx_hbm.at[pl.ds(off, rows)], idx_v[b])
            gather_copy(b).start()

        for b in range(SC_RING):
            start_gather(b, b)

        def group(g, carry):
            for b in range(SC_RING):
                j = g * SC_RING + b
                gather_copy(b).wait()
                off = pl.multiple_of(base + j * rows, 8)
                pltpu.sync_copy(rows_v[b], out_hbm.at[pl.ds(off, rows)])

                @pl.when(j + SC_RING < n_chunk)
                def _():
                    start_gather(j + SC_RING, b)
            return carry

        lax.fori_loop(0, n_chunk // SC_RING, group, 0)

    return gather(table, idx)


def _route_kernel(x_ref, mod_ref, wr_ref, rb_ref, hp_ref, meta_ref, cnt_ref, wcol_ref, carry_scr):
    t = pl.program_id(0)
    T = ROUTE_TILE

    @pl.when(t == 0)
    def _():
        carry_scr[...] = jnp.zeros_like(carry_scr)

    shift = mod_ref[0, 3:4, :]
    scale = mod_ref[0, 4:5, :]
    h = _layer_norm(x_ref[0]) * (1.0 + scale) + shift
    h_hi = h.astype(jnp.bfloat16)
    h_lo = (h - h_hi.astype(jnp.float32)).astype(jnp.bfloat16)
    hp_ref[...] = h

    wr_hi, wr_lo = _split_bf16(wr_ref[...])
    logits_t = _nt_dot(wr_hi, h_hi) + (_nt_dot(wr_lo, h_hi) + _nt_dot(wr_hi, h_lo))
    i1, i2, w1, w2 = _router_rows(logits_t, rb_ref)

    e_iota = lax.broadcasted_iota(jnp.int32, (N_EXPERTS, T), 0).astype(jnp.float32)
    sel1 = e_iota == i1
    sel2 = e_iota == i2
    onehot = jnp.where(sel1 | sel2, 1.0, 0.0)
    tr = lax.broadcasted_iota(jnp.int32, (T, T), 0)
    tc = lax.broadcasted_iota(jnp.int32, (T, T), 1)
    earlier = jnp.where(tr < tc, 1.0, 0.0).astype(jnp.bfloat16)
    rank = jnp.dot(onehot.astype(jnp.bfloat16), earlier,
                   preferred_element_type=jnp.float32) + carry_scr[:, 0:1]
    rank1 = jnp.sum(jnp.where(sel1, rank, 0.0), axis=0, keepdims=True)
    rank2 = jnp.sum(jnp.where(sel2, rank, 0.0), axis=0, keepdims=True)
    for r, row in enumerate((i1, i2, rank1, rank2)):
        meta_ref[r, 0] = row
    carry_scr[...] = carry_scr[...] + jnp.sum(onehot, axis=1, keepdims=True)
    cnt_ref[...] = carry_scr[...]

    r128 = lax.broadcasted_iota(jnp.int32, (LANES, T), 0)
    terms = jnp.zeros((LANES, T), jnp.float32)
    k = 0
    for w in (w1, w2):
        rest = w
        for _ in range(3):
            part = rest.astype(jnp.bfloat16).astype(jnp.float32)
            terms = jnp.where(r128 == k, part, terms)
            rest = rest - part
            k += 1
    eye = jnp.where(tr == tc, 1.0, 0.0).astype(jnp.bfloat16)
    wcol_ref[...] = _nt_dot(eye, terms.astype(jnp.bfloat16))


def _route_call(x, modl, w_router_t, rbias_col):
    B, S, D = x.shape
    tm = ROUTE_TILE
    nt = S // tm
    n_tiles = B * nt
    return pl.pallas_call(
        _route_kernel,
        grid=(n_tiles,),
        in_specs=[
            pl.BlockSpec((1, tm, D), lambda t: (t // nt, t % nt, 0)),
            pl.BlockSpec((1, N_MOD, D), lambda t: (t // nt, 0, 0)),
            pl.BlockSpec((N_EXPERTS, D), lambda t: (0, 0)),
            pl.BlockSpec((N_EXPERTS, 1), lambda t: (0, 0)),
        ],
        out_specs=[
            pl.BlockSpec((tm, D), lambda t: (t, 0)),
            pl.BlockSpec((4, 1, 1, tm), lambda t: (0, t, 0, 0)),
            pl.BlockSpec((N_EXPERTS, LANES), lambda t: (0, 0)),
            pl.BlockSpec((tm, LANES), lambda t: (t, 0)),
        ],
        out_shape=[
            jax.ShapeDtypeStruct((B * S, D), jnp.float32),
            jax.ShapeDtypeStruct((4, n_tiles, 1, tm), jnp.float32),
            jax.ShapeDtypeStruct((N_EXPERTS, LANES), jnp.float32),
            jax.ShapeDtypeStruct((B * S, LANES), jnp.float32),
        ],
        scratch_shapes=[pltpu.VMEM((N_EXPERTS, LANES), jnp.float32)],
        compiler_params=_params(("arbitrary",)),
    )(x, modl, w_router_t, rbias_col)


def _dispatch_plan(meta, counts, n_tokens):
    n_rows = 2 * n_tokens + N_EXPERTS * EXPERT_CHUNK
    n_chunks = n_rows // EXPERT_CHUNK
    meta = meta.reshape(meta.shape[0], n_tokens).astype(jnp.int32)
    per_token = lambda r: meta[r]
    e1 = per_token(0)
    e2 = per_token(1)
    cnt = counts[:, 0].astype(jnp.int32)
    padded = (cnt + (EXPERT_CHUNK - 1)) // EXPERT_CHUNK * EXPERT_CHUNK
    ends = jnp.cumsum(padded)
    starts = ends - padded
    first = jnp.cumsum(cnt) - cnt
    pos1 = starts[e1] + per_token(2)
    pos2 = starts[e2] + per_token(3)
    order = jnp.argsort(jnp.concatenate([pos1, pos2])).astype(jnp.int32)
    sorted_tok = jnp.where(order >= n_tokens, order - n_tokens, order)
    chunk_row0 = jnp.arange(n_chunks, dtype=jnp.int32) * EXPERT_CHUNK
    chunk_expert = jnp.minimum(
        jnp.sum((ends[None, :] <= chunk_row0[:, None]).astype(jnp.int32), axis=1), N_EXPERTS - 1)
    local = (chunk_row0 - starts[chunk_expert])[:, None] + jnp.arange(EXPERT_CHUNK, dtype=jnp.int32)
    real = local < cnt[chunk_expert][:, None]
    nth = jnp.clip(first[chunk_expert][:, None] + local, 0, 2 * n_tokens - 1)
    spread = (chunk_row0[:, None] + jnp.arange(EXPERT_CHUNK, dtype=jnp.int32)) % n_tokens
    src = jnp.where(real, sorted_tok[nth], spread).reshape(n_rows)
    n_used = (ends[-1] // EXPERT_CHUNK).reshape(1).astype(jnp.int32)
    return pos1, pos2, src, chunk_expert, n_used


def _expert_kernel(ce_ref, nu_ref, xs_ref, wg_ref, wu_ref, wd_ref, o_ref, wg_bf, wu_bf, wd_bf):
    c = pl.program_id(0)

    @pl.when(c < nu_ref[0])
    def _():
        @pl.when((c == 0) | (ce_ref[c] != ce_ref[jnp.maximum(c - 1, 0)]))
        def _():
            wg_bf[...] = wg_ref[0, 0].astype(jnp.bfloat16)
            wu_bf[...] = wu_ref[0, 0].astype(jnp.bfloat16)
            wd_bf[...] = wd_ref[0, 0].astype(jnp.bfloat16)

        xs = xs_ref[...].astype(jnp.bfloat16)
        gt = jnp.dot(xs, wg_bf[...], preferred_element_type=jnp.float32)
        up = jnp.dot(xs, wu_bf[...], preferred_element_type=jnp.float32)
        act = (gt * (1.0 / (1.0 + jnp.exp(-gt))) * up).astype(jnp.bfloat16)
        o_ref[...] = jnp.dot(act, wd_bf[...], preferred_element_type=jnp.float32)

    @pl.when(c >= nu_ref[0])
    def _():
        o_ref[...] = jnp.zeros_like(o_ref)


def _expert_call(chunk_expert, n_used, xs, w_gate, w_up, w_down, layer):
    n_rows, D = xs.shape
    grid_spec = pltpu.PrefetchScalarGridSpec(
        num_scalar_prefetch=2,
        grid=(n_rows // EXPERT_CHUNK,),
        in_specs=[
            pl.BlockSpec((EXPERT_CHUNK, D), lambda c, ce, nu: (c, 0)),
            pl.BlockSpec((1, 1, D, D_EXPERT), lambda c, ce, nu: (layer, ce[c], 0, 0)),
            pl.BlockSpec((1, 1, D, D_EXPERT), lambda c, ce, nu: (layer, ce[c], 0, 0)),
            pl.BlockSpec((1, 1, D_EXPERT, D), lambda c, ce, nu: (layer, ce[c], 0, 0)),
        ],
        out_specs=pl.BlockSpec((EXPERT_CHUNK, D), lambda c, ce, nu: (c, 0)),
        scratch_shapes=[
            pltpu.VMEM((D, D_EXPERT), jnp.bfloat16),
            pltpu.VMEM((D, D_EXPERT), jnp.bfloat16),
            pltpu.VMEM((D_EXPERT, D), jnp.bfloat16),
        ],
    )
    return pl.pallas_call(
        _expert_kernel,
        grid_spec=grid_spec,
        out_shape=jax.ShapeDtypeStruct((n_rows, D), jnp.float32),
        compiler_params=_params(("arbitrary",)),
    )(chunk_expert, n_used, xs, w_gate, w_up, w_down)


def _combine_kernel(y1_ref, y2_ref, wcol_ref, x_ref, mod_ref, g_ref, b_ref, o_ref):
    wc = wcol_ref[0]
    w1 = (wc[:, 0:1] + wc[:, 1:2]) + wc[:, 2:3]
    w2 = (wc[:, 3:4] + wc[:, 4:5]) + wc[:, 5:6]
    y = w1 * y1_ref[0] + w2 * y2_ref[0]
    gate = mod_ref[0, 5:6, :]
    z = DEEPNORM_ALPHA * x_ref[0] + gate * y
    o_ref[0] = _layer_norm(z) * g_ref[...] + b_ref[...]


def _combine_call(y1, y2, wcol, x, modl, ln_g, ln_b):
    B, S, D = x.shape
    tm = 512
    rows = pl.BlockSpec((1, tm, D), lambda b, i: (b, i, 0))
    vec = pl.BlockSpec((1, D), lambda b, i: (0, 0))
    return pl.pallas_call(
        _combine_kernel,
        grid=(B, S // tm),
        in_specs=[rows, rows, pl.BlockSpec((1, tm, LANES), lambda b, i: (b, i, 0)), rows,
                  pl.BlockSpec((1, N_MOD, D), lambda b, i: (b, 0, 0)), vec, vec],
        out_specs=rows,
        out_shape=jax.ShapeDtypeStruct((B, S, D), jnp.float32),
        compiler_params=_params(("arbitrary", "arbitrary")),
    )(y1.reshape(B, S, D), y2.reshape(B, S, D), wcol.reshape(B, S, LANES),
      x, modl, ln_g.reshape(1, D), ln_b.reshape(1, D))


def _moe_routed(x, modl, w_router_t, rbias_col, w_gate, w_up, w_down, layer, ln_g, ln_b):
    B, S, D = x.shape
    h, meta, counts, wcol = _route_call(x, modl, w_router_t, rbias_col)
    pos1, pos2, src, chunk_expert, n_used = _dispatch_plan(meta, counts, B * S)
    xs = _sc_gather_rows(h, src)
    ys = _expert_call(chunk_expert, n_used, xs, w_gate, w_up, w_down, layer)
    y1 = _sc_gather_rows(ys, pos1)
    y2 = _sc_gather_rows(ys, pos2)
    return _combine_call(y1, y2, wcol, x, modl, ln_g, ln_b)


def _rope_tables(S):
    half = ROT_DIM // 2
    inv_freq = ROPE_THETA ** (-(jnp.arange(half, dtype=jnp.float32) * 2.0 / ROT_DIM))
    ang = jnp.arange(S, dtype=jnp.float32)[:, None] * inv_freq[None, :]
    cos, sin = jnp.cos(ang), jnp.sin(ang)
    d = jnp.arange(LANES) % HEAD_DIM
    f = d % half
    rot = d[None, :] < ROT_DIM
    return jnp.where(rot, cos[:, f], 1.0), jnp.where(rot, sin[:, f], 0.0)


def _with_rotary_partner_columns(w_in_l):
    half = ROT_DIM // 2
    qk = w_in_l[:, :2 * D_ATTN]
    d = (jnp.arange(2 * D_ATTN) % HEAD_DIM)[None, :]
    partner = jnp.where(d < half, -jnp.roll(qk, -half, axis=1),
                        jnp.where(d < ROT_DIM, jnp.roll(qk, half, axis=1), 0.0))
    return jnp.concatenate([w_in_l, partner], axis=1)


def _bias_placement():
    src = jnp.arange(LANES)
    hd, u, j = src >> 4, (src >> 3) & 1, src & 7
    col0 = jnp.where(hd % 2 == 0, HEAD_DIM, 0)
    dst = LANES * hd + col0 + j
    onehot = (jnp.arange(N_HEADS * LANES)[None, :] == dst[:, None]) & (u[:, None] == 1)
    return onehot.astype(jnp.bfloat16)


def kernel(x, c, w_mod, b_mod, w_in, w_pool, pool_scale, w_out, ln1_g, ln1_b,
           w_router, router_bias, w_gate, w_up, w_down, ln2_g, ln2_b):
    B, S, D = x.shape
    bf = jnp.bfloat16
    mod = _mod_call(c, w_mod, b_mod).reshape(DEPTH, B, N_MOD, D)
    cos_t, sin_t = _rope_tables(S)
    place = _bias_placement()
    w_router_t = w_router.T
    rbias_col = router_bias.reshape(N_EXPERTS, 1)
    w_in_aug = [_with_rotary_partner_columns(w_in[l]).astype(bf) for l in range(DEPTH)]
    w_out_bf = [w_out[l].astype(bf) for l in range(DEPTH)]
    rows_per_chain = B // BATCH_CHAINS
    outs = []
    for ch in range(BATCH_CHAINS):
        rows = slice(ch * rows_per_chain, (ch + 1) * rows_per_chain)
        xc = x[rows]
        for l in range(DEPTH):
            modl = mod[l, rows]
            qa, ka, v, m, _ = _proj_call(xc, modl, w_in_aug[l], cos_t, sin_t, place,
                                         w_pool[l], pool_scale[l])
            a = _attn_call(qa, ka, v)
            xc = _mixout_call(a, m, xc, modl, w_out_bf[l], ln1_g[l], ln1_b[l])
            xc = _moe_routed(xc, modl, w_router_t, rbias_col, w_gate, w_up, w_down, l,
                             ln2_g[l], ln2_b[l])
        outs.append(xc)
    return jnp.concatenate(outs, axis=0)
```

```python
import functools

import jax
import jax.numpy as jnp
from jax import lax
from jax.experimental import pallas as pl
from jax.experimental.pallas import tpu as pltpu
from jax.experimental.pallas import tpu_sc as plsc

D_MODEL = 1024
DEPTH = 2
D_ATTN = 512
D_POOL = 512
N_HEADS = 8
HEAD_DIM = 64
ROT_DIM = 16
ROPE_THETA = 500000.0
MOBA_BLOCK = 256
MOBA_TOPK = 3
POOL_GROUP = 128
N_POOL_GROUPS = 4
D_IN = 3 * D_ATTN + D_POOL
N_EXPERTS = 16
EXPERTS_PER_GROUP = 4
D_EXPERT = 512
DEEPNORM_ALPHA = (2 * DEPTH) ** 0.25
N_MOD = 6
LN_EPS = 1e-5
NEG_INF = -1e30

POOL_HALO = 16
LANES = 128
VMEM_LIMIT = 56 * 1024 * 1024

_HI = lax.Precision.HIGHEST
_NT = (((1,), (1,)), ((), ()))


def _params(sem):
    return pltpu.CompilerParams(dimension_semantics=sem, vmem_limit_bytes=VMEM_LIMIT)


def _nt_dot(a, b):
    return lax.dot_general(a, b, _NT, preferred_element_type=jnp.float32)


def _mod_kernel(c_ref, w_ref, b_ref, o_ref):
    c = c_ref[...]
    cond = c * (1.0 / (1.0 + jnp.exp(-c)))
    o_ref[0] = jnp.dot(cond, w_ref[0], precision=_HI,
                       preferred_element_type=jnp.float32) + b_ref[0]


def _mod_call(c, w_mod, b_mod):
    B = c.shape[0]
    return pl.pallas_call(
        _mod_kernel,
        grid=(DEPTH, N_MOD),
        in_specs=[
            pl.BlockSpec((B, D_MODEL), lambda l, j: (0, 0)),
            pl.BlockSpec((1, D_MODEL, D_MODEL), lambda l, j: (l, 0, j)),
            pl.BlockSpec((1, 1, D_MODEL), lambda l, j: (l, 0, j)),
        ],
        out_specs=pl.BlockSpec((1, B, D_MODEL), lambda l, j: (l, 0, j)),
        out_shape=jax.ShapeDtypeStruct((DEPTH, B, N_MOD * D_MODEL), jnp.float32),
        compiler_params=_params(("arbitrary", "arbitrary")),
    )(c, w_mod, b_mod.reshape(DEPTH, 1, N_MOD * D_MODEL))


def _layer_norm(x):
    mu = jnp.mean(x, axis=-1, keepdims=True)
    xc = x - mu
    var = jnp.mean(xc * xc, axis=-1, keepdims=True)
    return xc * lax.rsqrt(var + LN_EPS)


def _split_bf16(t):
    hi = t.astype(jnp.bfloat16)
    lo = (t - hi.astype(jnp.float32)).astype(jnp.bfloat16)
    return hi, lo


def _proj_kernel(x_ref, mod_ref, w_ref, cos_ref, sin_ref, place_ref, wp_ref, ps_ref,
                 qa_ref, ka_ref, v_ref, m_ref, kbar_ref, halo_scr):
    i = pl.program_id(1)

    @pl.when(i == 0)
    def _():
        kbar_ref[...] = jnp.zeros_like(kbar_ref)
        halo_scr[...] = jnp.zeros_like(halo_scr)

    x = x_ref[0]
    shift = mod_ref[0, 0:1, :]
    scale = mod_ref[0, 1:2, :]
    h = (_layer_norm(x) * (1.0 + scale) + shift).astype(jnp.bfloat16)

    cos = cos_ref[...]
    sin = sin_ref[...]

    def proj(c0, width):
        return jnp.dot(h, w_ref[:, c0:c0 + width], preferred_element_type=jnp.float32)

    def slab(t, s):
        return t[:, LANES * s:LANES * (s + 1)]

    n_slab = D_ATTN // LANES
    q = proj(0, D_ATTN)
    qp = proj(D_IN, D_ATTN)
    q_slabs = [(slab(q, s) * cos + slab(qp, s) * sin) * (HEAD_DIM ** -0.5)
               for s in range(n_slab)]
    k = proj(D_ATTN, D_ATTN)
    kp = proj(D_IN + D_ATTN, D_ATTN)
    k_slabs = [slab(k, s) * cos + slab(kp, s) * sin for s in range(n_slab)]
    v_ref[0] = proj(2 * D_ATTN, D_ATTN).astype(jnp.bfloat16)

    p = proj(3 * D_ATTN, D_POOL)
    ext = jnp.concatenate([halo_scr[...], p], axis=0)
    halo_scr[...] = p[MOBA_BLOCK - POOL_HALO:, :]
    t_pos = i * MOBA_BLOCK + lax.broadcasted_iota(jnp.int32, (MOBA_BLOCK, LANES), 0)
    m_slabs = []
    for g in range(N_POOL_GROUPS):
        win = slab(ext, g)
        for step in range(g + 1):
            win = win + pltpu.roll(win, 1 << step, 0)
        cnt = jnp.minimum(t_pos + 1, 2 << g).astype(jnp.float32)
        d = (win[POOL_HALO:, :] / cnt - slab(p, g)).astype(jnp.bfloat16)
        y = jnp.dot(d, wp_ref[g].astype(jnp.bfloat16), preferred_element_type=jnp.float32)
        m_slabs.append(y * slab(ps_ref[...], g))
    m_ref[0] = jnp.concatenate(m_slabs, axis=1).astype(m_ref.dtype)

    kmean = jnp.concatenate(
        [jnp.mean(ks, axis=0, keepdims=True) for ks in k_slabs], axis=1)
    kbar_ref[0, pl.ds(i, 1), :] = kmean

    kb = kbar_ref[0]
    kb_rows = jnp.concatenate([kb] * (LANES // 8), axis=0)
    r_head = lax.broadcasted_iota(jnp.int32, (LANES, D_ATTN), 1) >> 6
    c_head = lax.broadcasted_iota(jnp.int32, (LANES, D_ATTN), 0) >> 4
    kbd_hi, kbd_lo = _split_bf16(jnp.where(r_head == c_head, kb_rows, 0.0))
    q_hi, q_lo = _split_bf16(jnp.concatenate(q_slabs, axis=1))
    gate = _nt_dot(q_hi, kbd_hi) + (_nt_dot(q_lo, kbd_hi) + _nt_dot(q_hi, kbd_lo))

    lane = lax.broadcasted_iota(jnp.int32, (MOBA_BLOCK, LANES), 1)
    j_of = lane & 7
    past = j_of < i
    gm = jnp.where(past, gate, NEG_INF)
    rank = jnp.zeros((MOBA_BLOCK, LANES), jnp.int32)
    for r in range(1, 8):
        other = pltpu.roll(gm, r, 1)
        beats = (other > gm) | ((other == gm) & (j_of >= r))
        rank = rank + beats.astype(jnp.int32)
    allowed = (past & (rank < MOBA_TOPK)) | (j_of == i)
    bias = jnp.where(allowed, 0.0, NEG_INF).astype(jnp.bfloat16)
    bias_cols = jnp.dot(bias, place_ref[...], preferred_element_type=jnp.float32)

    for hd in range(N_HEADS):
        own = (lane < HEAD_DIM) if hd % 2 == 0 else (lane >= HEAD_DIM)
        col0 = HEAD_DIM if hd % 2 == 0 else 0
        qa = jnp.where(own, q_slabs[hd // 2], slab(bias_cols, hd))
        ka = jnp.where(own, k_slabs[hd // 2], jnp.where(lane == col0 + i, 1.0, 0.0))
        qa_ref[0, hd] = qa.astype(jnp.bfloat16)
        ka_ref[0, hd] = ka.astype(jnp.bfloat16)


def _proj_call(x, modl, w_in_aug, cos_t, sin_t, place, w_pool_l, pool_scale_l):
    B, S, D = x.shape
    nb = S // MOBA_BLOCK
    tm = MOBA_BLOCK
    tab = pl.BlockSpec((tm, LANES), lambda b, i: (i, 0))
    head_spec = pl.BlockSpec((1, N_HEADS, tm, LANES), lambda b, i: (b, 0, i, 0))
    return pl.pallas_call(
        _proj_kernel,
        grid=(B, nb),
        in_specs=[
            pl.BlockSpec((1, tm, D), lambda b, i: (b, i, 0)),
            pl.BlockSpec((1, N_MOD, D), lambda b, i: (b, 0, 0)),
            pl.BlockSpec((D, D_IN + 2 * D_ATTN), lambda b, i: (0, 0)),
            tab, tab,
            pl.BlockSpec((LANES, N_HEADS * LANES), lambda b, i: (0, 0)),
            pl.BlockSpec((N_POOL_GROUPS, POOL_GROUP, POOL_GROUP), lambda b, i: (0, 0, 0)),
            pl.BlockSpec((1, D_POOL), lambda b, i: (0, 0)),
        ],
        out_specs=[
            head_spec, head_spec,
            pl.BlockSpec((1, tm, D_ATTN), lambda b, i: (b, i, 0)),
            pl.BlockSpec((1, tm, D_POOL), lambda b, i: (b, i, 0)),
            pl.BlockSpec((1, nb, D_ATTN), lambda b, i: (b, 0, 0)),
        ],
        out_shape=[
            jax.ShapeDtypeStruct((B, N_HEADS, S, LANES), jnp.bfloat16),
            jax.ShapeDtypeStruct((B, N_HEADS, S, LANES), jnp.bfloat16),
            jax.ShapeDtypeStruct((B, S, D_ATTN), jnp.bfloat16),
            jax.ShapeDtypeStruct((B, S, D_POOL), jnp.bfloat16),
            jax.ShapeDtypeStruct((B, nb, D_ATTN), jnp.float32),
        ],
        scratch_shapes=[pltpu.VMEM((POOL_HALO, D_POOL), jnp.float32)],
        compiler_params=_params(("arbitrary", "arbitrary")),
    )(x, modl, w_in_aug, cos_t, sin_t, place, w_pool_l, pool_scale_l.reshape(1, D_POOL))


def _attn_kernel(qa_ref, ka_ref, v_ref, o_ref):
    nb = v_ref.shape[1] // MOBA_BLOCK
    row = lax.broadcasted_iota(jnp.int32, (MOBA_BLOCK, MOBA_BLOCK), 0)
    col = lax.broadcasted_iota(jnp.int32, (MOBA_BLOCK, MOBA_BLOCK), 1)
    causal = col <= row
    lane = lax.broadcasted_iota(jnp.int32, (MOBA_BLOCK, LANES), 1)

    for i in reversed(range(nb)):
        r0 = i * MOBA_BLOCK
        outs = []
        for hh in range(2):
            q = qa_ref[0, hh, r0:r0 + MOBA_BLOCK, :]
            s_own = jnp.where(causal, _nt_dot(q, ka_ref[0, hh, r0:r0 + MOBA_BLOCK, :]), NEG_INF)
            m = jnp.max(s_own, axis=1, keepdims=True)
            if i > 0:
                s_past = _nt_dot(q, ka_ref[0, hh, 0:r0, :])
                m = jnp.maximum(m, jnp.max(s_past, axis=1, keepdims=True))
            p_own = jnp.exp(s_own - m)
            l = jnp.sum(p_own, axis=1, keepdims=True)
            acc = jnp.dot(p_own.astype(jnp.bfloat16), v_ref[0, r0:r0 + MOBA_BLOCK, :],
                          preferred_element_type=jnp.float32)
            if i > 0:
                p_past = jnp.exp(s_past - m)
                l = l + jnp.sum(p_past, axis=1, keepdims=True)
                acc = acc + jnp.dot(p_past.astype(jnp.bfloat16), v_ref[0, 0:r0, :],
                                    preferred_element_type=jnp.float32)
            outs.append(acc / l)
        o = jnp.where(lane < HEAD_DIM, outs[0], outs[1])
        o_ref[0, r0:r0 + MOBA_BLOCK, :] = o.astype(o_ref.dtype)


def _attn_call(qa, ka, v):
    B, _, S, _ = qa.shape
    n_pair = N_HEADS // 2
    pair_spec = pl.BlockSpec((1, 2, S, LANES), lambda b, hp: (b, hp, 0, 0))
    slab_spec = pl.BlockSpec((1, S, LANES), lambda b, hp: (b, 0, hp))
    return pl.pallas_call(
        _attn_kernel,
        grid=(B, n_pair),
        in_specs=[pair_spec, pair_spec, slab_spec],
        out_specs=slab_spec,
        out_shape=jax.ShapeDtypeStruct((B, S, D_ATTN), jnp.bfloat16),
        compiler_params=_params(("arbitrary", "arbitrary")),
    )(qa, ka, v)


def _mixout_kernel(a_ref, m_ref, x_ref, mod_ref, w_ref, g_ref, b_ref, o_ref):
    y = jnp.dot(a_ref[0], w_ref[0:D_ATTN, :], preferred_element_type=jnp.float32)
    y = y + jnp.dot(m_ref[0], w_ref[D_ATTN:, :], preferred_element_type=jnp.float32)
    gate = mod_ref[0, 2:3, :]
    z = DEEPNORM_ALPHA * x_ref[0] + gate * y
    o_ref[0] = _layer_norm(z) * g_ref[...] + b_ref[...]


def _mixout_call(a, m, x, modl, w_out_bf, ln_g, ln_b):
    B, S, D = x.shape
    tm = 512
    vec = pl.BlockSpec((1, D), lambda b, i: (0, 0))
    return pl.pallas_call(
        _mixout_kernel,
        grid=(B, S // tm),
        in_specs=[
            pl.BlockSpec((1, tm, D_ATTN), lambda b, i: (b, i, 0)),
            pl.BlockSpec((1, tm, D_POOL), lambda b, i: (b, i, 0)),
            pl.BlockSpec((1, tm, D), lambda b, i: (b, i, 0)),
            pl.BlockSpec((1, N_MOD, D), lambda b, i: (b, 0, 0)),
            pl.BlockSpec((D, D), lambda b, i: (0, 0)),
            vec, vec,
        ],
        out_specs=pl.BlockSpec((1, tm, D), lambda b, i: (b, i, 0)),
        out_shape=jax.ShapeDtypeStruct((B, S, D), jnp.float32),
        compiler_params=_params(("arbitrary", "arbitrary")),
    )(a, m, x, modl, w_out_bf, ln_g.reshape(1, D), ln_b.reshape(1, D))


def _top2_rows(vals):
    def first_max(rows):
        m = rows[0]
        for v in rows[1:]:
            m = jnp.maximum(m, v)
        idx = jnp.full_like(m, float(len(rows) - 1))
        for k in range(len(rows) - 2, -1, -1):
            idx = jnp.where(rows[k] == m, float(k), idx)
        return m, idx

    m1, i1 = first_max(vals)
    rest = [jnp.where(i1 == float(k), -jnp.inf, v) for k, v in enumerate(vals)]
    m2, i2 = first_max(rest)
    return m1, i1, m2, i2


def _router_rows(logits_t, rb_ref):
    lg = [logits_t[e:e + 1, :] for e in range(N_EXPERTS)]
    mx = lg[0]
    for v in lg[1:]:
        mx = jnp.maximum(mx, v)
    ex = [jnp.exp(v - mx) for v in lg]
    den = ex[0]
    for v in ex[1:]:
        den = den + v
    scores = [v / den for v in ex]
    sel = [scores[e] + rb_ref[e:e + 1, :] for e in range(N_EXPERTS)]
    best_score = None
    best = None
    for g in range(N_EXPERTS // EXPERTS_PER_GROUP):
        m1, _, m2, _ = _top2_rows(sel[g * EXPERTS_PER_GROUP:(g + 1) * EXPERTS_PER_GROUP])
        gs = m1 + m2
        if g == 0:
            best_score, best = gs, jnp.zeros_like(gs)
        else:
            better = gs > best_score
            best_score = jnp.where(better, gs, best_score)
            best = jnp.where(better, float(g), best)
    masked = [jnp.where(best == float(e // EXPERTS_PER_GROUP), sel[e], NEG_INF)
              for e in range(N_EXPERTS)]
    _, i1, _, i2 = _top2_rows(masked)
    w1 = jnp.zeros_like(i1)
    w2 = jnp.zeros_like(i2)
    for e in range(N_EXPERTS):
        w1 = jnp.where(i1 == float(e), scores[e], w1)
        w2 = jnp.where(i2 == float(e), scores[e], w2)
    tot = w1 + w2
    return i1, i2, w1 / tot, w2 / tot


SC_CORES = 2
SC_SUBCORES = 16
SC_MAX_INDEX_ROWS = 128
SC_RING = 4
SC_TILE_BYTES = 64 * 1024

ROUTE_TILE = 1024
EXPERT_CHUNK = 512


def _sc_gather_rows(table, idx):
    n_rows = idx.shape[0]
    width = table.shape[1]
    workers = SC_CORES * SC_SUBCORES
    rows = min(SC_MAX_INDEX_ROWS, SC_TILE_BYTES // (4 * width))
    per_worker = n_rows // workers
    assert per_worker * workers == n_rows and per_worker % rows == 0
    n_chunk = per_worker // rows
    assert n_chunk % SC_RING == 0
    mesh = plsc.VectorSubcoreMesh(core_axis_name="c", subcore_axis_name="s")
    row_buf = pltpu.VMEM((rows, width), table.dtype)
    idx_buf = pltpu.VMEM((rows,), jnp.int32)

    @functools.partial(
        pl.kernel, mesh=mesh,
        out_type=jax.ShapeDtypeStruct((n_rows, width), table.dtype),
        scratch_types=([idx_buf] * SC_RING + [row_buf] * SC_RING
                       + [pltpu.SemaphoreType.DMA] * SC_RING),
    )
    def gather(table_hbm, idx_hbm, out_hbm, *scratch):
        idx_v = scratch[:SC_RING]
        rows_v = scratch[SC_RING:2 * SC_RING]
        sems = scratch[2 * SC_RING:]
        wid = lax.axis_index("s") * SC_CORES + lax.axis_index("c")
        base = wid * per_worker

        def gather_copy(b):
            return pltpu.make_async_copy(table_hbm.at[idx_v[b]], rows_v[b], sems[b])

        def start_gather(j, b):
            off = pl.multiple_of(base + j * rows, 8)
            pltpu.sync_copy(idx_hbm.at[pl.ds(off, rows)], idx_v[b])
            gather_copy(b).start()

        for b in range(SC_RING):
            start_gather(b, b)

        def group(g, carry):
            for b in range(SC_RING):
                j = g * SC_RING + b
                gather_copy(b).wait()
                off = pl.multiple_of(base + j * rows, 8)
                pltpu.sync_copy(rows_v[b], out_hbm.at[pl.ds(off, rows)])

                @pl.when(j + SC_RING < n_chunk)
                def _():
                    start_gather(j + SC_RING, b)
            return carry

        lax.fori_loop(0, n_chunk // SC_RING, group, 0)

    return gather(table, idx)


def _route_kernel(x_ref, mod_ref, wr_ref, rb_ref, hp_ref, meta_ref, cnt_ref, wcol_ref, carry_scr):
    t = pl.program_id(0)
    T = ROUTE_TILE

    @pl.when(t == 0)
    def _():
        carry_scr[...] = jnp.zeros_like(carry_scr)

    shift = mod_ref[0, 3:4, :]
    scale = mod_ref[0, 4:5, :]
    h = _layer_norm(x_ref[0]) * (1.0 + scale) + shift
    h_hi = h.astype(jnp.bfloat16)
    h_lo = (h - h_hi.astype(jnp.float32)).astype(jnp.bfloat16)
    hp_ref[...] = h

    wr_hi, wr_lo = _split_bf16(wr_ref[...])
    logits_t = _nt_dot(wr_hi, h_hi) + (_nt_dot(wr_lo, h_hi) + _nt_dot(wr_hi, h_lo))
    i1, i2, w1, w2 = _router_rows(logits_t, rb_ref)

    e_iota = lax.broadcasted_iota(jnp.int32, (N_EXPERTS, T), 0).astype(jnp.float32)
    sel1 = e_iota == i1
    sel2 = e_iota == i2
    onehot = jnp.where(sel1 | sel2, 1.0, 0.0)
    tr = lax.broadcasted_iota(jnp.int32, (T, T), 0)
    tc = lax.broadcasted_iota(jnp.int32, (T, T), 1)
    earlier = jnp.where(tr < tc, 1.0, 0.0).astype(jnp.bfloat16)
    rank = jnp.dot(onehot.astype(jnp.bfloat16), earlier,
                   preferred_element_type=jnp.float32) + carry_scr[:, 0:1]
    rank1 = jnp.sum(jnp.where(sel1, rank, 0.0), axis=0, keepdims=True)
    rank2 = jnp.sum(jnp.where(sel2, rank, 0.0), axis=0, keepdims=True)
    for r, row in enumerate((i1, i2, rank1, rank2)):
        meta_ref[r, 0] = row
    carry_scr[...] = carry_scr[...] + jnp.sum(onehot, axis=1, keepdims=True)
    cnt_ref[...] = carry_scr[...]

    r128 = lax.broadcasted_iota(jnp.int32, (LANES, T), 0)
    terms = jnp.zeros((LANES, T), jnp.float32)
    k = 0
    for w in (w1, w2):
        rest = w
        for _ in range(3):
            part = rest.astype(jnp.bfloat16).astype(jnp.float32)
            terms = jnp.where(r128 == k, part, terms)
            rest = rest - part
            k += 1
    eye = jnp.where(tr == tc, 1.0, 0.0).astype(jnp.bfloat16)
    wcol_ref[...] = _nt_dot(eye, terms.astype(jnp.bfloat16))


def _route_call(x, modl, w_router_t, rbias_col):
    B, S, D = x.shape
    tm = ROUTE_TILE
    nt = S // tm
    n_tiles = B * nt
    return pl.pallas_call(
        _route_kernel,
        grid=(n_tiles,),
        in_specs=[
            pl.BlockSpec((1, tm, D), lambda t: (t // nt, t % nt, 0)),
            pl.BlockSpec((1, N_MOD, D), lambda t: (t // nt, 0, 0)),
            pl.BlockSpec((N_EXPERTS, D), lambda t: (0, 0)),
            pl.BlockSpec((N_EXPERTS, 1), lambda t: (0, 0)),
        ],
        out_specs=[
            pl.BlockSpec((tm, D), lambda t: (t, 0)),
            pl.BlockSpec((4, 1, 1, tm), lambda t: (0, t, 0, 0)),
            pl.BlockSpec((N_EXPERTS, LANES), lambda t: (0, 0)),
            pl.BlockSpec((tm, LANES), lambda t: (t, 0)),
        ],
        out_shape=[
            jax.ShapeDtypeStruct((B * S, D), jnp.float32),
            jax.ShapeDtypeStruct((4, n_tiles, 1, tm), jnp.float32),
            jax.ShapeDtypeStruct((N_EXPERTS, LANES), jnp.float32),
            jax.ShapeDtypeStruct((B * S, LANES), jnp.float32),
        ],
        scratch_shapes=[pltpu.VMEM((N_EXPERTS, LANES), jnp.float32)],
        compiler_params=_params(("arbitrary",)),
    )(x, modl, w_router_t, rbias_col)


def _dispatch_plan(meta, counts, n_tokens):
    n_rows = 2 * n_tokens + N_EXPERTS * EXPERT_CHUNK
    n_chunks = n_rows // EXPERT_CHUNK
    meta = meta.reshape(meta.shape[0], n_tokens).astype(jnp.int32)
    per_token = lambda r: meta[r]
    e1 = per_token(0)
    e2 = per_token(1)
    cnt = counts[:, 0].astype(jnp.int32)
    padded = (cnt + (EXPERT_CHUNK - 1)) // EXPERT_CHUNK * EXPERT_CHUNK
    ends = jnp.cumsum(padded)
    starts = ends - padded
    first = jnp.cumsum(cnt) - cnt
    pos1 = starts[e1] + per_token(2)
    pos2 = starts[e2] + per_token(3)
    order = jnp.argsort(jnp.concatenate([pos1, pos2])).astype(jnp.int32)
    sorted_tok = jnp.where(order >= n_tokens, order - n_tokens, order)
    chunk_row0 = jnp.arange(n_chunks, dtype=jnp.int32) * EXPERT_CHUNK
    chunk_expert = jnp.minimum(
        jnp.sum((ends[None, :] <= chunk_row0[:, None]).astype(jnp.int32), axis=1), N_EXPERTS - 1)
    local = (chunk_row0 - starts[chunk_expert])[:, None] + jnp.arange(EXPERT_CHUNK, dtype=jnp.int32)
    real = local < cnt[chunk_expert][:, None]
    nth = jnp.clip(first[chunk_expert][:, None] + local, 0, 2 * n_tokens - 1)
    spread = (chunk_row0[:, None] + jnp.arange(EXPERT_CHUNK, dtype=jnp.int32)) % n_tokens
    src = jnp.where(real, sorted_tok[nth], spread).reshape(n_rows)
    n_used = (ends[-1] // EXPERT_CHUNK).reshape(1).astype(jnp.int32)
    return pos1, pos2, src, chunk_expert, n_used


def _expert_kernel(ce_ref, nu_ref, xs_ref, wg_ref, wu_ref, wd_ref, o_ref, wg_bf, wu_bf, wd_bf):
    c = pl.program_id(0)

    @pl.when(c < nu_ref[0])
    def _():
        @pl.when((c == 0) | (ce_ref[c] != ce_ref[jnp.maximum(c - 1, 0)]))
        def _():
            wg_bf[...] = wg_ref[0, 0].astype(jnp.bfloat16)
            wu_bf[...] = wu_ref[0, 0].astype(jnp.bfloat16)
            wd_bf[...] = wd_ref[0, 0].astype(jnp.bfloat16)

        xs = xs_ref[...].astype(jnp.bfloat16)
        gt = jnp.dot(xs, wg_bf[...], preferred_element_type=jnp.float32)
        up = jnp.dot(xs, wu_bf[...], preferred_element_type=jnp.float32)
        act = (gt * (1.0 / (1.0 + jnp.exp(-gt))) * up).astype(jnp.bfloat16)
        o_ref[...] = jnp.dot(act, wd_bf[...], preferred_element_type=jnp.float32)

    @pl.when(c >= nu_ref[0])
    def _():
        o_ref[...] = jnp.zeros_like(o_ref)


def _expert_call(chunk_expert, n_used, xs, w_gate, w_up, w_down, layer):
    n_rows, D = xs.shape
    grid_spec = pltpu.PrefetchScalarGridSpec(
        num_scalar_prefetch=2,
        grid=(n_rows // EXPERT_CHUNK,),
        in_specs=[
            pl.BlockSpec((EXPERT_CHUNK, D), lambda c, ce, nu: (c, 0)),
            pl.BlockSpec((1, 1, D, D_EXPERT), lambda c, ce, nu: (layer, ce[c], 0, 0)),
            pl.BlockSpec((1, 1, D, D_EXPERT), lambda c, ce, nu: (layer, ce[c], 0, 0)),
            pl.BlockSpec((1, 1, D_EXPERT, D), lambda c, ce, nu: (layer, ce[c], 0, 0)),
        ],
        out_specs=pl.BlockSpec((EXPERT_CHUNK, D), lambda c, ce, nu: (c, 0)),
        scratch_shapes=[
            pltpu.VMEM((D, D_EXPERT), jnp.bfloat16),
            pltpu.VMEM((D, D_EXPERT), jnp.bfloat16),
            pltpu.VMEM((D_EXPERT, D), jnp.bfloat16),
        ],
    )
    return pl.pallas_call(
        _expert_kernel,
        grid_spec=grid_spec,
        out_shape=jax.ShapeDtypeStruct((n_rows, D), jnp.float32),
        compiler_params=_params(("arbitrary",)),
    )(chunk_expert, n_used, xs, w_gate, w_up, w_down)


def _combine_kernel(y1_ref, y2_ref, wcol_ref, x_ref, mod_ref, g_ref, b_ref, o_ref):
    wc = wcol_ref[0]
    w1 = (wc[:, 0:1] + wc[:, 1:2]) + wc[:, 2:3]
    w2 = (wc[:, 3:4] + wc[:, 4:5]) + wc[:, 5:6]
    y = w1 * y1_ref[0] + w2 * y2_ref[0]
    gate = mod_ref[0, 5:6, :]
    z = DEEPNORM_ALPHA * x_ref[0] + gate * y
    o_ref[0] = _layer_norm(z) * g_ref[...] + b_ref[...]


def _combine_call(y1, y2, wcol, x, modl, ln_g, ln_b):
    B, S, D = x.shape
    tm = 512
    rows = pl.BlockSpec((1, tm, D), lambda b, i: (b, i, 0))
    vec = pl.BlockSpec((1, D), lambda b, i: (0, 0))
    return pl.pallas_call(
        _combine_kernel,
        grid=(B, S // tm),
        in_specs=[rows, rows, pl.BlockSpec((1, tm, LANES), lambda b, i: (b, i, 0)), rows,
                  pl.BlockSpec((1, N_MOD, D), lambda b, i: (b, 0, 0)), vec, vec],
        out_specs=rows,
        out_shape=jax.ShapeDtypeStruct((B, S, D), jnp.float32),
        compiler_params=_params(("arbitrary", "arbitrary")),
    )(y1.reshape(B, S, D), y2.reshape(B, S, D), wcol.reshape(B, S, LANES),
      x, modl, ln_g.reshape(1, D), ln_b.reshape(1, D))


def _moe_routed(x, modl, w_router_t, rbias_col, w_gate, w_up, w_down, layer, ln_g, ln_b):
    B, S, D = x.shape
    h, meta, counts, wcol = _route_call(x, modl, w_router_t, rbias_col)
    pos1, pos2, src, chunk_expert, n_used = _dispatch_plan(meta, counts, B * S)
    xs = _sc_gather_rows(h, src)
    ys = _expert_call(chunk_expert, n_used, xs, w_gate, w_up, w_down, layer)
    y1 = _sc_gather_rows(ys, pos1)
    y2 = _sc_gather_rows(ys, pos2)
    return _combine_call(y1, y2, wcol, x, modl, ln_g, ln_b)


def _rope_tables(S):
    half = ROT_DIM // 2
    inv_freq = ROPE_THETA ** (-(jnp.arange(half, dtype=jnp.float32) * 2.0 / ROT_DIM))
    ang = jnp.arange(S, dtype=jnp.float32)[:, None] * inv_freq[None, :]
    cos, sin = jnp.cos(ang), jnp.sin(ang)
    d = jnp.arange(LANES) % HEAD_DIM
    f = d % half
    rot = d[None, :] < ROT_DIM
    return jnp.where(rot, cos[:, f], 1.0), jnp.where(rot, sin[:, f], 0.0)


def _with_rotary_partner_columns(w_in_l):
    half = ROT_DIM // 2
    qk = w_in_l[:, :2 * D_ATTN]
    d = (jnp.arange(2 * D_ATTN) % HEAD_DIM)[None, :]
    partner = jnp.where(d < half, -jnp.roll(qk, -half, axis=1),
                        jnp.where(d < ROT_DIM, jnp.roll(qk, half, axis=1), 0.0))
    return jnp.concatenate([w_in_l, partner], axis=1)


def _bias_placement():
    src = jnp.arange(LANES)
    hd, u, j = src >> 4, (src >> 3) & 1, src & 7
    col0 = jnp.where(hd % 2 == 0, HEAD_DIM, 0)
    dst = LANES * hd + col0 + j
    onehot = (jnp.arange(N_HEADS * LANES)[None, :] == dst[:, None]) & (u[:, None] == 1)
    return onehot.astype(jnp.bfloat16)


def kernel(x, c, w_mod, b_mod, w_in, w_pool, pool_scale, w_out, ln1_g, ln1_b,
           w_router, router_bias, w_gate, w_up, w_down, ln2_g, ln2_b):
    B, S, D = x.shape
    bf = jnp.bfloat16
    mod = _mod_call(c, w_mod, b_mod).reshape(DEPTH, B, N_MOD, D)
    cos_t, sin_t = _rope_tables(S)
    place = _bias_placement()
    w_router_t = w_router.T
    rbias_col = router_bias.reshape(N_EXPERTS, 1)
    for l in range(DEPTH):
        modl = mod[l]
        w_in_aug = _with_rotary_partner_columns(w_in[l]).astype(bf)
        qa, ka, v, m, _ = _proj_call(x, modl, w_in_aug, cos_t, sin_t, place,
                                     w_pool[l], pool_scale[l])
        a = _attn_call(qa, ka, v)
        x = _mixout_call(a, m, x, modl, w_out[l].astype(bf), ln1_g[l], ln1_b[l])
        x = _moe_routed(x, modl, w_router_t, rbias_col, w_gate, w_up, w_down, l,
                        ln2_g[l], ln2_b[l])
    return x
```

```python
import functools

import jax
import jax.numpy as jnp
from jax import lax
from jax.experimental import pallas as pl
from jax.experimental.pallas import tpu as pltpu
from jax.experimental.pallas import tpu_sc as plsc

D_MODEL = 1024
DEPTH = 2
D_ATTN = 512
D_POOL = 512
N_HEADS = 8
HEAD_DIM = 64
ROT_DIM = 16
ROPE_THETA = 500000.0
MOBA_BLOCK = 256
MOBA_TOPK = 3
POOL_GROUP = 128
N_POOL_GROUPS = 4
D_IN = 3 * D_ATTN + D_POOL
N_EXPERTS = 16
EXPERTS_PER_GROUP = 4
D_EXPERT = 512
DEEPNORM_ALPHA = (2 * DEPTH) ** 0.25
N_MOD = 6
LN_EPS = 1e-5
NEG_INF = -1e30

POOL_HALO = 16
LANES = 128
VMEM_LIMIT = 56 * 1024 * 1024

_HI = lax.Precision.HIGHEST
_NT = (((1,), (1,)), ((), ()))


def _params(sem):
    return pltpu.CompilerParams(dimension_semantics=sem, vmem_limit_bytes=VMEM_LIMIT)


def _nt_dot(a, b):
    return lax.dot_general(a, b, _NT, preferred_element_type=jnp.float32)


def _mod_kernel(c_ref, w_ref, b_ref, o_ref):
    c = c_ref[...]
    cond = c * (1.0 / (1.0 + jnp.exp(-c)))
    o_ref[0] = jnp.dot(cond, w_ref[0], precision=_HI,
                       preferred_element_type=jnp.float32) + b_ref[0]


def _mod_call(c, w_mod, b_mod):
    B = c.shape[0]
    return pl.pallas_call(
        _mod_kernel,
        grid=(DEPTH, N_MOD),
        in_specs=[
            pl.BlockSpec((B, D_MODEL), lambda l, j: (0, 0)),
            pl.BlockSpec((1, D_MODEL, D_MODEL), lambda l, j: (l, 0, j)),
            pl.BlockSpec((1, 1, D_MODEL), lambda l, j: (l, 0, j)),
        ],
        out_specs=pl.BlockSpec((1, B, D_MODEL), lambda l, j: (l, 0, j)),
        out_shape=jax.ShapeDtypeStruct((DEPTH, B, N_MOD * D_MODEL), jnp.float32),
        compiler_params=_params(("arbitrary", "arbitrary")),
    )(c, w_mod, b_mod.reshape(DEPTH, 1, N_MOD * D_MODEL))


def _layer_norm(x):
    mu = jnp.mean(x, axis=-1, keepdims=True)
    xc = x - mu
    var = jnp.mean(xc * xc, axis=-1, keepdims=True)
    return xc * lax.rsqrt(var + LN_EPS)


def _split_bf16(t):
    hi = t.astype(jnp.bfloat16)
    lo = (t - hi.astype(jnp.float32)).astype(jnp.bfloat16)
    return hi, lo


def _proj_kernel(x_ref, mod_ref, w_ref, cos_ref, sin_ref, place_ref, wp_ref, ps_ref,
                 qa_ref, ka_ref, v_ref, m_ref, kbar_ref, halo_scr):
    i = pl.program_id(1)

    @pl.when(i == 0)
    def _():
        kbar_ref[...] = jnp.zeros_like(kbar_ref)
        halo_scr[...] = jnp.zeros_like(halo_scr)

    x = x_ref[0]
    shift = mod_ref[0, 0:1, :]
    scale = mod_ref[0, 1:2, :]
    h = (_layer_norm(x) * (1.0 + scale) + shift).astype(jnp.bfloat16)

    cos = cos_ref[...]
    sin = sin_ref[...]

    def proj(c0, width):
        return jnp.dot(h, w_ref[:, c0:c0 + width], preferred_element_type=jnp.float32)

    def slab(t, s):
        return t[:, LANES * s:LANES * (s + 1)]

    n_slab = D_ATTN // LANES
    q = proj(0, D_ATTN)
    qp = proj(D_IN, D_ATTN)
    q_slabs = [(slab(q, s) * cos + slab(qp, s) * sin) * (HEAD_DIM ** -0.5)
               for s in range(n_slab)]
    k = proj(D_ATTN, D_ATTN)
    kp = proj(D_IN + D_ATTN, D_ATTN)
    k_slabs = [slab(k, s) * cos + slab(kp, s) * sin for s in range(n_slab)]
    v_ref[0] = proj(2 * D_ATTN, D_ATTN).astype(jnp.bfloat16)

    p = proj(3 * D_ATTN, D_POOL)
    ext = jnp.concatenate([halo_scr[...], p], axis=0)
    halo_scr[...] = p[MOBA_BLOCK - POOL_HALO:, :]
    t_pos = i * MOBA_BLOCK + lax.broadcasted_iota(jnp.int32, (MOBA_BLOCK, LANES), 0)
    m_slabs = []
    for g in range(N_POOL_GROUPS):
        win = slab(ext, g)
        for step in range(g + 1):
            win = win + pltpu.roll(win, 1 << step, 0)
        cnt = jnp.minimum(t_pos + 1, 2 << g).astype(jnp.float32)
        d = (win[POOL_HALO:, :] / cnt - slab(p, g)).astype(jnp.bfloat16)
        y = jnp.dot(d, wp_ref[g].astype(jnp.bfloat16), preferred_element_type=jnp.float32)
        m_slabs.append(y * slab(ps_ref[...], g))
    m_ref[0] = jnp.concatenate(m_slabs, axis=1).astype(m_ref.dtype)

    kmean = jnp.concatenate(
        [jnp.mean(ks, axis=0, keepdims=True) for ks in k_slabs], axis=1)
    kbar_ref[0, pl.ds(i, 1), :] = kmean

    kb = kbar_ref[0]
    kb_rows = jnp.concatenate([kb] * (LANES // 8), axis=0)
    r_head = lax.broadcasted_iota(jnp.int32, (LANES, D_ATTN), 1) >> 6
    c_head = lax.broadcasted_iota(jnp.int32, (LANES, D_ATTN), 0) >> 4
    kbd_hi, kbd_lo = _split_bf16(jnp.where(r_head == c_head, kb_rows, 0.0))
    q_hi, q_lo = _split_bf16(jnp.concatenate(q_slabs, axis=1))
    gate = _nt_dot(q_hi, kbd_hi) + (_nt_dot(q_lo, kbd_hi) + _nt_dot(q_hi, kbd_lo))

    lane = lax.broadcasted_iota(jnp.int32, (MOBA_BLOCK, LANES), 1)
    j_of = lane & 7
    past = j_of < i
    gm = jnp.where(past, gate, NEG_INF)
    rank = jnp.zeros((MOBA_BLOCK, LANES), jnp.int32)
    for r in range(1, 8):
        other = pltpu.roll(gm, r, 1)
        beats = (other > gm) | ((other == gm) & (j_of >= r))
        rank = rank + beats.astype(jnp.int32)
    allowed = (past & (rank < MOBA_TOPK)) | (j_of == i)
    bias = jnp.where(allowed, 0.0, NEG_INF).astype(jnp.bfloat16)
    bias_cols = jnp.dot(bias, place_ref[...], preferred_element_type=jnp.float32)

    for hd in range(N_HEADS):
        own = (lane < HEAD_DIM) if hd % 2 == 0 else (lane >= HEAD_DIM)
        col0 = HEAD_DIM if hd % 2 == 0 else 0
        qa = jnp.where(own, q_slabs[hd // 2], slab(bias_cols, hd))
        ka = jnp.where(own, k_slabs[hd // 2], jnp.where(lane == col0 + i, 1.0, 0.0))
        qa_ref[0, hd] = qa.astype(jnp.bfloat16)
        ka_ref[0, hd] = ka.astype(jnp.bfloat16)


def _proj_call(x, modl, w_in_aug, cos_t, sin_t, place, w_pool_l, pool_scale_l):
    B, S, D = x.shape
    nb = S // MOBA_BLOCK
    tm = MOBA_BLOCK
    tab = pl.BlockSpec((tm, LANES), lambda b, i: (i, 0))
    head_spec = pl.BlockSpec((1, N_HEADS, tm, LANES), lambda b, i: (b, 0, i, 0))
    return pl.pallas_call(
        _proj_kernel,
        grid=(B, nb),
        in_specs=[
            pl.BlockSpec((1, tm, D), lambda b, i: (b, i, 0)),
            pl.BlockSpec((1, N_MOD, D), lambda b, i: (b, 0, 0)),
            pl.BlockSpec((D, D_IN + 2 * D_ATTN), lambda b, i: (0, 0)),
            tab, tab,
            pl.BlockSpec((LANES, N_HEADS * LANES), lambda b, i: (0, 0)),
            pl.BlockSpec((N_POOL_GROUPS, POOL_GROUP, POOL_GROUP), lambda b, i: (0, 0, 0)),
            pl.BlockSpec((1, D_POOL), lambda b, i: (0, 0)),
        ],
        out_specs=[
            head_spec, head_spec,
            pl.BlockSpec((1, tm, D_ATTN), lambda b, i: (b, i, 0)),
            pl.BlockSpec((1, tm, D_POOL), lambda b, i: (b, i, 0)),
            pl.BlockSpec((1, nb, D_ATTN), lambda b, i: (b, 0, 0)),
        ],
        out_shape=[
            jax.ShapeDtypeStruct((B, N_HEADS, S, LANES), jnp.bfloat16),
            jax.ShapeDtypeStruct((B, N_HEADS, S, LANES), jnp.bfloat16),
            jax.ShapeDtypeStruct((B, S, D_ATTN), jnp.bfloat16),
            jax.ShapeDtypeStruct((B, S, D_POOL), jnp.bfloat16),
            jax.ShapeDtypeStruct((B, nb, D_ATTN), jnp.float32),
        ],
        scratch_shapes=[pltpu.VMEM((POOL_HALO, D_POOL), jnp.float32)],
        compiler_params=_params(("arbitrary", "arbitrary")),
    )(x, modl, w_in_aug, cos_t, sin_t, place, w_pool_l, pool_scale_l.reshape(1, D_POOL))


def _attn_kernel(qa_ref, ka_ref, v_ref, o_ref):
    nb = v_ref.shape[1] // MOBA_BLOCK
    row = lax.broadcasted_iota(jnp.int32, (MOBA_BLOCK, MOBA_BLOCK), 0)
    col = lax.broadcasted_iota(jnp.int32, (MOBA_BLOCK, MOBA_BLOCK), 1)
    causal = col <= row
    lane = lax.broadcasted_iota(jnp.int32, (MOBA_BLOCK, LANES), 1)

    for i in reversed(range(nb)):
        r0 = i * MOBA_BLOCK
        outs = []
        for hh in range(2):
            q = qa_ref[0, hh, r0:r0 + MOBA_BLOCK, :]
            s_own = jnp.where(causal, _nt_dot(q, ka_ref[0, hh, r0:r0 + MOBA_BLOCK, :]), NEG_INF)
            m = jnp.max(s_own, axis=1, keepdims=True)
            if i > 0:
                s_past = _nt_dot(q, ka_ref[0, hh, 0:r0, :])
                m = jnp.maximum(m, jnp.max(s_past, axis=1, keepdims=True))
            p_own = jnp.exp(s_own - m)
            l = jnp.sum(p_own, axis=1, keepdims=True)
            acc = jnp.dot(p_own.astype(jnp.bfloat16), v_ref[0, r0:r0 + MOBA_BLOCK, :],
                          preferred_element_type=jnp.float32)
            if i > 0:
                p_past = jnp.exp(s_past - m)
                l = l + jnp.sum(p_past, axis=1, keepdims=True)
                acc = acc + jnp.dot(p_past.astype(jnp.bfloat16), v_ref[0, 0:r0, :],
                                    preferred_element_type=jnp.float32)
            outs.append(acc / l)
        o = jnp.where(lane < HEAD_DIM, outs[0], outs[1])
        o_ref[0, r0:r0 + MOBA_BLOCK, :] = o.astype(o_ref.dtype)


def _attn_call(qa, ka, v):
    B, _, S, _ = qa.shape
    n_pair = N_HEADS // 2
    pair_spec = pl.BlockSpec((1, 2, S, LANES), lambda b, hp: (b, hp, 0, 0))
    slab_spec = pl.BlockSpec((1, S, LANES), lambda b, hp: (b, 0, hp))
    return pl.pallas_call(
        _attn_kernel,
        grid=(B, n_pair),
        in_specs=[pair_spec, pair_spec, slab_spec],
        out_specs=slab_spec,
        out_shape=jax.ShapeDtypeStruct((B, S, D_ATTN), jnp.bfloat16),
        compiler_params=_params(("arbitrary", "arbitrary")),
    )(qa, ka, v)


def _mixout_kernel(a_ref, m_ref, x_ref, mod_ref, w_ref, g_ref, b_ref, o_ref):
    y = jnp.dot(a_ref[0], w_ref[0:D_ATTN, :], preferred_element_type=jnp.float32)
    y = y + jnp.dot(m_ref[0], w_ref[D_ATTN:, :], preferred_element_type=jnp.float32)
    gate = mod_ref[0, 2:3, :]
    z = DEEPNORM_ALPHA * x_ref[0] + gate * y
    o_ref[0] = _layer_norm(z) * g_ref[...] + b_ref[...]


def _mixout_call(a, m, x, modl, w_out_bf, ln_g, ln_b):
    B, S, D = x.shape
    tm = 1024
    vec = pl.BlockSpec((1, D), lambda b, i: (0, 0))
    return pl.pallas_call(
        _mixout_kernel,
        grid=(B, S // tm),
        in_specs=[
            pl.BlockSpec((1, tm, D_ATTN), lambda b, i: (b, i, 0)),
            pl.BlockSpec((1, tm, D_POOL), lambda b, i: (b, i, 0)),
            pl.BlockSpec((1, tm, D), lambda b, i: (b, i, 0)),
            pl.BlockSpec((1, N_MOD, D), lambda b, i: (b, 0, 0)),
            pl.BlockSpec((D, D), lambda b, i: (0, 0)),
            vec, vec,
        ],
        out_specs=pl.BlockSpec((1, tm, D), lambda b, i: (b, i, 0)),
        out_shape=jax.ShapeDtypeStruct((B, S, D), jnp.float32),
        compiler_params=_params(("arbitrary", "arbitrary")),
    )(a, m, x, modl, w_out_bf, ln_g.reshape(1, D), ln_b.reshape(1, D))


def _top2_rows(vals):
    def first_max(rows):
        m = rows[0]
        for v in rows[1:]:
            m = jnp.maximum(m, v)
        idx = jnp.full_like(m, float(len(rows) - 1))
        for k in range(len(rows) - 2, -1, -1):
            idx = jnp.where(rows[k] == m, float(k), idx)
        return m, idx

    m1, i1 = first_max(vals)
    rest = [jnp.where(i1 == float(k), -jnp.inf, v) for k, v in enumerate(vals)]
    m2, i2 = first_max(rest)
    return m1, i1, m2, i2


def _router_rows(logits_t, rb_ref):
    lg = [logits_t[e:e + 1, :] for e in range(N_EXPERTS)]
    mx = lg[0]
    for v in lg[1:]:
        mx = jnp.maximum(mx, v)
    ex = [jnp.exp(v - mx) for v in lg]
    den = ex[0]
    for v in ex[1:]:
        den = den + v
    scores = [v / den for v in ex]
    sel = [scores[e] + rb_ref[e:e + 1, :] for e in range(N_EXPERTS)]
    best_score = None
    best = None
    for g in range(N_EXPERTS // EXPERTS_PER_GROUP):
        m1, _, m2, _ = _top2_rows(sel[g * EXPERTS_PER_GROUP:(g + 1) * EXPERTS_PER_GROUP])
        gs = m1 + m2
        if g == 0:
            best_score, best = gs, jnp.zeros_like(gs)
        else:
            better = gs > best_score
            best_score = jnp.where(better, gs, best_score)
            best = jnp.where(better, float(g), best)
    masked = [jnp.where(best == float(e // EXPERTS_PER_GROUP), sel[e], NEG_INF)
              for e in range(N_EXPERTS)]
    _, i1, _, i2 = _top2_rows(masked)
    w1 = jnp.zeros_like(i1)
    w2 = jnp.zeros_like(i2)
    for e in range(N_EXPERTS):
        w1 = jnp.where(i1 == float(e), scores[e], w1)
        w2 = jnp.where(i2 == float(e), scores[e], w2)
    tot = w1 + w2
    return i1, i2, w1 / tot, w2 / tot


SC_CORES = 2
SC_SUBCORES = 16
SC_MAX_INDEX_ROWS = 128
SC_RING = 4
SC_TILE_BYTES = 64 * 1024

ROUTE_TILE = 1024
EXPERT_CHUNK = 512


def _sc_gather_rows(table, idx):
    n_rows = idx.shape[0]
    width = table.shape[1]
    workers = SC_CORES * SC_SUBCORES
    rows = min(SC_MAX_INDEX_ROWS, SC_TILE_BYTES // (4 * width))
    per_worker = n_rows // workers
    assert per_worker * workers == n_rows and per_worker % rows == 0
    n_chunk = per_worker // rows
    assert n_chunk % SC_RING == 0
    mesh = plsc.VectorSubcoreMesh(core_axis_name="c", subcore_axis_name="s")
    row_buf = pltpu.VMEM((rows, width), table.dtype)
    idx_buf = pltpu.VMEM((rows,), jnp.int32)

    @functools.partial(
        pl.kernel, mesh=mesh,
        out_type=jax.ShapeDtypeStruct((n_rows, width), table.dtype),
        scratch_types=([idx_buf] * SC_RING + [row_buf] * SC_RING
                       + [pltpu.SemaphoreType.DMA] * SC_RING),
    )
    def gather(table_hbm, idx_hbm, out_hbm, *scratch):
        idx_v = scratch[:SC_RING]
        rows_v = scratch[SC_RING:2 * SC_RING]
        sems = scratch[2 * SC_RING:]
        wid = lax.axis_index("s") * SC_CORES + lax.axis_index("c")
        base = wid * per_worker

        def gather_copy(b):
            return pltpu.make_async_copy(table_hbm.at[idx_v[b]], rows_v[b], sems[b])

        def start_gather(j, b):
            off = pl.multiple_of(base + j * rows, 8)
            pltpu.sync_copy(idx_hbm.at[pl.ds(off, rows)], idx_v[b])
            gather_copy(b).start()

        for b in range(SC_RING):
            start_gather(b, b)

        def group(g, carry):
            for b in range(SC_RING):
                j = g * SC_RING + b
                gather_copy(b).wait()
                off = pl.multiple_of(base + j * rows, 8)
                pltpu.sync_copy(rows_v[b], out_hbm.at[pl.ds(off, rows)])

                @pl.when(j + SC_RING < n_chunk)
                def _():
                    start_gather(j + SC_RING, b)
            return carry

        lax.fori_loop(0, n_chunk // SC_RING, group, 0)

    return gather(table, idx)


def _route_kernel(x_ref, mod_ref, wr_ref, rb_ref, hp_ref, meta_ref, cnt_ref, wcol_ref, carry_scr):
    t = pl.program_id(0)
    T = ROUTE_TILE

    @pl.when(t == 0)
    def _():
        carry_scr[...] = jnp.zeros_like(carry_scr)

    shift = mod_ref[0, 3:4, :]
    scale = mod_ref[0, 4:5, :]
    h = _layer_norm(x_ref[0]) * (1.0 + scale) + shift
    h_hi = h.astype(jnp.bfloat16)
    h_lo = (h - h_hi.astype(jnp.float32)).astype(jnp.bfloat16)
    hp_ref[...] = h

    wr_hi, wr_lo = _split_bf16(wr_ref[...])
    logits_t = _nt_dot(wr_hi, h_hi) + (_nt_dot(wr_lo, h_hi) + _nt_dot(wr_hi, h_lo))
    i1, i2, w1, w2 = _router_rows(logits_t, rb_ref)

    e_iota = lax.broadcasted_iota(jnp.int32, (N_EXPERTS, T), 0).astype(jnp.float32)
    sel1 = e_iota == i1
    sel2 = e_iota == i2
    onehot = jnp.where(sel1 | sel2, 1.0, 0.0)
    tr = lax.broadcasted_iota(jnp.int32, (T, T), 0)
    tc = lax.broadcasted_iota(jnp.int32, (T, T), 1)
    earlier = jnp.where(tr < tc, 1.0, 0.0).astype(jnp.bfloat16)
    rank = jnp.dot(onehot.astype(jnp.bfloat16), earlier,
                   preferred_element_type=jnp.float32) + carry_scr[:, 0:1]
    rank1 = jnp.sum(jnp.where(sel1, rank, 0.0), axis=0, keepdims=True)
    rank2 = jnp.sum(jnp.where(sel2, rank, 0.0), axis=0, keepdims=True)
    for r, row in enumerate((i1, i2, rank1, rank2)):
        meta_ref[r, 0] = row
    carry_scr[...] = carry_scr[...] + jnp.sum(onehot, axis=1, keepdims=True)
    cnt_ref[...] = carry_scr[...]

    r128 = lax.broadcasted_iota(jnp.int32, (LANES, T), 0)
    terms = jnp.zeros((LANES, T), jnp.float32)
    k = 0
    for w in (w1, w2):
        rest = w
        for _ in range(3):
            part = rest.astype(jnp.bfloat16).astype(jnp.float32)
            terms = jnp.where(r128 == k, part, terms)
            rest = rest - part
            k += 1
    eye = jnp.where(tr == tc, 1.0, 0.0).astype(jnp.bfloat16)
    wcol_ref[...] = _nt_dot(eye, terms.astype(jnp.bfloat16))


def _route_call(x, modl, w_router_t, rbias_col):
    B, S, D = x.shape
    tm = ROUTE_TILE
    nt = S // tm
    n_tiles = B * nt
    return pl.pallas_call(
        _route_kernel,
        grid=(n_tiles,),
        in_specs=[
            pl.BlockSpec((1, tm, D), lambda t: (t // nt, t % nt, 0)),
            pl.BlockSpec((1, N_MOD, D), lambda t: (t // nt, 0, 0)),
            pl.BlockSpec((N_EXPERTS, D), lambda t: (0, 0)),
            pl.BlockSpec((N_EXPERTS, 1), lambda t: (0, 0)),
        ],
        out_specs=[
            pl.BlockSpec((tm, D), lambda t: (t, 0)),
            pl.BlockSpec((4, 1, 1, tm), lambda t: (0, t, 0, 0)),
            pl.BlockSpec((N_EXPERTS, LANES), lambda t: (0, 0)),
            pl.BlockSpec((tm, LANES), lambda t: (t, 0)),
        ],
        out_shape=[
            jax.ShapeDtypeStruct((B * S, D), jnp.float32),
            jax.ShapeDtypeStruct((4, n_tiles, 1, tm), jnp.float32),
            jax.ShapeDtypeStruct((N_EXPERTS, LANES), jnp.float32),
            jax.ShapeDtypeStruct((B * S, LANES), jnp.float32),
        ],
        scratch_shapes=[pltpu.VMEM((N_EXPERTS, LANES), jnp.float32)],
        compiler_params=_params(("arbitrary",)),
    )(x, modl, w_router_t, rbias_col)


def _dispatch_plan(meta, counts, n_tokens):
    n_rows = 2 * n_tokens + N_EXPERTS * EXPERT_CHUNK
    n_chunks = n_rows // EXPERT_CHUNK
    meta = meta.reshape(meta.shape[0], n_tokens).astype(jnp.int32)
    per_token = lambda r: meta[r]
    e1 = per_token(0)
    e2 = per_token(1)
    cnt = counts[:, 0].astype(jnp.int32)
    padded = (cnt + (EXPERT_CHUNK - 1)) // EXPERT_CHUNK * EXPERT_CHUNK
    ends = jnp.cumsum(padded)
    starts = ends - padded
    first = jnp.cumsum(cnt) - cnt
    pos1 = starts[e1] + per_token(2)
    pos2 = starts[e2] + per_token(3)
    order = jnp.argsort(jnp.concatenate([pos1, pos2])).astype(jnp.int32)
    sorted_tok = jnp.where(order >= n_tokens, order - n_tokens, order)
    chunk_row0 = jnp.arange(n_chunks, dtype=jnp.int32) * EXPERT_CHUNK
    chunk_expert = jnp.minimum(
        jnp.sum((ends[None, :] <= chunk_row0[:, None]).astype(jnp.int32), axis=1), N_EXPERTS - 1)
    local = (chunk_row0 - starts[chunk_expert])[:, None] + jnp.arange(EXPERT_CHUNK, dtype=jnp.int32)
    real = local < cnt[chunk_expert][:, None]
    nth = jnp.clip(first[chunk_expert][:, None] + local, 0, 2 * n_tokens - 1)
    spread = (chunk_row0[:, None] + jnp.arange(EXPERT_CHUNK, dtype=jnp.int32)) % n_tokens
    src = jnp.where(real, sorted_tok[nth], spread).reshape(n_rows)
    n_used = (ends[-1] // EXPERT_CHUNK).reshape(1).astype(jnp.int32)
    return pos1, pos2, src, chunk_expert, n_used


def _expert_kernel(ce_ref, nu_ref, xs_ref, wg_ref, wu_ref, wd_ref, o_ref, wg_bf, wu_bf, wd_bf):
    c = pl.program_id(0)

    @pl.when(c < nu_ref[0])
    def _():
        @pl.when((c == 0) | (ce_ref[c] != ce_ref[jnp.maximum(c - 1, 0)]))
        def _():
            wg_bf[...] = wg_ref[0, 0].astype(jnp.bfloat16)
            wu_bf[...] = wu_ref[0, 0].astype(jnp.bfloat16)
            wd_bf[...] = wd_ref[0, 0].astype(jnp.bfloat16)

        xs = xs_ref[...].astype(jnp.bfloat16)
        gt = jnp.dot(xs, wg_bf[...], preferred_element_type=jnp.float32)
        up = jnp.dot(xs, wu_bf[...], preferred_element_type=jnp.float32)
        act = (gt * (1.0 / (1.0 + jnp.exp(-gt))) * up).astype(jnp.bfloat16)
        o_ref[...] = jnp.dot(act, wd_bf[...], preferred_element_type=jnp.float32)

    @pl.when(c >= nu_ref[0])
    def _():
        o_ref[...] = jnp.zeros_like(o_ref)


def _expert_call(chunk_expert, n_used, xs, w_gate, w_up, w_down, layer):
    n_rows, D = xs.shape
    grid_spec = pltpu.PrefetchScalarGridSpec(
        num_scalar_prefetch=2,
        grid=(n_rows // EXPERT_CHUNK,),
        in_specs=[
            pl.BlockSpec((EXPERT_CHUNK, D), lambda c, ce, nu: (c, 0)),
            pl.BlockSpec((1, 1, D, D_EXPERT), lambda c, ce, nu: (layer, ce[c], 0, 0)),
            pl.BlockSpec((1, 1, D, D_EXPERT), lambda c, ce, nu: (layer, ce[c], 0, 0)),
            pl.BlockSpec((1, 1, D_EXPERT, D), lambda c, ce, nu: (layer, ce[c], 0, 0)),
        ],
        out_specs=pl.BlockSpec((EXPERT_CHUNK, D), lambda c, ce, nu: (c, 0)),
        scratch_shapes=[
            pltpu.VMEM((D, D_EXPERT), jnp.bfloat16),
            pltpu.VMEM((D, D_EXPERT), jnp.bfloat16),
            pltpu.VMEM((D_EXPERT, D), jnp.bfloat16),
        ],
    )
    return pl.pallas_call(
        _expert_kernel,
        grid_spec=grid_spec,
        out_shape=jax.ShapeDtypeStruct((n_rows, D), jnp.float32),
        compiler_params=_params(("arbitrary",)),
    )(chunk_expert, n_used, xs, w_gate, w_up, w_down)


def _combine_kernel(y1_ref, y2_ref, wcol_ref, x_ref, mod_ref, g_ref, b_ref, o_ref):
    wc = wcol_ref[0]
    w1 = (wc[:, 0:1] + wc[:, 1:2]) + wc[:, 2:3]
    w2 = (wc[:, 3:4] + wc[:, 4:5]) + wc[:, 5:6]
    y = w1 * y1_ref[0] + w2 * y2_ref[0]
    gate = mod_ref[0, 5:6, :]
    z = DEEPNORM_ALPHA * x_ref[0] + gate * y
    o_ref[0] = _layer_norm(z) * g_ref[...] + b_ref[...]


def _combine_call(y1, y2, wcol, x, modl, ln_g, ln_b):
    B, S, D = x.shape
    tm = 1024
    rows = pl.BlockSpec((1, tm, D), lambda b, i: (b, i, 0))
    vec = pl.BlockSpec((1, D), lambda b, i: (0, 0))
    return pl.pallas_call(
        _combine_kernel,
        grid=(B, S // tm),
        in_specs=[rows, rows, pl.BlockSpec((1, tm, LANES), lambda b, i: (b, i, 0)), rows,
                  pl.BlockSpec((1, N_MOD, D), lambda b, i: (b, 0, 0)), vec, vec],
        out_specs=rows,
        out_shape=jax.ShapeDtypeStruct((B, S, D), jnp.float32),
        compiler_params=_params(("arbitrary", "arbitrary")),
    )(y1.reshape(B, S, D), y2.reshape(B, S, D), wcol.reshape(B, S, LANES),
      x, modl, ln_g.reshape(1, D), ln_b.reshape(1, D))


def _moe_routed(x, modl, w_router_t, rbias_col, w_gate, w_up, w_down, layer, ln_g, ln_b):
    B, S, D = x.shape
    h, meta, counts, wcol = _route_call(x, modl, w_router_t, rbias_col)
    pos1, pos2, src, chunk_expert, n_used = _dispatch_plan(meta, counts, B * S)
    xs = _sc_gather_rows(h, src)
    ys = _expert_call(chunk_expert, n_used, xs, w_gate, w_up, w_down, layer)
    y1 = _sc_gather_rows(ys, pos1)
    y2 = _sc_gather_rows(ys, pos2)
    return _combine_call(y1, y2, wcol, x, modl, ln_g, ln_b)


def _rope_tables(S):
    half = ROT_DIM // 2
    inv_freq = ROPE_THETA ** (-(jnp.arange(half, dtype=jnp.float32) * 2.0 / ROT_DIM))
    ang = jnp.arange(S, dtype=jnp.float32)[:, None] * inv_freq[None, :]
    cos, sin = jnp.cos(ang), jnp.sin(ang)
    d = jnp.arange(LANES) % HEAD_DIM
    f = d % half
    rot = d[None, :] < ROT_DIM
    return jnp.where(rot, cos[:, f], 1.0), jnp.where(rot, sin[:, f], 0.0)


def _with_rotary_partner_columns(w_in_l):
    half = ROT_DIM // 2
    qk = w_in_l[:, :2 * D_ATTN]
    d = (jnp.arange(2 * D_ATTN) % HEAD_DIM)[None, :]
    partner = jnp.where(d < half, -jnp.roll(qk, -half, axis=1),
                        jnp.where(d < ROT_DIM, jnp.roll(qk, half, axis=1), 0.0))
    return jnp.concatenate([w_in_l, partner], axis=1)


def _bias_placement():
    src = jnp.arange(LANES)
    hd, u, j = src >> 4, (src >> 3) & 1, src & 7
    col0 = jnp.where(hd % 2 == 0, HEAD_DIM, 0)
    dst = LANES * hd + col0 + j
    onehot = (jnp.arange(N_HEADS * LANES)[None, :] == dst[:, None]) & (u[:, None] == 1)
    return onehot.astype(jnp.bfloat16)


def kernel(x, c, w_mod, b_mod, w_in, w_pool, pool_scale, w_out, ln1_g, ln1_b,
           w_router, router_bias, w_gate, w_up, w_down, ln2_g, ln2_b):
    B, S, D = x.shape
    bf = jnp.bfloat16
    mod = _mod_call(c, w_mod, b_mod).reshape(DEPTH, B, N_MOD, D)
    cos_t, sin_t = _rope_tables(S)
    place = _bias_placement()
    w_router_t = w_router.T
    rbias_col = router_bias.reshape(N_EXPERTS, 1)
    for l in range(DEPTH):
        modl = mod[l]
        w_in_aug = _with_rotary_partner_columns(w_in[l]).astype(bf)
        qa, ka, v, m, _ = _proj_call(x, modl, w_in_aug, cos_t, sin_t, place,
                                     w_pool[l], pool_scale[l])
        a = _attn_call(qa, ka, v)
        x = _mixout_call(a, m, x, modl, w_out[l].astype(bf), ln1_g[l], ln1_b[l])
        x = _moe_routed(x, modl, w_router_t, rbias_col, w_gate, w_up, w_down, l,
                        ln2_g[l], ln2_b[l])
    return x
```

```python
import functools

import jax
import jax.numpy as jnp
from jax import lax
from jax.experimental import pallas as pl
from jax.experimental.pallas import tpu as pltpu
from jax.experimental.pallas import tpu_sc as plsc

D_MODEL = 1024
DEPTH = 2
D_ATTN = 512
D_POOL = 512
N_HEADS = 8
HEAD_DIM = 64
ROT_DIM = 16
ROPE_THETA = 500000.0
MOBA_BLOCK = 256
MOBA_TOPK = 3
POOL_GROUP = 128
N_POOL_GROUPS = 4
D_IN = 3 * D_ATTN + D_POOL
N_EXPERTS = 16
EXPERTS_PER_GROUP = 4
D_EXPERT = 512
DEEPNORM_ALPHA = (2 * DEPTH) ** 0.25
N_MOD = 6
LN_EPS = 1e-5
NEG_INF = -1e30

POOL_HALO = 16
LANES = 128
VMEM_LIMIT = 56 * 1024 * 1024

_HI = lax.Precision.HIGHEST
_NT = (((1,), (1,)), ((), ()))


def _params(sem):
    return pltpu.CompilerParams(dimension_semantics=sem, vmem_limit_bytes=VMEM_LIMIT)


def _nt_dot(a, b):
    return lax.dot_general(a, b, _NT, preferred_element_type=jnp.float32)


def _mod_kernel(c_ref, w_ref, b_ref, o_ref):
    c = c_ref[...]
    cond = c * (1.0 / (1.0 + jnp.exp(-c)))
    o_ref[0] = jnp.dot(cond, w_ref[0], precision=_HI,
                       preferred_element_type=jnp.float32) + b_ref[0]


def _mod_call(c, w_mod, b_mod):
    B = c.shape[0]
    return pl.pallas_call(
        _mod_kernel,
        grid=(DEPTH, N_MOD),
        in_specs=[
            pl.BlockSpec((B, D_MODEL), lambda l, j: (0, 0)),
            pl.BlockSpec((1, D_MODEL, D_MODEL), lambda l, j: (l, 0, j)),
            pl.BlockSpec((1, 1, D_MODEL), lambda l, j: (l, 0, j)),
        ],
        out_specs=pl.BlockSpec((1, B, D_MODEL), lambda l, j: (l, 0, j)),
        out_shape=jax.ShapeDtypeStruct((DEPTH, B, N_MOD * D_MODEL), jnp.float32),
        compiler_params=_params(("arbitrary", "arbitrary")),
    )(c, w_mod, b_mod.reshape(DEPTH, 1, N_MOD * D_MODEL))


def _layer_norm(x):
    mu = jnp.mean(x, axis=-1, keepdims=True)
    xc = x - mu
    var = jnp.mean(xc * xc, axis=-1, keepdims=True)
    return xc * lax.rsqrt(var + LN_EPS)


def _split_bf16(t):
    hi = t.astype(jnp.bfloat16)
    lo = (t - hi.astype(jnp.float32)).astype(jnp.bfloat16)
    return hi, lo


def _proj_kernel(x_ref, mod_ref, w_ref, cos_ref, sin_ref, place_ref, wp_ref, ps_ref,
                 qa_ref, ka_ref, v_ref, m_ref, kbar_ref, halo_scr):
    i = pl.program_id(1)

    @pl.when(i == 0)
    def _():
        kbar_ref[...] = jnp.zeros_like(kbar_ref)
        halo_scr[...] = jnp.zeros_like(halo_scr)

    x = x_ref[0]
    shift = mod_ref[0, 0:1, :]
    scale = mod_ref[0, 1:2, :]
    h = (_layer_norm(x) * (1.0 + scale) + shift).astype(jnp.bfloat16)

    cos = cos_ref[...]
    sin = sin_ref[...]

    def proj(c0, width):
        return jnp.dot(h, w_ref[:, c0:c0 + width], preferred_element_type=jnp.float32)

    def slab(t, s):
        return t[:, LANES * s:LANES * (s + 1)]

    n_slab = D_ATTN // LANES
    q = proj(0, D_ATTN)
    qp = proj(D_IN, D_ATTN)
    q_slabs = [(slab(q, s) * cos + slab(qp, s) * sin) * (HEAD_DIM ** -0.5)
               for s in range(n_slab)]
    k = proj(D_ATTN, D_ATTN)
    kp = proj(D_IN + D_ATTN, D_ATTN)
    k_slabs = [slab(k, s) * cos + slab(kp, s) * sin for s in range(n_slab)]
    v_ref[0] = proj(2 * D_ATTN, D_ATTN).astype(jnp.bfloat16)

    p = proj(3 * D_ATTN, D_POOL)
    ext = jnp.concatenate([halo_scr[...], p], axis=0)
    halo_scr[...] = p[MOBA_BLOCK - POOL_HALO:, :]
    t_pos = i * MOBA_BLOCK + lax.broadcasted_iota(jnp.int32, (MOBA_BLOCK, LANES), 0)
    m_slabs = []
    for g in range(N_POOL_GROUPS):
        win = slab(ext, g)
        for step in range(g + 1):
            win = win + pltpu.roll(win, 1 << step, 0)
        cnt = jnp.minimum(t_pos + 1, 2 << g).astype(jnp.float32)
        d = (win[POOL_HALO:, :] / cnt - slab(p, g)).astype(jnp.bfloat16)
        y = jnp.dot(d, wp_ref[g].astype(jnp.bfloat16), preferred_element_type=jnp.float32)
        m_slabs.append(y * slab(ps_ref[...], g))
    m_ref[0] = jnp.concatenate(m_slabs, axis=1).astype(m_ref.dtype)

    kmean = jnp.concatenate(
        [jnp.mean(ks, axis=0, keepdims=True) for ks in k_slabs], axis=1)
    kbar_ref[0, pl.ds(i, 1), :] = kmean

    kb = kbar_ref[0]
    kb_rows = jnp.concatenate([kb] * (LANES // 8), axis=0)
    r_head = lax.broadcasted_iota(jnp.int32, (LANES, D_ATTN), 1) >> 6
    c_head = lax.broadcasted_iota(jnp.int32, (LANES, D_ATTN), 0) >> 4
    kbd_hi, kbd_lo = _split_bf16(jnp.where(r_head == c_head, kb_rows, 0.0))
    q_hi, q_lo = _split_bf16(jnp.concatenate(q_slabs, axis=1))
    gate = _nt_dot(q_hi, kbd_hi) + (_nt_dot(q_lo, kbd_hi) + _nt_dot(q_hi, kbd_lo))

    lane = lax.broadcasted_iota(jnp.int32, (MOBA_BLOCK, LANES), 1)
    j_of = lane & 7
    past = j_of < i
    gm = jnp.where(past, gate, NEG_INF)
    rank = jnp.zeros((MOBA_BLOCK, LANES), jnp.int32)
    for r in range(1, 8):
        other = pltpu.roll(gm, r, 1)
        beats = (other > gm) | ((other == gm) & (j_of >= r))
        rank = rank + beats.astype(jnp.int32)
    allowed = (past & (rank < MOBA_TOPK)) | (j_of == i)
    bias = jnp.where(allowed, 0.0, NEG_INF).astype(jnp.bfloat16)
    bias_cols = jnp.dot(bias, place_ref[...], preferred_element_type=jnp.float32)

    for hd in range(N_HEADS):
        own = (lane < HEAD_DIM) if hd % 2 == 0 else (lane >= HEAD_DIM)
        col0 = HEAD_DIM if hd % 2 == 0 else 0
        qa = jnp.where(own, q_slabs[hd // 2], slab(bias_cols, hd))
        ka = jnp.where(own, k_slabs[hd // 2], jnp.where(lane == col0 + i, 1.0, 0.0))
        qa_ref[0, hd] = qa.astype(jnp.bfloat16)
        ka_ref[0, hd] = ka.astype(jnp.bfloat16)


def _proj_call(x, modl, w_in_aug, cos_t, sin_t, place, w_pool_l, pool_scale_l):
    B, S, D = x.shape
    nb = S // MOBA_BLOCK
    tm = MOBA_BLOCK
    tab = pl.BlockSpec((tm, LANES), lambda b, i: (i, 0))
    head_spec = pl.BlockSpec((1, N_HEADS, tm, LANES), lambda b, i: (b, 0, i, 0))
    return pl.pallas_call(
        _proj_kernel,
        grid=(B, nb),
        in_specs=[
            pl.BlockSpec((1, tm, D), lambda b, i: (b, i, 0)),
            pl.BlockSpec((1, N_MOD, D), lambda b, i: (b, 0, 0)),
            pl.BlockSpec((D, D_IN + 2 * D_ATTN), lambda b, i: (0, 0)),
            tab, tab,
            pl.BlockSpec((LANES, N_HEADS * LANES), lambda b, i: (0, 0)),
            pl.BlockSpec((N_POOL_GROUPS, POOL_GROUP, POOL_GROUP), lambda b, i: (0, 0, 0)),
            pl.BlockSpec((1, D_POOL), lambda b, i: (0, 0)),
        ],
        out_specs=[
            head_spec, head_spec,
            pl.BlockSpec((1, tm, D_ATTN), lambda b, i: (b, i, 0)),
            pl.BlockSpec((1, tm, D_POOL), lambda b, i: (b, i, 0)),
            pl.BlockSpec((1, nb, D_ATTN), lambda b, i: (b, 0, 0)),
        ],
        out_shape=[
            jax.ShapeDtypeStruct((B, N_HEADS, S, LANES), jnp.bfloat16),
            jax.ShapeDtypeStruct((B, N_HEADS, S, LANES), jnp.bfloat16),
            jax.ShapeDtypeStruct((B, S, D_ATTN), jnp.bfloat16),
            jax.ShapeDtypeStruct((B, S, D_POOL), jnp.bfloat16),
            jax.ShapeDtypeStruct((B, nb, D_ATTN), jnp.float32),
        ],
        scratch_shapes=[pltpu.VMEM((POOL_HALO, D_POOL), jnp.float32)],
        compiler_params=_params(("arbitrary", "arbitrary")),
    )(x, modl, w_in_aug, cos_t, sin_t, place, w_pool_l, pool_scale_l.reshape(1, D_POOL))


def _attn_kernel(qa_ref, ka_ref, v_ref, o_ref):
    nb = v_ref.shape[1] // MOBA_BLOCK
    row = lax.broadcasted_iota(jnp.int32, (MOBA_BLOCK, MOBA_BLOCK), 0)
    col = lax.broadcasted_iota(jnp.int32, (MOBA_BLOCK, MOBA_BLOCK), 1)
    causal = col <= row
    lane = lax.broadcasted_iota(jnp.int32, (MOBA_BLOCK, LANES), 1)

    for i in reversed(range(nb)):
        r0 = i * MOBA_BLOCK
        outs = []
        for hh in range(2):
            q = qa_ref[0, hh, r0:r0 + MOBA_BLOCK, :]
            s = _nt_dot(q, ka_ref[0, hh, 0:r0 + MOBA_BLOCK, :])
            s_own = jnp.where(causal, s[:, r0:], NEG_INF)
            s = jnp.concatenate([s[:, :r0], s_own], axis=1) if i > 0 else s_own
            m = jnp.max(s, axis=1, keepdims=True)
            p = jnp.exp(s - m)
            l = jnp.sum(p, axis=1, keepdims=True)
            acc = jnp.dot(p.astype(jnp.bfloat16), v_ref[0, 0:r0 + MOBA_BLOCK, :],
                          preferred_element_type=jnp.float32)
            outs.append(acc / l)
        o = jnp.where(lane < HEAD_DIM, outs[0], outs[1])
        o_ref[0, r0:r0 + MOBA_BLOCK, :] = o.astype(o_ref.dtype)


def _attn_call(qa, ka, v):
    B, _, S, _ = qa.shape
    n_pair = N_HEADS // 2
    pair_spec = pl.BlockSpec((1, 2, S, LANES), lambda b, hp: (b, hp, 0, 0))
    slab_spec = pl.BlockSpec((1, S, LANES), lambda b, hp: (b, 0, hp))
    return pl.pallas_call(
        _attn_kernel,
        grid=(B, n_pair),
        in_specs=[pair_spec, pair_spec, slab_spec],
        out_specs=slab_spec,
        out_shape=jax.ShapeDtypeStruct((B, S, D_ATTN), jnp.bfloat16),
        compiler_params=_params(("arbitrary", "arbitrary")),
    )(qa, ka, v)


def _mixout_kernel(a_ref, m_ref, x_ref, mod_ref, w_ref, g_ref, b_ref, o_ref):
    y = jnp.dot(a_ref[0], w_ref[0:D_ATTN, :], preferred_element_type=jnp.float32)
    y = y + jnp.dot(m_ref[0], w_ref[D_ATTN:, :], preferred_element_type=jnp.float32)
    gate = mod_ref[0, 2:3, :]
    z = DEEPNORM_ALPHA * x_ref[0] + gate * y
    o_ref[0] = _layer_norm(z) * g_ref[...] + b_ref[...]


def _mixout_call(a, m, x, modl, w_out_bf, ln_g, ln_b):
    B, S, D = x.shape
    tm = 1024
    vec = pl.BlockSpec((1, D), lambda b, i: (0, 0))
    return pl.pallas_call(
        _mixout_kernel,
        grid=(B, S // tm),
        in_specs=[
            pl.BlockSpec((1, tm, D_ATTN), lambda b, i: (b, i, 0)),
            pl.BlockSpec((1, tm, D_POOL), lambda b, i: (b, i, 0)),
            pl.BlockSpec((1, tm, D), lambda b, i: (b, i, 0)),
            pl.BlockSpec((1, N_MOD, D), lambda b, i: (b, 0, 0)),
            pl.BlockSpec((D, D), lambda b, i: (0, 0)),
            vec, vec,
        ],
        out_specs=pl.BlockSpec((1, tm, D), lambda b, i: (b, i, 0)),
        out_shape=jax.ShapeDtypeStruct((B, S, D), jnp.float32),
        compiler_params=_params(("arbitrary", "arbitrary")),
    )(a, m, x, modl, w_out_bf, ln_g.reshape(1, D), ln_b.reshape(1, D))


def _top2_rows(vals):
    def first_max(rows):
        m = rows[0]
        for v in rows[1:]:
            m = jnp.maximum(m, v)
        idx = jnp.full_like(m, float(len(rows) - 1))
        for k in range(len(rows) - 2, -1, -1):
            idx = jnp.where(rows[k] == m, float(k), idx)
        return m, idx

    m1, i1 = first_max(vals)
    rest = [jnp.where(i1 == float(k), -jnp.inf, v) for k, v in enumerate(vals)]
    m2, i2 = first_max(rest)
    return m1, i1, m2, i2


def _router_rows(logits_t, rb_ref):
    lg = [logits_t[e:e + 1, :] for e in range(N_EXPERTS)]
    mx = lg[0]
    for v in lg[1:]:
        mx = jnp.maximum(mx, v)
    ex = [jnp.exp(v - mx) for v in lg]
    den = ex[0]
    for v in ex[1:]:
        den = den + v
    scores = [v / den for v in ex]
    sel = [scores[e] + rb_ref[e:e + 1, :] for e in range(N_EXPERTS)]
    best_score = None
    best = None
    for g in range(N_EXPERTS // EXPERTS_PER_GROUP):
        m1, _, m2, _ = _top2_rows(sel[g * EXPERTS_PER_GROUP:(g + 1) * EXPERTS_PER_GROUP])
        gs = m1 + m2
        if g == 0:
            best_score, best = gs, jnp.zeros_like(gs)
        else:
            better = gs > best_score
            best_score = jnp.where(better, gs, best_score)
            best = jnp.where(better, float(g), best)
    masked = [jnp.where(best == float(e // EXPERTS_PER_GROUP), sel[e], NEG_INF)
              for e in range(N_EXPERTS)]
    _, i1, _, i2 = _top2_rows(masked)
    w1 = jnp.zeros_like(i1)
    w2 = jnp.zeros_like(i2)
    for e in range(N_EXPERTS):
        w1 = jnp.where(i1 == float(e), scores[e], w1)
        w2 = jnp.where(i2 == float(e), scores[e], w2)
    tot = w1 + w2
    return i1, i2, w1 / tot, w2 / tot


SC_CORES = 2
SC_SUBCORES = 16
SC_MAX_INDEX_ROWS = 128
SC_RING = 4
SC_TILE_BYTES = 64 * 1024

ROUTE_TILE = 1024
EXPERT_CHUNK = 512


def _sc_gather_rows(table, idx):
    n_rows = idx.shape[0]
    width = table.shape[1]
    workers = SC_CORES * SC_SUBCORES
    rows = min(SC_MAX_INDEX_ROWS, SC_TILE_BYTES // (4 * width))
    per_worker = n_rows // workers
    assert per_worker * workers == n_rows and per_worker % rows == 0
    n_chunk = per_worker // rows
    assert n_chunk % SC_RING == 0
    mesh = plsc.VectorSubcoreMesh(core_axis_name="c", subcore_axis_name="s")
    row_buf = pltpu.VMEM((rows, width), table.dtype)
    idx_buf = pltpu.VMEM((rows,), jnp.int32)

    @functools.partial(
        pl.kernel, mesh=mesh,
        out_type=jax.ShapeDtypeStruct((n_rows, width), table.dtype),
        scratch_types=([idx_buf] * SC_RING + [row_buf] * SC_RING
                       + [pltpu.SemaphoreType.DMA] * SC_RING),
    )
    def gather(table_hbm, idx_hbm, out_hbm, *scratch):
        idx_v = scratch[:SC_RING]
        rows_v = scratch[SC_RING:2 * SC_RING]
        sems = scratch[2 * SC_RING:]
        wid = lax.axis_index("s") * SC_CORES + lax.axis_index("c")
        base = wid * per_worker

        def gather_copy(b):
            return pltpu.make_async_copy(table_hbm.at[idx_v[b]], rows_v[b], sems[b])

        def start_gather(j, b):
            off = pl.multiple_of(base + j * rows, 8)
            pltpu.sync_copy(idx_hbm.at[pl.ds(off, rows)], idx_v[b])
            gather_copy(b).start()

        for b in range(SC_RING):
            start_gather(b, b)

        def group(g, carry):
            for b in range(SC_RING):
                j = g * SC_RING + b
                gather_copy(b).wait()
                off = pl.multiple_of(base + j * rows, 8)
                pltpu.sync_copy(rows_v[b], out_hbm.at[pl.ds(off, rows)])

                @pl.when(j + SC_RING < n_chunk)
                def _():
                    start_gather(j + SC_RING, b)
            return carry

        lax.fori_loop(0, n_chunk // SC_RING, group, 0)

    return gather(table, idx)


def _route_kernel(x_ref, mod_ref, wr_ref, rb_ref, hp_ref, meta_ref, cnt_ref, wcol_ref, carry_scr):
    t = pl.program_id(0)
    T = ROUTE_TILE

    @pl.when(t == 0)
    def _():
        carry_scr[...] = jnp.zeros_like(carry_scr)

    shift = mod_ref[0, 3:4, :]
    scale = mod_ref[0, 4:5, :]
    h = _layer_norm(x_ref[0]) * (1.0 + scale) + shift
    h_hi = h.astype(jnp.bfloat16)
    h_lo = (h - h_hi.astype(jnp.float32)).astype(jnp.bfloat16)
    hp_ref[...] = h

    wr_hi, wr_lo = _split_bf16(wr_ref[...])
    logits_t = _nt_dot(wr_hi, h_hi) + (_nt_dot(wr_lo, h_hi) + _nt_dot(wr_hi, h_lo))
    i1, i2, w1, w2 = _router_rows(logits_t, rb_ref)

    e_iota = lax.broadcasted_iota(jnp.int32, (N_EXPERTS, T), 0).astype(jnp.float32)
    sel1 = e_iota == i1
    sel2 = e_iota == i2
    onehot = jnp.where(sel1 | sel2, 1.0, 0.0)
    tr = lax.broadcasted_iota(jnp.int32, (T, T), 0)
    tc = lax.broadcasted_iota(jnp.int32, (T, T), 1)
    earlier = jnp.where(tr < tc, 1.0, 0.0).astype(jnp.bfloat16)
    rank = jnp.dot(onehot.astype(jnp.bfloat16), earlier,
                   preferred_element_type=jnp.float32) + carry_scr[:, 0:1]
    rank1 = jnp.sum(jnp.where(sel1, rank, 0.0), axis=0, keepdims=True)
    rank2 = jnp.sum(jnp.where(sel2, rank, 0.0), axis=0, keepdims=True)
    for r, row in enumerate((i1, i2, rank1, rank2)):
        meta_ref[r, 0] = row
    carry_scr[...] = carry_scr[...] + jnp.sum(onehot, axis=1, keepdims=True)
    cnt_ref[...] = carry_scr[...]

    r128 = lax.broadcasted_iota(jnp.int32, (LANES, T), 0)
    terms = jnp.zeros((LANES, T), jnp.float32)
    k = 0
    for w in (w1, w2):
        rest = w
        for _ in range(3):
            part = rest.astype(jnp.bfloat16).astype(jnp.float32)
            terms = jnp.where(r128 == k, part, terms)
            rest = rest - part
            k += 1
    eye = jnp.where(tr == tc, 1.0, 0.0).astype(jnp.bfloat16)
    wcol_ref[...] = _nt_dot(eye, terms.astype(jnp.bfloat16))


def _route_call(x, modl, w_router_t, rbias_col):
    B, S, D = x.shape
    tm = ROUTE_TILE
    nt = S // tm
    n_tiles = B * nt
    return pl.pallas_call(
        _route_kernel,
        grid=(n_tiles,),
        in_specs=[
            pl.BlockSpec((1, tm, D), lambda t: (t // nt, t % nt, 0)),
            pl.BlockSpec((1, N_MOD, D), lambda t: (t // nt, 0, 0)),
            pl.BlockSpec((N_EXPERTS, D), lambda t: (0, 0)),
            pl.BlockSpec((N_EXPERTS, 1), lambda t: (0, 0)),
        ],
        out_specs=[
            pl.BlockSpec((tm, D), lambda t: (t, 0)),
            pl.BlockSpec((4, 1, 1, tm), lambda t: (0, t, 0, 0)),
            pl.BlockSpec((N_EXPERTS, LANES), lambda t: (0, 0)),
            pl.BlockSpec((tm, LANES), lambda t: (t, 0)),
        ],
        out_shape=[
            jax.ShapeDtypeStruct((B * S, D), jnp.float32),
            jax.ShapeDtypeStruct((4, n_tiles, 1, tm), jnp.float32),
            jax.ShapeDtypeStruct((N_EXPERTS, LANES), jnp.float32),
            jax.ShapeDtypeStruct((B * S, LANES), jnp.float32),
        ],
        scratch_shapes=[pltpu.VMEM((N_EXPERTS, LANES), jnp.float32)],
        compiler_params=_params(("arbitrary",)),
    )(x, modl, w_router_t, rbias_col)


def _dispatch_plan(meta, counts, n_tokens):
    n_rows = 2 * n_tokens + N_EXPERTS * EXPERT_CHUNK
    n_chunks = n_rows // EXPERT_CHUNK
    meta = meta.reshape(meta.shape[0], n_tokens).astype(jnp.int32)
    per_token = lambda r: meta[r]
    e1 = per_token(0)
    e2 = per_token(1)
    cnt = counts[:, 0].astype(jnp.int32)
    padded = (cnt + (EXPERT_CHUNK - 1)) // EXPERT_CHUNK * EXPERT_CHUNK
    ends = jnp.cumsum(padded)
    starts = ends - padded
    first = jnp.cumsum(cnt) - cnt
    pos1 = starts[e1] + per_token(2)
    pos2 = starts[e2] + per_token(3)
    order = jnp.argsort(jnp.concatenate([pos1, pos2])).astype(jnp.int32)
    sorted_tok = jnp.where(order >= n_tokens, order - n_tokens, order)
    chunk_row0 = jnp.arange(n_chunks, dtype=jnp.int32) * EXPERT_CHUNK
    chunk_expert = jnp.minimum(
        jnp.sum((ends[None, :] <= chunk_row0[:, None]).astype(jnp.int32), axis=1), N_EXPERTS - 1)
    local = (chunk_row0 - starts[chunk_expert])[:, None] + jnp.arange(EXPERT_CHUNK, dtype=jnp.int32)
    real = local < cnt[chunk_expert][:, None]
    nth = jnp.clip(first[chunk_expert][:, None] + local, 0, 2 * n_tokens - 1)
    spread = (chunk_row0[:, None] + jnp.arange(EXPERT_CHUNK, dtype=jnp.int32)) % n_tokens
    src = jnp.where(real, sorted_tok[nth], spread).reshape(n_rows)
    n_used = (ends[-1] // EXPERT_CHUNK).reshape(1).astype(jnp.int32)
    return pos1, pos2, src, chunk_expert, n_used


def _expert_kernel(ce_ref, nu_ref, xs_ref, wg_ref, wu_ref, wd_ref, o_ref, wg_bf, wu_bf, wd_bf):
    c = pl.program_id(0)

    @pl.when(c < nu_ref[0])
    def _():
        @pl.when((c == 0) | (ce_ref[c] != ce_ref[jnp.maximum(c - 1, 0)]))
        def _():
            wg_bf[...] = wg_ref[0, 0].astype(jnp.bfloat16)
            wu_bf[...] = wu_ref[0, 0].astype(jnp.bfloat16)
            wd_bf[...] = wd_ref[0, 0].astype(jnp.bfloat16)

        xs = xs_ref[...].astype(jnp.bfloat16)
        gt = jnp.dot(xs, wg_bf[...], preferred_element_type=jnp.float32)
        up = jnp.dot(xs, wu_bf[...], preferred_element_type=jnp.float32)
        act = (gt * (1.0 / (1.0 + jnp.exp(-gt))) * up).astype(jnp.bfloat16)
        o_ref[...] = jnp.dot(act, wd_bf[...], preferred_element_type=jnp.float32)

    @pl.when(c >= nu_ref[0])
    def _():
        o_ref[...] = jnp.zeros_like(o_ref)


def _expert_call(chunk_expert, n_used, xs, w_gate, w_up, w_down, layer):
    n_rows, D = xs.shape
    grid_spec = pltpu.PrefetchScalarGridSpec(
        num_scalar_prefetch=2,
        grid=(n_rows // EXPERT_CHUNK,),
        in_specs=[
            pl.BlockSpec((EXPERT_CHUNK, D), lambda c, ce, nu: (c, 0)),
            pl.BlockSpec((1, 1, D, D_EXPERT), lambda c, ce, nu: (layer, ce[c], 0, 0)),
            pl.BlockSpec((1, 1, D, D_EXPERT), lambda c, ce, nu: (layer, ce[c], 0, 0)),
            pl.BlockSpec((1, 1, D_EXPERT, D), lambda c, ce, nu: (layer, ce[c], 0, 0)),
        ],
        out_specs=pl.BlockSpec((EXPERT_CHUNK, D), lambda c, ce, nu: (c, 0)),
        scratch_shapes=[
            pltpu.VMEM((D, D_EXPERT), jnp.bfloat16),
            pltpu.VMEM((D, D_EXPERT), jnp.bfloat16),
            pltpu.VMEM((D_EXPERT, D), jnp.bfloat16),
        ],
    )
    return pl.pallas_call(
        _expert_kernel,
        grid_spec=grid_spec,
        out_shape=jax.ShapeDtypeStruct((n_rows, D), jnp.float32),
        compiler_params=_params(("arbitrary",)),
    )(chunk_expert, n_used, xs, w_gate, w_up, w_down)


def _combine_kernel(y1_ref, y2_ref, wcol_ref, x_ref, mod_ref, g_ref, b_ref, o_ref):
    wc = wcol_ref[0]
    w1 = (wc[:, 0:1] + wc[:, 1:2]) + wc[:, 2:3]
    w2 = (wc[:, 3:4] + wc[:, 4:5]) + wc[:, 5:6]
    y = w1 * y1_ref[0] + w2 * y2_ref[0]
    gate = mod_ref[0, 5:6, :]
    z = DEEPNORM_ALPHA * x_ref[0] + gate * y
    o_ref[0] = _layer_norm(z) * g_ref[...] + b_ref[...]


def _combine_call(y1, y2, wcol, x, modl, ln_g, ln_b):
    B, S, D = x.shape
    tm = 1024
    rows = pl.BlockSpec((1, tm, D), lambda b, i: (b, i, 0))
    vec = pl.BlockSpec((1, D), lambda b, i: (0, 0))
    return pl.pallas_call(
        _combine_kernel,
        grid=(B, S // tm),
        in_specs=[rows, rows, pl.BlockSpec((1, tm, LANES), lambda b, i: (b, i, 0)), rows,
                  pl.BlockSpec((1, N_MOD, D), lambda b, i: (b, 0, 0)), vec, vec],
        out_specs=rows,
        out_shape=jax.ShapeDtypeStruct((B, S, D), jnp.float32),
        compiler_params=_params(("arbitrary", "arbitrary")),
    )(y1.reshape(B, S, D), y2.reshape(B, S, D), wcol.reshape(B, S, LANES),
      x, modl, ln_g.reshape(1, D), ln_b.reshape(1, D))


def _moe_routed(x, modl, w_router_t, rbias_col, w_gate, w_up, w_down, layer, ln_g, ln_b):
    B, S, D = x.shape
    h, meta, counts, wcol = _route_call(x, modl, w_router_t, rbias_col)
    pos1, pos2, src, chunk_expert, n_used = _dispatch_plan(meta, counts, B * S)
    xs = _sc_gather_rows(h, src)
    ys = _expert_call(chunk_expert, n_used, xs, w_gate, w_up, w_down, layer)
    y1 = _sc_gather_rows(ys, pos1)
    y2 = _sc_gather_rows(ys, pos2)
    return _combine_call(y1, y2, wcol, x, modl, ln_g, ln_b)


def _rope_tables(S):
    half = ROT_DIM // 2
    inv_freq = ROPE_THETA ** (-(jnp.arange(half, dtype=jnp.float32) * 2.0 / ROT_DIM))
    ang = jnp.arange(S, dtype=jnp.float32)[:, None] * inv_freq[None, :]
    cos, sin = jnp.cos(ang), jnp.sin(ang)
    d = jnp.arange(LANES) % HEAD_DIM
    f = d % half
    rot = d[None, :] < ROT_DIM
    return jnp.where(rot, cos[:, f], 1.0), jnp.where(rot, sin[:, f], 0.0)


def _with_rotary_partner_columns(w_in_l):
    half = ROT_DIM // 2
    qk = w_in_l[:, :2 * D_ATTN]
    d = (jnp.arange(2 * D_ATTN) % HEAD_DIM)[None, :]
    partner = jnp.where(d < half, -jnp.roll(qk, -half, axis=1),
                        jnp.where(d < ROT_DIM, jnp.roll(qk, half, axis=1), 0.0))
    return jnp.concatenate([w_in_l, partner], axis=1)


def _bias_placement():
    src = jnp.arange(LANES)
    hd, u, j = src >> 4, (src >> 3) & 1, src & 7
    col0 = jnp.where(hd % 2 == 0, HEAD_DIM, 0)
    dst = LANES * hd + col0 + j
    onehot = (jnp.arange(N_HEADS * LANES)[None, :] == dst[:, None]) & (u[:, None] == 1)
    return onehot.astype(jnp.bfloat16)


def kernel(x, c, w_mod, b_mod, w_in, w_pool, pool_scale, w_out, ln1_g, ln1_b,
           w_router, router_bias, w_gate, w_up, w_down, ln2_g, ln2_b):
    B, S, D = x.shape
    bf = jnp.bfloat16
    mod = _mod_call(c, w_mod, b_mod).reshape(DEPTH, B, N_MOD, D)
    cos_t, sin_t = _rope_tables(S)
    place = _bias_placement()
    w_router_t = w_router.T
    rbias_col = router_bias.reshape(N_EXPERTS, 1)
    for l in range(DEPTH):
        modl = mod[l]
        w_in_aug = _with_rotary_partner_columns(w_in[l]).astype(bf)
        qa, ka, v, m, _ = _proj_call(x, modl, w_in_aug, cos_t, sin_t, place,
                                     w_pool[l], pool_scale[l])
        a = _attn_call(qa, ka, v)
        x = _mixout_call(a, m, x, modl, w_out[l].astype(bf), ln1_g[l], ln1_b[l])
        x = _moe_routed(x, modl, w_router_t, rbias_col, w_gate, w_up, w_down, l,
                        ln2_g[l], ln2_b[l])
    return x
```

```python
import functools

import jax
import jax.numpy as jnp
from jax import lax
from jax.experimental import pallas as pl
from jax.experimental.pallas import tpu as pltpu
from jax.experimental.pallas import tpu_sc as plsc

D_MODEL = 1024
DEPTH = 2
D_ATTN = 512
D_POOL = 512
N_HEADS = 8
HEAD_DIM = 64
ROT_DIM = 16
ROPE_THETA = 500000.0
MOBA_BLOCK = 256
MOBA_TOPK = 3
POOL_GROUP = 128
N_POOL_GROUPS = 4
D_IN = 3 * D_ATTN + D_POOL
N_EXPERTS = 16
EXPERTS_PER_GROUP = 4
D_EXPERT = 512
DEEPNORM_ALPHA = (2 * DEPTH) ** 0.25
N_MOD = 6
LN_EPS = 1e-5
NEG_INF = -1e30

POOL_HALO = 16
LANES = 128
VMEM_LIMIT = 56 * 1024 * 1024

_HI = lax.Precision.HIGHEST
_NT = (((1,), (1,)), ((), ()))


def _params(sem):
    return pltpu.CompilerParams(dimension_semantics=sem, vmem_limit_bytes=VMEM_LIMIT)


def _nt_dot(a, b):
    return lax.dot_general(a, b, _NT, preferred_element_type=jnp.float32)


def _mod_kernel(c_ref, w_ref, b_ref, o_ref):
    c = c_ref[...]
    cond = c * (1.0 / (1.0 + jnp.exp(-c)))
    o_ref[0] = jnp.dot(cond, w_ref[0], precision=_HI,
                       preferred_element_type=jnp.float32) + b_ref[0]


def _mod_call(c, w_mod, b_mod):
    B = c.shape[0]
    return pl.pallas_call(
        _mod_kernel,
        grid=(DEPTH, N_MOD),
        in_specs=[
            pl.BlockSpec((B, D_MODEL), lambda l, j: (0, 0)),
            pl.BlockSpec((1, D_MODEL, D_MODEL), lambda l, j: (l, 0, j)),
            pl.BlockSpec((1, 1, D_MODEL), lambda l, j: (l, 0, j)),
        ],
        out_specs=pl.BlockSpec((1, B, D_MODEL), lambda l, j: (l, 0, j)),
        out_shape=jax.ShapeDtypeStruct((DEPTH, B, N_MOD * D_MODEL), jnp.float32),
        compiler_params=_params(("arbitrary", "arbitrary")),
    )(c, w_mod, b_mod.reshape(DEPTH, 1, N_MOD * D_MODEL))


def _layer_norm(x):
    mu = jnp.mean(x, axis=-1, keepdims=True)
    xc = x - mu
    var = jnp.mean(xc * xc, axis=-1, keepdims=True)
    return xc * lax.rsqrt(var + LN_EPS)


def _split_bf16(t):
    hi = t.astype(jnp.bfloat16)
    lo = (t - hi.astype(jnp.float32)).astype(jnp.bfloat16)
    return hi, lo


def _proj_kernel(x_ref, mod_ref, w_ref, cos_ref, sin_ref, place_ref, wp_ref, ps_ref,
                 qa_ref, ka_ref, v_ref, m_ref, kbar_ref, halo_scr):
    i = pl.program_id(1)

    @pl.when(i == 0)
    def _():
        kbar_ref[...] = jnp.zeros_like(kbar_ref)
        halo_scr[...] = jnp.zeros_like(halo_scr)

    x = x_ref[0]
    shift = mod_ref[0, 0:1, :]
    scale = mod_ref[0, 1:2, :]
    h = (_layer_norm(x) * (1.0 + scale) + shift).astype(jnp.bfloat16)

    cos = cos_ref[...]
    sin = sin_ref[...]

    def proj(c0, width):
        return jnp.dot(h, w_ref[:, c0:c0 + width], preferred_element_type=jnp.float32)

    def slab(t, s):
        return t[:, LANES * s:LANES * (s + 1)]

    n_slab = D_ATTN // LANES
    q = proj(0, D_ATTN)
    qp = proj(D_IN, D_ATTN)
    q_slabs = [(slab(q, s) * cos + slab(qp, s) * sin) * (HEAD_DIM ** -0.5)
               for s in range(n_slab)]
    k = proj(D_ATTN, D_ATTN)
    kp = proj(D_IN + D_ATTN, D_ATTN)
    k_slabs = [slab(k, s) * cos + slab(kp, s) * sin for s in range(n_slab)]
    v_ref[0] = proj(2 * D_ATTN, D_ATTN).astype(jnp.bfloat16)

    p = proj(3 * D_ATTN, D_POOL)
    ext = jnp.concatenate([halo_scr[...], p], axis=0)
    halo_scr[...] = p[MOBA_BLOCK - POOL_HALO:, :]
    t_pos = i * MOBA_BLOCK + lax.broadcasted_iota(jnp.int32, (MOBA_BLOCK, LANES), 0)
    m_slabs = []
    for g in range(N_POOL_GROUPS):
        win = slab(ext, g)
        for step in range(g + 1):
            win = win + pltpu.roll(win, 1 << step, 0)
        cnt = jnp.minimum(t_pos + 1, 2 << g).astype(jnp.float32)
        d = (win[POOL_HALO:, :] / cnt - slab(p, g)).astype(jnp.bfloat16)
        y = jnp.dot(d, wp_ref[g].astype(jnp.bfloat16), preferred_element_type=jnp.float32)
        m_slabs.append(y * slab(ps_ref[...], g))
    m_ref[0] = jnp.concatenate(m_slabs, axis=1).astype(m_ref.dtype)

    kmean = jnp.concatenate(
        [jnp.mean(ks, axis=0, keepdims=True) for ks in k_slabs], axis=1)
    kbar_ref[0, pl.ds(i, 1), :] = kmean

    kb = kbar_ref[0]
    kb_rows = jnp.concatenate([kb] * (LANES // 8), axis=0)
    r_head = lax.broadcasted_iota(jnp.int32, (LANES, D_ATTN), 1) >> 6
    c_head = lax.broadcasted_iota(jnp.int32, (LANES, D_ATTN), 0) >> 4
    kbd_hi, kbd_lo = _split_bf16(jnp.where(r_head == c_head, kb_rows, 0.0))
    q_hi, q_lo = _split_bf16(jnp.concatenate(q_slabs, axis=1))
    gate = _nt_dot(q_hi, kbd_hi) + (_nt_dot(q_lo, kbd_hi) + _nt_dot(q_hi, kbd_lo))

    lane = lax.broadcasted_iota(jnp.int32, (MOBA_BLOCK, LANES), 1)
    j_of = lane & 7
    past = j_of < i
    gm = jnp.where(past, gate, NEG_INF)
    rank = jnp.zeros((MOBA_BLOCK, LANES), jnp.int32)
    for r in range(1, 8):
        other = pltpu.roll(gm, r, 1)
        beats = (other > gm) | ((other == gm) & (j_of >= r))
        rank = rank + beats.astype(jnp.int32)
    allowed = (past & (rank < MOBA_TOPK)) | (j_of == i)
    bias = jnp.where(allowed, 0.0, NEG_INF).astype(jnp.bfloat16)
    bias_cols = jnp.dot(bias, place_ref[...], preferred_element_type=jnp.float32)

    for hd in range(N_HEADS):
        own = (lane < HEAD_DIM) if hd % 2 == 0 else (lane >= HEAD_DIM)
        col0 = HEAD_DIM if hd % 2 == 0 else 0
        qa = jnp.where(own, q_slabs[hd // 2], slab(bias_cols, hd))
        ka = jnp.where(own, k_slabs[hd // 2], jnp.where(lane == col0 + i, 1.0, 0.0))
        qa_ref[0, hd] = qa.astype(jnp.bfloat16)
        ka_ref[0, hd] = ka.astype(jnp.bfloat16)


def _proj_call(x, modl, w_in_aug, cos_t, sin_t, place, w_pool_l, pool_scale_l):
    B, S, D = x.shape
    nb = S // MOBA_BLOCK
    tm = MOBA_BLOCK
    tab = pl.BlockSpec((tm, LANES), lambda b, i: (i, 0))
    head_spec = pl.BlockSpec((1, N_HEADS, tm, LANES), lambda b, i: (b, 0, i, 0))
    return pl.pallas_call(
        _proj_kernel,
        grid=(B, nb),
        in_specs=[
            pl.BlockSpec((1, tm, D), lambda b, i: (b, i, 0)),
            pl.BlockSpec((1, N_MOD, D), lambda b, i: (b, 0, 0)),
            pl.BlockSpec((D, D_IN + 2 * D_ATTN), lambda b, i: (0, 0)),
            tab, tab,
            pl.BlockSpec((LANES, N_HEADS * LANES), lambda b, i: (0, 0)),
            pl.BlockSpec((N_POOL_GROUPS, POOL_GROUP, POOL_GROUP), lambda b, i: (0, 0, 0)),
            pl.BlockSpec((1, D_POOL), lambda b, i: (0, 0)),
        ],
        out_specs=[
            head_spec, head_spec,
            pl.BlockSpec((1, tm, D_ATTN), lambda b, i: (b, i, 0)),
            pl.BlockSpec((1, tm, D_POOL), lambda b, i: (b, i, 0)),
            pl.BlockSpec((1, nb, D_ATTN), lambda b, i: (b, 0, 0)),
        ],
        out_shape=[
            jax.ShapeDtypeStruct((B, N_HEADS, S, LANES), jnp.bfloat16),
            jax.ShapeDtypeStruct((B, N_HEADS, S, LANES), jnp.bfloat16),
            jax.ShapeDtypeStruct((B, S, D_ATTN), jnp.bfloat16),
            jax.ShapeDtypeStruct((B, S, D_POOL), jnp.bfloat16),
            jax.ShapeDtypeStruct((B, nb, D_ATTN), jnp.float32),
        ],
        scratch_shapes=[pltpu.VMEM((POOL_HALO, D_POOL), jnp.float32)],
        compiler_params=_params(("arbitrary", "arbitrary")),
    )(x, modl, w_in_aug, cos_t, sin_t, place, w_pool_l, pool_scale_l.reshape(1, D_POOL))


def _attn_kernel(qa_ref, ka_ref, v_ref, o_ref):
    nb = v_ref.shape[1] // MOBA_BLOCK
    row = lax.broadcasted_iota(jnp.int32, (MOBA_BLOCK, MOBA_BLOCK), 0)
    col = lax.broadcasted_iota(jnp.int32, (MOBA_BLOCK, MOBA_BLOCK), 1)
    causal = col <= row
    lane = lax.broadcasted_iota(jnp.int32, (MOBA_BLOCK, LANES), 1)

    for i in reversed(range(nb)):
        r0 = i * MOBA_BLOCK
        outs = []
        for hh in range(2):
            q = qa_ref[0, hh, r0:r0 + MOBA_BLOCK, :]
            s = _nt_dot(q, ka_ref[0, hh, 0:r0 + MOBA_BLOCK, :])
            s_own = jnp.where(causal, s[:, r0:], NEG_INF)
            s = jnp.concatenate([s[:, :r0], s_own], axis=1) if i > 0 else s_own
            m = jnp.max(s, axis=1, keepdims=True)
            p = jnp.exp(s - m)
            l = jnp.sum(p, axis=1, keepdims=True)
            acc = jnp.dot(p.astype(jnp.bfloat16), v_ref[0, 0:r0 + MOBA_BLOCK, :],
                          preferred_element_type=jnp.float32)
            outs.append(acc / l)
        o = jnp.where(lane < HEAD_DIM, outs[0], outs[1])
        o_ref[0, r0:r0 + MOBA_BLOCK, :] = o.astype(o_ref.dtype)


def _attn_call(qa, ka, v):
    B, _, S, _ = qa.shape
    n_pair = N_HEADS // 2
    pair_spec = pl.BlockSpec((1, 2, S, LANES), lambda b, hp: (b, hp, 0, 0))
    slab_spec = pl.BlockSpec((1, S, LANES), lambda b, hp: (b, 0, hp))
    return pl.pallas_call(
        _attn_kernel,
        grid=(B, n_pair),
        in_specs=[pair_spec, pair_spec, slab_spec],
        out_specs=slab_spec,
        out_shape=jax.ShapeDtypeStruct((B, S, D_ATTN), jnp.bfloat16),
        compiler_params=_params(("arbitrary", "arbitrary")),
    )(qa, ka, v)


def _mixout_kernel(a_ref, m_ref, x_ref, mod_ref, w_ref, g_ref, b_ref, o_ref):
    y = jnp.dot(a_ref[0], w_ref[0:D_ATTN, :], preferred_element_type=jnp.float32)
    y = y + jnp.dot(m_ref[0], w_ref[D_ATTN:, :], preferred_element_type=jnp.float32)
    gate = mod_ref[0, 2:3, :]
    z = DEEPNORM_ALPHA * x_ref[0] + gate * y
    o_ref[0] = _layer_norm(z) * g_ref[...] + b_ref[...]


def _mixout_call(a, m, x, modl, w_out_bf, ln_g, ln_b):
    B, S, D = x.shape
    tm = 1024
    vec = pl.BlockSpec((1, D), lambda b, i: (0, 0))
    return pl.pallas_call(
        _mixout_kernel,
        grid=(B, S // tm),
        in_specs=[
            pl.BlockSpec((1, tm, D_ATTN), lambda b, i: (b, i, 0)),
            pl.BlockSpec((1, tm, D_POOL), lambda b, i: (b, i, 0)),
            pl.BlockSpec((1, tm, D), lambda b, i: (b, i, 0)),
            pl.BlockSpec((1, N_MOD, D), lambda b, i: (b, 0, 0)),
            pl.BlockSpec((D, D), lambda b, i: (0, 0)),
            vec, vec,
        ],
        out_specs=pl.BlockSpec((1, tm, D), lambda b, i: (b, i, 0)),
        out_shape=jax.ShapeDtypeStruct((B, S, D), jnp.float32),
        compiler_params=_params(("arbitrary", "arbitrary")),
    )(a, m, x, modl, w_out_bf, ln_g.reshape(1, D), ln_b.reshape(1, D))


def _top2_rows(vals):
    def first_max(rows):
        m = rows[0]
        for v in rows[1:]:
            m = jnp.maximum(m, v)
        idx = jnp.full_like(m, float(len(rows) - 1))
        for k in range(len(rows) - 2, -1, -1):
            idx = jnp.where(rows[k] == m, float(k), idx)
        return m, idx

    m1, i1 = first_max(vals)
    rest = [jnp.where(i1 == float(k), -jnp.inf, v) for k, v in enumerate(vals)]
    m2, i2 = first_max(rest)
    return m1, i1, m2, i2


def _router_rows(logits_t, rb_ref):
    lg = [logits_t[e:e + 1, :] for e in range(N_EXPERTS)]
    mx = lg[0]
    for v in lg[1:]:
        mx = jnp.maximum(mx, v)
    ex = [jnp.exp(v - mx) for v in lg]
    den = ex[0]
    for v in ex[1:]:
        den = den + v
    scores = [v / den for v in ex]
    sel = [scores[e] + rb_ref[e:e + 1, :] for e in range(N_EXPERTS)]
    best_score = None
    best = None
    for g in range(N_EXPERTS // EXPERTS_PER_GROUP):
        m1, _, m2, _ = _top2_rows(sel[g * EXPERTS_PER_GROUP:(g + 1) * EXPERTS_PER_GROUP])
        gs = m1 + m2
        if g == 0:
            best_score, best = gs, jnp.zeros_like(gs)
        else:
            better = gs > best_score
            best_score = jnp.where(better, gs, best_score)
            best = jnp.where(better, float(g), best)
    masked = [jnp.where(best == float(e // EXPERTS_PER_GROUP), sel[e], NEG_INF)
              for e in range(N_EXPERTS)]
    _, i1, _, i2 = _top2_rows(masked)
    w1 = jnp.zeros_like(i1)
    w2 = jnp.zeros_like(i2)
    for e in range(N_EXPERTS):
        w1 = jnp.where(i1 == float(e), scores[e], w1)
        w2 = jnp.where(i2 == float(e), scores[e], w2)
    tot = w1 + w2
    return i1, i2, w1 / tot, w2 / tot


SC_CORES = 2
SC_SUBCORES = 16
SC_LANES = 16
SC_MAX_INDEX_ROWS = 128
SC_RING = 4
SC_TILE_BYTES = 64 * 1024

ROUTE_TILE = 1024
EXPERT_CHUNK = 512


def _sc_gather_rows(table, idx):
    n_rows = idx.shape[0]
    width = table.shape[1]
    workers = SC_CORES * SC_SUBCORES
    rows = min(SC_MAX_INDEX_ROWS, SC_TILE_BYTES // (4 * width))
    per_worker = n_rows // workers
    assert per_worker * workers == n_rows and per_worker % rows == 0 and rows % SC_LANES == 0
    n_chunk = per_worker // rows
    assert n_chunk % SC_RING == 0
    mesh = plsc.VectorSubcoreMesh(core_axis_name="c", subcore_axis_name="s")
    row_buf = pltpu.VMEM((rows, width), table.dtype)
    idx_buf = pltpu.VMEM((rows,), jnp.int32)

    @functools.partial(
        pl.kernel, mesh=mesh,
        out_type=jax.ShapeDtypeStruct((n_rows, width), table.dtype),
        scratch_types=([idx_buf] * SC_RING + [row_buf] * SC_RING
                       + [pltpu.SemaphoreType.DMA] * SC_RING
                       + [pltpu.VMEM((per_worker,), jnp.int32)]),
    )
    def gather(table_hbm, idx_hbm, out_hbm, *scratch):
        idx_v = scratch[:SC_RING]
        rows_v = scratch[SC_RING:2 * SC_RING]
        sems = scratch[2 * SC_RING:3 * SC_RING]
        idx_all = scratch[3 * SC_RING]
        wid = lax.axis_index("s") * SC_CORES + lax.axis_index("c")
        base = wid * per_worker
        pltpu.sync_copy(idx_hbm.at[pl.ds(pl.multiple_of(base, 8), per_worker)], idx_all)

        def gather_copy(b):
            return pltpu.make_async_copy(table_hbm.at[idx_v[b]], rows_v[b], sems[b])

        def start_gather(j, b):
            for part in range(rows // SC_LANES):
                idx_v[b][pl.ds(part * SC_LANES, SC_LANES)] = (
                    idx_all[pl.ds(j * rows + part * SC_LANES, SC_LANES)])
            gather_copy(b).start()

        for b in range(SC_RING):
            start_gather(b, b)

        def group(g, carry):
            for b in range(SC_RING):
                j = g * SC_RING + b
                gather_copy(b).wait()
                off = pl.multiple_of(base + j * rows, 8)
                pltpu.sync_copy(rows_v[b], out_hbm.at[pl.ds(off, rows)])

                @pl.when(j + SC_RING < n_chunk)
                def _():
                    start_gather(j + SC_RING, b)
            return carry

        lax.fori_loop(0, n_chunk // SC_RING, group, 0)

    return gather(table, idx)


def _route_kernel(x_ref, mod_ref, wr_ref, rb_ref, hp_ref, meta_ref, cnt_ref, wcol_ref, carry_scr):
    t = pl.program_id(0)
    T = ROUTE_TILE

    @pl.when(t == 0)
    def _():
        carry_scr[...] = jnp.zeros_like(carry_scr)

    shift = mod_ref[0, 3:4, :]
    scale = mod_ref[0, 4:5, :]
    h = _layer_norm(x_ref[0]) * (1.0 + scale) + shift
    h_hi = h.astype(jnp.bfloat16)
    h_lo = (h - h_hi.astype(jnp.float32)).astype(jnp.bfloat16)
    hp_ref[...] = h

    wr_hi, wr_lo = _split_bf16(wr_ref[...])
    logits_t = _nt_dot(wr_hi, h_hi) + (_nt_dot(wr_lo, h_hi) + _nt_dot(wr_hi, h_lo))
    i1, i2, w1, w2 = _router_rows(logits_t, rb_ref)

    e_iota = lax.broadcasted_iota(jnp.int32, (N_EXPERTS, T), 0).astype(jnp.float32)
    sel1 = e_iota == i1
    sel2 = e_iota == i2
    onehot = jnp.where(sel1 | sel2, 1.0, 0.0)
    tr = lax.broadcasted_iota(jnp.int32, (T, T), 0)
    tc = lax.broadcasted_iota(jnp.int32, (T, T), 1)
    earlier = jnp.where(tr < tc, 1.0, 0.0).astype(jnp.bfloat16)
    rank = jnp.dot(onehot.astype(jnp.bfloat16), earlier,
                   preferred_element_type=jnp.float32) + carry_scr[:, 0:1]
    rank1 = jnp.sum(jnp.where(sel1, rank, 0.0), axis=0, keepdims=True)
    rank2 = jnp.sum(jnp.where(sel2, rank, 0.0), axis=0, keepdims=True)
    for r, row in enumerate((i1, i2, rank1, rank2)):
        meta_ref[r, 0] = row
    carry_scr[...] = carry_scr[...] + jnp.sum(onehot, axis=1, keepdims=True)
    cnt_ref[...] = carry_scr[...]

    r128 = lax.broadcasted_iota(jnp.int32, (LANES, T), 0)
    terms = jnp.zeros((LANES, T), jnp.float32)
    k = 0
    for w in (w1, w2):
        rest = w
        for _ in range(3):
            part = rest.astype(jnp.bfloat16).astype(jnp.float32)
            terms = jnp.where(r128 == k, part, terms)
            rest = rest - part
            k += 1
    eye = jnp.where(tr == tc, 1.0, 0.0).astype(jnp.bfloat16)
    wcol_ref[...] = _nt_dot(eye, terms.astype(jnp.bfloat16))


def _route_call(x, modl, w_router_t, rbias_col):
    B, S, D = x.shape
    tm = ROUTE_TILE
    nt = S // tm
    n_tiles = B * nt
    return pl.pallas_call(
        _route_kernel,
        grid=(n_tiles,),
        in_specs=[
            pl.BlockSpec((1, tm, D), lambda t: (t // nt, t % nt, 0)),
            pl.BlockSpec((1, N_MOD, D), lambda t: (t // nt, 0, 0)),
            pl.BlockSpec((N_EXPERTS, D), lambda t: (0, 0)),
            pl.BlockSpec((N_EXPERTS, 1), lambda t: (0, 0)),
        ],
        out_specs=[
            pl.BlockSpec((tm, D), lambda t: (t, 0)),
            pl.BlockSpec((4, 1, 1, tm), lambda t: (0, t, 0, 0)),
            pl.BlockSpec((N_EXPERTS, LANES), lambda t: (0, 0)),
            pl.BlockSpec((tm, LANES), lambda t: (t, 0)),
        ],
        out_shape=[
            jax.ShapeDtypeStruct((B * S, D), jnp.float32),
            jax.ShapeDtypeStruct((4, n_tiles, 1, tm), jnp.float32),
            jax.ShapeDtypeStruct((N_EXPERTS, LANES), jnp.float32),
            jax.ShapeDtypeStruct((B * S, LANES), jnp.float32),
        ],
        scratch_shapes=[pltpu.VMEM((N_EXPERTS, LANES), jnp.float32)],
        compiler_params=_params(("arbitrary",)),
    )(x, modl, w_router_t, rbias_col)


def _dispatch_plan(meta, counts, n_tokens):
    n_rows = 2 * n_tokens + N_EXPERTS * EXPERT_CHUNK
    n_chunks = n_rows // EXPERT_CHUNK
    meta = meta.reshape(meta.shape[0], n_tokens).astype(jnp.int32)
    per_token = lambda r: meta[r]
    e1 = per_token(0)
    e2 = per_token(1)
    cnt = counts[:, 0].astype(jnp.int32)
    padded = (cnt + (EXPERT_CHUNK - 1)) // EXPERT_CHUNK * EXPERT_CHUNK
    ends = jnp.cumsum(padded)
    starts = ends - padded
    first = jnp.cumsum(cnt) - cnt
    pos1 = starts[e1] + per_token(2)
    pos2 = starts[e2] + per_token(3)
    order = jnp.argsort(jnp.concatenate([pos1, pos2])).astype(jnp.int32)
    sorted_tok = jnp.where(order >= n_tokens, order - n_tokens, order)
    chunk_row0 = jnp.arange(n_chunks, dtype=jnp.int32) * EXPERT_CHUNK
    chunk_expert = jnp.minimum(
        jnp.sum((ends[None, :] <= chunk_row0[:, None]).astype(jnp.int32), axis=1), N_EXPERTS - 1)
    local = (chunk_row0 - starts[chunk_expert])[:, None] + jnp.arange(EXPERT_CHUNK, dtype=jnp.int32)
    real = local < cnt[chunk_expert][:, None]
    nth = jnp.clip(first[chunk_expert][:, None] + local, 0, 2 * n_tokens - 1)
    spread = (chunk_row0[:, None] + jnp.arange(EXPERT_CHUNK, dtype=jnp.int32)) % n_tokens
    src = jnp.where(real, sorted_tok[nth], spread).reshape(n_rows)
    n_used = (ends[-1] // EXPERT_CHUNK).reshape(1).astype(jnp.int32)
    return pos1, pos2, src, chunk_expert, n_used


def _expert_kernel(ce_ref, nu_ref, xs_ref, wg_ref, wu_ref, wd_ref, o_ref, wg_bf, wu_bf, wd_bf):
    c = pl.program_id(0)

    @pl.when(c < nu_ref[0])
    def _():
        @pl.when((c == 0) | (ce_ref[c] != ce_ref[jnp.maximum(c - 1, 0)]))
        def _():
            wg_bf[...] = wg_ref[0, 0].astype(jnp.bfloat16)
            wu_bf[...] = wu_ref[0, 0].astype(jnp.bfloat16)
            wd_bf[...] = wd_ref[0, 0].astype(jnp.bfloat16)

        xs = xs_ref[...].astype(jnp.bfloat16)
        gt = jnp.dot(xs, wg_bf[...], preferred_element_type=jnp.float32)
        up = jnp.dot(xs, wu_bf[...], preferred_element_type=jnp.float32)
        act = (gt * (1.0 / (1.0 + jnp.exp(-gt))) * up).astype(jnp.bfloat16)
        o_ref[...] = jnp.dot(act, wd_bf[...], preferred_element_type=jnp.float32)

    @pl.when(c >= nu_ref[0])
    def _():
        o_ref[...] = jnp.zeros_like(o_ref)


def _expert_call(chunk_expert, n_used, xs, w_gate, w_up, w_down, layer):
    n_rows, D = xs.shape
    grid_spec = pltpu.PrefetchScalarGridSpec(
        num_scalar_prefetch=2,
        grid=(n_rows // EXPERT_CHUNK,),
        in_specs=[
            pl.BlockSpec((EXPERT_CHUNK, D), lambda c, ce, nu: (c, 0)),
            pl.BlockSpec((1, 1, D, D_EXPERT), lambda c, ce, nu: (layer, ce[c], 0, 0)),
            pl.BlockSpec((1, 1, D, D_EXPERT), lambda c, ce, nu: (layer, ce[c], 0, 0)),
            pl.BlockSpec((1, 1, D_EXPERT, D), lambda c, ce, nu: (layer, ce[c], 0, 0)),
        ],
        out_specs=pl.BlockSpec((EXPERT_CHUNK, D), lambda c, ce, nu: (c, 0)),
        scratch_shapes=[
            pltpu.VMEM((D, D_EXPERT), jnp.bfloat16),
            pltpu.VMEM((D, D_EXPERT), jnp.bfloat16),
            pltpu.VMEM((D_EXPERT, D), jnp.bfloat16),
        ],
    )
    return pl.pallas_call(
        _expert_kernel,
        grid_spec=grid_spec,
        out_shape=jax.ShapeDtypeStruct((n_rows, D), jnp.float32),
        compiler_params=_params(("arbitrary",)),
    )(chunk_expert, n_used, xs, w_gate, w_up, w_down)


def _combine_kernel(y1_ref, y2_ref, wcol_ref, x_ref, mod_ref, g_ref, b_ref, o_ref):
    wc = wcol_ref[0]
    w1 = (wc[:, 0:1] + wc[:, 1:2]) + wc[:, 2:3]
    w2 = (wc[:, 3:4] + wc[:, 4:5]) + wc[:, 5:6]
    y = w1 * y1_ref[0] + w2 * y2_ref[0]
    gate = mod_ref[0, 5:6, :]
    z = DEEPNORM_ALPHA * x_ref[0] + gate * y
    o_ref[0] = _layer_norm(z) * g_ref[...] + b_ref[...]


def _combine_call(y1, y2, wcol, x, modl, ln_g, ln_b):
    B, S, D = x.shape
    tm = 1024
    rows = pl.BlockSpec((1, tm, D), lambda b, i: (b, i, 0))
    vec = pl.BlockSpec((1, D), lambda b, i: (0, 0))
    return pl.pallas_call(
        _combine_kernel,
        grid=(B, S // tm),
        in_specs=[rows, rows, pl.BlockSpec((1, tm, LANES), lambda b, i: (b, i, 0)), rows,
                  pl.BlockSpec((1, N_MOD, D), lambda b, i: (b, 0, 0)), vec, vec],
        out_specs=rows,
        out_shape=jax.ShapeDtypeStruct((B, S, D), jnp.float32),
        compiler_params=_params(("arbitrary", "arbitrary")),
    )(y1.reshape(B, S, D), y2.reshape(B, S, D), wcol.reshape(B, S, LANES),
      x, modl, ln_g.reshape(1, D), ln_b.reshape(1, D))


def _moe_routed(x, modl, w_router_t, rbias_col, w_gate, w_up, w_down, layer, ln_g, ln_b):
    B, S, D = x.shape
    h, meta, counts, wcol = _route_call(x, modl, w_router_t, rbias_col)
    pos1, pos2, src, chunk_expert, n_used = _dispatch_plan(meta, counts, B * S)
    xs = _sc_gather_rows(h, src)
    ys = _expert_call(chunk_expert, n_used, xs, w_gate, w_up, w_down, layer)
    y1 = _sc_gather_rows(ys, pos1)
    y2 = _sc_gather_rows(ys, pos2)
    return _combine_call(y1, y2, wcol, x, modl, ln_g, ln_b)


def _rope_tables(S):
    half = ROT_DIM // 2
    inv_freq = ROPE_THETA ** (-(jnp.arange(half, dtype=jnp.float32) * 2.0 / ROT_DIM))
    ang = jnp.arange(S, dtype=jnp.float32)[:, None] * inv_freq[None, :]
    cos, sin = jnp.cos(ang), jnp.sin(ang)
    d = jnp.arange(LANES) % HEAD_DIM
    f = d % half
    rot = d[None, :] < ROT_DIM
    return jnp.where(rot, cos[:, f], 1.0), jnp.where(rot, sin[:, f], 0.0)


def _with_rotary_partner_columns(w_in_l):
    half = ROT_DIM // 2
    qk = w_in_l[:, :2 * D_ATTN]
    d = (jnp.arange(2 * D_ATTN) % HEAD_DIM)[None, :]
    partner = jnp.where(d < half, -jnp.roll(qk, -half, axis=1),
                        jnp.where(d < ROT_DIM, jnp.roll(qk, half, axis=1), 0.0))
    return jnp.concatenate([w_in_l, partner], axis=1)


def _bias_placement():
    src = jnp.arange(LANES)
    hd, u, j = src >> 4, (src >> 3) & 1, src & 7
    col0 = jnp.where(hd % 2 == 0, HEAD_DIM, 0)
    dst = LANES * hd + col0 + j
    onehot = (jnp.arange(N_HEADS * LANES)[None, :] == dst[:, None]) & (u[:, None] == 1)
    return onehot.astype(jnp.bfloat16)


def kernel(x, c, w_mod, b_mod, w_in, w_pool, pool_scale, w_out, ln1_g, ln1_b,
           w_router, router_bias, w_gate, w_up, w_down, ln2_g, ln2_b):
    B, S, D = x.shape
    bf = jnp.bfloat16
    mod = _mod_call(c, w_mod, b_mod).reshape(DEPTH, B, N_MOD, D)
    cos_t, sin_t = _rope_tables(S)
    place = _bias_placement()
    w_router_t = w_router.T
    rbias_col = router_bias.reshape(N_EXPERTS, 1)
    for l in range(DEPTH):
        modl = mod[l]
        w_in_aug = _with_rotary_partner_columns(w_in[l]).astype(bf)
        qa, ka, v, m, _ = _proj_call(x, modl, w_in_aug, cos_t, sin_t, place,
                                     w_pool[l], pool_scale[l])
        a = _attn_call(qa, ka, v)
        x = _mixout_call(a, m, x, modl, w_out[l].astype(bf), ln1_g[l], ln1_b[l])
        x = _moe_routed(x, modl, w_router_t, rbias_col, w_gate, w_up, w_down, l,
                        ln2_g[l], ln2_b[l])
    return x
```
